```python
import jax, jax.numpy as jnp
from jax import lax
import numpy as np

D_MODEL = 4096
BATCH = 8
SEQ = 4096
DEPTH = 1

MEM_LEN = 256
MIX_WIDTH = D_MODEL
GROUP_DIM = 128
CONV_WIDTH = MIX_WIDTH // 2
SCONV_WIDTH = MIX_WIDTH - CONV_WIDTH
N_CONV_GROUPS = CONV_WIDTH // GROUP_DIM
N_SCONV_HEADS = SCONV_WIDTH // GROUP_DIM
IN_WIDTH = 2 * CONV_WIDTH + 3 * SCONV_WIDTH
CONV_KERNEL = 31
SHORT_KERNEL = 3
FFN_KERNEL = 3
D_FF = ((8 * D_MODEL // 3 + 255) // 256) * 256
N_XATTN_HEADS = 4
XATTN_HEAD_DIM = D_MODEL // N_XATTN_HEADS
EPS = 1e-6

kernel_name = "hybrid_conformer_shortconv_xattn_convffn"


def rms_norm(x, g):
    xf = x.astype(jnp.float32)
    y = xf * lax.rsqrt(jnp.mean(xf * xf, axis=-1, keepdims=True) + EPS)
    return (y * g.astype(jnp.float32)).astype(x.dtype)


def group_layer_norm(x, g, b, group):
    shape = x.shape
    xf = x.astype(jnp.float32).reshape(*shape[:-1], shape[-1] // group, group)
    mu = jnp.mean(xf, axis=-1, keepdims=True)
    xc = xf - mu
    var = jnp.mean(xc * xc, axis=-1, keepdims=True)
    y = (xc * lax.rsqrt(var + EPS)).reshape(shape)
    return (y * g.astype(jnp.float32) + b.astype(jnp.float32)).astype(x.dtype)


def causal_dwconv(x, w):
    K = w.shape[0]
    T = x.shape[1]
    xp = jnp.pad(x, ((0, 0), (K - 1, 0), (0, 0)))
    y = xp[:, 0:T, :] * w[0]
    for k in range(1, K):
        y = y + xp[:, k:k + T, :] * w[k]
    return y


def _fwd_setup_inputs(seed: int = 0) -> dict:
    key = jax.random.key(seed)
    ks = jax.random.split(key, 24)

    def nrm(k, shape, scale):
        return jax.random.normal(k, shape, jnp.float32) * scale

    def gain(k, shape):
        return 1.0 + 0.02 * jax.random.normal(k, shape, jnp.float32)

    L = DEPTH
    return {
        "x": nrm(ks[0], (BATCH, SEQ, D_MODEL), 1.0),
        "mem": nrm(ks[1], (BATCH, MEM_LEN, D_MODEL), 1.0),
        "g_mix": gain(ks[2], (L, D_MODEL)),
        "w_in": nrm(ks[3], (L, D_MODEL, IN_WIDTH), D_MODEL ** -0.5),
        "conv_a_w": nrm(ks[4], (L, CONV_KERNEL, CONV_WIDTH), CONV_KERNEL ** -0.5),
        "conv_a_b": nrm(ks[5], (L, CONV_WIDTH), 0.02),
        "ln_a_g": gain(ks[6], (L, CONV_WIDTH)),
        "ln_a_b": nrm(ks[7], (L, CONV_WIDTH), 0.02),
        "conv_b_w": nrm(ks[8], (L, SHORT_KERNEL, SCONV_WIDTH), SHORT_KERNEL ** -0.5),
        "w_out": nrm(ks[9], (L, MIX_WIDTH, D_MODEL), MIX_WIDTH ** -0.5),
        "g_xattn": gain(ks[10], (L, D_MODEL)),
        "g_mem": gain(ks[11], (D_MODEL,)),
        "w_q": nrm(ks[12], (L, D_MODEL, D_MODEL), D_MODEL ** -0.5),
        "w_k": nrm(ks[13], (L, D_MODEL, D_MODEL), D_MODEL ** -0.5),
        "w_v": nrm(ks[14], (L, D_MODEL, D_MODEL), D_MODEL ** -0.5),
        "w_o": nrm(ks[15], (L, D_MODEL, D_MODEL), D_MODEL ** -0.5),
        "g_ffn": gain(ks[16], (L, D_MODEL)),
        "w_gate": nrm(ks[17], (L, D_MODEL, D_FF), D_MODEL ** -0.5),
        "w_up": nrm(ks[18], (L, D_MODEL, D_FF), D_MODEL ** -0.5),
        "conv_f_w": nrm(ks[19], (L, FFN_KERNEL, D_FF), FFN_KERNEL ** -0.5),
        "w_down": nrm(ks[20], (L, D_FF, D_MODEL), D_FF ** -0.5),
        "g_final": gain(ks[21], (D_MODEL,)),
    }


def _fwd_reference(x, mem, g_mix, w_in, conv_a_w, conv_a_b, ln_a_g, ln_a_b, conv_b_w,
              w_out, g_xattn, g_mem, w_q, w_k, w_v, w_o, g_ffn, w_gate, w_up,
              conv_f_w, w_down, g_final):
    B, T, D = x.shape
    M = mem.shape[1]
    memn = rms_norm(mem, g_mem)
    splits = [CONV_WIDTH, 2 * CONV_WIDTH, 2 * CONV_WIDTH + SCONV_WIDTH,
              2 * CONV_WIDTH + 2 * SCONV_WIDTH]
    h = x
    for l in range(DEPTH):
        xn = rms_norm(h, g_mix[l])
        proj = xn @ w_in[l]
        a_val, a_gate, b_gate, c_gate, b_h = jnp.split(proj, splits, axis=-1)
        u = a_val * jax.nn.sigmoid(a_gate)
        u = causal_dwconv(u, conv_a_w[l]) + conv_a_b[l]
        u = jax.nn.silu(group_layer_norm(u, ln_a_g[l], ln_a_b[l], GROUP_DIM))
        v = b_gate * causal_dwconv(c_gate * b_h, conv_b_w[l])
        mix = jnp.concatenate([u, v], axis=-1)
        h = h + mix @ w_out[l]

        xn = rms_norm(h, g_xattn[l])
        q = (xn @ w_q[l]).reshape(B, T, N_XATTN_HEADS, XATTN_HEAD_DIM)
        k = (memn @ w_k[l]).reshape(B, M, N_XATTN_HEADS, XATTN_HEAD_DIM)
        vm = (memn @ w_v[l]).reshape(B, M, N_XATTN_HEADS, XATTN_HEAD_DIM)
        s = jnp.einsum('bthd,bmhd->bhtm', q.astype(jnp.float32), k.astype(jnp.float32))
        p = jax.nn.softmax(s * (XATTN_HEAD_DIM ** -0.5), axis=-1).astype(vm.dtype)
        o = jnp.einsum('bhtm,bmhd->bthd', p, vm).reshape(B, T, D)
        h = h + o @ w_o[l]

        xn = rms_norm(h, g_ffn[l])
        g = causal_dwconv(xn @ w_gate[l], conv_f_w[l])
        f = jax.nn.silu(g) * (xn @ w_up[l])
        h = h + f @ w_down[l]
    return rms_norm(h, g_final)


import jax as _jax
import jax.numpy as _jnp

TWIN_FORMAT = 'train_step'
FWD_PARAMS = ['x', 'mem', 'g_mix', 'w_in', 'conv_a_w', 'conv_a_b', 'ln_a_g', 'ln_a_b', 'conv_b_w', 'w_out', 'g_xattn', 'g_mem', 'w_q', 'w_k', 'w_v', 'w_o', 'g_ffn', 'w_gate', 'w_up', 'conv_f_w', 'w_down', 'g_final']
TWIN_WEIGHTS = ['g_mix', 'w_in', 'conv_a_w', 'conv_a_b', 'ln_a_g', 'ln_a_b', 'conv_b_w', 'w_out', 'g_xattn', 'g_mem', 'w_q', 'w_k', 'w_v', 'w_o', 'g_ffn', 'w_gate', 'w_up', 'conv_f_w', 'w_down', 'g_final']
TWIN_DIFF_INPUT = 'x'
TWIN_INPUTS = ['x', 'mem', 'g_mix', 'w_in', 'conv_a_w', 'conv_a_b', 'ln_a_g', 'ln_a_b', 'conv_b_w', 'w_out', 'g_xattn', 'g_mem', 'w_q', 'w_k', 'w_v', 'w_o', 'g_ffn', 'w_gate', 'w_up', 'conv_f_w', 'w_down', 'g_final', 'loss_target', 'm_g_mix', 'm_w_in', 'm_conv_a_w', 'm_conv_a_b', 'm_ln_a_g', 'm_ln_a_b', 'm_conv_b_w', 'm_w_out', 'm_g_xattn', 'm_g_mem', 'm_w_q', 'm_w_k', 'm_w_v', 'm_w_o', 'm_g_ffn', 'm_w_gate', 'm_w_up', 'm_conv_f_w', 'm_w_down', 'm_g_final', 'v_g_mix', 'v_w_in', 'v_conv_a_w', 'v_conv_a_b', 'v_ln_a_g', 'v_ln_a_b', 'v_conv_b_w', 'v_w_out', 'v_g_xattn', 'v_g_mem', 'v_w_q', 'v_w_k', 'v_w_v', 'v_w_o', 'v_g_ffn', 'v_w_gate', 'v_w_up', 'v_conv_f_w', 'v_w_down', 'v_g_final']
TWIN_OUTPUTS = ['loss', 'grad_x', 'grad_g_mix', 'grad_w_in', 'grad_conv_a_w', 'grad_conv_a_b', 'grad_ln_a_g', 'grad_ln_a_b', 'grad_conv_b_w', 'grad_w_out', 'grad_g_xattn', 'grad_g_mem', 'grad_w_q', 'grad_w_k', 'grad_w_v', 'grad_w_o', 'grad_g_ffn', 'grad_w_gate', 'grad_w_up', 'grad_conv_f_w', 'grad_w_down', 'grad_g_final', 'delta_g_mix', 'delta_w_in', 'delta_conv_a_w', 'delta_conv_a_b', 'delta_ln_a_g', 'delta_ln_a_b', 'delta_conv_b_w', 'delta_w_out', 'delta_g_xattn', 'delta_g_mem', 'delta_w_q', 'delta_w_k', 'delta_w_v', 'delta_w_o', 'delta_g_ffn', 'delta_w_gate', 'delta_w_up', 'delta_conv_f_w', 'delta_w_down', 'delta_g_final', 'new_m_g_mix', 'new_m_w_in', 'new_m_conv_a_w', 'new_m_conv_a_b', 'new_m_ln_a_g', 'new_m_ln_a_b', 'new_m_conv_b_w', 'new_m_w_out', 'new_m_g_xattn', 'new_m_g_mem', 'new_m_w_q', 'new_m_w_k', 'new_m_w_v', 'new_m_w_o', 'new_m_g_ffn', 'new_m_w_gate', 'new_m_w_up', 'new_m_conv_f_w', 'new_m_w_down', 'new_m_g_final', 'new_v_g_mix', 'new_v_w_in', 'new_v_conv_a_w', 'new_v_conv_a_b', 'new_v_ln_a_g', 'new_v_ln_a_b', 'new_v_conv_b_w', 'new_v_w_out', 'new_v_g_xattn', 'new_v_g_mem', 'new_v_w_q', 'new_v_w_k', 'new_v_w_v', 'new_v_w_o', 'new_v_g_ffn', 'new_v_w_gate', 'new_v_w_up', 'new_v_conv_f_w', 'new_v_w_down', 'new_v_g_final']
TWIN_LEAF_KINDS = {'loss': 'loss', 'grad_x': 'grad_x', 'grad_g_mix': 'grad_w', 'grad_w_in': 'grad_w', 'grad_conv_a_w': 'grad_w', 'grad_conv_a_b': 'grad_w', 'grad_ln_a_g': 'grad_w', 'grad_ln_a_b': 'grad_w', 'grad_conv_b_w': 'grad_w', 'grad_w_out': 'grad_w', 'grad_g_xattn': 'grad_w', 'grad_g_mem': 'grad_w', 'grad_w_q': 'grad_w', 'grad_w_k': 'grad_w', 'grad_w_v': 'grad_w', 'grad_w_o': 'grad_w', 'grad_g_ffn': 'grad_w', 'grad_w_gate': 'grad_w', 'grad_w_up': 'grad_w', 'grad_conv_f_w': 'grad_w', 'grad_w_down': 'grad_w', 'grad_g_final': 'grad_w', 'delta_g_mix': 'delta_w', 'delta_w_in': 'delta_w', 'delta_conv_a_w': 'delta_w', 'delta_conv_a_b': 'delta_w', 'delta_ln_a_g': 'delta_w', 'delta_ln_a_b': 'delta_w', 'delta_conv_b_w': 'delta_w', 'delta_w_out': 'delta_w', 'delta_g_xattn': 'delta_w', 'delta_g_mem': 'delta_w', 'delta_w_q': 'delta_w', 'delta_w_k': 'delta_w', 'delta_w_v': 'delta_w', 'delta_w_o': 'delta_w', 'delta_g_ffn': 'delta_w', 'delta_w_gate': 'delta_w', 'delta_w_up': 'delta_w', 'delta_conv_f_w': 'delta_w', 'delta_w_down': 'delta_w', 'delta_g_final': 'delta_w', 'new_m_g_mix': 'new_m', 'new_m_w_in': 'new_m', 'new_m_conv_a_w': 'new_m', 'new_m_conv_a_b': 'new_m', 'new_m_ln_a_g': 'new_m', 'new_m_ln_a_b': 'new_m', 'new_m_conv_b_w': 'new_m', 'new_m_w_out': 'new_m', 'new_m_g_xattn': 'new_m', 'new_m_g_mem': 'new_m', 'new_m_w_q': 'new_m', 'new_m_w_k': 'new_m', 'new_m_w_v': 'new_m', 'new_m_w_o': 'new_m', 'new_m_g_ffn': 'new_m', 'new_m_w_gate': 'new_m', 'new_m_w_up': 'new_m', 'new_m_conv_f_w': 'new_m', 'new_m_w_down': 'new_m', 'new_m_g_final': 'new_m', 'new_v_g_mix': 'new_v', 'new_v_w_in': 'new_v', 'new_v_conv_a_w': 'new_v', 'new_v_conv_a_b': 'new_v', 'new_v_ln_a_g': 'new_v', 'new_v_ln_a_b': 'new_v', 'new_v_conv_b_w': 'new_v', 'new_v_w_out': 'new_v', 'new_v_g_xattn': 'new_v', 'new_v_g_mem': 'new_v', 'new_v_w_q': 'new_v', 'new_v_w_k': 'new_v', 'new_v_w_v': 'new_v', 'new_v_w_o': 'new_v', 'new_v_g_ffn': 'new_v', 'new_v_w_gate': 'new_v', 'new_v_w_up': 'new_v', 'new_v_conv_f_w': 'new_v', 'new_v_w_down': 'new_v', 'new_v_g_final': 'new_v'}


def _forward(args):
    return _fwd_reference(*[args[k] for k in FWD_PARAMS])


def _output_shape():
    out = _jax.eval_shape(lambda: _forward(_fwd_setup_inputs(0)))
    return out.shape, out.dtype

N_MICROBATCH = 1
ADAM_LR = 0.001
ADAM_B1 = 0.9
ADAM_B2 = 0.999
ADAM_EPS = 1e-08
ADAM_WD = 0.01
ADAM_STEP = 10
PER_EXAMPLE_BATCH_AXIS = {'x': 0, 'mem': 0, 'loss_target': 0}
SHARED_INPUTS = []
_WEIGHT_DTYPES = {'g_mix': _jnp.float32, 'w_in': _jnp.float32, 'conv_a_w': _jnp.float32, 'conv_a_b': _jnp.float32, 'ln_a_g': _jnp.float32, 'ln_a_b': _jnp.float32, 'conv_b_w': _jnp.float32, 'w_out': _jnp.float32, 'g_xattn': _jnp.float32, 'g_mem': _jnp.float32, 'w_q': _jnp.float32, 'w_k': _jnp.float32, 'w_v': _jnp.float32, 'w_o': _jnp.float32, 'g_ffn': _jnp.float32, 'w_gate': _jnp.float32, 'w_up': _jnp.float32, 'conv_f_w': _jnp.float32, 'w_down': _jnp.float32, 'g_final': _jnp.float32}
MOMENT_SCALE = {'g_mix': 4.864933e-02, 'w_in': 3.087594e-02, 'conv_a_w': 2.318392e-02, 'conv_a_b': 4.956648e-02, 'ln_a_g': 2.698777e-02, 'ln_a_b': 2.452406e-02, 'conv_b_w': 3.806767e-02, 'w_out': 3.070481e-02, 'g_xattn': 3.841732e-03, 'g_mem': 5.678513e-03, 'w_q': 3.774999e-03, 'w_k': 3.770822e-03, 'w_v': 3.882146e-03, 'w_o': 3.836871e-03, 'g_ffn': 2.705590e-02, 'w_gate': 1.181759e-02, 'w_up': 1.144612e-02, 'conv_f_w': 1.206293e-02, 'w_down': 1.875351e-02, 'g_final': 7.986570e+00}


def _to_microbatches(a, axis):
    t = _jnp.moveaxis(a, axis, 0)
    t = t.reshape((N_MICROBATCH, t.shape[0] // N_MICROBATCH) + t.shape[1:])
    return _jnp.moveaxis(t, 1, axis + 1)


def setup_inputs(seed: int = 0) -> dict:
    inp = _fwd_setup_inputs(seed)
    key = _jax.random.fold_in(_jax.random.key(seed), 7919)
    shape, _ = _output_shape()
    out = dict(inp)
    out["loss_target"] = _jax.random.normal(_jax.random.fold_in(key, 0), shape, _jnp.float32)
    for i, name in enumerate(TWIN_WEIGHTS):
        w = inp[name].astype(_jnp.float32)
        if MOMENT_SCALE is None:
            s = _jnp.sqrt(_jnp.mean(_jnp.square(w)) + 1e-30)
        else:
            s = MOMENT_SCALE[name]
        km, kv = _jax.random.split(_jax.random.fold_in(key, i + 1))
        out[name] = w
        out["m_" + name] = s * _jax.random.normal(km, w.shape, _jnp.float32)
        out["v_" + name] = (s * s) * _jax.random.uniform(kv, w.shape, _jnp.float32, 0.5, 1.5)
    if N_MICROBATCH > 1:
        for name, axis in PER_EXAMPLE_BATCH_AXIS.items():
            out[name] = _to_microbatches(out[name], axis)
    return {'x': out['x'], 'mem': out['mem'], 'g_mix': out['g_mix'], 'w_in': out['w_in'], 'conv_a_w': out['conv_a_w'], 'conv_a_b': out['conv_a_b'], 'ln_a_g': out['ln_a_g'], 'ln_a_b': out['ln_a_b'], 'conv_b_w': out['conv_b_w'], 'w_out': out['w_out'], 'g_xattn': out['g_xattn'], 'g_mem': out['g_mem'], 'w_q': out['w_q'], 'w_k': out['w_k'], 'w_v': out['w_v'], 'w_o': out['w_o'], 'g_ffn': out['g_ffn'], 'w_gate': out['w_gate'], 'w_up': out['w_up'], 'conv_f_w': out['conv_f_w'], 'w_down': out['w_down'], 'g_final': out['g_final'], 'loss_target': out['loss_target'], 'm_g_mix': out['m_g_mix'], 'm_w_in': out['m_w_in'], 'm_conv_a_w': out['m_conv_a_w'], 'm_conv_a_b': out['m_conv_a_b'], 'm_ln_a_g': out['m_ln_a_g'], 'm_ln_a_b': out['m_ln_a_b'], 'm_conv_b_w': out['m_conv_b_w'], 'm_w_out': out['m_w_out'], 'm_g_xattn': out['m_g_xattn'], 'm_g_mem': out['m_g_mem'], 'm_w_q': out['m_w_q'], 'm_w_k': out['m_w_k'], 'm_w_v': out['m_w_v'], 'm_w_o': out['m_w_o'], 'm_g_ffn': out['m_g_ffn'], 'm_w_gate': out['m_w_gate'], 'm_w_up': out['m_w_up'], 'm_conv_f_w': out['m_conv_f_w'], 'm_w_down': out['m_w_down'], 'm_g_final': out['m_g_final'], 'v_g_mix': out['v_g_mix'], 'v_w_in': out['v_w_in'], 'v_conv_a_w': out['v_conv_a_w'], 'v_conv_a_b': out['v_conv_a_b'], 'v_ln_a_g': out['v_ln_a_g'], 'v_ln_a_b': out['v_ln_a_b'], 'v_conv_b_w': out['v_conv_b_w'], 'v_w_out': out['v_w_out'], 'v_g_xattn': out['v_g_xattn'], 'v_g_mem': out['v_g_mem'], 'v_w_q': out['v_w_q'], 'v_w_k': out['v_w_k'], 'v_w_v': out['v_w_v'], 'v_w_o': out['v_w_o'], 'v_g_ffn': out['v_g_ffn'], 'v_w_gate': out['v_w_gate'], 'v_w_up': out['v_w_up'], 'v_conv_f_w': out['v_conv_f_w'], 'v_w_down': out['v_w_down'], 'v_g_final': out['v_g_final']}


def _loss(weights, diff, rest, loss_target):
    with _jax.named_scope("forward"):
        args = {**rest, TWIN_DIFF_INPUT: diff, **{k: w.astype(_WEIGHT_DTYPES[k]) for k, w in weights.items()}}
        y = _forward(args)
    with _jax.named_scope("loss_head"):
        err = _jnp.square(y.astype(_jnp.float32) - loss_target)
        return 0.5 * _jnp.sum(_jnp.mean(err, axis=-1)) if err.ndim else 0.5 * err


def _adamw(w, g, m, v):
    m = ADAM_B1 * m + (1.0 - ADAM_B1) * g
    v = ADAM_B2 * v + (1.0 - ADAM_B2) * _jnp.square(g)
    m_hat = m / (1.0 - ADAM_B1 ** ADAM_STEP)
    v_hat = v / (1.0 - ADAM_B2 ** ADAM_STEP)
    delta = -ADAM_LR * (m_hat / (_jnp.sqrt(v_hat) + ADAM_EPS) + ADAM_WD * w)
    return delta, m, v


def reference(x, mem, g_mix, w_in, conv_a_w, conv_a_b, ln_a_g, ln_a_b, conv_b_w, w_out, g_xattn, g_mem, w_q, w_k, w_v, w_o, g_ffn, w_gate, w_up, conv_f_w, w_down, g_final, loss_target, m_g_mix, m_w_in, m_conv_a_w, m_conv_a_b, m_ln_a_g, m_ln_a_b, m_conv_b_w, m_w_out, m_g_xattn, m_g_mem, m_w_q, m_w_k, m_w_v, m_w_o, m_g_ffn, m_w_gate, m_w_up, m_conv_f_w, m_w_down, m_g_final, v_g_mix, v_w_in, v_conv_a_w, v_conv_a_b, v_ln_a_g, v_ln_a_b, v_conv_b_w, v_w_out, v_g_xattn, v_g_mem, v_w_q, v_w_k, v_w_v, v_w_o, v_g_ffn, v_w_gate, v_w_up, v_conv_f_w, v_w_down, v_g_final):
    given = dict(x=x, mem=mem, g_mix=g_mix, w_in=w_in, conv_a_w=conv_a_w, conv_a_b=conv_a_b, ln_a_g=ln_a_g, ln_a_b=ln_a_b, conv_b_w=conv_b_w, w_out=w_out, g_xattn=g_xattn, g_mem=g_mem, w_q=w_q, w_k=w_k, w_v=w_v, w_o=w_o, g_ffn=g_ffn, w_gate=w_gate, w_up=w_up, conv_f_w=conv_f_w, w_down=w_down, g_final=g_final, loss_target=loss_target, m_g_mix=m_g_mix, m_w_in=m_w_in, m_conv_a_w=m_conv_a_w, m_conv_a_b=m_conv_a_b, m_ln_a_g=m_ln_a_g, m_ln_a_b=m_ln_a_b, m_conv_b_w=m_conv_b_w, m_w_out=m_w_out, m_g_xattn=m_g_xattn, m_g_mem=m_g_mem, m_w_q=m_w_q, m_w_k=m_w_k, m_w_v=m_w_v, m_w_o=m_w_o, m_g_ffn=m_g_ffn, m_w_gate=m_w_gate, m_w_up=m_w_up, m_conv_f_w=m_conv_f_w, m_w_down=m_w_down, m_g_final=m_g_final, v_g_mix=v_g_mix, v_w_in=v_w_in, v_conv_a_w=v_conv_a_w, v_conv_a_b=v_conv_a_b, v_ln_a_g=v_ln_a_g, v_ln_a_b=v_ln_a_b, v_conv_b_w=v_conv_b_w, v_w_out=v_w_out, v_g_xattn=v_g_xattn, v_g_mem=v_g_mem, v_w_q=v_w_q, v_w_k=v_w_k, v_w_v=v_w_v, v_w_o=v_w_o, v_g_ffn=v_g_ffn, v_w_gate=v_w_gate, v_w_up=v_w_up, v_conv_f_w=v_conv_f_w, v_w_down=v_w_down, v_g_final=v_g_final)
    weights = {n: given[n] for n in TWIN_WEIGHTS}
    shared = {n: given[n] for n in SHARED_INPUTS}
    per_example = {n: given[n] for n in ['x', 'mem']}
    grad_fn = _jax.value_and_grad(_loss, argnums=(0, 1))

    def one_microbatch(ex, loss_target):
        ex = dict(ex)
        diff = ex.pop(TWIN_DIFF_INPUT)
        return grad_fn(weights, diff, {**shared, **ex}, loss_target)

    if N_MICROBATCH == 1:
        loss, (grad_w, grad_x) = one_microbatch(per_example, given["loss_target"])
    else:
        def body(carry, xs):
            loss_sum, grad_sum = carry
            l_k, (gw_k, gx_k) = one_microbatch(xs[0], xs[1])
            with _jax.named_scope("update"):
                return (loss_sum + l_k, _jax.tree.map(_jnp.add, grad_sum, gw_k)), gx_k

        init = (_jnp.zeros((), _jnp.float32), _jax.tree.map(_jnp.zeros_like, weights))
        (loss, grad_w), grad_x = _jax.lax.scan(body, init, (per_example, given["loss_target"]))
    with _jax.named_scope("update"):
        delta_w, new_m, new_v = {}, {}, {}
        for n in TWIN_WEIGHTS:
            delta_w[n], new_m[n], new_v[n] = _adamw(weights[n], grad_w[n], given["m_" + n], given["v_" + n])
    return (loss, grad_x, *[grad_w[n] for n in TWIN_WEIGHTS], *[delta_w[n] for n in TWIN_WEIGHTS],
            *[new_m[n] for n in TWIN_WEIGHTS], *[new_v[n] for n in TWIN_WEIGHTS])
```

```python
import functools

import jax
import jax.numpy as jnp
from jax import lax
from jax.experimental import pallas as pl
from jax.experimental.pallas import tpu as pltpu

F32 = jnp.float32
BF16 = jnp.bfloat16

N_DEV = 8
EPS = 1e-6
GROUP_DIM = 128
N_XATTN_HEADS = 4
ADAM_LR = 0.001
ADAM_B1 = 0.9
ADAM_B2 = 0.999
ADAM_EPS = 1e-08
ADAM_WD = 0.01
ADAM_STEP = 10

HALO = 32
VMEM_V7X_BYTES = 64 * 1024 * 1024
VMEM_TEMP_ALLOWANCE = 12 * 1024 * 1024

HBM_SPEC = pl.BlockSpec(memory_space=pltpu.HBM)
VMEM_SPEC = pl.BlockSpec(memory_space=pltpu.VMEM)
MESH = pl.DeviceIdType.MESH


def _tile(n, pref, align):
    if n <= pref:
        return n
    t = (pref // align) * align
    while t >= align:
        if n % t == 0:
            return t
        t -= align
    return n


def _nbytes(shape, dtype):
    n = 1
    for d in shape:
        if d is not None:
            n *= d
    return n * jnp.dtype(dtype).itemsize


def _call(body, name, grid, in_specs, out_specs, out_shape, operands, scratch=(), sem=None):
    outs = out_shape if isinstance(out_shape, (tuple, list)) else (out_shape,)
    ospecs = out_specs if isinstance(out_specs, (tuple, list)) else (out_specs,)
    est = 0
    for spec, arr in list(zip(in_specs, operands)) + list(zip(ospecs, outs)):
        est += 2 * _nbytes(spec.block_shape, arr.dtype)
    for s in scratch:
        if hasattr(s, "shape") and hasattr(s, "dtype"):
            est += _nbytes(s.shape, s.dtype)
    limit = min(est + VMEM_TEMP_ALLOWANCE, VMEM_V7X_BYTES - 4 * 1024 * 1024)
    if sem is None:
        sem = ("arbitrary",) * len(grid)
    return pl.pallas_call(
        body, name=name, grid=grid, in_specs=in_specs, out_specs=out_specs, out_shape=out_shape,
        scratch_shapes=list(scratch),
        compiler_params=pltpu.CompilerParams(dimension_semantics=sem, vmem_limit_bytes=int(limit)),
    )(*operands)


_DOT_DIMS = {"nn": (((1,), (0,)), ((), ())), "nt": (((1,), (1,)), ((), ())), "tn": (((0,), (0,)), ((), ()))}


def _operand_spec(layout, tr, tc, cols_per_block, pick):
    if layout == "plain":
        return pl.BlockSpec((tr, tc), lambda *g: pick(*g)[1:])
    if layout == "blk":
        return pl.BlockSpec((None, tr, tc), lambda *g: pick(*g))
    assert layout == "col"
    per = cols_per_block // tc
    return pl.BlockSpec((tr, tc), lambda *g: (pick(*g)[1], pick(*g)[0] * per + pick(*g)[2]))


def _matmul(name, dims, a, b, *, M, N, K, nb=1, a_lay="plain", b_lay="plain", o_lay="plain",
            red_block=False, out_dtype=F32, res=None, tm=1024, tn=1024, tk=512):
    tm = _tile(M, tm, 128 if dims == "tn" else 16)
    tn = _tile(N, tn, 128)
    tk = _tile(K, tk, 128 if dims != "tn" else 16)
    gi, gj, gk = M // tm, N // tn, K // tk
    if red_block:
        grid = (gi, gj, nb, gk)
        unpack = lambda i, j, bb, k: (bb, i, j, k)
        red_axes, sem = (2, 3), ("parallel", "parallel", "arbitrary", "arbitrary")
    else:
        grid = (nb, gi, gj, gk)
        unpack = lambda bb, i, j, k: (bb, i, j, k)
        red_axes, sem = (3,), ("parallel", "parallel", "parallel", "arbitrary")

    def picker(f):
        return lambda *g: f(*unpack(*g))

    if dims == "tn":
        a_spec = _operand_spec(a_lay, tk, tm, M, picker(lambda bb, i, j, k: (bb, k, i)))
    else:
        a_spec = _operand_spec(a_lay, tm, tk, K, picker(lambda bb, i, j, k: (bb, i, k)))
    if dims == "nt":
        b_spec = _operand_spec(b_lay, tn, tk, K, picker(lambda bb, i, j, k: (bb, j, k)))
    else:
        b_spec = _operand_spec(b_lay, tk, tn, N, picker(lambda bb, i, j, k: (bb, k, j)))
    o_spec = _operand_spec(o_lay, tm, tn, N, picker(lambda bb, i, j, k: (bb, i, j)))
    if o_lay == "plain":
        out_shape = jax.ShapeDtypeStruct((M, N), out_dtype)
    elif o_lay == "blk":
        out_shape = jax.ShapeDtypeStruct((nb, M, N), out_dtype)
    else:
        out_shape = jax.ShapeDtypeStruct((M, nb * N), out_dtype)
    n_red = [grid[ax] for ax in red_axes]
    has_res = res is not None

    def body(*refs):
        if has_res:
            a_ref, b_ref, r_ref, o_ref, acc = refs
        else:
            a_ref, b_ref, o_ref, acc = refs
        first = functools.reduce(jnp.logical_and, [pl.program_id(ax) == 0 for ax in red_axes])
        last = functools.reduce(jnp.logical_and, [pl.program_id(ax) == n - 1 for ax, n in zip(red_axes, n_red)])

        @pl.when(first)
        def _():
            acc[...] = jnp.zeros_like(acc)

        acc[...] += lax.dot_general(a_ref[...], b_ref[...], _DOT_DIMS[dims], preferred_element_type=F32)

        @pl.when(last)
        def _():
            r = acc[...]
            if has_res:
                r = r + r_ref[...]
            o_ref[...] = r.astype(o_ref.dtype)

    in_specs = [a_spec, b_spec]
    operands = [a, b]
    if has_res:
        in_specs.append(_operand_spec("plain", tm, tn, N, picker(lambda bb, i, j, k: (bb, i, j))))
        operands.append(res)
    return _call(body, name, grid, in_specs, o_spec, out_shape, operands,
                 scratch=[pltpu.VMEM((tm, tn), F32)], sem=sem)


def _cast_bf16(name, w):
    R, C = w.shape
    tr = _tile(R, max(8, (1 << 20) // C), 16)

    def body(w_ref, o_ref):
        o_ref[...] = w_ref[...].astype(BF16)

    return _call(body, name, (R // tr,), [pl.BlockSpec((tr, C), lambda i: (i, 0))],
                 pl.BlockSpec((tr, C), lambda i: (i, 0)), jax.ShapeDtypeStruct((R, C), BF16), [w],
                 sem=("parallel",))


def _rms_fwd(name, x, g):
    T, D = x.shape
    tm = _tile(T, 128, 16)

    def body(x_ref, g_ref, o_ref):
        xv = x_ref[...]
        r = lax.rsqrt(jnp.mean(xv * xv, axis=-1, keepdims=True) + EPS)
        o_ref[...] = (xv * r * g_ref[...]).astype(BF16)

    return _call(body, name, (T // tm,),
                 [pl.BlockSpec((tm, D), lambda i: (i, 0)), pl.BlockSpec((1, D), lambda i: (0, 0))],
                 pl.BlockSpec((tm, D), lambda i: (i, 0)), jax.ShapeDtypeStruct((T, D), BF16), [x, g],
                 sem=("parallel",))


def _rms_bwd(name, dxn, x, g, dh=None):
    T, D = x.shape
    tm = _tile(T, 128, 16)
    with_dx = dh is not None

    def body(*refs):
        if with_dx:
            dxn_ref, x_ref, g_ref, dh_ref, o_ref, ob_ref, dg_ref = refs
        else:
            dxn_ref, x_ref, g_ref, dg_ref = refs
        xv = x_ref[...]
        r = lax.rsqrt(jnp.mean(xv * xv, axis=-1, keepdims=True) + EPS)
        xh = xv * r
        dy = dxn_ref[...]

        @pl.when(pl.program_id(0) == 0)
        def _():
            dg_ref[...] = jnp.zeros_like(dg_ref)

        dg_ref[...] += jnp.sum(dy * xh, axis=0, keepdims=True)
        if with_dx:
            dyg = dy * g_ref[...]
            tot = dh_ref[...] + r * (dyg - xh * jnp.mean(dyg * xh, axis=-1, keepdims=True))
            o_ref[...] = tot
            ob_ref[...] = tot.astype(BF16)

    row = pl.BlockSpec((tm, D), lambda i: (i, 0))
    vec = pl.BlockSpec((1, D), lambda i: (0, 0))
    if with_dx:
        return _call(body, name, (T // tm,), [row, row, vec, row], (row, row, vec),
                     (jax.ShapeDtypeStruct((T, D), F32), jax.ShapeDtypeStruct((T, D), BF16),
                      jax.ShapeDtypeStruct((1, D), F32)), [dxn, x, g, dh])
    return _call(body, name, (T // tm,), [row, row, vec], vec, jax.ShapeDtypeStruct((1, D), F32), [dxn, x, g])


def _loss_head(h, target, g):
    T, D = h.shape
    tm = _tile(T, 128, 16)

    def body(h_ref, t_ref, g_ref, o_ref, ob_ref, loss_ref, dg_ref):
        xv = h_ref[...]
        gv = g_ref[...]
        r = lax.rsqrt(jnp.mean(xv * xv, axis=-1, keepdims=True) + EPS)
        xh = xv * r
        e = xh * gv - t_ref[...]

        @pl.when(pl.program_id(0) == 0)
        def _():
            dg_ref[...] = jnp.zeros_like(dg_ref)
            loss_ref[...] = jnp.zeros_like(loss_ref)

        loss_ref[...] += 0.5 * jnp.sum(jnp.mean(e * e, axis=-1, keepdims=True), axis=0, keepdims=True)
        dy = e * (1.0 / D)
        dg_ref[...] += jnp.sum(dy * xh, axis=0, keepdims=True)
        dyg = dy * gv
        dx = r * (dyg - xh * jnp.mean(dyg * xh, axis=-1, keepdims=True))
        o_ref[...] = dx
        ob_ref[...] = dx.astype(BF16)

    row = pl.BlockSpec((tm, D), lambda i: (i, 0))
    vec = pl.BlockSpec((1, D), lambda i: (0, 0))
    return _call(body, "loss_head", (T // tm,), [row, row, vec],
                 (row, row, pl.BlockSpec((1, 128), lambda i: (0, 0)), vec),
                 (jax.ShapeDtypeStruct((T, D), F32), jax.ShapeDtypeStruct((T, D), BF16),
                  jax.ShapeDtypeStruct((1, 128), F32), jax.ShapeDtypeStruct((1, D), F32)), [h, target, g])


def _conv_fwd(buf, w_ref, K, tm):
    y = buf[pl.ds(HALO - (K - 1), tm), :] * w_ref[pl.ds(0, 1), :]
    for k in range(1, K):
        y = y + buf[pl.ds(HALO - (K - 1) + k, tm), :] * w_ref[pl.ds(k, 1), :]
    return y


def _conv_bwd_input(buf, w_ref, K, tm):
    dx = buf[pl.ds(K - 1, tm), :] * w_ref[pl.ds(0, 1), :]
    for k in range(1, K):
        dx = dx + buf[pl.ds(K - 1 - k, tm), :] * w_ref[pl.ds(k, 1), :]
    return dx


def _conv_bwd_weight(dw_ref, dy, buf, K, tm):
    for k in range(K):
        dw_ref[pl.ds(k, 1), :] += jnp.sum(dy * buf[pl.ds(HALO - (K - 1) + k, tm), :], axis=0, keepdims=True)


def _sigmoid(z):
    return 1.0 / (1.0 + jnp.exp(-z))


def _silu_grad(z, sig):
    return sig * (1.0 + z * (1.0 - sig))


def _group_norm(u1):
    out = []
    for gi in range(u1.shape[1] // GROUP_DIM):
        xg = u1[:, gi * GROUP_DIM:(gi + 1) * GROUP_DIM]
        xc = xg - jnp.mean(xg, axis=-1, keepdims=True)
        rstd = lax.rsqrt(jnp.mean(xc * xc, axis=-1, keepdims=True) + EPS)
        out.append((xc * rstd, rstd))
    return out


def _mixer_tiles(T, CW):
    tm = _tile(T, 512, HALO)
    tc = _tile(CW, 256, GROUP_DIM)
    return tm, tc, tm // HALO, CW // tc


def _mixer_fwd(proj, caw, cab, lng, lnb, cbw, T, CW):
    KA, KB = caw.shape[0], cbw.shape[0]
    tm, tc, hb, nc = _mixer_tiles(T, CW)

    def sec(s):
        return pl.BlockSpec((tm, tc), lambda i, c: (i, s * nc + c))

    def sec_prev(s):
        return pl.BlockSpec((HALO, tc), lambda i, c: (jnp.maximum(i * hb - 1, 0), s * nc + c))

    def chan(rows):
        return pl.BlockSpec((rows, tc), lambda i, c: (0, c))

    def body(av, ag, bg, cg, bh, avh, agh, cgh, bhh, caw_ref, cab_ref, lng_ref, lnb_ref, cbw_ref,
             mix_ref, u1_ref, bufa, bufb):
        first = pl.program_id(0) == 0
        bufa[pl.ds(HALO, tm), :] = av[...] * _sigmoid(ag[...])
        bufa[pl.ds(0, HALO), :] = jnp.where(first, 0.0, avh[...] * _sigmoid(agh[...]))
        u1 = _conv_fwd(bufa, caw_ref, KA, tm) + cab_ref[...]
        u1_ref[...] = u1
        for gi, (y, _) in enumerate(_group_norm(u1)):
            sl = slice(gi * GROUP_DIM, (gi + 1) * GROUP_DIM)
            z = y * lng_ref[:, sl] + lnb_ref[:, sl]
            mix_ref[0, :, sl] = (z * _sigmoid(z)).astype(BF16)
        bufb[pl.ds(HALO, tm), :] = cg[...] * bh[...]
        bufb[pl.ds(0, HALO), :] = jnp.where(first, 0.0, cgh[...] * bhh[...])
        mix_ref[1, :, :] = (bg[...] * _conv_fwd(bufb, cbw_ref, KB, tm)).astype(BF16)

    in_specs = [sec(0), sec(1), sec(2), sec(3), sec(4), sec_prev(0), sec_prev(1), sec_prev(3), sec_prev(4),
                chan(KA), chan(1), chan(1), chan(1), chan(KB)]
    operands = [proj] * 9 + [caw, cab, lng, lnb, cbw]
    return _call(body, "mixer_fwd", (T // tm, nc), in_specs,
                 (pl.BlockSpec((2, tm, tc), lambda i, c: (0, i, c)), pl.BlockSpec((tm, tc), lambda i, c: (i, c))),
                 (jax.ShapeDtypeStruct((2, T, CW), BF16), jax.ShapeDtypeStruct((T, CW), F32)), operands,
                 scratch=[pltpu.VMEM((HALO + tm, tc), F32), pltpu.VMEM((HALO + tm, tc), F32)],
                 sem=("parallel", "parallel"))


def _mixer_bwd1(dmix, proj, u1, caw, lng, lnb, cbw, T, CW):
    KA, KB = caw.shape[0], cbw.shape[0]
    tm, tc, hb, nc = _mixer_tiles(T, CW)

    def sec(s):
        return pl.BlockSpec((tm, tc), lambda c, i: (i, s * nc + c))

    def sec_prev(s):
        return pl.BlockSpec((HALO, tc), lambda c, i: (jnp.maximum(i * hb - 1, 0), s * nc + c))

    def chan(rows):
        return pl.BlockSpec((rows, tc), lambda c, i: (0, c))

    tile = pl.BlockSpec((tm, tc), lambda c, i: (i, c))

    def body(du, dv, u1_ref, av, ag, bg, cg, bh, avh, agh, cgh, bhh, lng_ref, lnb_ref, cbw_ref,
             du1_ref, dcv_ref, dbg_ref, dcaw_ref, dcab_ref, dlng_ref, dlnb_ref, dcbw_ref, bufa, bufb):
        first = pl.program_id(1) == 0

        @pl.when(first)
        def _():
            for r in (dcaw_ref, dcab_ref, dlng_ref, dlnb_ref, dcbw_ref):
                r[...] = jnp.zeros_like(r)

        duv = du[...]
        for gi, (y, rstd) in enumerate(_group_norm(u1_ref[...])):
            sl = slice(gi * GROUP_DIM, (gi + 1) * GROUP_DIM)
            gamma = lng_ref[:, sl]
            z = y * gamma + lnb_ref[:, sl]
            dz = duv[:, sl] * _silu_grad(z, _sigmoid(z))
            dlng_ref[:, sl] += jnp.sum(dz * y, axis=0, keepdims=True)
            dlnb_ref[:, sl] += jnp.sum(dz, axis=0, keepdims=True)
            dy = dz * gamma
            du1_ref[:, sl] = rstd * (dy - jnp.mean(dy, axis=-1, keepdims=True)
                                     - y * jnp.mean(dy * y, axis=-1, keepdims=True))
        du1 = du1_ref[...]
        dcab_ref[...] += jnp.sum(du1, axis=0, keepdims=True)
        bufa[pl.ds(HALO, tm), :] = av[...] * _sigmoid(ag[...])
        bufa[pl.ds(0, HALO), :] = jnp.where(first, 0.0, avh[...] * _sigmoid(agh[...]))
        _conv_bwd_weight(dcaw_ref, du1, bufa, KA, tm)

        bufb[pl.ds(HALO, tm), :] = cg[...] * bh[...]
        bufb[pl.ds(0, HALO), :] = jnp.where(first, 0.0, cgh[...] * bhh[...])
        dvv = dv[...]
        dbg_ref[...] = (dvv * _conv_fwd(bufb, cbw_ref, KB, tm)).astype(BF16)
        dcv = dvv * bg[...]
        dcv_ref[...] = dcv
        _conv_bwd_weight(dcbw_ref, dcv, bufb, KB, tm)

    in_specs = [sec(0), sec(1), tile, sec(0), sec(1), sec(2), sec(3), sec(4),
                sec_prev(0), sec_prev(1), sec_prev(3), sec_prev(4), chan(1), chan(1), chan(KB)]
    operands = [dmix, dmix, u1] + [proj] * 9 + [lng, lnb, cbw]
    return _call(body, "mixer_bwd1", (nc, T // tm), in_specs,
                 (tile, tile, tile, chan(KA), chan(1), chan(1), chan(1), chan(KB)),
                 (jax.ShapeDtypeStruct((T, CW), F32), jax.ShapeDtypeStruct((T, CW), F32),
                  jax.ShapeDtypeStruct((T, CW), BF16), jax.ShapeDtypeStruct((KA, CW), F32),
                  jax.ShapeDtypeStruct((1, CW), F32), jax.ShapeDtypeStruct((1, CW), F32),
                  jax.ShapeDtypeStruct((1, CW), F32), jax.ShapeDtypeStruct((KB, CW), F32)), operands,
                 scratch=[pltpu.VMEM((HALO + tm, tc), F32), pltpu.VMEM((HALO + tm, tc), F32)],
                 sem=("parallel", "arbitrary"))


def _mixer_bwd2(du1, dcv, proj, caw, cbw, T, CW):
    KA, KB = caw.shape[0], cbw.shape[0]
    tm, tc, hb, nc = _mixer_tiles(T, CW)
    n_i = T // tm

    def sec(s):
        return pl.BlockSpec((tm, tc), lambda i, c: (i, s * nc + c))

    def chan(rows):
        return pl.BlockSpec((rows, tc), lambda i, c: (0, c))

    tile = pl.BlockSpec((tm, tc), lambda i, c: (i, c))
    nxt = pl.BlockSpec((HALO, tc), lambda i, c: (jnp.minimum((i + 1) * hb, n_i * hb - 1), c))

    def body(du1_ref, du1n, dcv_ref, dcvn, av, ag, cg, bh, caw_ref, cbw_ref, dav, dag, dcg, dbh, bufa, bufb):
        last = pl.program_id(0) == n_i - 1
        bufa[pl.ds(0, tm), :] = du1_ref[...]
        bufa[pl.ds(tm, HALO), :] = jnp.where(last, 0.0, du1n[...])
        du0 = _conv_bwd_input(bufa, caw_ref, KA, tm)
        sig = _sigmoid(ag[...])
        dav[...] = (du0 * sig).astype(BF16)
        dag[...] = (du0 * av[...] * (sig * (1.0 - sig))).astype(BF16)
        bufb[pl.ds(0, tm), :] = dcv_ref[...]
        bufb[pl.ds(tm, HALO), :] = jnp.where(last, 0.0, dcvn[...])
        dch = _conv_bwd_input(bufb, cbw_ref, KB, tm)
        dcg[...] = (dch * bh[...]).astype(BF16)
        dbh[...] = (dch * cg[...]).astype(BF16)

    in_specs = [tile, nxt, tile, nxt, sec(0), sec(1), sec(3), sec(4), chan(KA), chan(KB)]
    operands = [du1, du1, dcv, dcv, proj, proj, proj, proj, caw, cbw]
    out = jax.ShapeDtypeStruct((T, CW), BF16)
    return _call(body, "mixer_bwd2", (n_i, nc), in_specs, (tile, tile, tile, tile), (out, out, out, out),
                 operands, scratch=[pltpu.VMEM((HALO + tm, tc), F32), pltpu.VMEM((HALO + tm, tc), F32)],
                 sem=("parallel", "parallel"))


def _ffn_tiles(T):
    tm = _tile(T, 256, HALO)
    return tm, tm // HALO, T // tm


def _ffn_act_fwd(gpre, up, cfw):
    nb, T, F = gpre.shape
    KF = cfw.shape[1]
    tm, hb, n_i = _ffn_tiles(T)
    tile = pl.BlockSpec((None, tm, F), lambda b, i: (b, i, 0))
    prev = pl.BlockSpec((None, HALO, F), lambda b, i: (b, jnp.maximum(i * hb - 1, 0), 0))
    wspec = pl.BlockSpec((None, KF, F), lambda b, i: (b, 0, 0))

    def body(g_ref, gh_ref, up_ref, w_ref, f_ref, buf):
        buf[pl.ds(HALO, tm), :] = g_ref[...]
        buf[pl.ds(0, HALO), :] = jnp.where(pl.program_id(1) == 0, 0.0, gh_ref[...])
        g = _conv_fwd(buf, w_ref, KF, tm)
        f_ref[...] = (g * _sigmoid(g) * up_ref[...]).astype(BF16)

    return _call(body, "ffn_act_fwd", (nb, n_i), [tile, prev, tile, wspec], tile,
                 jax.ShapeDtypeStruct((nb, T, F), BF16), [gpre, gpre, up, cfw],
                 scratch=[pltpu.VMEM((HALO + tm, F), F32)], sem=("parallel", "parallel"))


def _ffn_act_bwd1(df, gpre, up, cfw):
    nb, T, F = gpre.shape
    KF = cfw.shape[1]
    tm, hb, n_i = _ffn_tiles(T)
    tile = pl.BlockSpec((None, tm, F), lambda b, i: (b, i, 0))
    prev = pl.BlockSpec((None, HALO, F), lambda b, i: (b, jnp.maximum(i * hb - 1, 0), 0))
    wspec = pl.BlockSpec((None, KF, F), lambda b, i: (b, 0, 0))

    def body(df_ref, g_ref, gh_ref, up_ref, w_ref, dg_ref, dup_ref, dw_ref, buf):
        first = pl.program_id(1) == 0

        @pl.when(first)
        def _():
            dw_ref[...] = jnp.zeros_like(dw_ref)

        buf[pl.ds(HALO, tm), :] = g_ref[...]
        buf[pl.ds(0, HALO), :] = jnp.where(first, 0.0, gh_ref[...])
        g = _conv_fwd(buf, w_ref, KF, tm)
        sig = _sigmoid(g)
        dfv = df_ref[...]
        dup_ref[...] = (dfv * (g * sig)).astype(BF16)
        dg = dfv * up_ref[...] * _silu_grad(g, sig)
        dg_ref[...] = dg
        _conv_bwd_weight(dw_ref, dg, buf, KF, tm)

    return _call(body, "ffn_act_bwd1", (nb, n_i), [tile, tile, prev, tile, wspec], (tile, tile, wspec),
                 (jax.ShapeDtypeStruct((nb, T, F), F32), jax.ShapeDtypeStruct((nb, T, F), BF16),
                  jax.ShapeDtypeStruct((nb, KF, F), F32)), [df, gpre, gpre, up, cfw],
                 scratch=[pltpu.VMEM((HALO + tm, F), F32)], sem=("parallel", "arbitrary"))


def _ffn_act_bwd2(dg, cfw):
    nb, T, F = dg.shape
    KF = cfw.shape[1]
    tm, hb, n_i = _ffn_tiles(T)
    tile = pl.BlockSpec((None, tm, F), lambda b, i: (b, i, 0))
    nxt = pl.BlockSpec((None, HALO, F), lambda b, i: (b, jnp.minimum((i + 1) * hb, n_i * hb - 1), 0))
    wspec = pl.BlockSpec((None, KF, F), lambda b, i: (b, 0, 0))

    def body(dg_ref, dgn_ref, w_ref, o_ref, buf):
        buf[pl.ds(0, tm), :] = dg_ref[...]
        buf[pl.ds(tm, HALO), :] = jnp.where(pl.program_id(1) == n_i - 1, 0.0, dgn_ref[...])
        o_ref[...] = _conv_bwd_input(buf, w_ref, KF, tm).astype(BF16)

    return _call(body, "ffn_act_bwd2", (nb, n_i), [tile, nxt, wspec], tile,
                 jax.ShapeDtypeStruct((nb, T, F), BF16), [dg, dg, cfw],
                 scratch=[pltpu.VMEM((HALO + tm, F), F32)], sem=("parallel", "parallel"))


def _softmax_rows(s):
    e = jnp.exp(s - jnp.max(s, axis=-1, keepdims=True))
    return e / jnp.sum(e, axis=-1, keepdims=True)


def _attn_fwd(q, k, v):
    T, D = q.shape
    Mm = k.shape[0]
    hd = D // N_XATTN_HEADS
    scale = hd ** -0.5
    tm = _tile(T, 256, 16)

    def body(q_ref, k_ref, v_ref, o_ref):
        for h in range(N_XATTN_HEADS):
            sl = slice(h * hd, (h + 1) * hd)
            s = lax.dot_general(q_ref[:, sl], k_ref[:, sl], _DOT_DIMS["nt"], preferred_element_type=F32) * scale
            p = _softmax_rows(s).astype(BF16)
            o_ref[:, sl] = jnp.dot(p, v_ref[:, sl], preferred_element_type=F32).astype(BF16)

    row = pl.BlockSpec((tm, D), lambda i: (i, 0))
    full = pl.BlockSpec((Mm, D), lambda i: (0, 0))
    return _call(body, "attn_fwd", (T // tm,), [row, full, full], row, jax.ShapeDtypeStruct((T, D), BF16),
                 [q, k, v], sem=("parallel",))


def _attn_bwd(q, k, v, do):
    T, D = q.shape
    Mm = k.shape[0]
    hd = D // N_XATTN_HEADS
    scale = hd ** -0.5
    tm = _tile(T, 256, 16)
    n_i = T // tm

    def body(q_ref, do_ref, k_ref, v_ref, dq_ref, dk_ref, dv_ref, dk_acc, dv_acc):
        @pl.when(pl.program_id(0) == 0)
        def _():
            dk_acc[...] = jnp.zeros_like(dk_acc)
            dv_acc[...] = jnp.zeros_like(dv_acc)

        for h in range(N_XATTN_HEADS):
            sl = slice(h * hd, (h + 1) * hd)
            qh, kh, doh = q_ref[:, sl], k_ref[:, sl], do_ref[:, sl]
            s = lax.dot_general(qh, kh, _DOT_DIMS["nt"], preferred_element_type=F32) * scale
            p = _softmax_rows(s)
            dv_acc[:, sl] += lax.dot_general(p.astype(BF16), doh, _DOT_DIMS["tn"], preferred_element_type=F32)
            dp = lax.dot_general(doh, v_ref[:, sl], _DOT_DIMS["nt"], preferred_element_type=F32)
            ds = (p * (dp - jnp.sum(dp * p, axis=-1, keepdims=True)) * scale).astype(BF16)
            dq_ref[:, sl] = jnp.dot(ds, kh, preferred_element_type=F32).astype(BF16)
            dk_acc[:, sl] += lax.dot_general(ds, qh, _DOT_DIMS["tn"], preferred_element_type=F32)

        @pl.when(pl.program_id(0) == n_i - 1)
        def _():
            dk_ref[...] = dk_acc[...].astype(BF16)
            dv_ref[...] = dv_acc[...].astype(BF16)

    row = pl.BlockSpec((tm, D), lambda i: (i, 0))
    full = pl.BlockSpec((Mm, D), lambda i: (0, 0))
    return _call(body, "attn_bwd", (n_i,), [row, row, full, full], (row, full, full),
                 (jax.ShapeDtypeStruct((T, D), BF16), jax.ShapeDtypeStruct((Mm, D), BF16),
                  jax.ShapeDtypeStruct((Mm, D), BF16)), [q, do, k, v],
                 scratch=[pltpu.VMEM((Mm, D), F32), pltpu.VMEM((Mm, D), F32)])


def _position():
    x, y, c = lax.axis_index("x"), lax.axis_index("y"), lax.axis_index("c")
    return x, y, c


def _peer(pos, k):
    x, y, c = pos
    return (1 - x if k & 4 else x, 1 - y if k & 2 else y, 1 - c if k & 1 else c)


def _index(pos):
    x, y, c = pos
    return 4 * x + 2 * y + c


def _all_gather(name, shard):
    def body(x_ref, out_ref, send_sems, recv_sems, local_sem):
        me = _position()
        sibling = _peer(me, 1)
        chips = [_peer(me, 4), _peer(me, 2), _peer(me, 6)]

        def copy(k, block, to, src=None):
            dst = out_ref.at[_index(block)]
            return pltpu.make_async_remote_copy(
                src_ref=dst if src is None else src, dst_ref=dst, send_sem=send_sems.at[k],
                recv_sem=recv_sems.at[k], device_id=to, device_id_type=MESH)

        mine = pltpu.make_async_copy(x_ref, out_ref.at[_index(me)], local_sem)
        mine.start()
        first = [copy(0, me, sibling, src=x_ref)]
        first += [copy(1 + j, me, chip, src=x_ref) for j, chip in enumerate(chips)]
        for cp in first:
            cp.start()
        passed = [copy(4 + j, chip, sibling) for j, chip in enumerate(chips)]
        for j, chip in enumerate(chips):
            copy(1 + j, chip, me).wait_recv()
            passed[j].start()
        copy(0, sibling, me).wait_recv()
        for j, chip in enumerate(chips):
            copy(4 + j, _peer(chip, 1), me).wait_recv()
        for cp in first + passed:
            cp.wait_send()
        mine.wait()

    return pl.pallas_call(
        body, name=name, out_shape=jax.ShapeDtypeStruct((N_DEV,) + shard.shape, shard.dtype),
        in_specs=[HBM_SPEC], out_specs=HBM_SPEC,
        scratch_shapes=[pltpu.SemaphoreType.DMA((7,)), pltpu.SemaphoreType.DMA((7,)), pltpu.SemaphoreType.DMA],
    )(shard)


def _exchange_blocks(name, parts):
    def body(p_ref, out_ref, send_sems, recv_sems, local_sem):
        me = _position()
        mine = pltpu.make_async_copy(p_ref.at[_index(me)], out_ref.at[_index(me)], local_sem)
        mine.start()
        sends = []
        for k in range(1, N_DEV):
            peer = _peer(me, k)
            sends.append(pltpu.make_async_remote_copy(
                src_ref=p_ref.at[_index(peer)], dst_ref=out_ref.at[_index(me)], send_sem=send_sems.at[k - 1],
                recv_sem=recv_sems.at[k - 1], device_id=peer, device_id_type=MESH))
        for cp in sends:
            cp.start()
        for k in range(1, N_DEV):
            peer = _peer(me, k)
            pltpu.make_async_remote_copy(
                src_ref=p_ref.at[_index(me)], dst_ref=out_ref.at[_index(peer)], send_sem=send_sems.at[k - 1],
                recv_sem=recv_sems.at[k - 1], device_id=peer, device_id_type=MESH).wait_recv()
        for cp in sends:
            cp.wait_send()
        mine.wait()

    return pl.pallas_call(
        body, name=name, out_shape=jax.ShapeDtypeStruct(parts.shape, parts.dtype),
        in_specs=[HBM_SPEC], out_specs=HBM_SPEC,
        scratch_shapes=[pltpu.SemaphoreType.DMA((7,)), pltpu.SemaphoreType.DMA((7,)), pltpu.SemaphoreType.DMA],
    )(parts)


def _all_reduce_rows(name, v):
    R, C = v.shape

    def body(v_ref, out_ref, gath, send_sems, recv_sems):
        me = _position()
        gath[_index(me)] = v_ref[...]
        sends = []
        for k in range(1, N_DEV):
            peer = _peer(me, k)
            sends.append(pltpu.make_async_remote_copy(
                src_ref=v_ref, dst_ref=gath.at[_index(me)], send_sem=send_sems.at[k - 1],
                recv_sem=recv_sems.at[k - 1], device_id=peer, device_id_type=MESH))
        for cp in sends:
            cp.start()
        for k in range(1, N_DEV):
            peer = _peer(me, k)
            pltpu.make_async_remote_copy(
                src_ref=v_ref, dst_ref=gath.at[_index(peer)], send_sem=send_sems.at[k - 1],
                recv_sem=recv_sems.at[k - 1], device_id=peer, device_id_type=MESH).wait_recv()
        for cp in sends:
            cp.wait_send()
        tot = gath[0]
        for s in range(1, N_DEV):
            tot = tot + gath[s]
        out_ref[...] = tot

    return pl.pallas_call(
        body, name=name, out_shape=jax.ShapeDtypeStruct((R, C), F32),
        in_specs=[VMEM_SPEC], out_specs=VMEM_SPEC,
        scratch_shapes=[pltpu.VMEM((N_DEV, R, C), F32), pltpu.SemaphoreType.DMA((7,)),
                        pltpu.SemaphoreType.DMA((7,))],
    )(v)


def _adamw_math(g, w, m, v):
    m = ADAM_B1 * m + (1.0 - ADAM_B1) * g
    v = ADAM_B2 * v + (1.0 - ADAM_B2) * (g * g)
    m_hat = m / (1.0 - ADAM_B1 ** ADAM_STEP)
    v_hat = v / (1.0 - ADAM_B2 ** ADAM_STEP)
    delta = -ADAM_LR * (m_hat / (jnp.sqrt(v_hat) + ADAM_EPS) + ADAM_WD * w)
    return delta, m, v


def _adamw(name, parts, w, m, v):
    n, R, C = parts.shape
    tr = _tile(R, max(8, (1 << 18) // C), 16)

    def body(p_ref, w_ref, m_ref, v_ref, g_ref, d_ref, nm_ref, nv_ref):
        g = p_ref[0].astype(F32)
        for s in range(1, n):
            g = g + p_ref[s].astype(F32)
        g_ref[...] = g
        d_ref[...], nm_ref[...], nv_ref[...] = _adamw_math(g, w_ref[...], m_ref[...], v_ref[...])

    blk = pl.BlockSpec((tr, C), lambda i: (i, 0))
    out = jax.ShapeDtypeStruct((R, C), F32)
    return _call(body, name, (R // tr,), [pl.BlockSpec((n, tr, C), lambda i: (0, i, 0)), blk, blk, blk],
                 (blk, blk, blk, blk), (out, out, out, out), [parts, w, m, v], sem=("parallel",))


def kernel(x, mem, g_mix, w_in, conv_a_w, conv_a_b, ln_a_g, ln_a_b, conv_b_w, w_out, g_xattn, g_mem, w_q, w_k, w_v, w_o, g_ffn, w_gate, w_up, conv_f_w, w_down, g_final, loss_target, m_g_mix, m_w_in, m_conv_a_w, m_conv_a_b, m_ln_a_g, m_ln_a_b, m_conv_b_w, m_w_out, m_g_xattn, m_g_mem, m_w_q, m_w_k, m_w_v, m_w_o, m_g_ffn, m_w_gate, m_w_up, m_conv_f_w, m_w_down, m_g_final, v_g_mix, v_w_in, v_conv_a_w, v_conv_a_b, v_ln_a_g, v_ln_a_b, v_conv_b_w, v_w_out, v_g_xattn, v_g_mem, v_w_q, v_w_k, v_w_v, v_w_o, v_g_ffn, v_w_gate, v_w_up, v_conv_f_w, v_w_down, v_g_final):
    T, D = x.shape[1], x.shape[2]
    Mm = mem.shape[1]
    CW = conv_a_b.shape[1]
    INB = w_in.shape[2]
    FB = w_gate.shape[2]
    KA, KB, KF = conv_a_w.shape[1], conv_b_w.shape[1], conv_f_w.shape[1]
    DB = D // N_DEV
    assert 5 * CW == N_DEV * INB and 2 * CW == D

    x2, mem2, tgt = x[0], mem[0], loss_target[0]
    g_mem2, g_final2 = g_mem.reshape(1, D), g_final.reshape(1, D)

    def gather_bf16(name, w):
        return _all_gather("ag_" + name, _cast_bf16("cast_" + name, w))

    Win = gather_bf16("w_in", w_in[0])
    Wout = gather_bf16("w_out", w_out[0]).reshape(D, D)
    Wq = gather_bf16("w_q", w_q[0]).reshape(D, D)
    Wk = gather_bf16("w_k", w_k[0]).reshape(D, D)
    Wv = gather_bf16("w_v", w_v[0]).reshape(D, D)
    Wo = gather_bf16("w_o", w_o[0]).reshape(D, D)
    Wgate = gather_bf16("w_gate", w_gate[0])
    Wup = gather_bf16("w_up", w_up[0])
    Wdown = gather_bf16("w_down", w_down[0])
    caw = jnp.transpose(_all_gather("ag_conv_a_w", conv_a_w[0]), (1, 0, 2)).reshape(KA, CW)
    cbw = jnp.transpose(_all_gather("ag_conv_b_w", conv_b_w[0]), (1, 0, 2)).reshape(KB, CW)
    cfw = _all_gather("ag_conv_f_w", conv_f_w[0])

    xn1 = _rms_fwd("rms_mix", x2, g_mix)
    proj = _matmul("mm_proj", "nn", xn1, Win, M=T, N=INB, K=D, nb=N_DEV, b_lay="blk", o_lay="col", tn=INB)
    mix, u1 = _mixer_fwd(proj, caw, conv_a_b, ln_a_g, ln_a_b, cbw, T, CW)
    h1 = _matmul("mm_h1", "nn", mix, Wout.reshape(2, CW, D), M=T, N=D, K=CW, nb=2, a_lay="blk", b_lay="blk",
                 red_block=True, res=x2)
    xn2 = _rms_fwd("rms_xattn", h1, g_xattn)
    q = _matmul("mm_q", "nn", xn2, Wq, M=T, N=D, K=D, out_dtype=BF16)
    memn = _rms_fwd("rms_mem", mem2, g_mem2)
    kk = _matmul("mm_k", "nn", memn, Wk, M=Mm, N=D, K=D, out_dtype=BF16)
    vv = _matmul("mm_v", "nn", memn, Wv, M=Mm, N=D, K=D, out_dtype=BF16)
    o = _attn_fwd(q, kk, vv)
    h2 = _matmul("mm_h2", "nn", o, Wo, M=T, N=D, K=D, res=h1)
    xn3 = _rms_fwd("rms_ffn", h2, g_ffn)
    gpre = _matmul("mm_gate", "nn", xn3, Wgate, M=T, N=FB, K=D, nb=N_DEV, b_lay="blk", o_lay="blk", tn=FB)
    up = _matmul("mm_up", "nn", xn3, Wup, M=T, N=FB, K=D, nb=N_DEV, b_lay="blk", o_lay="blk", tn=FB)
    f = _ffn_act_fwd(gpre, up, cfw)
    h3 = _matmul("mm_h3", "nn", f, Wdown, M=T, N=D, K=FB, nb=N_DEV, a_lay="blk", b_lay="blk", red_block=True,
                 res=h2, tk=FB)
    dh3, dh3b, loss_part, dg_final = _loss_head(h3, tgt, g_final2)

    df = _matmul("mm_df", "nt", dh3b, Wdown, M=T, N=FB, K=D, nb=N_DEV, b_lay="blk", o_lay="blk", tn=FB)
    dWdown = _matmul("mm_dw_down", "tn", f, dh3b, M=FB, N=D, K=T, nb=N_DEV, a_lay="blk", o_lay="blk",
                     out_dtype=BF16, tm=FB)
    dg, dup, dcfw = _ffn_act_bwd1(df, gpre, up, cfw)
    dgpre = _ffn_act_bwd2(dg, cfw)
    dWgate = _matmul("mm_dw_gate", "tn", xn3, dgpre, M=D, N=FB, K=T, nb=N_DEV, b_lay="blk", o_lay="blk",
                     out_dtype=BF16, tn=FB)
    dWup = _matmul("mm_dw_up", "tn", xn3, dup, M=D, N=FB, K=T, nb=N_DEV, b_lay="blk", o_lay="blk",
                   out_dtype=BF16, tn=FB)
    dxn3 = _matmul("mm_dxn3_gate", "nt", dgpre, Wgate, M=T, N=D, K=FB, nb=N_DEV, a_lay="blk", b_lay="blk",
                   red_block=True, tk=FB)
    dxn3 = _matmul("mm_dxn3_up", "nt", dup, Wup, M=T, N=D, K=FB, nb=N_DEV, a_lay="blk", b_lay="blk",
                   red_block=True, res=dxn3, tk=FB)
    dh2, dh2b, dg_ffn = _rms_bwd("rms_bwd_ffn", dxn3, h2, g_ffn, dh3)

    do = _matmul("mm_do", "nt", dh2b, Wo, M=T, N=D, K=D, out_dtype=BF16)
    dWo = _matmul("mm_dw_o", "tn", o, dh2b, M=D, N=D, K=T, out_dtype=BF16)
    dq, dk, dv = _attn_bwd(q, kk, vv, do)
    dWq = _matmul("mm_dw_q", "tn", xn2, dq, M=D, N=D, K=T, out_dtype=BF16)
    dxn2 = _matmul("mm_dxn2", "nt", dq, Wq, M=T, N=D, K=D)
    dh1, dh1b, dg_xattn = _rms_bwd("rms_bwd_xattn", dxn2, h1, g_xattn, dh2)
    dWk = _matmul("mm_dw_k", "tn", memn, dk, M=D, N=D, K=Mm, out_dtype=BF16)
    dWv = _matmul("mm_dw_v", "tn", memn, dv, M=D, N=D, K=Mm, out_dtype=BF16)
    dmemn = _matmul("mm_dmem_k", "nt", dk, Wk, M=Mm, N=D, K=D)
    dmemn = _matmul("mm_dmem_v", "nt", dv, Wv, M=Mm, N=D, K=D, res=dmemn)
    dg_mem = _rms_bwd("rms_bwd_mem", dmemn, mem2, g_mem2)

    dmix = _matmul("mm_dmix", "nt", dh1b, Wout, M=T, N=D, K=D)
    dWout = _matmul("mm_dw_out", "tn", mix, dh1b, M=CW, N=D, K=T, nb=2, a_lay="blk", o_lay="blk", out_dtype=BF16)
    du1, dcv, dbg, dcaw, dcab, dlng, dlnb, dcbw = _mixer_bwd1(dmix, proj, u1, caw, ln_a_g, ln_a_b, cbw, T, CW)
    dav, dag, dcg, dbh = _mixer_bwd2(du1, dcv, proj, caw, cbw, T, CW)
    dproj = jnp.concatenate([dav, dag, dbg, dcg, dbh], axis=1)
    dWin = _matmul("mm_dw_in", "tn", xn1, dproj, M=D, N=INB, K=T, nb=N_DEV, b_lay="col", o_lay="blk",
                   out_dtype=BF16, tn=INB)
    dxn1 = _matmul("mm_dxn1", "nt", dproj, Win, M=T, N=D, K=INB, nb=N_DEV, a_lay="col", b_lay="blk",
                   red_block=True, tk=INB)
    dx, _, dg_mix = _rms_bwd("rms_bwd_mix", dxn1, x2, g_mix, dh1)

    def update(name, parts, w, m, v):
        got = _exchange_blocks("rs_" + name, parts)
        return _adamw("adamw_" + name, got, w, m, v)

    def row_blocks(dw):
        return dw.reshape(N_DEV, DB, D)

    def conv_blocks(dw, K):
        return jnp.transpose(dw.reshape(K, N_DEV, CW // N_DEV), (1, 0, 2))

    res = {}
    res["w_in"] = update("w_in", dWin, w_in[0], m_w_in[0], v_w_in[0])
    res["conv_a_w"] = update("conv_a_w", conv_blocks(dcaw, KA), conv_a_w[0], m_conv_a_w[0], v_conv_a_w[0])
    res["conv_b_w"] = update("conv_b_w", conv_blocks(dcbw, KB), conv_b_w[0], m_conv_b_w[0], v_conv_b_w[0])
    res["w_out"] = update("w_out", row_blocks(dWout.reshape(D, D)), w_out[0], m_w_out[0], v_w_out[0])
    res["w_q"] = update("w_q", row_blocks(dWq), w_q[0], m_w_q[0], v_w_q[0])
    res["w_k"] = update("w_k", row_blocks(dWk), w_k[0], m_w_k[0], v_w_k[0])
    res["w_v"] = update("w_v", row_blocks(dWv), w_v[0], m_w_v[0], v_w_v[0])
    res["w_o"] = update("w_o", row_blocks(dWo), w_o[0], m_w_o[0], v_w_o[0])
    res["w_gate"] = update("w_gate", dWgate, w_gate[0], m_w_gate[0], v_w_gate[0])
    res["w_up"] = update("w_up", dWup, w_up[0], m_w_up[0], v_w_up[0])
    res["conv_f_w"] = update("conv_f_w", dcfw, conv_f_w[0], m_conv_f_w[0], v_conv_f_w[0])
    res["w_down"] = update("w_down", dWdown, w_down[0], m_w_down[0], v_w_down[0])

    def pair(a, b):
        return jnp.concatenate([a, b], axis=1)

    zeros_half = jnp.zeros((1, CW), F32)
    small_g = jnp.concatenate([
        dg_mix, pair(dcab, dlng), pair(dlnb, zeros_half), dg_xattn, dg_mem, dg_ffn, dg_final,
        jnp.broadcast_to(loss_part[:, :1], (1, D))], axis=0)
    small_sum = _all_reduce_rows("ar_small", small_g)
    loss = small_sum[7, 0]

    def pack(a_mix, a_cab, a_lng, a_lnb, a_xattn, a_mem, a_ffn, a_final):
        return jnp.concatenate([a_mix, pair(a_cab, a_lng), pair(a_lnb, zeros_half), a_xattn, a_mem.reshape(1, D),
                                a_ffn, a_final.reshape(1, D), jnp.zeros((1, D), F32)], axis=0)

    small = _adamw("adamw_small", small_sum[None],
                   pack(g_mix, conv_a_b, ln_a_g, ln_a_b, g_xattn, g_mem, g_ffn, g_final),
                   pack(m_g_mix, m_conv_a_b, m_ln_a_g, m_ln_a_b, m_g_xattn, m_g_mem, m_g_ffn, m_g_final),
                   pack(v_g_mix, v_conv_a_b, v_ln_a_g, v_ln_a_b, v_g_xattn, v_g_mem, v_g_ffn, v_g_final))

    def unpack(a):
        return {"g_mix": a[0:1], "conv_a_b": a[1:2, :CW], "ln_a_g": a[1:2, CW:], "ln_a_b": a[2:3, :CW],
                "g_xattn": a[3:4], "g_mem": a[4], "g_ffn": a[5:6], "g_final": a[6]}

    small = [unpack(a) for a in small]
    order = ["g_mix", "w_in", "conv_a_w", "conv_a_b", "ln_a_g", "ln_a_b", "conv_b_w", "w_out", "g_xattn", "g_mem",
             "w_q", "w_k", "w_v", "w_o", "g_ffn", "w_gate", "w_up", "conv_f_w", "w_down", "g_final"]
    outs = [loss, dx[None]]
    for kind in range(4):
        for n in order:
            outs.append(res[n][kind][None] if n in res else small[kind][n])
    return tuple(outs)
```

```python
import functools

import jax
import jax.numpy as jnp
from jax import lax
from jax.experimental import pallas as pl
from jax.experimental.pallas import tpu as pltpu
from jax.experimental.pallas import tpu_sc as plsc

F32 = jnp.float32
BF16 = jnp.bfloat16

N_DEV = 8
EPS = 1e-6
GROUP_DIM = 128
N_XATTN_HEADS = 4
ADAM_LR = 0.001
ADAM_B1 = 0.9
ADAM_B2 = 0.999
ADAM_EPS = 1e-08
ADAM_WD = 0.01
ADAM_STEP = 10

AG_ID, SIBLING_ID, CHIPS_ID = 1, 2, 3

HALO = 32
VMEM_V7X_BYTES = 64 * 1024 * 1024
VMEM_TEMP_ALLOWANCE = 12 * 1024 * 1024

VMEM_SPEC = pl.BlockSpec(memory_space=pltpu.VMEM)
MESH = pl.DeviceIdType.MESH


def _tile(n, pref, align):
    if n <= pref:
        return n
    t = (pref // align) * align
    while t >= align:
        if n % t == 0:
            return t
        t -= align
    return n


def _nbytes(shape, dtype):
    n = 1
    for d in shape:
        if d is not None:
            n *= d
    return n * jnp.dtype(dtype).itemsize


def _call(body, name, grid, in_specs, out_specs, out_shape, operands, scratch=(), sem=None, after=()):
    outs = out_shape if isinstance(out_shape, (tuple, list)) else (out_shape,)
    ospecs = out_specs if isinstance(out_specs, (tuple, list)) else (out_specs,)
    est = 0
    for spec, arr in list(zip(in_specs, operands)) + list(zip(ospecs, outs)):
        est += 2 * _nbytes(spec.block_shape, arr.dtype)
    for s in scratch:
        if hasattr(s, "shape") and hasattr(s, "dtype"):
            est += _nbytes(s.shape, s.dtype)
    limit = min(est + VMEM_TEMP_ALLOWANCE, VMEM_V7X_BYTES - 4 * 1024 * 1024)
    if sem is None:
        sem = ("arbitrary",) * len(grid)
    n_in, n_after = len(operands), len(after)

    def ordered_body(*refs):
        body(*refs[:n_in], *refs[n_in + n_after:])

    return pl.pallas_call(
        ordered_body if n_after else body, name=name, grid=grid,
        in_specs=list(in_specs) + [pl.BlockSpec(memory_space=pl.ANY)] * n_after,
        out_specs=out_specs, out_shape=out_shape, scratch_shapes=list(scratch),
        compiler_params=pltpu.CompilerParams(dimension_semantics=sem, vmem_limit_bytes=int(limit)),
    )(*operands, *after)


_DOT_DIMS = {"nn": (((1,), (0,)), ((), ())), "nt": (((1,), (1,)), ((), ())), "tn": (((0,), (0,)), ((), ()))}


def _operand_spec(layout, tr, tc, cols_per_block, pick):
    if layout == "plain":
        return pl.BlockSpec((tr, tc), lambda *g: pick(*g)[1:])
    if layout == "blk":
        return pl.BlockSpec((None, tr, tc), lambda *g: pick(*g))
    assert layout == "col"
    per = cols_per_block // tc
    return pl.BlockSpec((tr, tc), lambda *g: (pick(*g)[1], pick(*g)[0] * per + pick(*g)[2]))


def _matmul(name, dims, a, b, *, M, N, K, nb=1, a_lay="plain", b_lay="plain", o_lay="plain",
            red_block=False, out_dtype=F32, res=None, tm=1024, tn=1024, tk=512, after=()):
    tm = _tile(M, tm, 128 if dims == "tn" else 16)
    tn = _tile(N, tn, 128)
    tk = _tile(K, tk, 128 if dims != "tn" else 16)
    gi, gj, gk = M // tm, N // tn, K // tk
    if red_block:
        grid = (gi, gj, nb, gk)
        unpack = lambda i, j, bb, k: (bb, i, j, k)
        red_axes, sem = (2, 3), ("parallel", "parallel", "arbitrary", "arbitrary")
    else:
        grid = (nb, gi, gj, gk)
        unpack = lambda bb, i, j, k: (bb, i, j, k)
        red_axes, sem = (3,), ("parallel", "parallel", "parallel", "arbitrary")

    def picker(f):
        return lambda *g: f(*unpack(*g))

    if dims == "tn":
        a_spec = _operand_spec(a_lay, tk, tm, M, picker(lambda bb, i, j, k: (bb, k, i)))
    else:
        a_spec = _operand_spec(a_lay, tm, tk, K, picker(lambda bb, i, j, k: (bb, i, k)))
    if dims == "nt":
        b_spec = _operand_spec(b_lay, tn, tk, K, picker(lambda bb, i, j, k: (bb, j, k)))
    else:
        b_spec = _operand_spec(b_lay, tk, tn, N, picker(lambda bb, i, j, k: (bb, k, j)))
    o_spec = _operand_spec(o_lay, tm, tn, N, picker(lambda bb, i, j, k: (bb, i, j)))
    if o_lay == "plain":
        out_shape = jax.ShapeDtypeStruct((M, N), out_dtype)
    elif o_lay == "blk":
        out_shape = jax.ShapeDtypeStruct((nb, M, N), out_dtype)
    else:
        out_shape = jax.ShapeDtypeStruct((M, nb * N), out_dtype)
    n_red = [grid[ax] for ax in red_axes]
    has_res = res is not None

    def body(*refs):
        if has_res:
            a_ref, b_ref, r_ref, o_ref, acc = refs
        else:
            a_ref, b_ref, o_ref, acc = refs
        first = functools.reduce(jnp.logical_and, [pl.program_id(ax) == 0 for ax in red_axes])
        last = functools.reduce(jnp.logical_and, [pl.program_id(ax) == n - 1 for ax, n in zip(red_axes, n_red)])

        @pl.when(first)
        def _():
            acc[...] = jnp.zeros_like(acc)

        acc[...] += lax.dot_general(a_ref[...], b_ref[...], _DOT_DIMS[dims], preferred_element_type=F32)

        @pl.when(last)
        def _():
            r = acc[...]
            if has_res:
                r = r + r_ref[...]
            o_ref[...] = r.astype(o_ref.dtype)

    in_specs = [a_spec, b_spec]
    operands = [a, b]
    if has_res:
        in_specs.append(_operand_spec("plain", tm, tn, N, picker(lambda bb, i, j, k: (bb, i, j))))
        operands.append(res)
    return _call(body, name, grid, in_specs, o_spec, out_shape, operands,
                 scratch=[pltpu.VMEM((tm, tn), F32)], sem=sem, after=after)


def _cast_bf16(name, w):
    R, C = w.shape
    tr = _tile(R, max(8, (1 << 20) // C), 16)

    def body(w_ref, o_ref):
        o_ref[...] = w_ref[...].astype(BF16)

    return _call(body, name, (R // tr,), [pl.BlockSpec((tr, C), lambda i: (i, 0))],
                 pl.BlockSpec((tr, C), lambda i: (i, 0)), jax.ShapeDtypeStruct((R, C), BF16), [w],
                 sem=("parallel",))


def _rms_fwd(name, x, g, after=()):
    T, D = x.shape
    tm = _tile(T, 128, 16)

    def body(x_ref, g_ref, o_ref):
        xv = x_ref[...]
        r = lax.rsqrt(jnp.mean(xv * xv, axis=-1, keepdims=True) + EPS)
        o_ref[...] = (xv * r * g_ref[...]).astype(BF16)

    return _call(body, name, (T // tm,),
                 [pl.BlockSpec((tm, D), lambda i: (i, 0)), pl.BlockSpec((1, D), lambda i: (0, 0))],
                 pl.BlockSpec((tm, D), lambda i: (i, 0)), jax.ShapeDtypeStruct((T, D), BF16), [x, g],
                 sem=("parallel",), after=after)


def _rms_bwd(name, dxn, x, g, dh=None, after=()):
    T, D = x.shape
    tm = _tile(T, 128, 16)
    with_dx = dh is not None

    def body(*refs):
        if with_dx:
            dxn_ref, x_ref, g_ref, dh_ref, o_ref, ob_ref, dg_ref = refs
        else:
            dxn_ref, x_ref, g_ref, dg_ref = refs
        xv = x_ref[...]
        r = lax.rsqrt(jnp.mean(xv * xv, axis=-1, keepdims=True) + EPS)
        xh = xv * r
        dy = dxn_ref[...]

        @pl.when(pl.program_id(0) == 0)
        def _():
            dg_ref[...] = jnp.zeros_like(dg_ref)

        dg_ref[...] += jnp.sum(dy * xh, axis=0, keepdims=True)
        if with_dx:
            dyg = dy * g_ref[...]
            tot = dh_ref[...] + r * (dyg - xh * jnp.mean(dyg * xh, axis=-1, keepdims=True))
            o_ref[...] = tot
            ob_ref[...] = tot.astype(BF16)

    row = pl.BlockSpec((tm, D), lambda i: (i, 0))
    vec = pl.BlockSpec((1, D), lambda i: (0, 0))
    if with_dx:
        return _call(body, name, (T // tm,), [row, row, vec, row], (row, row, vec),
                     (jax.ShapeDtypeStruct((T, D), F32), jax.ShapeDtypeStruct((T, D), BF16),
                      jax.ShapeDtypeStruct((1, D), F32)), [dxn, x, g, dh], after=after)
    return _call(body, name, (T // tm,), [row, row, vec], vec, jax.ShapeDtypeStruct((1, D), F32), [dxn, x, g])


def _loss_head(h, target, g):
    T, D = h.shape
    tm = _tile(T, 128, 16)

    def body(h_ref, t_ref, g_ref, o_ref, ob_ref, loss_ref, dg_ref):
        xv = h_ref[...]
        gv = g_ref[...]
        r = lax.rsqrt(jnp.mean(xv * xv, axis=-1, keepdims=True) + EPS)
        xh = xv * r
        e = xh * gv - t_ref[...]

        @pl.when(pl.program_id(0) == 0)
        def _():
            dg_ref[...] = jnp.zeros_like(dg_ref)
            loss_ref[...] = jnp.zeros_like(loss_ref)

        loss_ref[...] += 0.5 * jnp.sum(jnp.mean(e * e, axis=-1, keepdims=True), axis=0, keepdims=True)
        dy = e * (1.0 / D)
        dg_ref[...] += jnp.sum(dy * xh, axis=0, keepdims=True)
        dyg = dy * gv
        dx = r * (dyg - xh * jnp.mean(dyg * xh, axis=-1, keepdims=True))
        o_ref[...] = dx
        ob_ref[...] = dx.astype(BF16)

    row = pl.BlockSpec((tm, D), lambda i: (i, 0))
    vec = pl.BlockSpec((1, D), lambda i: (0, 0))
    return _call(body, "loss_head", (T // tm,), [row, row, vec],
                 (row, row, pl.BlockSpec((1, 128), lambda i: (0, 0)), vec),
                 (jax.ShapeDtypeStruct((T, D), F32), jax.ShapeDtypeStruct((T, D), BF16),
                  jax.ShapeDtypeStruct((1, 128), F32), jax.ShapeDtypeStruct((1, D), F32)), [h, target, g])


def _conv_fwd(buf, w_ref, K, tm):
    y = buf[pl.ds(HALO - (K - 1), tm), :] * w_ref[pl.ds(0, 1), :]
    for k in range(1, K):
        y = y + buf[pl.ds(HALO - (K - 1) + k, tm), :] * w_ref[pl.ds(k, 1), :]
    return y


def _conv_bwd_input(buf, w_ref, K, tm):
    dx = buf[pl.ds(K - 1, tm), :] * w_ref[pl.ds(0, 1), :]
    for k in range(1, K):
        dx = dx + buf[pl.ds(K - 1 - k, tm), :] * w_ref[pl.ds(k, 1), :]
    return dx


def _conv_bwd_weight(dw_ref, dy, buf, K, tm):
    for k in range(K):
        dw_ref[pl.ds(k, 1), :] += jnp.sum(dy * buf[pl.ds(HALO - (K - 1) + k, tm), :], axis=0, keepdims=True)


def _sigmoid(z):
    return 1.0 / (1.0 + jnp.exp(-z))


def _silu_grad(z, sig):
    return sig * (1.0 + z * (1.0 - sig))


def _group_norm(u1):
    out = []
    for gi in range(u1.shape[1] // GROUP_DIM):
        xg = u1[:, gi * GROUP_DIM:(gi + 1) * GROUP_DIM]
        xc = xg - jnp.mean(xg, axis=-1, keepdims=True)
        rstd = lax.rsqrt(jnp.mean(xc * xc, axis=-1, keepdims=True) + EPS)
        out.append((xc * rstd, rstd))
    return out


def _mixer_tiles(T, CW):
    tm = _tile(T, 512, HALO)
    tc = _tile(CW, 256, GROUP_DIM)
    return tm, tc, tm // HALO, CW // tc


def _mixer_fwd(proj, caw, cab, lng, lnb, cbw, T, CW):
    KA, KB = caw.shape[0], cbw.shape[0]
    tm, tc, hb, nc = _mixer_tiles(T, CW)

    def sec(s):
        return pl.BlockSpec((tm, tc), lambda i, c: (i, s * nc + c))

    def sec_prev(s):
        return pl.BlockSpec((HALO, tc), lambda i, c: (jnp.maximum(i * hb - 1, 0), s * nc + c))

    def chan(rows):
        return pl.BlockSpec((rows, tc), lambda i, c: (0, c))

    def body(av, ag, bg, cg, bh, avh, agh, cgh, bhh, caw_ref, cab_ref, lng_ref, lnb_ref, cbw_ref,
             mix_ref, u1_ref, bufa, bufb):
        first = pl.program_id(0) == 0
        bufa[pl.ds(HALO, tm), :] = av[...] * _sigmoid(ag[...])
        bufa[pl.ds(0, HALO), :] = jnp.where(first, 0.0, avh[...] * _sigmoid(agh[...]))
        u1 = _conv_fwd(bufa, caw_ref, KA, tm) + cab_ref[...]
        u1_ref[...] = u1
        for gi, (y, _) in enumerate(_group_norm(u1)):
            sl = slice(gi * GROUP_DIM, (gi + 1) * GROUP_DIM)
            z = y * lng_ref[:, sl] + lnb_ref[:, sl]
            mix_ref[0, :, sl] = (z * _sigmoid(z)).astype(BF16)
        bufb[pl.ds(HALO, tm), :] = cg[...] * bh[...]
        bufb[pl.ds(0, HALO), :] = jnp.where(first, 0.0, cgh[...] * bhh[...])
        mix_ref[1, :, :] = (bg[...] * _conv_fwd(bufb, cbw_ref, KB, tm)).astype(BF16)

    in_specs = [sec(0), sec(1), sec(2), sec(3), sec(4), sec_prev(0), sec_prev(1), sec_prev(3), sec_prev(4),
                chan(KA), chan(1), chan(1), chan(1), chan(KB)]
    operands = [proj] * 9 + [caw, cab, lng, lnb, cbw]
    return _call(body, "mixer_fwd", (T // tm, nc), in_specs,
                 (pl.BlockSpec((2, tm, tc), lambda i, c: (0, i, c)), pl.BlockSpec((tm, tc), lambda i, c: (i, c))),
                 (jax.ShapeDtypeStruct((2, T, CW), BF16), jax.ShapeDtypeStruct((T, CW), F32)), operands,
                 scratch=[pltpu.VMEM((HALO + tm, tc), F32), pltpu.VMEM((HALO + tm, tc), F32)],
                 sem=("parallel", "parallel"))


def _mixer_bwd1(dmix, proj, u1, caw, lng, lnb, cbw, T, CW):
    KA, KB = caw.shape[0], cbw.shape[0]
    tm, tc, hb, nc = _mixer_tiles(T, CW)

    def sec(s):
        return pl.BlockSpec((tm, tc), lambda c, i: (i, s * nc + c))

    def sec_prev(s):
        return pl.BlockSpec((HALO, tc), lambda c, i: (jnp.maximum(i * hb - 1, 0), s * nc + c))

    def chan(rows):
        return pl.BlockSpec((rows, tc), lambda c, i: (0, c))

    tile = pl.BlockSpec((tm, tc), lambda c, i: (i, c))

    def body(du, dv, u1_ref, av, ag, bg, cg, bh, avh, agh, cgh, bhh, lng_ref, lnb_ref, cbw_ref,
             du1_ref, dcv_ref, dbg_ref, dcaw_ref, dcab_ref, dlng_ref, dlnb_ref, dcbw_ref, bufa, bufb):
        first = pl.program_id(1) == 0

        @pl.when(first)
        def _():
            for r in (dcaw_ref, dcab_ref, dlng_ref, dlnb_ref, dcbw_ref):
                r[...] = jnp.zeros_like(r)

        duv = du[...]
        for gi, (y, rstd) in enumerate(_group_norm(u1_ref[...])):
            sl = slice(gi * GROUP_DIM, (gi + 1) * GROUP_DIM)
            gamma = lng_ref[:, sl]
            z = y * gamma + lnb_ref[:, sl]
            dz = duv[:, sl] * _silu_grad(z, _sigmoid(z))
            dlng_ref[:, sl] += jnp.sum(dz * y, axis=0, keepdims=True)
            dlnb_ref[:, sl] += jnp.sum(dz, axis=0, keepdims=True)
            dy = dz * gamma
            du1_ref[:, sl] = rstd * (dy - jnp.mean(dy, axis=-1, keepdims=True)
                                     - y * jnp.mean(dy * y, axis=-1, keepdims=True))
        du1 = du1_ref[...]
        dcab_ref[...] += jnp.sum(du1, axis=0, keepdims=True)
        bufa[pl.ds(HALO, tm), :] = av[...] * _sigmoid(ag[...])
        bufa[pl.ds(0, HALO), :] = jnp.where(first, 0.0, avh[...] * _sigmoid(agh[...]))
        _conv_bwd_weight(dcaw_ref, du1, bufa, KA, tm)

        bufb[pl.ds(HALO, tm), :] = cg[...] * bh[...]
        bufb[pl.ds(0, HALO), :] = jnp.where(first, 0.0, cgh[...] * bhh[...])
        dvv = dv[...]
        dbg_ref[...] = (dvv * _conv_fwd(bufb, cbw_ref, KB, tm)).astype(BF16)
        dcv = dvv * bg[...]
        dcv_ref[...] = dcv
        _conv_bwd_weight(dcbw_ref, dcv, bufb, KB, tm)

    in_specs = [sec(0), sec(1), tile, sec(0), sec(1), sec(2), sec(3), sec(4),
                sec_prev(0), sec_prev(1), sec_prev(3), sec_prev(4), chan(1), chan(1), chan(KB)]
    operands = [dmix, dmix, u1] + [proj] * 9 + [lng, lnb, cbw]
    return _call(body, "mixer_bwd1", (nc, T // tm), in_specs,
                 (tile, tile, tile, chan(KA), chan(1), chan(1), chan(1), chan(KB)),
                 (jax.ShapeDtypeStruct((T, CW), F32), jax.ShapeDtypeStruct((T, CW), F32),
                  jax.ShapeDtypeStruct((T, CW), BF16), jax.ShapeDtypeStruct((KA, CW), F32),
                  jax.ShapeDtypeStruct((1, CW), F32), jax.ShapeDtypeStruct((1, CW), F32),
                  jax.ShapeDtypeStruct((1, CW), F32), jax.ShapeDtypeStruct((KB, CW), F32)), operands,
                 scratch=[pltpu.VMEM((HALO + tm, tc), F32), pltpu.VMEM((HALO + tm, tc), F32)],
                 sem=("parallel", "arbitrary"))


def _mixer_bwd2(du1, dcv, proj, caw, cbw, T, CW):
    KA, KB = caw.shape[0], cbw.shape[0]
    tm, tc, hb, nc = _mixer_tiles(T, CW)
    n_i = T // tm

    def sec(s):
        return pl.BlockSpec((tm, tc), lambda i, c: (i, s * nc + c))

    def chan(rows):
        return pl.BlockSpec((rows, tc), lambda i, c: (0, c))

    tile = pl.BlockSpec((tm, tc), lambda i, c: (i, c))
    nxt = pl.BlockSpec((HALO, tc), lambda i, c: (jnp.minimum((i + 1) * hb, n_i * hb - 1), c))

    def body(du1_ref, du1n, dcv_ref, dcvn, av, ag, cg, bh, caw_ref, cbw_ref, dav, dag, dcg, dbh, bufa, bufb):
        last = pl.program_id(0) == n_i - 1
        bufa[pl.ds(0, tm), :] = du1_ref[...]
        bufa[pl.ds(tm, HALO), :] = jnp.where(last, 0.0, du1n[...])
        du0 = _conv_bwd_input(bufa, caw_ref, KA, tm)
        sig = _sigmoid(ag[...])
        dav[...] = (du0 * sig).astype(BF16)
        dag[...] = (du0 * av[...] * (sig * (1.0 - sig))).astype(BF16)
        bufb[pl.ds(0, tm), :] = dcv_ref[...]
        bufb[pl.ds(tm, HALO), :] = jnp.where(last, 0.0, dcvn[...])
        dch = _conv_bwd_input(bufb, cbw_ref, KB, tm)
        dcg[...] = (dch * bh[...]).astype(BF16)
        dbh[...] = (dch * cg[...]).astype(BF16)

    in_specs = [tile, nxt, tile, nxt, sec(0), sec(1), sec(3), sec(4), chan(KA), chan(KB)]
    operands = [du1, du1, dcv, dcv, proj, proj, proj, proj, caw, cbw]
    out = jax.ShapeDtypeStruct((T, CW), BF16)
    return _call(body, "mixer_bwd2", (n_i, nc), in_specs, (tile, tile, tile, tile), (out, out, out, out),
                 operands, scratch=[pltpu.VMEM((HALO + tm, tc), F32), pltpu.VMEM((HALO + tm, tc), F32)],
                 sem=("parallel", "parallel"))


def _ffn_tiles(T):
    tm = _tile(T, 256, HALO)
    return tm, tm // HALO, T // tm


def _ffn_act_fwd(gpre, up, cfw):
    nb, T, F = gpre.shape
    KF = cfw.shape[1]
    tm, hb, n_i = _ffn_tiles(T)
    tile = pl.BlockSpec((None, tm, F), lambda b, i: (b, i, 0))
    prev = pl.BlockSpec((None, HALO, F), lambda b, i: (b, jnp.maximum(i * hb - 1, 0), 0))
    wspec = pl.BlockSpec((None, KF, F), lambda b, i: (b, 0, 0))

    def body(g_ref, gh_ref, up_ref, w_ref, f_ref, buf):
        buf[pl.ds(HALO, tm), :] = g_ref[...]
        buf[pl.ds(0, HALO), :] = jnp.where(pl.program_id(1) == 0, 0.0, gh_ref[...])
        g = _conv_fwd(buf, w_ref, KF, tm)
        f_ref[...] = (g * _sigmoid(g) * up_ref[...]).astype(BF16)

    return _call(body, "ffn_act_fwd", (nb, n_i), [tile, prev, tile, wspec], tile,
                 jax.ShapeDtypeStruct((nb, T, F), BF16), [gpre, gpre, up, cfw],
                 scratch=[pltpu.VMEM((HALO + tm, F), F32)], sem=("parallel", "parallel"))


def _ffn_act_bwd1(df, gpre, up, cfw):
    nb, T, F = gpre.shape
    KF = cfw.shape[1]
    tm, hb, n_i = _ffn_tiles(T)
    tile = pl.BlockSpec((None, tm, F), lambda b, i: (b, i, 0))
    prev = pl.BlockSpec((None, HALO, F), lambda b, i: (b, jnp.maximum(i * hb - 1, 0), 0))
    wspec = pl.BlockSpec((None, KF, F), lambda b, i: (b, 0, 0))

    def body(df_ref, g_ref, gh_ref, up_ref, w_ref, dg_ref, dup_ref, dw_ref, buf):
        first = pl.program_id(1) == 0

        @pl.when(first)
        def _():
            dw_ref[...] = jnp.zeros_like(dw_ref)

        buf[pl.ds(HALO, tm), :] = g_ref[...]
        buf[pl.ds(0, HALO), :] = jnp.where(first, 0.0, gh_ref[...])
        g = _conv_fwd(buf, w_ref, KF, tm)
        sig = _sigmoid(g)
        dfv = df_ref[...]
        dup_ref[...] = (dfv * (g * sig)).astype(BF16)
        dg = dfv * up_ref[...] * _silu_grad(g, sig)
        dg_ref[...] = dg
        _conv_bwd_weight(dw_ref, dg, buf, KF, tm)

    return _call(body, "ffn_act_bwd1", (nb, n_i), [tile, tile, prev, tile, wspec], (tile, tile, wspec),
                 (jax.ShapeDtypeStruct((nb, T, F), F32), jax.ShapeDtypeStruct((nb, T, F), BF16),
                  jax.ShapeDtypeStruct((nb, KF, F), F32)), [df, gpre, gpre, up, cfw],
                 scratch=[pltpu.VMEM((HALO + tm, F), F32)], sem=("parallel", "arbitrary"))


def _ffn_act_bwd2(dg, cfw):
    nb, T, F = dg.shape
    KF = cfw.shape[1]
    tm, hb, n_i = _ffn_tiles(T)
    tile = pl.BlockSpec((None, tm, F), lambda b, i: (b, i, 0))
    nxt = pl.BlockSpec((None, HALO, F), lambda b, i: (b, jnp.minimum((i + 1) * hb, n_i * hb - 1), 0))
    wspec = pl.BlockSpec((None, KF, F), lambda b, i: (b, 0, 0))

    def body(dg_ref, dgn_ref, w_ref, o_ref, buf):
        buf[pl.ds(0, tm), :] = dg_ref[...]
        buf[pl.ds(tm, HALO), :] = jnp.where(pl.program_id(1) == n_i - 1, 0.0, dgn_ref[...])
        o_ref[...] = _conv_bwd_input(buf, w_ref, KF, tm).astype(BF16)

    return _call(body, "ffn_act_bwd2", (nb, n_i), [tile, nxt, wspec], tile,
                 jax.ShapeDtypeStruct((nb, T, F), BF16), [dg, dg, cfw],
                 scratch=[pltpu.VMEM((HALO + tm, F), F32)], sem=("parallel", "parallel"))


def _softmax_rows(s):
    e = jnp.exp(s - jnp.max(s, axis=-1, keepdims=True))
    return e / jnp.sum(e, axis=-1, keepdims=True)


def _attn_fwd(q, k, v):
    T, D = q.shape
    Mm = k.shape[0]
    hd = D // N_XATTN_HEADS
    scale = hd ** -0.5
    tm = _tile(T, 256, 16)

    def body(q_ref, k_ref, v_ref, o_ref):
        for h in range(N_XATTN_HEADS):
            sl = slice(h * hd, (h + 1) * hd)
            s = lax.dot_general(q_ref[:, sl], k_ref[:, sl], _DOT_DIMS["nt"], preferred_element_type=F32) * scale
            p = _softmax_rows(s).astype(BF16)
            o_ref[:, sl] = jnp.dot(p, v_ref[:, sl], preferred_element_type=F32).astype(BF16)

    row = pl.BlockSpec((tm, D), lambda i: (i, 0))
    full = pl.BlockSpec((Mm, D), lambda i: (0, 0))
    return _call(body, "attn_fwd", (T // tm,), [row, full, full], row, jax.ShapeDtypeStruct((T, D), BF16),
                 [q, k, v], sem=("parallel",))


def _attn_bwd(q, k, v, do):
    T, D = q.shape
    Mm = k.shape[0]
    hd = D // N_XATTN_HEADS
    scale = hd ** -0.5
    tm = _tile(T, 256, 16)
    n_i = T // tm

    def body(q_ref, do_ref, k_ref, v_ref, dq_ref, dk_ref, dv_ref, dk_acc, dv_acc):
        @pl.when(pl.program_id(0) == 0)
        def _():
            dk_acc[...] = jnp.zeros_like(dk_acc)
            dv_acc[...] = jnp.zeros_like(dv_acc)

        for h in range(N_XATTN_HEADS):
            sl = slice(h * hd, (h + 1) * hd)
            qh, kh, doh = q_ref[:, sl], k_ref[:, sl], do_ref[:, sl]
            s = lax.dot_general(qh, kh, _DOT_DIMS["nt"], preferred_element_type=F32) * scale
            p = _softmax_rows(s)
            dv_acc[:, sl] += lax.dot_general(p.astype(BF16), doh, _DOT_DIMS["tn"], preferred_element_type=F32)
            dp = lax.dot_general(doh, v_ref[:, sl], _DOT_DIMS["nt"], preferred_element_type=F32)
            ds = (p * (dp - jnp.sum(dp * p, axis=-1, keepdims=True)) * scale).astype(BF16)
            dq_ref[:, sl] = jnp.dot(ds, kh, preferred_element_type=F32).astype(BF16)
            dk_acc[:, sl] += lax.dot_general(ds, qh, _DOT_DIMS["tn"], preferred_element_type=F32)

        @pl.when(pl.program_id(0) == n_i - 1)
        def _():
            dk_ref[...] = dk_acc[...].astype(BF16)
            dv_ref[...] = dv_acc[...].astype(BF16)

    row = pl.BlockSpec((tm, D), lambda i: (i, 0))
    full = pl.BlockSpec((Mm, D), lambda i: (0, 0))
    return _call(body, "attn_bwd", (n_i,), [row, row, full, full], (row, full, full),
                 (jax.ShapeDtypeStruct((T, D), BF16), jax.ShapeDtypeStruct((Mm, D), BF16),
                  jax.ShapeDtypeStruct((Mm, D), BF16)), [q, do, k, v],
                 scratch=[pltpu.VMEM((Mm, D), F32), pltpu.VMEM((Mm, D), F32)])


def _position():
    x, y, c = lax.axis_index("x"), lax.axis_index("y"), lax.axis_index("c")
    return x, y, c


def _peer(pos, k):
    x, y, c = pos
    return (1 - x if k & 4 else x, 1 - y if k & 2 else y, 1 - c if k & 1 else c)


def _index(pos):
    x, y, c = pos
    return 4 * x + 2 * y + c


def _sequencer_kernel(body, name, collective_id, out_type, operands):
    return pl.kernel(
        body, name=name, out_type=out_type,
        mesh=plsc.ScalarSubcoreMesh(axis_name="sequencer", num_cores=1),
        scratch_types=[pltpu.SemaphoreType.DMA, pltpu.SemaphoreType.DMA((7,)), pltpu.SemaphoreType.DMA],
        compiler_params=pltpu.CompilerParams(collective_id=collective_id),
    )(*operands)


def _handshake(peers):
    barrier = pltpu.get_barrier_semaphore()
    for peer in peers:
        pl.semaphore_signal(barrier, inc=1, device_id=peer, device_id_type=MESH)
    pl.semaphore_wait(barrier, len(peers))


def _sequencer_all_gather(name, collective_id, shards):
    n = len(shards)

    def body(*refs):
        x_refs, out_refs = refs[:n], refs[n:2 * n]
        send_sem, recv_sems, local_sem = refs[2 * n:]
        me = _position()
        sibling = _peer(me, 1)
        chips = [_peer(me, 4), _peer(me, 2), _peer(me, 6)]
        _handshake([sibling] + chips)

        def copy(a, k, block, to, own=False):
            dst = out_refs[a].at[_index(block)]
            return pltpu.make_async_remote_copy(
                src_ref=x_refs[a] if own else dst, dst_ref=dst, send_sem=send_sem, recv_sem=recv_sems.at[k],
                device_id=to, device_id_type=MESH)

        local = [pltpu.make_async_copy(x_refs[a], out_refs[a].at[_index(me)], local_sem) for a in range(n)]
        started = [copy(a, 1 + j, me, chip, own=True) for a in range(n) for j, chip in enumerate(chips)]
        started += [copy(a, 0, me, sibling, own=True) for a in range(n)]
        for cp in started + local:
            cp.start()
        for j, chip in enumerate(chips):
            for a in range(n):
                copy(a, 1 + j, chip, me).wait_recv()
            passed = [copy(a, 4 + j, chip, sibling) for a in range(n)]
            for cp in passed:
                cp.start()
            started += passed
        for a in range(n):
            copy(a, 0, sibling, me).wait_recv()
        for j, chip in enumerate(chips):
            for a in range(n):
                copy(a, 4 + j, _peer(chip, 1), me).wait_recv()
        for cp in started:
            cp.wait_send()
        for cp in local:
            cp.wait()

    return _sequencer_kernel(body, name, collective_id,
                             [jax.ShapeDtypeStruct((N_DEV,) + s.shape, s.dtype) for s in shards], shards)


def _chip_index(pos):
    return 2 * pos[0] + pos[1]


def _sequencer_to_sibling(name, collective_id, parts):
    n = len(parts)

    def body(*refs):
        p_refs, out_refs = refs[:n], refs[n:2 * n]
        send_sem, recv_sems, _ = refs[2 * n:]
        me = _position()
        sibling = _peer(me, 1)
        _handshake([sibling])
        copies = [pltpu.make_async_remote_copy(
            src_ref=p_refs[a].at[2 * q + sibling[2]], dst_ref=out_refs[a].at[q], send_sem=send_sem,
            recv_sem=recv_sems.at[0], device_id=sibling, device_id_type=MESH)
            for a in range(n) for q in range(N_DEV // 2)]
        for cp in copies:
            cp.start()
        for cp in copies:
            cp.wait_recv()
        for cp in copies:
            cp.wait_send()

    return _sequencer_kernel(body, name, collective_id,
                             [jax.ShapeDtypeStruct((N_DEV // 2,) + p.shape[1:], p.dtype) for p in parts], parts)


def _sequencer_to_chips(name, collective_id, sums):
    n = len(sums)

    def body(*refs):
        s_refs, out_refs = refs[:n], refs[n:2 * n]
        send_sem, recv_sems, local_sem = refs[2 * n:]
        me = _position()
        my_chip = _chip_index(me)
        peers = [_peer(me, 4), _peer(me, 2), _peer(me, 6)]
        _handshake(peers)
        local = [pltpu.make_async_copy(s_refs[a].at[my_chip], out_refs[a].at[my_chip], local_sem) for a in range(n)]
        sends = [pltpu.make_async_remote_copy(
            src_ref=s_refs[a].at[_chip_index(peer)], dst_ref=out_refs[a].at[my_chip], send_sem=send_sem,
            recv_sem=recv_sems.at[1 + j], device_id=peer, device_id_type=MESH)
            for a in range(n) for j, peer in enumerate(peers)]
        for cp in sends + local:
            cp.start()
        for j, peer in enumerate(peers):
            for a in range(n):
                pltpu.make_async_remote_copy(
                    src_ref=s_refs[a].at[my_chip], dst_ref=out_refs[a].at[_chip_index(peer)], send_sem=send_sem,
                    recv_sem=recv_sems.at[1 + j], device_id=peer, device_id_type=MESH).wait_recv()
        for cp in sends:
            cp.wait_send()
        for cp in local:
            cp.wait()

    return _sequencer_kernel(body, name, collective_id,
                             [jax.ShapeDtypeStruct(s.shape, s.dtype) for s in sums], sums)


def _chip_sum(name, parts, got, after=()):
    _, R, C = parts.shape
    tr = _tile(R, max(8, (1 << 19) // C), 16)
    n_after = len(after)

    def body(c_ref, p_ref, g_ref, *rest):
        o_ref = rest[n_after]
        o_ref[...] = (p_ref[...].astype(F32) + g_ref[...].astype(F32)).astype(o_ref.dtype)

    blk = pl.BlockSpec((None, tr, C), lambda q, i, c_ref: (q, i, 0))
    mine = pl.BlockSpec((None, tr, C), lambda q, i, c_ref: (2 * q + c_ref[0], i, 0))
    core = lax.axis_index("c").astype(jnp.int32).reshape(1)
    return pl.pallas_call(
        body, name=name, out_shape=jax.ShapeDtypeStruct((N_DEV // 2, R, C), parts.dtype),
        grid_spec=pltpu.PrefetchScalarGridSpec(
            num_scalar_prefetch=1, grid=(N_DEV // 2, R // tr),
            in_specs=[mine, blk] + [pl.BlockSpec(memory_space=pl.ANY)] * n_after, out_specs=blk),
        compiler_params=pltpu.CompilerParams(dimension_semantics=("parallel", "parallel")),
    )(core, parts, got, *after)


def _all_reduce_rows(name, v):
    R, C = v.shape

    def body(v_ref, out_ref, gath, send_sems, recv_sems):
        me = _position()
        gath[_index(me)] = v_ref[...]
        sends = []
        for k in range(1, N_DEV):
            peer = _peer(me, k)
            sends.append(pltpu.make_async_remote_copy(
                src_ref=v_ref, dst_ref=gath.at[_index(me)], send_sem=send_sems.at[k - 1],
                recv_sem=recv_sems.at[k - 1], device_id=peer, device_id_type=MESH))
        for cp in sends:
            cp.start()
        for k in range(1, N_DEV):
            peer = _peer(me, k)
            pltpu.make_async_remote_copy(
                src_ref=v_ref, dst_ref=gath.at[_index(peer)], send_sem=send_sems.at[k - 1],
                recv_sem=recv_sems.at[k - 1], device_id=peer, device_id_type=MESH).wait_recv()
        for cp in sends:
            cp.wait_send()
        tot = gath[0]
        for s in range(1, N_DEV):
            tot = tot + gath[s]
        out_ref[...] = tot

    return pl.pallas_call(
        body, name=name, out_shape=jax.ShapeDtypeStruct((R, C), F32),
        in_specs=[VMEM_SPEC], out_specs=VMEM_SPEC,
        scratch_shapes=[pltpu.VMEM((N_DEV, R, C), F32), pltpu.SemaphoreType.DMA((7,)),
                        pltpu.SemaphoreType.DMA((7,))],
    )(v)


def _adamw_math(g, w, m, v):
    m = ADAM_B1 * m + (1.0 - ADAM_B1) * g
    v = ADAM_B2 * v + (1.0 - ADAM_B2) * (g * g)
    m_hat = m / (1.0 - ADAM_B1 ** ADAM_STEP)
    v_hat = v / (1.0 - ADAM_B2 ** ADAM_STEP)
    delta = -ADAM_LR * (m_hat / (jnp.sqrt(v_hat) + ADAM_EPS) + ADAM_WD * w)
    return delta, m, v


def _adamw(name, parts, w, m, v, after=()):
    n, R, C = parts.shape
    tr = _tile(R, max(8, (1 << 18) // C), 16)

    def body(p_ref, w_ref, m_ref, v_ref, g_ref, d_ref, nm_ref, nv_ref):
        g = p_ref[0].astype(F32)
        for s in range(1, n):
            g = g + p_ref[s].astype(F32)
        g_ref[...] = g
        d_ref[...], nm_ref[...], nv_ref[...] = _adamw_math(g, w_ref[...], m_ref[...], v_ref[...])

    blk = pl.BlockSpec((tr, C), lambda i: (i, 0))
    out = jax.ShapeDtypeStruct((R, C), F32)
    return _call(body, name, (R // tr,), [pl.BlockSpec((n, tr, C), lambda i: (0, i, 0)), blk, blk, blk],
                 (blk, blk, blk, blk), (out, out, out, out), [parts, w, m, v], sem=("parallel",), after=after)


def kernel(x, mem, g_mix, w_in, conv_a_w, conv_a_b, ln_a_g, ln_a_b, conv_b_w, w_out, g_xattn, g_mem, w_q, w_k, w_v, w_o, g_ffn, w_gate, w_up, conv_f_w, w_down, g_final, loss_target, m_g_mix, m_w_in, m_conv_a_w, m_conv_a_b, m_ln_a_g, m_ln_a_b, m_conv_b_w, m_w_out, m_g_xattn, m_g_mem, m_w_q, m_w_k, m_w_v, m_w_o, m_g_ffn, m_w_gate, m_w_up, m_conv_f_w, m_w_down, m_g_final, v_g_mix, v_w_in, v_conv_a_w, v_conv_a_b, v_ln_a_g, v_ln_a_b, v_conv_b_w, v_w_out, v_g_xattn, v_g_mem, v_w_q, v_w_k, v_w_v, v_w_o, v_g_ffn, v_w_gate, v_w_up, v_conv_f_w, v_w_down, v_g_final):
    T, D = x.shape[1], x.shape[2]
    Mm = mem.shape[1]
    CW = conv_a_b.shape[1]
    INB = w_in.shape[2]
    FB = w_gate.shape[2]
    KA, KB, KF = conv_a_w.shape[1], conv_b_w.shape[1], conv_f_w.shape[1]
    DB = D // N_DEV
    assert 5 * CW == N_DEV * INB and 2 * CW == D

    x2, mem2, tgt = x[0], mem[0], loss_target[0]
    g_mem2, g_final2 = g_mem.reshape(1, D), g_final.reshape(1, D)

    def bf16(name, w):
        return _cast_bf16("cast_" + name, w[0])

    Win, caw, cbw = _sequencer_all_gather(
        "ag_in", AG_ID, [bf16("w_in", w_in), conv_a_w[0], conv_b_w[0]])
    Wout, = _sequencer_all_gather("ag_out", AG_ID, [bf16("w_out", w_out)])
    Wq, Wk, Wv, Wo = _sequencer_all_gather(
        "ag_attn", AG_ID, [bf16("w_q", w_q), bf16("w_k", w_k), bf16("w_v", w_v), bf16("w_o", w_o)])
    Wgate, cfw = _sequencer_all_gather("ag_gate", AG_ID, [bf16("w_gate", w_gate), conv_f_w[0]])
    Wup, = _sequencer_all_gather("ag_up", AG_ID, [bf16("w_up", w_up)])
    Wdown, = _sequencer_all_gather("ag_down", AG_ID, [bf16("w_down", w_down)])
    Wout, Wq, Wk, Wv, Wo = [w.reshape(D, D) for w in (Wout, Wq, Wk, Wv, Wo)]
    caw = jnp.transpose(caw, (1, 0, 2)).reshape(KA, CW)
    cbw = jnp.transpose(cbw, (1, 0, 2)).reshape(KB, CW)

    xn1 = _rms_fwd("rms_mix", x2, g_mix)
    proj = _matmul("mm_proj", "nn", xn1, Win, M=T, N=INB, K=D, nb=N_DEV, b_lay="blk", o_lay="col", tn=INB)
    mix, u1 = _mixer_fwd(proj, caw, conv_a_b, ln_a_g, ln_a_b, cbw, T, CW)
    h1 = _matmul("mm_h1", "nn", mix, Wout.reshape(2, CW, D), M=T, N=D, K=CW, nb=2, a_lay="blk", b_lay="blk",
                 red_block=True, res=x2)
    xn2 = _rms_fwd("rms_xattn", h1, g_xattn)
    q = _matmul("mm_q", "nn", xn2, Wq, M=T, N=D, K=D, out_dtype=BF16)
    memn = _rms_fwd("rms_mem", mem2, g_mem2, after=[q])
    kk = _matmul("mm_k", "nn", memn, Wk, M=Mm, N=D, K=D, out_dtype=BF16)
    vv = _matmul("mm_v", "nn", memn, Wv, M=Mm, N=D, K=D, out_dtype=BF16)
    o = _attn_fwd(q, kk, vv)
    h2 = _matmul("mm_h2", "nn", o, Wo, M=T, N=D, K=D, res=h1)
    xn3 = _rms_fwd("rms_ffn", h2, g_ffn)
    gpre = _matmul("mm_gate", "nn", xn3, Wgate, M=T, N=FB, K=D, nb=N_DEV, b_lay="blk", o_lay="blk", tn=FB)
    up = _matmul("mm_up", "nn", xn3, Wup, M=T, N=FB, K=D, nb=N_DEV, b_lay="blk", o_lay="blk", tn=FB)
    f = _ffn_act_fwd(gpre, up, cfw)
    h3 = _matmul("mm_h3", "nn", f, Wdown, M=T, N=D, K=FB, nb=N_DEV, a_lay="blk", b_lay="blk", red_block=True,
                 res=h2, tk=FB)
    dh3, dh3b, loss_part, dg_final = _loss_head(h3, tgt, g_final2)

    wmv = {"w_in": (w_in, m_w_in, v_w_in), "conv_a_w": (conv_a_w, m_conv_a_w, v_conv_a_w),
           "conv_b_w": (conv_b_w, m_conv_b_w, v_conv_b_w), "w_out": (w_out, m_w_out, v_w_out),
           "w_q": (w_q, m_w_q, v_w_q), "w_k": (w_k, m_w_k, v_w_k), "w_v": (w_v, m_w_v, v_w_v),
           "w_o": (w_o, m_w_o, v_w_o), "w_gate": (w_gate, m_w_gate, v_w_gate), "w_up": (w_up, m_w_up, v_w_up),
           "conv_f_w": (conv_f_w, m_conv_f_w, v_conv_f_w), "w_down": (w_down, m_w_down, v_w_down)}
    res = {}
    pending = []

    def mm(*args, after=(), **kwargs):
        behind = list(after) + pending
        pending.clear()
        return _matmul(*args, after=behind, **kwargs)

    def to_sibling(tag, named_parts):
        got = _sequencer_to_sibling("rs1_" + tag, SIBLING_ID, [p for _, p in named_parts])
        return named_parts, got

    def to_chips(tag, stage1, after):
        named_parts, got = stage1
        sums = [_chip_sum("sum_" + n, p, g, after=after) for (n, p), g in zip(named_parts, got)]
        pending.extend(sums)
        return [n for n, _ in named_parts], _sequencer_to_chips("rs2_" + tag, CHIPS_ID, sums)

    def finish(stage2, after):
        names, got = stage2
        for n, g in zip(names, got):
            w, m, v = wmv[n]
            res[n] = _adamw("adamw_" + n, g, w[0], m[0], v[0], after=after)
            pending.append(res[n][0])

    def row_blocks(dw):
        return dw.reshape(N_DEV, DB, D)

    def conv_blocks(dw, K):
        return jnp.transpose(dw.reshape(K, N_DEV, CW // N_DEV), (1, 0, 2))

    dWdown = mm("mm_dw_down", "tn", f, dh3b, M=FB, N=D, K=T, nb=N_DEV, a_lay="blk", o_lay="blk",
                     out_dtype=BF16, tm=FB)
    s_down = to_sibling("down", [("w_down", dWdown)])
    df = mm("mm_df", "nt", dh3b, Wdown, M=T, N=FB, K=D, nb=N_DEV, b_lay="blk", o_lay="blk", tn=FB,
                 after=[dWdown])
    dg, dup, dcfw = _ffn_act_bwd1(df, gpre, up, cfw)
    dgpre = _ffn_act_bwd2(dg, cfw)
    c_down = to_chips("down", s_down, after=[dgpre])
    dWgate = mm("mm_dw_gate", "tn", xn3, dgpre, M=D, N=FB, K=T, nb=N_DEV, b_lay="blk", o_lay="blk",
                     out_dtype=BF16, tn=FB)
    s_gate = to_sibling("gate", [("w_gate", dWgate), ("conv_f_w", dcfw)])
    dWup = mm("mm_dw_up", "tn", xn3, dup, M=D, N=FB, K=T, nb=N_DEV, b_lay="blk", o_lay="blk",
                   out_dtype=BF16, tn=FB, after=[dWgate])
    s_up = to_sibling("up", [("w_up", dWup)])
    dxn3 = mm("mm_dxn3_gate", "nt", dgpre, Wgate, M=T, N=D, K=FB, nb=N_DEV, a_lay="blk", b_lay="blk",
                   red_block=True, tk=FB, after=[dWup])
    c_gate = to_chips("gate", s_gate, after=[dxn3])
    dxn3 = mm("mm_dxn3_up", "nt", dup, Wup, M=T, N=D, K=FB, nb=N_DEV, a_lay="blk", b_lay="blk",
                   red_block=True, res=dxn3, tk=FB)
    c_up = to_chips("up", s_up, after=[dxn3])
    dh2, dh2b, dg_ffn = _rms_bwd("rms_bwd_ffn", dxn3, h2, g_ffn, dh3)

    dWo = mm("mm_dw_o", "tn", o, dh2b, M=D, N=D, K=T, out_dtype=BF16)
    s_o = to_sibling("o", [("w_o", row_blocks(dWo))])
    do = mm("mm_do", "nt", dh2b, Wo, M=T, N=D, K=D, out_dtype=BF16, after=[dWo])
    finish(c_down, after=[do])
    dq, dk, dv = _attn_bwd(q, kk, vv, do)
    c_o = to_chips("o", s_o, after=[dq])
    dWq = mm("mm_dw_q", "tn", xn2, dq, M=D, N=D, K=T, out_dtype=BF16)
    s_q = to_sibling("q", [("w_q", row_blocks(dWq))])
    dxn2 = mm("mm_dxn2", "nt", dq, Wq, M=T, N=D, K=D, after=[dWq])
    finish(c_gate, after=[dxn2])
    c_q = to_chips("q", s_q, after=[dxn2])
    dh1, dh1b, dg_xattn = _rms_bwd("rms_bwd_xattn", dxn2, h1, g_xattn, dh2)
    dWk = mm("mm_dw_k", "tn", memn, dk, M=D, N=D, K=Mm, out_dtype=BF16, after=[dh1b])
    dWv = mm("mm_dw_v", "tn", memn, dv, M=D, N=D, K=Mm, out_dtype=BF16, after=[dh1b])
    s_kv = to_sibling("kv", [("w_k", row_blocks(dWk)), ("w_v", row_blocks(dWv))])
    dmemn = mm("mm_dmem_k", "nt", dk, Wk, M=Mm, N=D, K=D, after=[dWk, dWv])
    dmemn = mm("mm_dmem_v", "nt", dv, Wv, M=Mm, N=D, K=D, res=dmemn)
    dg_mem = _rms_bwd("rms_bwd_mem", dmemn, mem2, g_mem2)
    finish(c_up, after=[dg_mem])

    dWout = mm("mm_dw_out", "tn", mix, dh1b, M=CW, N=D, K=T, nb=2, a_lay="blk", o_lay="blk", out_dtype=BF16,
                    after=[dg_mem])
    s_out = to_sibling("out", [("w_out", row_blocks(dWout.reshape(D, D)))])
    dmix = mm("mm_dmix", "nt", dh1b, Wout, M=T, N=D, K=D, after=[dWout])
    c_kv = to_chips("kv", s_kv, after=[dmix])
    du1, dcv, dbg, dcaw, dcab, dlng, dlnb, dcbw = _mixer_bwd1(dmix, proj, u1, caw, ln_a_g, ln_a_b, cbw, T, CW)
    c_out = to_chips("out", s_out, after=[du1])
    finish(c_o, after=[du1])
    finish(c_q, after=[du1])
    dav, dag, dcg, dbh = _mixer_bwd2(du1, dcv, proj, caw, cbw, T, CW)
    dproj = jnp.concatenate([dav, dag, dbg, dcg, dbh], axis=1)
    dWin = mm("mm_dw_in", "tn", xn1, dproj, M=D, N=INB, K=T, nb=N_DEV, b_lay="col", o_lay="blk",
                   out_dtype=BF16, tn=INB)
    s_in = to_sibling("in", [("w_in", dWin), ("conv_a_w", conv_blocks(dcaw, KA)),
                             ("conv_b_w", conv_blocks(dcbw, KB))])
    finish(c_kv, after=[dWin])
    c_in = to_chips("in", s_in, after=list(pending))
    dxn1 = mm("mm_dxn1", "nt", dproj, Win, M=T, N=D, K=INB, nb=N_DEV, a_lay="col", b_lay="blk",
                   red_block=True, tk=INB, after=[dWin])
    finish(c_out, after=[dxn1])
    dx, _, dg_mix = _rms_bwd("rms_bwd_mix", dxn1, x2, g_mix, dh1, after=list(pending))

    def pair(a, b):
        return jnp.concatenate([a, b], axis=1)

    zeros_half = jnp.zeros((1, CW), F32)
    small_g = jnp.concatenate([
        dg_mix, pair(dcab, dlng), pair(dlnb, zeros_half), dg_xattn, dg_mem, dg_ffn, dg_final,
        jnp.broadcast_to(loss_part[:, :1], (1, D))], axis=0)
    small_sum = _all_reduce_rows("ar_small", small_g)
    loss = small_sum[7, 0]
    finish(c_in, after=[small_sum])

    def pack(a_mix, a_cab, a_lng, a_lnb, a_xattn, a_mem, a_ffn, a_final):
        return jnp.concatenate([a_mix, pair(a_cab, a_lng), pair(a_lnb, zeros_half), a_xattn, a_mem.reshape(1, D),
                                a_ffn, a_final.reshape(1, D), jnp.zeros((1, D), F32)], axis=0)

    small = _adamw("adamw_small", small_sum[None],
                   pack(g_mix, conv_a_b, ln_a_g, ln_a_b, g_xattn, g_mem, g_ffn, g_final),
                   pack(m_g_mix, m_conv_a_b, m_ln_a_g, m_ln_a_b, m_g_xattn, m_g_mem, m_g_ffn, m_g_final),
                   pack(v_g_mix, v_conv_a_b, v_ln_a_g, v_ln_a_b, v_g_xattn, v_g_mem, v_g_ffn, v_g_final))

    def unpack(a):
        return {"g_mix": a[0:1], "conv_a_b": a[1:2, :CW], "ln_a_g": a[1:2, CW:], "ln_a_b": a[2:3, :CW],
                "g_xattn": a[3:4], "g_mem": a[4], "g_ffn": a[5:6], "g_final": a[6]}

    small = [unpack(a) for a in small]
    order = ["g_mix", "w_in", "conv_a_w", "conv_a_b", "ln_a_g", "ln_a_b", "conv_b_w", "w_out", "g_xattn", "g_mem",
             "w_q", "w_k", "w_v", "w_o", "g_ffn", "w_gate", "w_up", "conv_f_w", "w_down", "g_final"]
    outs = [loss, dx[None]]
    for kind in range(4):
        for n in order:
            outs.append(res[n][kind][None] if n in res else small[kind][n])
    return tuple(outs)
```

```python
import functools

import jax
import jax.numpy as jnp
from jax import lax
from jax.experimental import pallas as pl
from jax.experimental.pallas import tpu as pltpu
from jax.experimental.pallas import tpu_sc as plsc

F32 = jnp.float32
BF16 = jnp.bfloat16

N_DEV = 8
EPS = 1e-6
GROUP_DIM = 128
N_XATTN_HEADS = 4
ADAM_LR = 0.001
ADAM_B1 = 0.9
ADAM_B2 = 0.999
ADAM_EPS = 1e-08
ADAM_WD = 0.01
ADAM_STEP = 10

AG_ID, SIBLING_ID, CHIPS_ID = 1, 2, 3

HALO = 32
VMEM_V7X_BYTES = 64 * 1024 * 1024
VMEM_TEMP_ALLOWANCE = 12 * 1024 * 1024

VMEM_SPEC = pl.BlockSpec(memory_space=pltpu.VMEM)
MESH = pl.DeviceIdType.MESH


def _tile(n, pref, align):
    if n <= pref:
        return n
    t = (pref // align) * align
    while t >= align:
        if n % t == 0:
            return t
        t -= align
    return n


def _nbytes(shape, dtype):
    n = 1
    for d in shape:
        if d is not None:
            n *= d
    return n * jnp.dtype(dtype).itemsize


def _call(body, name, grid, in_specs, out_specs, out_shape, operands, scratch=(), sem=None, after=()):
    outs = out_shape if isinstance(out_shape, (tuple, list)) else (out_shape,)
    ospecs = out_specs if isinstance(out_specs, (tuple, list)) else (out_specs,)
    est = 0
    for spec, arr in list(zip(in_specs, operands)) + list(zip(ospecs, outs)):
        est += 2 * _nbytes(spec.block_shape, arr.dtype)
    for s in scratch:
        if hasattr(s, "shape") and hasattr(s, "dtype"):
            est += _nbytes(s.shape, s.dtype)
    limit = min(est + VMEM_TEMP_ALLOWANCE, VMEM_V7X_BYTES - 4 * 1024 * 1024)
    if sem is None:
        sem = ("arbitrary",) * len(grid)
    n_in, n_after = len(operands), len(after)

    def ordered_body(*refs):
        body(*refs[:n_in], *refs[n_in + n_after:])

    return pl.pallas_call(
        ordered_body if n_after else body, name=name, grid=grid,
        in_specs=list(in_specs) + [pl.BlockSpec(memory_space=pl.ANY)] * n_after,
        out_specs=out_specs, out_shape=out_shape, scratch_shapes=list(scratch),
        compiler_params=pltpu.CompilerParams(dimension_semantics=sem, vmem_limit_bytes=int(limit)),
    )(*operands, *after)


_DOT_DIMS = {"nn": (((1,), (0,)), ((), ())), "nt": (((1,), (1,)), ((), ())), "tn": (((0,), (0,)), ((), ()))}


def _operand_spec(layout, tr, tc, cols_per_block, pick):
    if layout == "plain":
        return pl.BlockSpec((tr, tc), lambda *g: pick(*g)[1:])
    if layout == "blk":
        return pl.BlockSpec((None, tr, tc), lambda *g: pick(*g))
    assert layout == "col"
    per = cols_per_block // tc
    return pl.BlockSpec((tr, tc), lambda *g: (pick(*g)[1], pick(*g)[0] * per + pick(*g)[2]))


def _matmul(name, dims, a, b, *, M, N, K, nb=1, a_lay="plain", b_lay="plain", o_lay="plain",
            red_block=False, out_dtype=F32, res=None, tm=1024, tn=1024, tk=2048, after=()):
    tm = _tile(M, tm, 128 if dims == "tn" else 16)
    tn = _tile(N, tn, 128)
    tk = _tile(K, tk, 128 if dims != "tn" else 16)
    gi, gj, gk = M // tm, N // tn, K // tk
    if red_block:
        grid = (gi, gj, nb, gk)
        unpack = lambda i, j, bb, k: (bb, i, j, k)
        red_axes, sem = (2, 3), ("parallel", "parallel", "arbitrary", "arbitrary")
    else:
        grid = (nb, gi, gj, gk)
        unpack = lambda bb, i, j, k: (bb, i, j, k)
        red_axes, sem = (3,), ("parallel", "parallel", "parallel", "arbitrary")

    def picker(f):
        return lambda *g: f(*unpack(*g))

    if dims == "tn":
        a_spec = _operand_spec(a_lay, tk, tm, M, picker(lambda bb, i, j, k: (bb, k, i)))
    else:
        a_spec = _operand_spec(a_lay, tm, tk, K, picker(lambda bb, i, j, k: (bb, i, k)))
    if dims == "nt":
        b_spec = _operand_spec(b_lay, tn, tk, K, picker(lambda bb, i, j, k: (bb, j, k)))
    else:
        b_spec = _operand_spec(b_lay, tk, tn, N, picker(lambda bb, i, j, k: (bb, k, j)))
    o_spec = _operand_spec(o_lay, tm, tn, N, picker(lambda bb, i, j, k: (bb, i, j)))
    if o_lay == "plain":
        out_shape = jax.ShapeDtypeStruct((M, N), out_dtype)
    elif o_lay == "blk":
        out_shape = jax.ShapeDtypeStruct((nb, M, N), out_dtype)
    else:
        out_shape = jax.ShapeDtypeStruct((M, nb * N), out_dtype)
    n_red = [grid[ax] for ax in red_axes]
    has_res = res is not None

    def body(*refs):
        if has_res:
            a_ref, b_ref, r_ref, o_ref, acc = refs
        else:
            a_ref, b_ref, o_ref, acc = refs
        first = functools.reduce(jnp.logical_and, [pl.program_id(ax) == 0 for ax in red_axes])
        last = functools.reduce(jnp.logical_and, [pl.program_id(ax) == n - 1 for ax, n in zip(red_axes, n_red)])

        @pl.when(first)
        def _():
            acc[...] = jnp.zeros_like(acc)

        acc[...] += lax.dot_general(a_ref[...], b_ref[...], _DOT_DIMS[dims], preferred_element_type=F32)

        @pl.when(last)
        def _():
            r = acc[...]
            if has_res:
                r = r + r_ref[...]
            o_ref[...] = r.astype(o_ref.dtype)

    in_specs = [a_spec, b_spec]
    operands = [a, b]
    if has_res:
        in_specs.append(_operand_spec("plain", tm, tn, N, picker(lambda bb, i, j, k: (bb, i, j))))
        operands.append(res)
    return _call(body, name, grid, in_specs, o_spec, out_shape, operands,
                 scratch=[pltpu.VMEM((tm, tn), F32)], sem=sem, after=after)


def _cast_bf16(name, w):
    R, C = w.shape
    tr = _tile(R, max(8, (1 << 20) // C), 16)

    def body(w_ref, o_ref):
        o_ref[...] = w_ref[...].astype(BF16)

    return _call(body, name, (R // tr,), [pl.BlockSpec((tr, C), lambda i: (i, 0))],
                 pl.BlockSpec((tr, C), lambda i: (i, 0)), jax.ShapeDtypeStruct((R, C), BF16), [w],
                 sem=("parallel",))


def _rms_fwd(name, x, g, after=()):
    T, D = x.shape
    tm = _tile(T, 128, 16)

    def body(x_ref, g_ref, o_ref):
        xv = x_ref[...]
        r = lax.rsqrt(jnp.mean(xv * xv, axis=-1, keepdims=True) + EPS)
        o_ref[...] = (xv * r * g_ref[...]).astype(BF16)

    return _call(body, name, (T // tm,),
                 [pl.BlockSpec((tm, D), lambda i: (i, 0)), pl.BlockSpec((1, D), lambda i: (0, 0))],
                 pl.BlockSpec((tm, D), lambda i: (i, 0)), jax.ShapeDtypeStruct((T, D), BF16), [x, g],
                 sem=("parallel",), after=after)


def _rms_bwd(name, dxn, x, g, dh=None, after=()):
    T, D = x.shape
    tm = _tile(T, 128, 16)
    with_dx = dh is not None

    def body(*refs):
        if with_dx:
            dxn_ref, x_ref, g_ref, dh_ref, o_ref, ob_ref, dg_ref = refs
        else:
            dxn_ref, x_ref, g_ref, dg_ref = refs
        xv = x_ref[...]
        r = lax.rsqrt(jnp.mean(xv * xv, axis=-1, keepdims=True) + EPS)
        xh = xv * r
        dy = dxn_ref[...]

        @pl.when(pl.program_id(0) == 0)
        def _():
            dg_ref[...] = jnp.zeros_like(dg_ref)

        dg_ref[...] += jnp.sum(dy * xh, axis=0, keepdims=True)
        if with_dx:
            dyg = dy * g_ref[...]
            tot = dh_ref[...] + r * (dyg - xh * jnp.mean(dyg * xh, axis=-1, keepdims=True))
            o_ref[...] = tot
            ob_ref[...] = tot.astype(BF16)

    row = pl.BlockSpec((tm, D), lambda i: (i, 0))
    vec = pl.BlockSpec((1, D), lambda i: (0, 0))
    if with_dx:
        return _call(body, name, (T // tm,), [row, row, vec, row], (row, row, vec),
                     (jax.ShapeDtypeStruct((T, D), F32), jax.ShapeDtypeStruct((T, D), BF16),
                      jax.ShapeDtypeStruct((1, D), F32)), [dxn, x, g, dh], after=after)
    return _call(body, name, (T // tm,), [row, row, vec], vec, jax.ShapeDtypeStruct((1, D), F32), [dxn, x, g])


def _loss_head(h, target, g):
    T, D = h.shape
    tm = _tile(T, 128, 16)

    def body(h_ref, t_ref, g_ref, o_ref, ob_ref, loss_ref, dg_ref):
        xv = h_ref[...]
        gv = g_ref[...]
        r = lax.rsqrt(jnp.mean(xv * xv, axis=-1, keepdims=True) + EPS)
        xh = xv * r
        e = xh * gv - t_ref[...]

        @pl.when(pl.program_id(0) == 0)
        def _():
            dg_ref[...] = jnp.zeros_like(dg_ref)
            loss_ref[...] = jnp.zeros_like(loss_ref)

        loss_ref[...] += 0.5 * jnp.sum(jnp.mean(e * e, axis=-1, keepdims=True), axis=0, keepdims=True)
        dy = e * (1.0 / D)
        dg_ref[...] += jnp.sum(dy * xh, axis=0, keepdims=True)
        dyg = dy * gv
        dx = r * (dyg - xh * jnp.mean(dyg * xh, axis=-1, keepdims=True))
        o_ref[...] = dx
        ob_ref[...] = dx.astype(BF16)

    row = pl.BlockSpec((tm, D), lambda i: (i, 0))
    vec = pl.BlockSpec((1, D), lambda i: (0, 0))
    return _call(body, "loss_head", (T // tm,), [row, row, vec],
                 (row, row, pl.BlockSpec((1, 128), lambda i: (0, 0)), vec),
                 (jax.ShapeDtypeStruct((T, D), F32), jax.ShapeDtypeStruct((T, D), BF16),
                  jax.ShapeDtypeStruct((1, 128), F32), jax.ShapeDtypeStruct((1, D), F32)), [h, target, g])


ROW_CHUNK = 64
SUBLANES = 8


def _col_chunks(width):
    return [slice(c0, min(c0 + GROUP_DIM, width)) for c0 in range(0, width, GROUP_DIM)]


def _row_chunks(n_rows):
    return [(r0, min(ROW_CHUNK, n_rows - r0)) for r0 in range(0, n_rows, ROW_CHUNK)]


def _conv_fwd(buf, w_ref, K, r0, nr, cs):
    base = HALO - (K - 1) + r0
    y = buf[pl.ds(base, nr), cs] * w_ref[pl.ds(0, 1), cs]
    for k in range(1, K):
        y = y + buf[pl.ds(base + k, nr), cs] * w_ref[pl.ds(k, 1), cs]
    return y


def _conv_bwd_input(buf, w_ref, K, r0, nr, cs):
    dx = buf[pl.ds(r0 + K - 1, nr), cs] * w_ref[pl.ds(0, 1), cs]
    for k in range(1, K):
        dx = dx + buf[pl.ds(r0 + K - 1 - k, nr), cs] * w_ref[pl.ds(k, 1), cs]
    return dx


def _fold_rows(v):
    nr, lanes = v.shape
    if nr % SUBLANES:
        return jnp.sum(v, axis=0, keepdims=True)
    return jnp.sum(v.reshape(nr // SUBLANES, SUBLANES, lanes), axis=0)


def _conv_bwd_weight(accs, dy, buf, K, r0, nr, cs):
    base = HALO - (K - 1) + r0
    return [accs[k] + _fold_rows(dy * buf[pl.ds(base + k, nr), cs]) for k in range(K)]


def _add_row(ref, row, cs, acc):
    ref[pl.ds(row, 1), cs] += jnp.sum(acc, axis=0, keepdims=True)


def _sigmoid(z):
    return 1.0 / (1.0 + jnp.exp(-z))


def _silu_grad(z, sig):
    return sig * (1.0 + z * (1.0 - sig))


def _group_norm(xg):
    xc = xg - jnp.mean(xg, axis=-1, keepdims=True)
    rstd = lax.rsqrt(jnp.mean(xc * xc, axis=-1, keepdims=True) + EPS)
    return xc * rstd, rstd


def _mixer_tiles(T, CW):
    tm = _tile(T, 512, HALO)
    tc = _tile(CW, 256, GROUP_DIM)
    return tm, tc, tm // HALO, CW // tc


def _mixer_fwd(proj, caw, cab, lng, lnb, cbw, T, CW):
    KA, KB = caw.shape[0], cbw.shape[0]
    tm, tc, hb, nc = _mixer_tiles(T, CW)

    def sec(s):
        return pl.BlockSpec((tm, tc), lambda i, c: (i, s * nc + c))

    def sec_prev(s):
        return pl.BlockSpec((HALO, tc), lambda i, c: (jnp.maximum(i * hb - 1, 0), s * nc + c))

    def chan(rows):
        return pl.BlockSpec((rows, tc), lambda i, c: (0, c))

    def body(av, ag, bg, cg, bh, avh, agh, cgh, bhh, caw_ref, cab_ref, lng_ref, lnb_ref, cbw_ref,
             mix_ref, u1_ref, bufa, bufb):
        first = pl.program_id(0) == 0
        bufa[pl.ds(0, HALO), :] = jnp.where(first, 0.0, avh[...].astype(F32) * _sigmoid(agh[...].astype(F32)))
        bufb[pl.ds(0, HALO), :] = jnp.where(first, 0.0, cgh[...].astype(F32) * bhh[...].astype(F32))
        for cs in _col_chunks(tc):
            for r0, nr in _row_chunks(tm):
                rows = pl.ds(r0, nr)
                bufa[pl.ds(HALO + r0, nr), cs] = av[rows, cs].astype(F32) * _sigmoid(ag[rows, cs].astype(F32))
                bufb[pl.ds(HALO + r0, nr), cs] = cg[rows, cs].astype(F32) * bh[rows, cs].astype(F32)
        for cs in _col_chunks(tc):
            for r0, nr in _row_chunks(tm):
                rows = pl.ds(r0, nr)
                u1 = _conv_fwd(bufa, caw_ref, KA, r0, nr, cs) + cab_ref[:, cs]
                u1_ref[rows, cs] = u1
                y, _ = _group_norm(u1)
                z = y * lng_ref[:, cs] + lnb_ref[:, cs]
                mix_ref[0, rows, cs] = (z * _sigmoid(z)).astype(BF16)
                mix_ref[1, rows, cs] = (bg[rows, cs].astype(F32) * _conv_fwd(bufb, cbw_ref, KB, r0, nr, cs)).astype(BF16)

    in_specs = [sec(0), sec(1), sec(2), sec(3), sec(4), sec_prev(0), sec_prev(1), sec_prev(3), sec_prev(4),
                chan(KA), chan(1), chan(1), chan(1), chan(KB)]
    operands = [proj] * 9 + [caw, cab, lng, lnb, cbw]
    return _call(body, "mixer_fwd", (T // tm, nc), in_specs,
                 (pl.BlockSpec((2, tm, tc), lambda i, c: (0, i, c)), pl.BlockSpec((tm, tc), lambda i, c: (i, c))),
                 (jax.ShapeDtypeStruct((2, T, CW), BF16), jax.ShapeDtypeStruct((T, CW), F32)), operands,
                 scratch=[pltpu.VMEM((HALO + tm, tc), F32), pltpu.VMEM((HALO + tm, tc), F32)],
                 sem=("parallel", "parallel"))


def _mixer_bwd1(dmix, proj, u1, caw, lng, lnb, cbw, T, CW):
    KA, KB = caw.shape[0], cbw.shape[0]
    tm, tc, hb, nc = _mixer_tiles(T, CW)

    def sec(s):
        return pl.BlockSpec((tm, tc), lambda c, i: (i, s * nc + c))

    def sec_prev(s):
        return pl.BlockSpec((HALO, tc), lambda c, i: (jnp.maximum(i * hb - 1, 0), s * nc + c))

    def chan(rows):
        return pl.BlockSpec((rows, tc), lambda c, i: (0, c))

    tile = pl.BlockSpec((tm, tc), lambda c, i: (i, c))

    def body(du, dv, u1_ref, av, ag, bg, cg, bh, avh, agh, cgh, bhh, lng_ref, lnb_ref, cbw_ref,
             du1_ref, dcv_ref, dbg_ref, dcaw_ref, dcab_ref, dlng_ref, dlnb_ref, dcbw_ref, bufa, bufb):
        first = pl.program_id(1) == 0

        @pl.when(first)
        def _():
            for r in (dcaw_ref, dcab_ref, dlng_ref, dlnb_ref, dcbw_ref):
                r[...] = jnp.zeros_like(r)

        bufa[pl.ds(0, HALO), :] = jnp.where(first, 0.0, avh[...].astype(F32) * _sigmoid(agh[...].astype(F32)))
        bufb[pl.ds(0, HALO), :] = jnp.where(first, 0.0, cgh[...].astype(F32) * bhh[...].astype(F32))
        for cs in _col_chunks(tc):
            for r0, nr in _row_chunks(tm):
                rows = pl.ds(r0, nr)
                bufa[pl.ds(HALO + r0, nr), cs] = av[rows, cs].astype(F32) * _sigmoid(ag[rows, cs].astype(F32))
                bufb[pl.ds(HALO + r0, nr), cs] = cg[rows, cs].astype(F32) * bh[rows, cs].astype(F32)
        for cs in _col_chunks(tc):
            lanes = cs.stop - cs.start
            zero = jnp.zeros((SUBLANES, lanes), F32)
            a_lng, a_lnb, a_cab = zero, zero, zero
            a_caw, a_cbw = [zero] * KA, [zero] * KB
            gamma, beta = lng_ref[:, cs], lnb_ref[:, cs]
            for r0, nr in _row_chunks(tm):
                rows = pl.ds(r0, nr)
                y, rstd = _group_norm(u1_ref[rows, cs])
                z = y * gamma + beta
                dz = du[rows, cs] * _silu_grad(z, _sigmoid(z))
                a_lng = a_lng + _fold_rows(dz * y)
                a_lnb = a_lnb + _fold_rows(dz)
                dy = dz * gamma
                du1 = rstd * (dy - jnp.mean(dy, axis=-1, keepdims=True)
                              - y * jnp.mean(dy * y, axis=-1, keepdims=True))
                du1_ref[rows, cs] = du1
                a_cab = a_cab + _fold_rows(du1)
                a_caw = _conv_bwd_weight(a_caw, du1, bufa, KA, r0, nr, cs)

                dvv = dv[rows, cs]
                dbg_ref[rows, cs] = (dvv * _conv_fwd(bufb, cbw_ref, KB, r0, nr, cs)).astype(BF16)
                dcv = dvv * bg[rows, cs].astype(F32)
                dcv_ref[rows, cs] = dcv
                a_cbw = _conv_bwd_weight(a_cbw, dcv, bufb, KB, r0, nr, cs)
            _add_row(dlng_ref, 0, cs, a_lng)
            _add_row(dlnb_ref, 0, cs, a_lnb)
            _add_row(dcab_ref, 0, cs, a_cab)
            for k in range(KA):
                _add_row(dcaw_ref, k, cs, a_caw[k])
            for k in range(KB):
                _add_row(dcbw_ref, k, cs, a_cbw[k])

    in_specs = [sec(0), sec(1), tile, sec(0), sec(1), sec(2), sec(3), sec(4),
                sec_prev(0), sec_prev(1), sec_prev(3), sec_prev(4), chan(1), chan(1), chan(KB)]
    operands = [dmix, dmix, u1] + [proj] * 9 + [lng, lnb, cbw]
    return _call(body, "mixer_bwd1", (nc, T // tm), in_specs,
                 (tile, tile, tile, chan(KA), chan(1), chan(1), chan(1), chan(KB)),
                 (jax.ShapeDtypeStruct((T, CW), F32), jax.ShapeDtypeStruct((T, CW), F32),
                  jax.ShapeDtypeStruct((T, CW), BF16), jax.ShapeDtypeStruct((KA, CW), F32),
                  jax.ShapeDtypeStruct((1, CW), F32), jax.ShapeDtypeStruct((1, CW), F32),
                  jax.ShapeDtypeStruct((1, CW), F32), jax.ShapeDtypeStruct((KB, CW), F32)), operands,
                 scratch=[pltpu.VMEM((HALO + tm, tc), F32), pltpu.VMEM((HALO + tm, tc), F32)],
                 sem=("parallel", "arbitrary"))


def _mixer_bwd2(du1, dcv, proj, caw, cbw, T, CW):
    KA, KB = caw.shape[0], cbw.shape[0]
    tm, tc, hb, nc = _mixer_tiles(T, CW)
    n_i = T // tm

    def sec(s):
        return pl.BlockSpec((tm, tc), lambda i, c: (i, s * nc + c))

    def chan(rows):
        return pl.BlockSpec((rows, tc), lambda i, c: (0, c))

    tile = pl.BlockSpec((tm, tc), lambda i, c: (i, c))
    nxt = pl.BlockSpec((HALO, tc), lambda i, c: (jnp.minimum((i + 1) * hb, n_i * hb - 1), c))

    def body(du1_ref, du1n, dcv_ref, dcvn, av, ag, cg, bh, caw_ref, cbw_ref, dav, dag, dcg, dbh, bufa, bufb):
        last = pl.program_id(0) == n_i - 1
        bufa[pl.ds(0, tm), :] = du1_ref[...]
        bufa[pl.ds(tm, HALO), :] = jnp.where(last, 0.0, du1n[...])
        bufb[pl.ds(0, tm), :] = dcv_ref[...]
        bufb[pl.ds(tm, HALO), :] = jnp.where(last, 0.0, dcvn[...])
        for cs in _col_chunks(tc):
            for r0, nr in _row_chunks(tm):
                rows = pl.ds(r0, nr)
                du0 = _conv_bwd_input(bufa, caw_ref, KA, r0, nr, cs)
                sig = _sigmoid(ag[rows, cs].astype(F32))
                dav[rows, cs] = (du0 * sig).astype(BF16)
                dag[rows, cs] = (du0 * av[rows, cs].astype(F32) * (sig * (1.0 - sig))).astype(BF16)
                dch = _conv_bwd_input(bufb, cbw_ref, KB, r0, nr, cs)
                dcg[rows, cs] = (dch * bh[rows, cs].astype(F32)).astype(BF16)
                dbh[rows, cs] = (dch * cg[rows, cs].astype(F32)).astype(BF16)

    in_specs = [tile, nxt, tile, nxt, sec(0), sec(1), sec(3), sec(4), chan(KA), chan(KB)]
    operands = [du1, du1, dcv, dcv, proj, proj, proj, proj, caw, cbw]
    out = jax.ShapeDtypeStruct((T, CW), BF16)
    return _call(body, "mixer_bwd2", (n_i, nc), in_specs, (tile, tile, tile, tile), (out, out, out, out),
                 operands, scratch=[pltpu.VMEM((HALO + tm, tc), F32), pltpu.VMEM((HALO + tm, tc), F32)],
                 sem=("parallel", "parallel"))


def _ffn_tiles(T):
    tm = _tile(T, 512, HALO)
    return tm, tm // HALO, T // tm


def _ffn_act_fwd(gpre, up, cfw):
    nb, T, F = gpre.shape
    KF = cfw.shape[1]
    tm, hb, n_i = _ffn_tiles(T)
    tile = pl.BlockSpec((None, tm, F), lambda b, i: (b, i, 0))
    prev = pl.BlockSpec((None, HALO, F), lambda b, i: (b, jnp.maximum(i * hb - 1, 0), 0))
    wspec = pl.BlockSpec((None, KF, F), lambda b, i: (b, 0, 0))

    def body(g_ref, gh_ref, up_ref, w_ref, f_ref, buf):
        buf[pl.ds(HALO, tm), :] = g_ref[...].astype(F32)
        buf[pl.ds(0, HALO), :] = jnp.where(pl.program_id(1) == 0, 0.0, gh_ref[...].astype(F32))
        for cs in _col_chunks(F):
            for r0, nr in _row_chunks(tm):
                rows = pl.ds(r0, nr)
                g = _conv_fwd(buf, w_ref, KF, r0, nr, cs)
                f_ref[rows, cs] = (g * _sigmoid(g) * up_ref[rows, cs].astype(F32)).astype(BF16)

    return _call(body, "ffn_act_fwd", (nb, n_i), [tile, prev, tile, wspec], tile,
                 jax.ShapeDtypeStruct((nb, T, F), BF16), [gpre, gpre, up, cfw],
                 scratch=[pltpu.VMEM((HALO + tm, F), F32)], sem=("parallel", "parallel"))


def _ffn_act_bwd1(df, gpre, up, cfw):
    nb, T, F = gpre.shape
    KF = cfw.shape[1]
    tm, hb, n_i = _ffn_tiles(T)
    tile = pl.BlockSpec((None, tm, F), lambda b, i: (b, i, 0))
    prev = pl.BlockSpec((None, HALO, F), lambda b, i: (b, jnp.maximum(i * hb - 1, 0), 0))
    wspec = pl.BlockSpec((None, KF, F), lambda b, i: (b, 0, 0))

    def body(df_ref, g_ref, gh_ref, up_ref, w_ref, dg_ref, dup_ref, dw_ref, buf):
        first = pl.program_id(1) == 0

        @pl.when(first)
        def _():
            dw_ref[...] = jnp.zeros_like(dw_ref)

        buf[pl.ds(HALO, tm), :] = g_ref[...].astype(F32)
        buf[pl.ds(0, HALO), :] = jnp.where(first, 0.0, gh_ref[...].astype(F32))
        for cs in _col_chunks(F):
            accs = [jnp.zeros((SUBLANES, cs.stop - cs.start), F32)] * KF
            for r0, nr in _row_chunks(tm):
                rows = pl.ds(r0, nr)
                g = _conv_fwd(buf, w_ref, KF, r0, nr, cs)
                sig = _sigmoid(g)
                dfv = df_ref[rows, cs].astype(F32)
                dup_ref[rows, cs] = (dfv * (g * sig)).astype(BF16)
                dg = dfv * up_ref[rows, cs].astype(F32) * _silu_grad(g, sig)
                dg_ref[rows, cs] = dg.astype(BF16)
                accs = _conv_bwd_weight(accs, dg, buf, KF, r0, nr, cs)
            for k in range(KF):
                _add_row(dw_ref, k, cs, accs[k])

    return _call(body, "ffn_act_bwd1", (nb, n_i), [tile, tile, prev, tile, wspec], (tile, tile, wspec),
                 (jax.ShapeDtypeStruct((nb, T, F), BF16), jax.ShapeDtypeStruct((nb, T, F), BF16),
                  jax.ShapeDtypeStruct((nb, KF, F), F32)), [df, gpre, gpre, up, cfw],
                 scratch=[pltpu.VMEM((HALO + tm, F), F32)], sem=("parallel", "arbitrary"))


def _ffn_act_bwd2(dg, cfw):
    nb, T, F = dg.shape
    KF = cfw.shape[1]
    tm, hb, n_i = _ffn_tiles(T)
    tile = pl.BlockSpec((None, tm, F), lambda b, i: (b, i, 0))
    nxt = pl.BlockSpec((None, HALO, F), lambda b, i: (b, jnp.minimum((i + 1) * hb, n_i * hb - 1), 0))
    wspec = pl.BlockSpec((None, KF, F), lambda b, i: (b, 0, 0))

    def body(dg_ref, dgn_ref, w_ref, o_ref, buf):
        buf[pl.ds(0, tm), :] = dg_ref[...].astype(F32)
        buf[pl.ds(tm, HALO), :] = jnp.where(pl.program_id(1) == n_i - 1, 0.0, dgn_ref[...].astype(F32))
        for cs in _col_chunks(F):
            for r0, nr in _row_chunks(tm):
                o_ref[pl.ds(r0, nr), cs] = _conv_bwd_input(buf, w_ref, KF, r0, nr, cs).astype(BF16)

    return _call(body, "ffn_act_bwd2", (nb, n_i), [tile, nxt, wspec], tile,
                 jax.ShapeDtypeStruct((nb, T, F), BF16), [dg, dg, cfw],
                 scratch=[pltpu.VMEM((HALO + tm, F), F32)], sem=("parallel", "parallel"))


def _softmax_rows(s):
    e = jnp.exp(s - jnp.max(s, axis=-1, keepdims=True))
    return e / jnp.sum(e, axis=-1, keepdims=True)


def _attn_fwd(q, k, v):
    T, D = q.shape
    Mm = k.shape[0]
    hd = D // N_XATTN_HEADS
    scale = hd ** -0.5
    tm = _tile(T, 256, 16)

    def body(q_ref, k_ref, v_ref, o_ref):
        for h in range(N_XATTN_HEADS):
            sl = slice(h * hd, (h + 1) * hd)
            s = lax.dot_general(q_ref[:, sl], k_ref[:, sl], _DOT_DIMS["nt"], preferred_element_type=F32) * scale
            p = _softmax_rows(s).astype(BF16)
            o_ref[:, sl] = jnp.dot(p, v_ref[:, sl], preferred_element_type=F32).astype(BF16)

    row = pl.BlockSpec((tm, D), lambda i: (i, 0))
    full = pl.BlockSpec((Mm, D), lambda i: (0, 0))
    return _call(body, "attn_fwd", (T // tm,), [row, full, full], row, jax.ShapeDtypeStruct((T, D), BF16),
                 [q, k, v], sem=("parallel",))


def _attn_bwd(q, k, v, do):
    T, D = q.shape
    Mm = k.shape[0]
    hd = D // N_XATTN_HEADS
    scale = hd ** -0.5
    tm = _tile(T, 256, 16)
    n_i = T // tm

    def body(q_ref, do_ref, k_ref, v_ref, dq_ref, dk_ref, dv_ref, dk_acc, dv_acc):
        @pl.when(pl.program_id(0) == 0)
        def _():
            dk_acc[...] = jnp.zeros_like(dk_acc)
            dv_acc[...] = jnp.zeros_like(dv_acc)

        for h in range(N_XATTN_HEADS):
            sl = slice(h * hd, (h + 1) * hd)
            qh, kh, doh = q_ref[:, sl], k_ref[:, sl], do_ref[:, sl]
            s = lax.dot_general(qh, kh, _DOT_DIMS["nt"], preferred_element_type=F32) * scale
            p = _softmax_rows(s)
            dv_acc[:, sl] += lax.dot_general(p.astype(BF16), doh, _DOT_DIMS["tn"], preferred_element_type=F32)
            dp = lax.dot_general(doh, v_ref[:, sl], _DOT_DIMS["nt"], preferred_element_type=F32)
            ds = (p * (dp - jnp.sum(dp * p, axis=-1, keepdims=True)) * scale).astype(BF16)
            dq_ref[:, sl] = jnp.dot(ds, kh, preferred_element_type=F32).astype(BF16)
            dk_acc[:, sl] += lax.dot_general(ds, qh, _DOT_DIMS["tn"], preferred_element_type=F32)

        @pl.when(pl.program_id(0) == n_i - 1)
        def _():
            dk_ref[...] = dk_acc[...].astype(BF16)
            dv_ref[...] = dv_acc[...].astype(BF16)

    row = pl.BlockSpec((tm, D), lambda i: (i, 0))
    full = pl.BlockSpec((Mm, D), lambda i: (0, 0))
    return _call(body, "attn_bwd", (n_i,), [row, row, full, full], (row, full, full),
                 (jax.ShapeDtypeStruct((T, D), BF16), jax.ShapeDtypeStruct((Mm, D), BF16),
                  jax.ShapeDtypeStruct((Mm, D), BF16)), [q, do, k, v],
                 scratch=[pltpu.VMEM((Mm, D), F32), pltpu.VMEM((Mm, D), F32)])


def _position():
    x, y, c = lax.axis_index("x"), lax.axis_index("y"), lax.axis_index("c")
    return x, y, c


def _peer(pos, k):
    x, y, c = pos
    return (1 - x if k & 4 else x, 1 - y if k & 2 else y, 1 - c if k & 1 else c)


def _index(pos):
    x, y, c = pos
    return 4 * x + 2 * y + c


def _sequencer_kernel(body, name, collective_id, out_type, operands):
    return pl.kernel(
        body, name=name, out_type=out_type,
        mesh=plsc.ScalarSubcoreMesh(axis_name="sequencer", num_cores=1),
        scratch_types=[pltpu.SemaphoreType.DMA, pltpu.SemaphoreType.DMA((7,)), pltpu.SemaphoreType.DMA],
        compiler_params=pltpu.CompilerParams(collective_id=collective_id),
    )(*operands)


def _handshake(peers):
    barrier = pltpu.get_barrier_semaphore()
    for peer in peers:
        pl.semaphore_signal(barrier, inc=1, device_id=peer, device_id_type=MESH)
    pl.semaphore_wait(barrier, len(peers))


def _sequencer_all_gather(name, collective_id, shards):
    n = len(shards)

    def body(*refs):
        x_refs, out_refs = refs[:n], refs[n:2 * n]
        send_sem, recv_sems, local_sem = refs[2 * n:]
        me = _position()
        sibling = _peer(me, 1)
        chips = [_peer(me, 4), _peer(me, 2), _peer(me, 6)]
        _handshake([sibling] + chips)

        def copy(a, k, block, to, own=False):
            dst = out_refs[a].at[_index(block)]
            return pltpu.make_async_remote_copy(
                src_ref=x_refs[a] if own else dst, dst_ref=dst, send_sem=send_sem, recv_sem=recv_sems.at[k],
                device_id=to, device_id_type=MESH)

        local = [pltpu.make_async_copy(x_refs[a], out_refs[a].at[_index(me)], local_sem) for a in range(n)]
        started = [copy(a, 1 + j, me, chip, own=True) for a in range(n) for j, chip in enumerate(chips)]
        started += [copy(a, 0, me, sibling, own=True) for a in range(n)]
        for cp in started + local:
            cp.start()
        for j, chip in enumerate(chips):
            for a in range(n):
                copy(a, 1 + j, chip, me).wait_recv()
            passed = [copy(a, 4 + j, chip, sibling) for a in range(n)]
            for cp in passed:
                cp.start()
            started += passed
        for a in range(n):
            copy(a, 0, sibling, me).wait_recv()
        for j, chip in enumerate(chips):
            for a in range(n):
                copy(a, 4 + j, _peer(chip, 1), me).wait_recv()
        for cp in started:
            cp.wait_send()
        for cp in local:
            cp.wait()

    return _sequencer_kernel(body, name, collective_id,
                             [jax.ShapeDtypeStruct((N_DEV,) + s.shape, s.dtype) for s in shards], shards)


def _chip_index(pos):
    return 2 * pos[0] + pos[1]


def _sequencer_to_sibling(name, collective_id, parts):
    n = len(parts)

    def body(*refs):
        p_refs, out_refs = refs[:n], refs[n:2 * n]
        send_sem, recv_sems, _ = refs[2 * n:]
        me = _position()
        sibling = _peer(me, 1)
        _handshake([sibling])
        copies = [pltpu.make_async_remote_copy(
            src_ref=p_refs[a].at[2 * q + sibling[2]], dst_ref=out_refs[a].at[q], send_sem=send_sem,
            recv_sem=recv_sems.at[0], device_id=sibling, device_id_type=MESH)
            for a in range(n) for q in range(N_DEV // 2)]
        for cp in copies:
            cp.start()
        for cp in copies:
            cp.wait_recv()
        for cp in copies:
            cp.wait_send()

    return _sequencer_kernel(body, name, collective_id,
                             [jax.ShapeDtypeStruct((N_DEV // 2,) + p.shape[1:], p.dtype) for p in parts], parts)


def _sequencer_to_chips(name, collective_id, sums):
    n = len(sums)

    def body(*refs):
        s_refs, out_refs = refs[:n], refs[n:2 * n]
        send_sem, recv_sems, local_sem = refs[2 * n:]
        me = _position()
        my_chip = _chip_index(me)
        peers = [_peer(me, 4), _peer(me, 2), _peer(me, 6)]
        _handshake(peers)
        local = [pltpu.make_async_copy(s_refs[a].at[my_chip], out_refs[a].at[my_chip], local_sem) for a in range(n)]
        sends = [pltpu.make_async_remote_copy(
            src_ref=s_refs[a].at[_chip_index(peer)], dst_ref=out_refs[a].at[my_chip], send_sem=send_sem,
            recv_sem=recv_sems.at[1 + j], device_id=peer, device_id_type=MESH)
            for a in range(n) for j, peer in enumerate(peers)]
        for cp in sends + local:
            cp.start()
        for j, peer in enumerate(peers):
            for a in range(n):
                pltpu.make_async_remote_copy(
                    src_ref=s_refs[a].at[my_chip], dst_ref=out_refs[a].at[_chip_index(peer)], send_sem=send_sem,
                    recv_sem=recv_sems.at[1 + j], device_id=peer, device_id_type=MESH).wait_recv()
        for cp in sends:
            cp.wait_send()
        for cp in local:
            cp.wait()

    return _sequencer_kernel(body, name, collective_id,
                             [jax.ShapeDtypeStruct(s.shape, s.dtype) for s in sums], sums)


def _chip_sum(name, parts, got, after=()):
    _, R, C = parts.shape
    tr = _tile(R, max(8, (3 << 20) // C), 16)
    n_after = len(after)
    limit = 6 * _nbytes((tr, C), parts.dtype) + VMEM_TEMP_ALLOWANCE

    def body(c_ref, p_ref, g_ref, *rest):
        o_ref = rest[n_after]
        o_ref[...] = (p_ref[...].astype(F32) + g_ref[...].astype(F32)).astype(o_ref.dtype)

    blk = pl.BlockSpec((None, tr, C), lambda q, i, c_ref: (q, i, 0))
    mine = pl.BlockSpec((None, tr, C), lambda q, i, c_ref: (2 * q + c_ref[0], i, 0))
    core = lax.axis_index("c").astype(jnp.int32).reshape(1)
    return pl.pallas_call(
        body, name=name, out_shape=jax.ShapeDtypeStruct((N_DEV // 2, R, C), parts.dtype),
        grid_spec=pltpu.PrefetchScalarGridSpec(
            num_scalar_prefetch=1, grid=(N_DEV // 2, R // tr),
            in_specs=[mine, blk] + [pl.BlockSpec(memory_space=pl.ANY)] * n_after, out_specs=blk),
        compiler_params=pltpu.CompilerParams(dimension_semantics=("parallel", "parallel"),
                                             vmem_limit_bytes=int(limit)),
    )(core, parts, got, *after)


def _all_reduce_rows(name, v):
    R, C = v.shape

    def body(v_ref, out_ref, gath, send_sems, recv_sems):
        me = _position()
        gath[_index(me)] = v_ref[...]
        sends = []
        for k in range(1, N_DEV):
            peer = _peer(me, k)
            sends.append(pltpu.make_async_remote_copy(
                src_ref=v_ref, dst_ref=gath.at[_index(me)], send_sem=send_sems.at[k - 1],
                recv_sem=recv_sems.at[k - 1], device_id=peer, device_id_type=MESH))
        for cp in sends:
            cp.start()
        for k in range(1, N_DEV):
            peer = _peer(me, k)
            pltpu.make_async_remote_copy(
                src_ref=v_ref, dst_ref=gath.at[_index(peer)], send_sem=send_sems.at[k - 1],
                recv_sem=recv_sems.at[k - 1], device_id=peer, device_id_type=MESH).wait_recv()
        for cp in sends:
            cp.wait_send()
        tot = gath[0]
        for s in range(1, N_DEV):
            tot = tot + gath[s]
        out_ref[...] = tot

    return pl.pallas_call(
        body, name=name, out_shape=jax.ShapeDtypeStruct((R, C), F32),
        in_specs=[VMEM_SPEC], out_specs=VMEM_SPEC,
        scratch_shapes=[pltpu.VMEM((N_DEV, R, C), F32), pltpu.SemaphoreType.DMA((7,)),
                        pltpu.SemaphoreType.DMA((7,))],
    )(v)


def _adamw_math(g, w, m, v):
    m = ADAM_B1 * m + (1.0 - ADAM_B1) * g
    v = ADAM_B2 * v + (1.0 - ADAM_B2) * (g * g)
    m_hat = m / (1.0 - ADAM_B1 ** ADAM_STEP)
    v_hat = v / (1.0 - ADAM_B2 ** ADAM_STEP)
    delta = -ADAM_LR * (m_hat / (jnp.sqrt(v_hat) + ADAM_EPS) + ADAM_WD * w)
    return delta, m, v


def _adamw(name, parts, w, m, v, after=()):
    n, R, C = parts.shape
    tr = _tile(R, max(8, (1 << 18) // C), 16)

    def body(p_ref, w_ref, m_ref, v_ref, g_ref, d_ref, nm_ref, nv_ref):
        g = p_ref[0].astype(F32)
        for s in range(1, n):
            g = g + p_ref[s].astype(F32)
        g_ref[...] = g
        d_ref[...], nm_ref[...], nv_ref[...] = _adamw_math(g, w_ref[...], m_ref[...], v_ref[...])

    blk = pl.BlockSpec((tr, C), lambda i: (i, 0))
    out = jax.ShapeDtypeStruct((R, C), F32)
    return _call(body, name, (R // tr,), [pl.BlockSpec((n, tr, C), lambda i: (0, i, 0)), blk, blk, blk],
                 (blk, blk, blk, blk), (out, out, out, out), [parts, w, m, v], sem=("parallel",), after=after)


def kernel(x, mem, g_mix, w_in, conv_a_w, conv_a_b, ln_a_g, ln_a_b, conv_b_w, w_out, g_xattn, g_mem, w_q, w_k, w_v, w_o, g_ffn, w_gate, w_up, conv_f_w, w_down, g_final, loss_target, m_g_mix, m_w_in, m_conv_a_w, m_conv_a_b, m_ln_a_g, m_ln_a_b, m_conv_b_w, m_w_out, m_g_xattn, m_g_mem, m_w_q, m_w_k, m_w_v, m_w_o, m_g_ffn, m_w_gate, m_w_up, m_conv_f_w, m_w_down, m_g_final, v_g_mix, v_w_in, v_conv_a_w, v_conv_a_b, v_ln_a_g, v_ln_a_b, v_conv_b_w, v_w_out, v_g_xattn, v_g_mem, v_w_q, v_w_k, v_w_v, v_w_o, v_g_ffn, v_w_gate, v_w_up, v_conv_f_w, v_w_down, v_g_final):
    T, D = x.shape[1], x.shape[2]
    Mm = mem.shape[1]
    CW = conv_a_b.shape[1]
    INB = w_in.shape[2]
    FB = w_gate.shape[2]
    KA, KB, KF = conv_a_w.shape[1], conv_b_w.shape[1], conv_f_w.shape[1]
    DB = D // N_DEV
    assert 5 * CW == N_DEV * INB and 2 * CW == D

    x2, mem2, tgt = x[0], mem[0], loss_target[0]
    g_mem2, g_final2 = g_mem.reshape(1, D), g_final.reshape(1, D)

    def bf16(name, w):
        return _cast_bf16("cast_" + name, w[0])

    Win, caw, cbw = _sequencer_all_gather(
        "ag_in", AG_ID, [bf16("w_in", w_in), conv_a_w[0], conv_b_w[0]])
    Wout, = _sequencer_all_gather("ag_out", AG_ID, [bf16("w_out", w_out)])
    Wq, Wk, Wv, Wo = _sequencer_all_gather(
        "ag_attn", AG_ID, [bf16("w_q", w_q), bf16("w_k", w_k), bf16("w_v", w_v), bf16("w_o", w_o)])
    Wgate, cfw = _sequencer_all_gather("ag_gate", AG_ID, [bf16("w_gate", w_gate), conv_f_w[0]])
    Wup, = _sequencer_all_gather("ag_up", AG_ID, [bf16("w_up", w_up)])
    Wdown, = _sequencer_all_gather("ag_down", AG_ID, [bf16("w_down", w_down)])
    Wout, Wq, Wk, Wv, Wo = [w.reshape(D, D) for w in (Wout, Wq, Wk, Wv, Wo)]
    caw = jnp.transpose(caw, (1, 0, 2)).reshape(KA, CW)
    cbw = jnp.transpose(cbw, (1, 0, 2)).reshape(KB, CW)

    xn1 = _rms_fwd("rms_mix", x2, g_mix)
    proj = _matmul("mm_proj", "nn", xn1, Win, M=T, N=INB, K=D, nb=N_DEV, b_lay="blk", o_lay="col", tn=INB,
                   out_dtype=BF16)
    mix, u1 = _mixer_fwd(proj, caw, conv_a_b, ln_a_g, ln_a_b, cbw, T, CW)
    h1 = _matmul("mm_h1", "nn", mix, Wout.reshape(2, CW, D), M=T, N=D, K=CW, nb=2, a_lay="blk", b_lay="blk",
                 red_block=True, res=x2)
    xn2 = _rms_fwd("rms_xattn", h1, g_xattn)
    q = _matmul("mm_q", "nn", xn2, Wq, M=T, N=D, K=D, out_dtype=BF16)
    memn = _rms_fwd("rms_mem", mem2, g_mem2, after=[q])
    kk = _matmul("mm_k", "nn", memn, Wk, M=Mm, N=D, K=D, out_dtype=BF16)
    vv = _matmul("mm_v", "nn", memn, Wv, M=Mm, N=D, K=D, out_dtype=BF16)
    o = _attn_fwd(q, kk, vv)
    h2 = _matmul("mm_h2", "nn", o, Wo, M=T, N=D, K=D, res=h1)
    xn3 = _rms_fwd("rms_ffn", h2, g_ffn)
    gpre = _matmul("mm_gate", "nn", xn3, Wgate, M=T, N=FB, K=D, nb=N_DEV, b_lay="blk", o_lay="blk", tn=FB,
                   out_dtype=BF16)
    up = _matmul("mm_up", "nn", xn3, Wup, M=T, N=FB, K=D, nb=N_DEV, b_lay="blk", o_lay="blk", tn=FB,
                 out_dtype=BF16)
    f = _ffn_act_fwd(gpre, up, cfw)
    h3 = _matmul("mm_h3", "nn", f, Wdown, M=T, N=D, K=FB, nb=N_DEV, a_lay="blk", b_lay="blk", red_block=True,
                 res=h2, tk=FB)
    dh3, dh3b, loss_part, dg_final = _loss_head(h3, tgt, g_final2)

    wmv = {"w_in": (w_in, m_w_in, v_w_in), "conv_a_w": (conv_a_w, m_conv_a_w, v_conv_a_w),
           "conv_b_w": (conv_b_w, m_conv_b_w, v_conv_b_w), "w_out": (w_out, m_w_out, v_w_out),
           "w_q": (w_q, m_w_q, v_w_q), "w_k": (w_k, m_w_k, v_w_k), "w_v": (w_v, m_w_v, v_w_v),
           "w_o": (w_o, m_w_o, v_w_o), "w_gate": (w_gate, m_w_gate, v_w_gate), "w_up": (w_up, m_w_up, v_w_up),
           "conv_f_w": (conv_f_w, m_conv_f_w, v_conv_f_w), "w_down": (w_down, m_w_down, v_w_down)}
    res = {}
    pending = []

    def mm(*args, after=(), **kwargs):
        behind = list(after) + pending
        pending.clear()
        return _matmul(*args, after=behind, **kwargs)

    def to_sibling(tag, named_parts):
        got = _sequencer_to_sibling("rs1_" + tag, SIBLING_ID, [p for _, p in named_parts])
        return named_parts, got

    def to_chips(tag, stage1, after):
        named_parts, got = stage1
        sums = [_chip_sum("sum_" + n, p, g, after=after) for (n, p), g in zip(named_parts, got)]
        pending.extend(sums)
        return [n for n, _ in named_parts], _sequencer_to_chips("rs2_" + tag, CHIPS_ID, sums)

    def finish(stage2, after):
        names, got = stage2
        for n, g in zip(names, got):
            w, m, v = wmv[n]
            res[n] = _adamw("adamw_" + n, g, w[0], m[0], v[0], after=after)
            pending.append(res[n][0])

    def row_blocks(dw):
        return dw.reshape(N_DEV, DB, D)

    def conv_blocks(dw, K):
        return jnp.transpose(dw.reshape(K, N_DEV, CW // N_DEV), (1, 0, 2))

    dWdown = mm("mm_dw_down", "tn", f, dh3b, M=FB, N=D, K=T, nb=N_DEV, a_lay="blk", o_lay="blk",
                     out_dtype=BF16, tm=FB)
    s_down = to_sibling("down", [("w_down", dWdown)])
    df = mm("mm_df", "nt", dh3b, Wdown, M=T, N=FB, K=D, nb=N_DEV, b_lay="blk", o_lay="blk", tn=FB, out_dtype=BF16,
                 after=[dWdown])
    dg, dup, dcfw = _ffn_act_bwd1(df, gpre, up, cfw)
    dgpre = _ffn_act_bwd2(dg, cfw)
    c_down = to_chips("down", s_down, after=[dgpre])
    dWgate = mm("mm_dw_gate", "tn", xn3, dgpre, M=D, N=FB, K=T, nb=N_DEV, b_lay="blk", o_lay="blk",
                     out_dtype=BF16, tn=FB)
    s_gate = to_sibling("gate", [("w_gate", dWgate), ("conv_f_w", dcfw)])
    dWup = mm("mm_dw_up", "tn", xn3, dup, M=D, N=FB, K=T, nb=N_DEV, b_lay="blk", o_lay="blk",
                   out_dtype=BF16, tn=FB, after=[dWgate])
    s_up = to_sibling("up", [("w_up", dWup)])
    dxn3 = mm("mm_dxn3_gate", "nt", dgpre, Wgate, M=T, N=D, K=FB, nb=N_DEV, a_lay="blk", b_lay="blk",
                   red_block=True, tk=FB, after=[dWup])
    c_gate = to_chips("gate", s_gate, after=[dxn3])
    dxn3 = mm("mm_dxn3_up", "nt", dup, Wup, M=T, N=D, K=FB, nb=N_DEV, a_lay="blk", b_lay="blk",
                   red_block=True, res=dxn3, tk=FB)
    c_up = to_chips("up", s_up, after=[dxn3])
    dh2, dh2b, dg_ffn = _rms_bwd("rms_bwd_ffn", dxn3, h2, g_ffn, dh3)

    dWo = mm("mm_dw_o", "tn", o, dh2b, M=D, N=D, K=T, out_dtype=BF16)
    s_o = to_sibling("o", [("w_o", row_blocks(dWo))])
    do = mm("mm_do", "nt", dh2b, Wo, M=T, N=D, K=D, out_dtype=BF16, after=[dWo])
    finish(c_down, after=[do])
    dq, dk, dv = _attn_bwd(q, kk, vv, do)
    c_o = to_chips("o", s_o, after=[dq])
    dWq = mm("mm_dw_q", "tn", xn2, dq, M=D, N=D, K=T, out_dtype=BF16)
    s_q = to_sibling("q", [("w_q", row_blocks(dWq))])
    dxn2 = mm("mm_dxn2", "nt", dq, Wq, M=T, N=D, K=D, after=[dWq])
    finish(c_gate, after=[dxn2])
    c_q = to_chips("q", s_q, after=[dxn2])
    dh1, dh1b, dg_xattn = _rms_bwd("rms_bwd_xattn", dxn2, h1, g_xattn, dh2)
    dWk = mm("mm_dw_k", "tn", memn, dk, M=D, N=D, K=Mm, out_dtype=BF16, after=[dh1b])
    dWv = mm("mm_dw_v", "tn", memn, dv, M=D, N=D, K=Mm, out_dtype=BF16, after=[dh1b])
    s_kv = to_sibling("kv", [("w_k", row_blocks(dWk)), ("w_v", row_blocks(dWv))])
    dmemn = mm("mm_dmem_k", "nt", dk, Wk, M=Mm, N=D, K=D, after=[dWk, dWv])
    dmemn = mm("mm_dmem_v", "nt", dv, Wv, M=Mm, N=D, K=D, res=dmemn)
    dg_mem = _rms_bwd("rms_bwd_mem", dmemn, mem2, g_mem2)
    finish(c_up, after=[dg_mem])

    dWout = mm("mm_dw_out", "tn", mix, dh1b, M=CW, N=D, K=T, nb=2, a_lay="blk", o_lay="blk", out_dtype=BF16,
                    after=[dg_mem])
    s_out = to_sibling("out", [("w_out", row_blocks(dWout.reshape(D, D)))])
    dmix = mm("mm_dmix", "nt", dh1b, Wout, M=T, N=D, K=D, after=[dWout])
    c_kv = to_chips("kv", s_kv, after=[dmix])
    du1, dcv, dbg, dcaw, dcab, dlng, dlnb, dcbw = _mixer_bwd1(dmix, proj, u1, caw, ln_a_g, ln_a_b, cbw, T, CW)
    c_out = to_chips("out", s_out, after=[du1])
    finish(c_o, after=[du1])
    finish(c_q, after=[du1])
    dav, dag, dcg, dbh = _mixer_bwd2(du1, dcv, proj, caw, cbw, T, CW)
    dproj = jnp.concatenate([dav, dag, dbg, dcg, dbh], axis=1)
    dWin = mm("mm_dw_in", "tn", xn1, dproj, M=D, N=INB, K=T, nb=N_DEV, b_lay="col", o_lay="blk",
                   out_dtype=BF16, tn=INB)
    s_in = to_sibling("in", [("w_in", dWin), ("conv_a_w", conv_blocks(dcaw, KA)),
                             ("conv_b_w", conv_blocks(dcbw, KB))])
    finish(c_kv, after=[dWin])
    c_in = to_chips("in", s_in, after=list(pending))
    dxn1 = mm("mm_dxn1", "nt", dproj, Win, M=T, N=D, K=INB, nb=N_DEV, a_lay="col", b_lay="blk",
                   red_block=True, tk=INB, after=[dWin])
    finish(c_out, after=[dxn1])
    dx, _, dg_mix = _rms_bwd("rms_bwd_mix", dxn1, x2, g_mix, dh1, after=list(pending))

    def pair(a, b):
        return jnp.concatenate([a, b], axis=1)

    zeros_half = jnp.zeros((1, CW), F32)
    small_g = jnp.concatenate([
        dg_mix, pair(dcab, dlng), pair(dlnb, zeros_half), dg_xattn, dg_mem, dg_ffn, dg_final,
        jnp.broadcast_to(loss_part[:, :1], (1, D))], axis=0)
    small_sum = _all_reduce_rows("ar_small", small_g)
    loss = small_sum[7, 0]
    finish(c_in, after=[small_sum])

    def pack(a_mix, a_cab, a_lng, a_lnb, a_xattn, a_mem, a_ffn, a_final):
        return jnp.concatenate([a_mix, pair(a_cab, a_lng), pair(a_lnb, zeros_half), a_xattn, a_mem.reshape(1, D),
                                a_ffn, a_final.reshape(1, D), jnp.zeros((1, D), F32)], axis=0)

    small = _adamw("adamw_small", small_sum[None],
                   pack(g_mix, conv_a_b, ln_a_g, ln_a_b, g_xattn, g_mem, g_ffn, g_final),
                   pack(m_g_mix, m_conv_a_b, m_ln_a_g, m_ln_a_b, m_g_xattn, m_g_mem, m_g_ffn, m_g_final),
                   pack(v_g_mix, v_conv_a_b, v_ln_a_g, v_ln_a_b, v_g_xattn, v_g_mem, v_g_ffn, v_g_final))

    def unpack(a):
        return {"g_mix": a[0:1], "conv_a_b": a[1:2, :CW], "ln_a_g": a[1:2, CW:], "ln_a_b": a[2:3, :CW],
                "g_xattn": a[3:4], "g_mem": a[4], "g_ffn": a[5:6], "g_final": a[6]}

    small = [unpack(a) for a in small]
    order = ["g_mix", "w_in", "conv_a_w", "conv_a_b", "ln_a_g", "ln_a_b", "conv_b_w", "w_out", "g_xattn", "g_mem",
             "w_q", "w_k", "w_v", "w_o", "g_ffn", "w_gate", "w_up", "conv_f_w", "w_down", "g_final"]
    outs = [loss, dx[None]]
    for kind in range(4):
        for n in order:
            outs.append(res[n][kind][None] if n in res else small[kind][n])
    return tuple(outs)
```

```python
import functools

import jax
import jax.numpy as jnp
from jax import lax
from jax.experimental import pallas as pl
from jax.experimental.pallas import tpu as pltpu
from jax.experimental.pallas import tpu_sc as plsc

F32 = jnp.float32
BF16 = jnp.bfloat16

N_DEV = 8
EPS = 1e-6
GROUP_DIM = 128
N_XATTN_HEADS = 4
ADAM_LR = 0.001
ADAM_B1 = 0.9
ADAM_B2 = 0.999
ADAM_EPS = 1e-08
ADAM_WD = 0.01
ADAM_STEP = 10

AG_ID, SIBLING_ID, CHIPS_ID = 1, 2, 3

HALO = 32
VMEM_V7X_BYTES = 64 * 1024 * 1024
VMEM_TEMP_ALLOWANCE = 12 * 1024 * 1024

VMEM_SPEC = pl.BlockSpec(memory_space=pltpu.VMEM)
MESH = pl.DeviceIdType.MESH


def _tile(n, pref, align):
    if n <= pref:
        return n
    t = (pref // align) * align
    while t >= align:
        if n % t == 0:
            return t
        t -= align
    return n


def _nbytes(shape, dtype):
    n = 1
    for d in shape:
        if d is not None:
            n *= d
    return n * jnp.dtype(dtype).itemsize


def _call(body, name, grid, in_specs, out_specs, out_shape, operands, scratch=(), sem=None, after=()):
    outs = out_shape if isinstance(out_shape, (tuple, list)) else (out_shape,)
    ospecs = out_specs if isinstance(out_specs, (tuple, list)) else (out_specs,)
    est = 0
    for spec, arr in list(zip(in_specs, operands)) + list(zip(ospecs, outs)):
        est += 2 * _nbytes(spec.block_shape, arr.dtype)
    for s in scratch:
        if hasattr(s, "shape") and hasattr(s, "dtype"):
            est += _nbytes(s.shape, s.dtype)
    limit = min(est + VMEM_TEMP_ALLOWANCE, VMEM_V7X_BYTES - 4 * 1024 * 1024)
    if sem is None:
        sem = ("arbitrary",) * len(grid)
    n_in, n_after = len(operands), len(after)
    operands = [pltpu.with_memory_space_constraint(o, pltpu.HBM) for o in operands]
    after = [pltpu.with_memory_space_constraint(o, pltpu.HBM) for o in after]
    in_hbm = [pltpu.HBM(o.shape, o.dtype) for o in outs]
    out_shape = in_hbm if isinstance(out_shape, (tuple, list)) else in_hbm[0]

    def ordered_body(*refs):
        body(*refs[:n_in], *refs[n_in + n_after:])

    return pl.pallas_call(
        ordered_body if n_after else body, name=name, grid=grid,
        in_specs=list(in_specs) + [pl.BlockSpec(memory_space=pl.ANY)] * n_after,
        out_specs=out_specs, out_shape=out_shape, scratch_shapes=list(scratch),
        compiler_params=pltpu.CompilerParams(dimension_semantics=sem, vmem_limit_bytes=int(limit)),
    )(*operands, *after)


_DOT_DIMS = {"nn": (((1,), (0,)), ((), ())), "nt": (((1,), (1,)), ((), ())), "tn": (((0,), (0,)), ((), ()))}


def _operand_spec(layout, tr, tc, cols_per_block, pick):
    if layout == "plain":
        return pl.BlockSpec((tr, tc), lambda *g: pick(*g)[1:])
    if layout == "blk":
        return pl.BlockSpec((None, tr, tc), lambda *g: pick(*g))
    assert layout == "col"
    per = cols_per_block // tc
    return pl.BlockSpec((tr, tc), lambda *g: (pick(*g)[1], pick(*g)[0] * per + pick(*g)[2]))


def _matmul(name, dims, a, b, *, M, N, K, nb=1, a_lay="plain", b_lay="plain", o_lay="plain",
            red_block=False, out_dtype=F32, res=None, tm=1024, tn=None, after=()):
    tm = _tile(M, tm, 128 if dims == "tn" else 16)
    tn = _tile(N, tn or (1024 if red_block else 512), 128)
    tk = K
    gi, gj, gk = M // tm, N // tn, K // tk
    if red_block:
        grid = (gi, gj, nb, gk)
        unpack = lambda i, j, bb, k: (bb, i, j, k)
        red_axes, sem = (2, 3), ("parallel", "parallel", "arbitrary", "arbitrary")
    else:
        grid = (nb, gi, gj, gk)
        unpack = lambda bb, i, j, k: (bb, i, j, k)
        red_axes, sem = (3,), ("parallel", "parallel", "parallel", "arbitrary")

    def picker(f):
        return lambda *g: f(*unpack(*g))

    if dims == "tn":
        a_spec = _operand_spec(a_lay, tk, tm, M, picker(lambda bb, i, j, k: (bb, k, i)))
    else:
        a_spec = _operand_spec(a_lay, tm, tk, K, picker(lambda bb, i, j, k: (bb, i, k)))
    if dims == "nt":
        b_spec = _operand_spec(b_lay, tn, tk, K, picker(lambda bb, i, j, k: (bb, j, k)))
    else:
        b_spec = _operand_spec(b_lay, tk, tn, N, picker(lambda bb, i, j, k: (bb, k, j)))
    o_spec = _operand_spec(o_lay, tm, tn, N, picker(lambda bb, i, j, k: (bb, i, j)))
    if o_lay == "plain":
        out_shape = jax.ShapeDtypeStruct((M, N), out_dtype)
    elif o_lay == "blk":
        out_shape = jax.ShapeDtypeStruct((nb, M, N), out_dtype)
    else:
        out_shape = jax.ShapeDtypeStruct((M, nb * N), out_dtype)
    n_red = [grid[ax] for ax in red_axes]
    has_res = res is not None
    one_step = all(n == 1 for n in n_red)

    def body_one_step(*refs):
        a_ref, b_ref = refs[:2]
        o_ref = refs[-1]
        r = lax.dot_general(a_ref[...], b_ref[...], _DOT_DIMS[dims], preferred_element_type=F32)
        if has_res:
            r = r + refs[2][...]
        o_ref[...] = r.astype(o_ref.dtype)

    def body(*refs):
        if has_res:
            a_ref, b_ref, r_ref, o_ref, acc = refs
        else:
            a_ref, b_ref, o_ref, acc = refs
        first = functools.reduce(jnp.logical_and, [pl.program_id(ax) == 0 for ax in red_axes])
        last = functools.reduce(jnp.logical_and, [pl.program_id(ax) == n - 1 for ax, n in zip(red_axes, n_red)])

        @pl.when(first)
        def _():
            acc[...] = jnp.zeros_like(acc)

        acc[...] += lax.dot_general(a_ref[...], b_ref[...], _DOT_DIMS[dims], preferred_element_type=F32)

        @pl.when(last)
        def _():
            r = acc[...]
            if has_res:
                r = r + r_ref[...]
            o_ref[...] = r.astype(o_ref.dtype)

    in_specs = [a_spec, b_spec]
    operands = [a, b]
    if has_res:
        in_specs.append(_operand_spec("plain", tm, tn, N, picker(lambda bb, i, j, k: (bb, i, j))))
        operands.append(res)
    if one_step:
        return _call(body_one_step, name, grid, in_specs, o_spec, out_shape, operands, sem=sem, after=after)
    return _call(body, name, grid, in_specs, o_spec, out_shape, operands,
                 scratch=[pltpu.VMEM((tm, tn), F32)], sem=sem, after=after)


def _cast_bf16(name, w):
    R, C = w.shape
    tr = _tile(R, max(8, (1 << 20) // C), 16)

    def body(w_ref, o_ref):
        o_ref[...] = w_ref[...].astype(BF16)

    return _call(body, name, (R // tr,), [pl.BlockSpec((tr, C), lambda i: (i, 0))],
                 pl.BlockSpec((tr, C), lambda i: (i, 0)), jax.ShapeDtypeStruct((R, C), BF16), [w],
                 sem=("parallel",))


def _rms_fwd(name, x, g, after=()):
    T, D = x.shape
    tm = _tile(T, 128, 16)

    def body(x_ref, g_ref, o_ref):
        xv = x_ref[...]
        r = lax.rsqrt(jnp.mean(xv * xv, axis=-1, keepdims=True) + EPS)
        o_ref[...] = (xv * r * g_ref[...]).astype(BF16)

    return _call(body, name, (T // tm,),
                 [pl.BlockSpec((tm, D), lambda i: (i, 0)), pl.BlockSpec((1, D), lambda i: (0, 0))],
                 pl.BlockSpec((tm, D), lambda i: (i, 0)), jax.ShapeDtypeStruct((T, D), BF16), [x, g],
                 sem=("parallel",), after=after)


def _rms_bwd(name, dxn, x, g, dh=None, after=()):
    T, D = x.shape
    tm = _tile(T, 128, 16)
    with_dx = dh is not None

    def body(*refs):
        if with_dx:
            dxn_ref, x_ref, g_ref, dh_ref, o_ref, ob_ref, dg_ref = refs
        else:
            dxn_ref, x_ref, g_ref, dg_ref = refs
        xv = x_ref[...]
        r = lax.rsqrt(jnp.mean(xv * xv, axis=-1, keepdims=True) + EPS)
        xh = xv * r
        dy = dxn_ref[...]

        @pl.when(pl.program_id(0) == 0)
        def _():
            dg_ref[...] = jnp.zeros_like(dg_ref)

        dg_ref[...] += jnp.sum(dy * xh, axis=0, keepdims=True)
        if with_dx:
            dyg = dy * g_ref[...]
            tot = dh_ref[...] + r * (dyg - xh * jnp.mean(dyg * xh, axis=-1, keepdims=True))
            o_ref[...] = tot
            ob_ref[...] = tot.astype(BF16)

    row = pl.BlockSpec((tm, D), lambda i: (i, 0))
    vec = pl.BlockSpec((1, D), lambda i: (0, 0))
    if with_dx:
        return _call(body, name, (T // tm,), [row, row, vec, row], (row, row, vec),
                     (jax.ShapeDtypeStruct((T, D), F32), jax.ShapeDtypeStruct((T, D), BF16),
                      jax.ShapeDtypeStruct((1, D), F32)), [dxn, x, g, dh], after=after)
    return _call(body, name, (T // tm,), [row, row, vec], vec, jax.ShapeDtypeStruct((1, D), F32), [dxn, x, g])


def _loss_head(h, target, g):
    T, D = h.shape
    tm = _tile(T, 128, 16)

    def body(h_ref, t_ref, g_ref, o_ref, ob_ref, loss_ref, dg_ref):
        xv = h_ref[...]
        gv = g_ref[...]
        r = lax.rsqrt(jnp.mean(xv * xv, axis=-1, keepdims=True) + EPS)
        xh = xv * r
        e = xh * gv - t_ref[...]

        @pl.when(pl.program_id(0) == 0)
        def _():
            dg_ref[...] = jnp.zeros_like(dg_ref)
            loss_ref[...] = jnp.zeros_like(loss_ref)

        loss_ref[...] += 0.5 * jnp.sum(jnp.mean(e * e, axis=-1, keepdims=True), axis=0, keepdims=True)
        dy = e * (1.0 / D)
        dg_ref[...] += jnp.sum(dy * xh, axis=0, keepdims=True)
        dyg = dy * gv
        dx = r * (dyg - xh * jnp.mean(dyg * xh, axis=-1, keepdims=True))
        o_ref[...] = dx
        ob_ref[...] = dx.astype(BF16)

    row = pl.BlockSpec((tm, D), lambda i: (i, 0))
    vec = pl.BlockSpec((1, D), lambda i: (0, 0))
    return _call(body, "loss_head", (T // tm,), [row, row, vec],
                 (row, row, pl.BlockSpec((1, 128), lambda i: (0, 0)), vec),
                 (jax.ShapeDtypeStruct((T, D), F32), jax.ShapeDtypeStruct((T, D), BF16),
                  jax.ShapeDtypeStruct((1, 128), F32), jax.ShapeDtypeStruct((1, D), F32)), [h, target, g])


ROW_CHUNK = 64
SUBLANES = 8


def _col_chunks(width):
    return [slice(c0, min(c0 + GROUP_DIM, width)) for c0 in range(0, width, GROUP_DIM)]


def _row_chunks(n_rows):
    return [(r0, min(ROW_CHUNK, n_rows - r0)) for r0 in range(0, n_rows, ROW_CHUNK)]


def _conv_fwd(buf, w_ref, K, r0, nr, cs):
    base = HALO - (K - 1) + r0
    y = buf[pl.ds(base, nr), cs] * w_ref[pl.ds(0, 1), cs]
    for k in range(1, K):
        y = y + buf[pl.ds(base + k, nr), cs] * w_ref[pl.ds(k, 1), cs]
    return y


def _conv_bwd_input(buf, w_ref, K, r0, nr, cs):
    dx = buf[pl.ds(r0 + K - 1, nr), cs] * w_ref[pl.ds(0, 1), cs]
    for k in range(1, K):
        dx = dx + buf[pl.ds(r0 + K - 1 - k, nr), cs] * w_ref[pl.ds(k, 1), cs]
    return dx


def _fold_rows(v):
    nr, lanes = v.shape
    if nr % SUBLANES:
        return jnp.sum(v, axis=0, keepdims=True)
    return jnp.sum(v.reshape(nr // SUBLANES, SUBLANES, lanes), axis=0)


def _conv_bwd_weight(accs, dy, buf, K, r0, nr, cs):
    base = HALO - (K - 1) + r0
    return [accs[k] + _fold_rows(dy * buf[pl.ds(base + k, nr), cs]) for k in range(K)]


def _add_row(ref, row, cs, acc):
    ref[pl.ds(row, 1), cs] += jnp.sum(acc, axis=0, keepdims=True)


def _sigmoid(z):
    return 1.0 / (1.0 + jnp.exp(-z))


def _silu_grad(z, sig):
    return sig * (1.0 + z * (1.0 - sig))


def _group_norm(xg):
    xc = xg - jnp.mean(xg, axis=-1, keepdims=True)
    rstd = lax.rsqrt(jnp.mean(xc * xc, axis=-1, keepdims=True) + EPS)
    return xc * rstd, rstd


def _mixer_tiles(T, CW):
    tm = _tile(T, 512, HALO)
    tc = _tile(CW, 256, GROUP_DIM)
    return tm, tc, tm // HALO, CW // tc


def _mixer_fwd(proj, caw, cab, lng, lnb, cbw, T, CW):
    KA, KB = caw.shape[0], cbw.shape[0]
    tm, tc, hb, nc = _mixer_tiles(T, CW)

    def sec(s):
        return pl.BlockSpec((tm, tc), lambda i, c: (i, s * nc + c))

    def sec_prev(s):
        return pl.BlockSpec((HALO, tc), lambda i, c: (jnp.maximum(i * hb - 1, 0), s * nc + c))

    def chan(rows):
        return pl.BlockSpec((rows, tc), lambda i, c: (0, c))

    def body(av, ag, bg, cg, bh, avh, agh, cgh, bhh, caw_ref, cab_ref, lng_ref, lnb_ref, cbw_ref,
             mix_ref, u1_ref, bufa, bufb):
        first = pl.program_id(0) == 0
        bufa[pl.ds(0, HALO), :] = jnp.where(first, 0.0, avh[...].astype(F32) * _sigmoid(agh[...].astype(F32)))
        bufb[pl.ds(0, HALO), :] = jnp.where(first, 0.0, cgh[...].astype(F32) * bhh[...].astype(F32))
        for cs in _col_chunks(tc):
            for r0, nr in _row_chunks(tm):
                rows = pl.ds(r0, nr)
                bufa[pl.ds(HALO + r0, nr), cs] = av[rows, cs].astype(F32) * _sigmoid(ag[rows, cs].astype(F32))
                bufb[pl.ds(HALO + r0, nr), cs] = cg[rows, cs].astype(F32) * bh[rows, cs].astype(F32)
        for cs in _col_chunks(tc):
            for r0, nr in _row_chunks(tm):
                rows = pl.ds(r0, nr)
                u1 = _conv_fwd(bufa, caw_ref, KA, r0, nr, cs) + cab_ref[:, cs]
                u1_ref[rows, cs] = u1
                y, _ = _group_norm(u1)
                z = y * lng_ref[:, cs] + lnb_ref[:, cs]
                mix_ref[0, rows, cs] = (z * _sigmoid(z)).astype(BF16)
                mix_ref[1, rows, cs] = (bg[rows, cs].astype(F32) * _conv_fwd(bufb, cbw_ref, KB, r0, nr, cs)).astype(BF16)

    in_specs = [sec(0), sec(1), sec(2), sec(3), sec(4), sec_prev(0), sec_prev(1), sec_prev(3), sec_prev(4),
                chan(KA), chan(1), chan(1), chan(1), chan(KB)]
    operands = [proj] * 9 + [caw, cab, lng, lnb, cbw]
    return _call(body, "mixer_fwd", (T // tm, nc), in_specs,
                 (pl.BlockSpec((2, tm, tc), lambda i, c: (0, i, c)), pl.BlockSpec((tm, tc), lambda i, c: (i, c))),
                 (jax.ShapeDtypeStruct((2, T, CW), BF16), jax.ShapeDtypeStruct((T, CW), F32)), operands,
                 scratch=[pltpu.VMEM((HALO + tm, tc), F32), pltpu.VMEM((HALO + tm, tc), F32)],
                 sem=("parallel", "parallel"))


def _mixer_bwd1(dmix, proj, u1, caw, lng, lnb, cbw, T, CW):
    KA, KB = caw.shape[0], cbw.shape[0]
    tm, tc, hb, nc = _mixer_tiles(T, CW)

    def sec(s):
        return pl.BlockSpec((tm, tc), lambda c, i: (i, s * nc + c))

    def sec_prev(s):
        return pl.BlockSpec((HALO, tc), lambda c, i: (jnp.maximum(i * hb - 1, 0), s * nc + c))

    def chan(rows):
        return pl.BlockSpec((rows, tc), lambda c, i: (0, c))

    tile = pl.BlockSpec((tm, tc), lambda c, i: (i, c))

    def body(du, dv, u1_ref, av, ag, bg, cg, bh, avh, agh, cgh, bhh, lng_ref, lnb_ref, cbw_ref,
             du1_ref, dcv_ref, dbg_ref, dcaw_ref, dcab_ref, dlng_ref, dlnb_ref, dcbw_ref, bufa, bufb):
        first = pl.program_id(1) == 0

        @pl.when(first)
        def _():
            for r in (dcaw_ref, dcab_ref, dlng_ref, dlnb_ref, dcbw_ref):
                r[...] = jnp.zeros_like(r)

        bufa[pl.ds(0, HALO), :] = jnp.where(first, 0.0, avh[...].astype(F32) * _sigmoid(agh[...].astype(F32)))
        bufb[pl.ds(0, HALO), :] = jnp.where(first, 0.0, cgh[...].astype(F32) * bhh[...].astype(F32))
        for cs in _col_chunks(tc):
            for r0, nr in _row_chunks(tm):
                rows = pl.ds(r0, nr)
                bufa[pl.ds(HALO + r0, nr), cs] = av[rows, cs].astype(F32) * _sigmoid(ag[rows, cs].astype(F32))
                bufb[pl.ds(HALO + r0, nr), cs] = cg[rows, cs].astype(F32) * bh[rows, cs].astype(F32)
        for cs in _col_chunks(tc):
            lanes = cs.stop - cs.start
            zero = jnp.zeros((SUBLANES, lanes), F32)
            a_lng, a_lnb, a_cab = zero, zero, zero
            a_caw, a_cbw = [zero] * KA, [zero] * KB
            gamma, beta = lng_ref[:, cs], lnb_ref[:, cs]
            for r0, nr in _row_chunks(tm):
                rows = pl.ds(r0, nr)
                y, rstd = _group_norm(u1_ref[rows, cs])
                z = y * gamma + beta
                dz = du[rows, cs] * _silu_grad(z, _sigmoid(z))
                a_lng = a_lng + _fold_rows(dz * y)
                a_lnb = a_lnb + _fold_rows(dz)
                dy = dz * gamma
                du1 = rstd * (dy - jnp.mean(dy, axis=-1, keepdims=True)
                              - y * jnp.mean(dy * y, axis=-1, keepdims=True))
                du1_ref[rows, cs] = du1
                a_cab = a_cab + _fold_rows(du1)
                a_caw = _conv_bwd_weight(a_caw, du1, bufa, KA, r0, nr, cs)

                dvv = dv[rows, cs]
                dbg_ref[rows, cs] = (dvv * _conv_fwd(bufb, cbw_ref, KB, r0, nr, cs)).astype(BF16)
                dcv = dvv * bg[rows, cs].astype(F32)
                dcv_ref[rows, cs] = dcv
                a_cbw = _conv_bwd_weight(a_cbw, dcv, bufb, KB, r0, nr, cs)
            _add_row(dlng_ref, 0, cs, a_lng)
            _add_row(dlnb_ref, 0, cs, a_lnb)
            _add_row(dcab_ref, 0, cs, a_cab)
            for k in range(KA):
                _add_row(dcaw_ref, k, cs, a_caw[k])
            for k in range(KB):
                _add_row(dcbw_ref, k, cs, a_cbw[k])

    in_specs = [sec(0), sec(1), tile, sec(0), sec(1), sec(2), sec(3), sec(4),
                sec_prev(0), sec_prev(1), sec_prev(3), sec_prev(4), chan(1), chan(1), chan(KB)]
    operands = [dmix, dmix, u1] + [proj] * 9 + [lng, lnb, cbw]
    return _call(body, "mixer_bwd1", (nc, T // tm), in_specs,
                 (tile, tile, tile, chan(KA), chan(1), chan(1), chan(1), chan(KB)),
                 (jax.ShapeDtypeStruct((T, CW), F32), jax.ShapeDtypeStruct((T, CW), F32),
                  jax.ShapeDtypeStruct((T, CW), BF16), jax.ShapeDtypeStruct((KA, CW), F32),
                  jax.ShapeDtypeStruct((1, CW), F32), jax.ShapeDtypeStruct((1, CW), F32),
                  jax.ShapeDtypeStruct((1, CW), F32), jax.ShapeDtypeStruct((KB, CW), F32)), operands,
                 scratch=[pltpu.VMEM((HALO + tm, tc), F32), pltpu.VMEM((HALO + tm, tc), F32)],
                 sem=("parallel", "arbitrary"))


def _mixer_bwd2(du1, dcv, proj, caw, cbw, T, CW):
    KA, KB = caw.shape[0], cbw.shape[0]
    tm, tc, hb, nc = _mixer_tiles(T, CW)
    n_i = T // tm

    def sec(s):
        return pl.BlockSpec((tm, tc), lambda i, c: (i, s * nc + c))

    def chan(rows):
        return pl.BlockSpec((rows, tc), lambda i, c: (0, c))

    tile = pl.BlockSpec((tm, tc), lambda i, c: (i, c))
    nxt = pl.BlockSpec((HALO, tc), lambda i, c: (jnp.minimum((i + 1) * hb, n_i * hb - 1), c))

    def body(du1_ref, du1n, dcv_ref, dcvn, av, ag, cg, bh, caw_ref, cbw_ref, dav, dag, dcg, dbh, bufa, bufb):
        last = pl.program_id(0) == n_i - 1
        bufa[pl.ds(0, tm), :] = du1_ref[...]
        bufa[pl.ds(tm, HALO), :] = jnp.where(last, 0.0, du1n[...])
        bufb[pl.ds(0, tm), :] = dcv_ref[...]
        bufb[pl.ds(tm, HALO), :] = jnp.where(last, 0.0, dcvn[...])
        for cs in _col_chunks(tc):
            for r0, nr in _row_chunks(tm):
                rows = pl.ds(r0, nr)
                du0 = _conv_bwd_input(bufa, caw_ref, KA, r0, nr, cs)
                sig = _sigmoid(ag[rows, cs].astype(F32))
                dav[rows, cs] = (du0 * sig).astype(BF16)
                dag[rows, cs] = (du0 * av[rows, cs].astype(F32) * (sig * (1.0 - sig))).astype(BF16)
                dch = _conv_bwd_input(bufb, cbw_ref, KB, r0, nr, cs)
                dcg[rows, cs] = (dch * bh[rows, cs].astype(F32)).astype(BF16)
                dbh[rows, cs] = (dch * cg[rows, cs].astype(F32)).astype(BF16)

    in_specs = [tile, nxt, tile, nxt, sec(0), sec(1), sec(3), sec(4), chan(KA), chan(KB)]
    operands = [du1, du1, dcv, dcv, proj, proj, proj, proj, caw, cbw]
    out = jax.ShapeDtypeStruct((T, CW), BF16)
    return _call(body, "mixer_bwd2", (n_i, nc), in_specs, (tile, tile, tile, tile), (out, out, out, out),
                 operands, scratch=[pltpu.VMEM((HALO + tm, tc), F32), pltpu.VMEM((HALO + tm, tc), F32)],
                 sem=("parallel", "parallel"))


def _ffn_tiles(T):
    tm = _tile(T, 512, HALO)
    return tm, tm // HALO, T // tm


def _ffn_act_fwd(gpre, up, cfw):
    nb, T, F = gpre.shape
    KF = cfw.shape[1]
    tm, hb, n_i = _ffn_tiles(T)
    tile = pl.BlockSpec((None, tm, F), lambda b, i: (b, i, 0))
    prev = pl.BlockSpec((None, HALO, F), lambda b, i: (b, jnp.maximum(i * hb - 1, 0), 0))
    wspec = pl.BlockSpec((None, KF, F), lambda b, i: (b, 0, 0))

    def body(g_ref, gh_ref, up_ref, w_ref, f_ref, buf):
        buf[pl.ds(HALO, tm), :] = g_ref[...].astype(F32)
        buf[pl.ds(0, HALO), :] = jnp.where(pl.program_id(1) == 0, 0.0, gh_ref[...].astype(F32))
        for cs in _col_chunks(F):
            for r0, nr in _row_chunks(tm):
                rows = pl.ds(r0, nr)
                g = _conv_fwd(buf, w_ref, KF, r0, nr, cs)
                f_ref[rows, cs] = (g * _sigmoid(g) * up_ref[rows, cs].astype(F32)).astype(BF16)

    return _call(body, "ffn_act_fwd", (nb, n_i), [tile, prev, tile, wspec], tile,
                 jax.ShapeDtypeStruct((nb, T, F), BF16), [gpre, gpre, up, cfw],
                 scratch=[pltpu.VMEM((HALO + tm, F), F32)], sem=("parallel", "parallel"))


def _ffn_act_bwd1(df, gpre, up, cfw):
    nb, T, F = gpre.shape
    KF = cfw.shape[1]
    tm, hb, n_i = _ffn_tiles(T)
    tile = pl.BlockSpec((None, tm, F), lambda b, i: (b, i, 0))
    prev = pl.BlockSpec((None, HALO, F), lambda b, i: (b, jnp.maximum(i * hb - 1, 0), 0))
    wspec = pl.BlockSpec((None, KF, F), lambda b, i: (b, 0, 0))

    def body(df_ref, g_ref, gh_ref, up_ref, w_ref, dg_ref, dup_ref, dw_ref, buf):
        first = pl.program_id(1) == 0

        @pl.when(first)
        def _():
            dw_ref[...] = jnp.zeros_like(dw_ref)

        buf[pl.ds(HALO, tm), :] = g_ref[...].astype(F32)
        buf[pl.ds(0, HALO), :] = jnp.where(first, 0.0, gh_ref[...].astype(F32))
        for cs in _col_chunks(F):
            accs = [jnp.zeros((SUBLANES, cs.stop - cs.start), F32)] * KF
            for r0, nr in _row_chunks(tm):
                rows = pl.ds(r0, nr)
                g = _conv_fwd(buf, w_ref, KF, r0, nr, cs)
                sig = _sigmoid(g)
                dfv = df_ref[rows, cs].astype(F32)
                dup_ref[rows, cs] = (dfv * (g * sig)).astype(BF16)
                dg = dfv * up_ref[rows, cs].astype(F32) * _silu_grad(g, sig)
                dg_ref[rows, cs] = dg.astype(BF16)
                accs = _conv_bwd_weight(accs, dg, buf, KF, r0, nr, cs)
            for k in range(KF):
                _add_row(dw_ref, k, cs, accs[k])

    return _call(body, "ffn_act_bwd1", (nb, n_i), [tile, tile, prev, tile, wspec], (tile, tile, wspec),
                 (jax.ShapeDtypeStruct((nb, T, F), BF16), jax.ShapeDtypeStruct((nb, T, F), BF16),
                  jax.ShapeDtypeStruct((nb, KF, F), F32)), [df, gpre, gpre, up, cfw],
                 scratch=[pltpu.VMEM((HALO + tm, F), F32)], sem=("parallel", "arbitrary"))


def _ffn_act_bwd2(dg, cfw):
    nb, T, F = dg.shape
    KF = cfw.shape[1]
    tm, hb, n_i = _ffn_tiles(T)
    tile = pl.BlockSpec((None, tm, F), lambda b, i: (b, i, 0))
    nxt = pl.BlockSpec((None, HALO, F), lambda b, i: (b, jnp.minimum((i + 1) * hb, n_i * hb - 1), 0))
    wspec = pl.BlockSpec((None, KF, F), lambda b, i: (b, 0, 0))

    def body(dg_ref, dgn_ref, w_ref, o_ref, buf):
        buf[pl.ds(0, tm), :] = dg_ref[...].astype(F32)
        buf[pl.ds(tm, HALO), :] = jnp.where(pl.program_id(1) == n_i - 1, 0.0, dgn_ref[...].astype(F32))
        for cs in _col_chunks(F):
            for r0, nr in _row_chunks(tm):
                o_ref[pl.ds(r0, nr), cs] = _conv_bwd_input(buf, w_ref, KF, r0, nr, cs).astype(BF16)

    return _call(body, "ffn_act_bwd2", (nb, n_i), [tile, nxt, wspec], tile,
                 jax.ShapeDtypeStruct((nb, T, F), BF16), [dg, dg, cfw],
                 scratch=[pltpu.VMEM((HALO + tm, F), F32)], sem=("parallel", "parallel"))


def _softmax_rows(s):
    e = jnp.exp(s - jnp.max(s, axis=-1, keepdims=True))
    return e / jnp.sum(e, axis=-1, keepdims=True)


def _attn_fwd(q, k, v):
    T, D = q.shape
    Mm = k.shape[0]
    hd = D // N_XATTN_HEADS
    scale = hd ** -0.5
    tm = _tile(T, 256, 16)

    def body(q_ref, k_ref, v_ref, o_ref):
        for h in range(N_XATTN_HEADS):
            sl = slice(h * hd, (h + 1) * hd)
            s = lax.dot_general(q_ref[:, sl], k_ref[:, sl], _DOT_DIMS["nt"], preferred_element_type=F32) * scale
            p = _softmax_rows(s).astype(BF16)
            o_ref[:, sl] = jnp.dot(p, v_ref[:, sl], preferred_element_type=F32).astype(BF16)

    row = pl.BlockSpec((tm, D), lambda i: (i, 0))
    full = pl.BlockSpec((Mm, D), lambda i: (0, 0))
    return _call(body, "attn_fwd", (T // tm,), [row, full, full], row, jax.ShapeDtypeStruct((T, D), BF16),
                 [q, k, v], sem=("parallel",))


def _attn_bwd(q, k, v, do):
    T, D = q.shape
    Mm = k.shape[0]
    hd = D // N_XATTN_HEADS
    scale = hd ** -0.5
    tm = _tile(T, 256, 16)
    n_i = T // tm

    def body(q_ref, do_ref, k_ref, v_ref, dq_ref, dk_ref, dv_ref, dk_acc, dv_acc):
        @pl.when(pl.program_id(0) == 0)
        def _():
            dk_acc[...] = jnp.zeros_like(dk_acc)
            dv_acc[...] = jnp.zeros_like(dv_acc)

        for h in range(N_XATTN_HEADS):
            sl = slice(h * hd, (h + 1) * hd)
            qh, kh, doh = q_ref[:, sl], k_ref[:, sl], do_ref[:, sl]
            s = lax.dot_general(qh, kh, _DOT_DIMS["nt"], preferred_element_type=F32) * scale
            p = _softmax_rows(s)
            dv_acc[:, sl] += lax.dot_general(p.astype(BF16), doh, _DOT_DIMS["tn"], preferred_element_type=F32)
            dp = lax.dot_general(doh, v_ref[:, sl], _DOT_DIMS["nt"], preferred_element_type=F32)
            ds = (p * (dp - jnp.sum(dp * p, axis=-1, keepdims=True)) * scale).astype(BF16)
            dq_ref[:, sl] = jnp.dot(ds, kh, preferred_element_type=F32).astype(BF16)
            dk_acc[:, sl] += lax.dot_general(ds, qh, _DOT_DIMS["tn"], preferred_element_type=F32)

        @pl.when(pl.program_id(0) == n_i - 1)
        def _():
            dk_ref[...] = dk_acc[...].astype(BF16)
            dv_ref[...] = dv_acc[...].astype(BF16)

    row = pl.BlockSpec((tm, D), lambda i: (i, 0))
    full = pl.BlockSpec((Mm, D), lambda i: (0, 0))
    return _call(body, "attn_bwd", (n_i,), [row, row, full, full], (row, full, full),
                 (jax.ShapeDtypeStruct((T, D), BF16), jax.ShapeDtypeStruct((Mm, D), BF16),
                  jax.ShapeDtypeStruct((Mm, D), BF16)), [q, do, k, v],
                 scratch=[pltpu.VMEM((Mm, D), F32), pltpu.VMEM((Mm, D), F32)])


def _position():
    x, y, c = lax.axis_index("x"), lax.axis_index("y"), lax.axis_index("c")
    return x, y, c


def _peer(pos, k):
    x, y, c = pos
    return (1 - x if k & 4 else x, 1 - y if k & 2 else y, 1 - c if k & 1 else c)


def _index(pos):
    x, y, c = pos
    return 4 * x + 2 * y + c


def _sequencer_kernel(body, name, collective_id, out_type, operands):
    return pl.kernel(
        body, name=name, out_type=out_type,
        mesh=plsc.ScalarSubcoreMesh(axis_name="sequencer", num_cores=1),
        scratch_types=[pltpu.SemaphoreType.DMA, pltpu.SemaphoreType.DMA((7,)), pltpu.SemaphoreType.DMA],
        compiler_params=pltpu.CompilerParams(collective_id=collective_id),
    )(*operands)


def _handshake(peers):
    barrier = pltpu.get_barrier_semaphore()
    for peer in peers:
        pl.semaphore_signal(barrier, inc=1, device_id=peer, device_id_type=MESH)
    pl.semaphore_wait(barrier, len(peers))


def _sequencer_all_gather(name, collective_id, shards):
    n = len(shards)

    def body(*refs):
        x_refs, out_refs = refs[:n], refs[n:2 * n]
        send_sem, recv_sems, local_sem = refs[2 * n:]
        me = _position()
        x, y, c = me
        sibling = _peer(me, 1)
        first = (x + (1 - c) - 2 * x * (1 - c), y + c - 2 * y * c, c)
        second = (x + c - 2 * x * c, y + (1 - c) - 2 * y * (1 - c), c)
        diagonal = _peer(me, 6)
        _handshake([sibling, first, second])

        def copy(a, k, block, to, own=False):
            dst = out_refs[a].at[_index(block)]
            return pltpu.make_async_remote_copy(
                src_ref=x_refs[a] if own else dst, dst_ref=dst, send_sem=send_sem, recv_sem=recv_sems.at[k],
                device_id=to, device_id_type=MESH)

        local = [pltpu.make_async_copy(x_refs[a], out_refs[a].at[_index(me)], local_sem) for a in range(n)]
        started = [copy(a, 1 + j, me, peer, own=True) for a in range(n) for j, peer in enumerate((first, second))]
        started += [copy(a, 0, me, sibling, own=True) for a in range(n)]
        for cp in started + local:
            cp.start()
        for k, origin in ((1, first), (2, second), (3, diagonal)):
            for a in range(n):
                copy(a, k, origin, me).wait_recv()
            passed = [copy(a, 3 + k, origin, sibling) for a in range(n)]
            if k == 1:
                passed = [copy(a, 3, origin, second) for a in range(n)] + passed
            for cp in passed:
                cp.start()
            started += passed
        for k in (0, 4, 5, 6):
            for a in range(n):
                copy(a, k, sibling, me).wait_recv()
        for cp in started:
            cp.wait_send()
        for cp in local:
            cp.wait()

    return _sequencer_kernel(body, name, collective_id,
                             [jax.ShapeDtypeStruct((N_DEV,) + s.shape, s.dtype) for s in shards], shards)


def _chip_index(pos):
    return 2 * pos[0] + pos[1]


def _sequencer_to_sibling(name, collective_id, parts):
    n = len(parts)

    def body(*refs):
        p_refs, out_refs = refs[:n], refs[n:2 * n]
        send_sem, recv_sems, _ = refs[2 * n:]
        me = _position()
        sibling = _peer(me, 1)
        _handshake([sibling])
        copies = [pltpu.make_async_remote_copy(
            src_ref=p_refs[a].at[2 * q + sibling[2]], dst_ref=out_refs[a].at[q], send_sem=send_sem,
            recv_sem=recv_sems.at[0], device_id=sibling, device_id_type=MESH)
            for a in range(n) for q in range(N_DEV // 2)]
        for cp in copies:
            cp.start()
        for cp in copies:
            cp.wait_recv()
        for cp in copies:
            cp.wait_send()

    return _sequencer_kernel(body, name, collective_id,
                             [jax.ShapeDtypeStruct((N_DEV // 2,) + p.shape[1:], p.dtype) for p in parts], parts)


def _sequencer_to_chips(name, collective_id, sums):
    n = len(sums)

    def body(*refs):
        s_refs, out_refs = refs[:n], refs[n:2 * n]
        send_sem, recv_sems, local_sem = refs[2 * n:]
        me = _position()
        my_chip = _chip_index(me)
        peers = [_peer(me, 4), _peer(me, 2), _peer(me, 6)]
        _handshake(peers)
        local = [pltpu.make_async_copy(s_refs[a].at[my_chip], out_refs[a].at[my_chip], local_sem) for a in range(n)]
        sends = [pltpu.make_async_remote_copy(
            src_ref=s_refs[a].at[_chip_index(peer)], dst_ref=out_refs[a].at[my_chip], send_sem=send_sem,
            recv_sem=recv_sems.at[1 + j], device_id=peer, device_id_type=MESH)
            for a in range(n) for j, peer in enumerate(peers)]
        for cp in sends + local:
            cp.start()
        for j, peer in enumerate(peers):
            for a in range(n):
                pltpu.make_async_remote_copy(
                    src_ref=s_refs[a].at[my_chip], dst_ref=out_refs[a].at[_chip_index(peer)], send_sem=send_sem,
                    recv_sem=recv_sems.at[1 + j], device_id=peer, device_id_type=MESH).wait_recv()
        for cp in sends:
            cp.wait_send()
        for cp in local:
            cp.wait()

    return _sequencer_kernel(body, name, collective_id,
                             [jax.ShapeDtypeStruct(s.shape, s.dtype) for s in sums], sums)


def _chip_sum(name, parts, got, after=()):
    _, R, C = parts.shape
    tr = _tile(R, max(8, (3 << 20) // C), 16)
    n_after = len(after)
    limit = 6 * _nbytes((tr, C), parts.dtype) + VMEM_TEMP_ALLOWANCE

    def body(c_ref, p_ref, g_ref, *rest):
        o_ref = rest[n_after]
        o_ref[...] = (p_ref[...].astype(F32) + g_ref[...].astype(F32)).astype(o_ref.dtype)

    blk = pl.BlockSpec((None, tr, C), lambda q, i, c_ref: (q, i, 0))
    mine = pl.BlockSpec((None, tr, C), lambda q, i, c_ref: (2 * q + c_ref[0], i, 0))
    core = lax.axis_index("c").astype(jnp.int32).reshape(1)
    parts, got = [pltpu.with_memory_space_constraint(o, pltpu.HBM) for o in (parts, got)]
    after = [pltpu.with_memory_space_constraint(o, pltpu.HBM) for o in after]
    return pl.pallas_call(
        body, name=name, out_shape=pltpu.HBM((N_DEV // 2, R, C), parts.dtype),
        grid_spec=pltpu.PrefetchScalarGridSpec(
            num_scalar_prefetch=1, grid=(N_DEV // 2, R // tr),
            in_specs=[mine, blk] + [pl.BlockSpec(memory_space=pl.ANY)] * n_after, out_specs=blk),
        compiler_params=pltpu.CompilerParams(dimension_semantics=("parallel", "parallel"),
                                             vmem_limit_bytes=int(limit)),
    )(core, parts, got, *after)


def _all_reduce_rows(name, v):
    R, C = v.shape

    def body(v_ref, out_ref, gath, send_sems, recv_sems):
        me = _position()
        gath[_index(me)] = v_ref[...]
        sends = []
        for k in range(1, N_DEV):
            peer = _peer(me, k)
            sends.append(pltpu.make_async_remote_copy(
                src_ref=v_ref, dst_ref=gath.at[_index(me)], send_sem=send_sems.at[k - 1],
                recv_sem=recv_sems.at[k - 1], device_id=peer, device_id_type=MESH))
        for cp in sends:
            cp.start()
        for k in range(1, N_DEV):
            peer = _peer(me, k)
            pltpu.make_async_remote_copy(
                src_ref=v_ref, dst_ref=gath.at[_index(peer)], send_sem=send_sems.at[k - 1],
                recv_sem=recv_sems.at[k - 1], device_id=peer, device_id_type=MESH).wait_recv()
        for cp in sends:
            cp.wait_send()
        tot = gath[0]
        for s in range(1, N_DEV):
            tot = tot + gath[s]
        out_ref[...] = tot

    return pl.pallas_call(
        body, name=name, out_shape=jax.ShapeDtypeStruct((R, C), F32),
        in_specs=[VMEM_SPEC], out_specs=VMEM_SPEC,
        scratch_shapes=[pltpu.VMEM((N_DEV, R, C), F32), pltpu.SemaphoreType.DMA((7,)),
                        pltpu.SemaphoreType.DMA((7,))],
    )(v)


def _adamw_math(g, w, m, v):
    m = ADAM_B1 * m + (1.0 - ADAM_B1) * g
    v = ADAM_B2 * v + (1.0 - ADAM_B2) * (g * g)
    m_hat = m / (1.0 - ADAM_B1 ** ADAM_STEP)
    v_hat = v / (1.0 - ADAM_B2 ** ADAM_STEP)
    delta = -ADAM_LR * (m_hat / (jnp.sqrt(v_hat) + ADAM_EPS) + ADAM_WD * w)
    return delta, m, v


def _adamw(name, parts, w, m, v, after=()):
    n, R, C = parts.shape
    tr = _tile(R, max(8, (1 << 18) // C), 16)

    def body(p_ref, w_ref, m_ref, v_ref, g_ref, d_ref, nm_ref, nv_ref):
        g = p_ref[0].astype(F32)
        for s in range(1, n):
            g = g + p_ref[s].astype(F32)
        g_ref[...] = g
        d_ref[...], nm_ref[...], nv_ref[...] = _adamw_math(g, w_ref[...], m_ref[...], v_ref[...])

    blk = pl.BlockSpec((tr, C), lambda i: (i, 0))
    out = jax.ShapeDtypeStruct((R, C), F32)
    return _call(body, name, (R // tr,), [pl.BlockSpec((n, tr, C), lambda i: (0, i, 0)), blk, blk, blk],
                 (blk, blk, blk, blk), (out, out, out, out), [parts, w, m, v], sem=("parallel",), after=after)


def kernel(x, mem, g_mix, w_in, conv_a_w, conv_a_b, ln_a_g, ln_a_b, conv_b_w, w_out, g_xattn, g_mem, w_q, w_k, w_v, w_o, g_ffn, w_gate, w_up, conv_f_w, w_down, g_final, loss_target, m_g_mix, m_w_in, m_conv_a_w, m_conv_a_b, m_ln_a_g, m_ln_a_b, m_conv_b_w, m_w_out, m_g_xattn, m_g_mem, m_w_q, m_w_k, m_w_v, m_w_o, m_g_ffn, m_w_gate, m_w_up, m_conv_f_w, m_w_down, m_g_final, v_g_mix, v_w_in, v_conv_a_w, v_conv_a_b, v_ln_a_g, v_ln_a_b, v_conv_b_w, v_w_out, v_g_xattn, v_g_mem, v_w_q, v_w_k, v_w_v, v_w_o, v_g_ffn, v_w_gate, v_w_up, v_conv_f_w, v_w_down, v_g_final):
    T, D = x.shape[1], x.shape[2]
    Mm = mem.shape[1]
    CW = conv_a_b.shape[1]
    INB = w_in.shape[2]
    FB = w_gate.shape[2]
    KA, KB, KF = conv_a_w.shape[1], conv_b_w.shape[1], conv_f_w.shape[1]
    DB = D // N_DEV
    assert 5 * CW == N_DEV * INB and 2 * CW == D

    x2, mem2, tgt = x[0], mem[0], loss_target[0]
    g_mem2, g_final2 = g_mem.reshape(1, D), g_final.reshape(1, D)

    def bf16(name, w):
        return _cast_bf16("cast_" + name, w[0])

    Win, caw, cbw = _sequencer_all_gather(
        "ag_in", AG_ID, [bf16("w_in", w_in), conv_a_w[0], conv_b_w[0]])
    Wout, = _sequencer_all_gather("ag_out", AG_ID, [bf16("w_out", w_out)])
    Wq, Wk, Wv, Wo = _sequencer_all_gather(
        "ag_attn", AG_ID, [bf16("w_q", w_q), bf16("w_k", w_k), bf16("w_v", w_v), bf16("w_o", w_o)])
    Wgate, cfw = _sequencer_all_gather("ag_gate", AG_ID, [bf16("w_gate", w_gate), conv_f_w[0]])
    Wup, = _sequencer_all_gather("ag_up", AG_ID, [bf16("w_up", w_up)])
    Wdown, = _sequencer_all_gather("ag_down", AG_ID, [bf16("w_down", w_down)])
    Wout, Wq, Wk, Wv, Wo = [w.reshape(D, D) for w in (Wout, Wq, Wk, Wv, Wo)]
    caw = jnp.transpose(caw, (1, 0, 2)).reshape(KA, CW)
    cbw = jnp.transpose(cbw, (1, 0, 2)).reshape(KB, CW)

    xn1 = _rms_fwd("rms_mix", x2, g_mix)
    proj = _matmul("mm_proj", "nn", xn1, Win, M=T, N=INB, K=D, nb=N_DEV, b_lay="blk", o_lay="col", tn=INB,
                   out_dtype=BF16)
    mix, u1 = _mixer_fwd(proj, caw, conv_a_b, ln_a_g, ln_a_b, cbw, T, CW)
    h1 = _matmul("mm_h1", "nn", mix, Wout.reshape(2, CW, D), M=T, N=D, K=CW, nb=2, a_lay="blk", b_lay="blk",
                 red_block=True, res=x2)
    xn2 = _rms_fwd("rms_xattn", h1, g_xattn)
    q = _matmul("mm_q", "nn", xn2, Wq, M=T, N=D, K=D, out_dtype=BF16)
    memn = _rms_fwd("rms_mem", mem2, g_mem2, after=[q])
    kk = _matmul("mm_k", "nn", memn, Wk, M=Mm, N=D, K=D, out_dtype=BF16)
    vv = _matmul("mm_v", "nn", memn, Wv, M=Mm, N=D, K=D, out_dtype=BF16)
    o = _attn_fwd(q, kk, vv)
    h2 = _matmul("mm_h2", "nn", o, Wo, M=T, N=D, K=D, res=h1)
    xn3 = _rms_fwd("rms_ffn", h2, g_ffn)
    gpre = _matmul("mm_gate", "nn", xn3, Wgate, M=T, N=FB, K=D, nb=N_DEV, b_lay="blk", o_lay="blk", tn=FB,
                   out_dtype=BF16)
    up = _matmul("mm_up", "nn", xn3, Wup, M=T, N=FB, K=D, nb=N_DEV, b_lay="blk", o_lay="blk", tn=FB,
                 out_dtype=BF16)
    f = _ffn_act_fwd(gpre, up, cfw)
    h3 = _matmul("mm_h3", "nn", f, Wdown, M=T, N=D, K=FB, nb=N_DEV, a_lay="blk", b_lay="blk", red_block=True,
                 res=h2)
    dh3, dh3b, loss_part, dg_final = _loss_head(h3, tgt, g_final2)

    wmv = {"w_in": (w_in, m_w_in, v_w_in), "conv_a_w": (conv_a_w, m_conv_a_w, v_conv_a_w),
           "conv_b_w": (conv_b_w, m_conv_b_w, v_conv_b_w), "w_out": (w_out, m_w_out, v_w_out),
           "w_q": (w_q, m_w_q, v_w_q), "w_k": (w_k, m_w_k, v_w_k), "w_v": (w_v, m_w_v, v_w_v),
           "w_o": (w_o, m_w_o, v_w_o), "w_gate": (w_gate, m_w_gate, v_w_gate), "w_up": (w_up, m_w_up, v_w_up),
           "conv_f_w": (conv_f_w, m_conv_f_w, v_conv_f_w), "w_down": (w_down, m_w_down, v_w_down)}
    res = {}
    pending = []

    def mm(*args, after=(), **kwargs):
        behind = list(after) + pending
        pending.clear()
        return _matmul(*args, after=behind, **kwargs)

    def to_sibling(tag, named_parts):
        got = _sequencer_to_sibling("rs1_" + tag, SIBLING_ID, [p for _, p in named_parts])
        return named_parts, got

    def to_chips(tag, stage1, after):
        named_parts, got = stage1
        sums = [_chip_sum("sum_" + n, p, g, after=after) for (n, p), g in zip(named_parts, got)]
        pending.extend(sums)
        return [n for n, _ in named_parts], _sequencer_to_chips("rs2_" + tag, CHIPS_ID, sums)

    def finish(stage2, after):
        names, got = stage2
        for n, g in zip(names, got):
            w, m, v = wmv[n]
            res[n] = _adamw("adamw_" + n, g, w[0], m[0], v[0], after=after)
            pending.append(res[n][0])

    def row_blocks(dw):
        return dw.reshape(N_DEV, DB, D)

    def conv_blocks(dw, K):
        return jnp.transpose(dw.reshape(K, N_DEV, CW // N_DEV), (1, 0, 2))

    dWdown = mm("mm_dw_down", "tn", f, dh3b, M=FB, N=D, K=T, nb=N_DEV, a_lay="blk", o_lay="blk",
                     out_dtype=BF16, tm=FB)
    s_down = to_sibling("down", [("w_down", dWdown)])
    df = mm("mm_df", "nt", dh3b, Wdown, M=T, N=FB, K=D, nb=N_DEV, b_lay="blk", o_lay="blk", tn=FB, out_dtype=BF16,
                 after=[dWdown])
    dg, dup, dcfw = _ffn_act_bwd1(df, gpre, up, cfw)
    dgpre = _ffn_act_bwd2(dg, cfw)
    c_down = to_chips("down", s_down, after=[dgpre])
    dWgate = mm("mm_dw_gate", "tn", xn3, dgpre, M=D, N=FB, K=T, nb=N_DEV, b_lay="blk", o_lay="blk",
                     out_dtype=BF16, tn=FB)
    s_gate = to_sibling("gate", [("w_gate", dWgate), ("conv_f_w", dcfw)])
    dWup = mm("mm_dw_up", "tn", xn3, dup, M=D, N=FB, K=T, nb=N_DEV, b_lay="blk", o_lay="blk",
                   out_dtype=BF16, tn=FB, after=[dWgate])
    s_up = to_sibling("up", [("w_up", dWup)])
    dxn3 = mm("mm_dxn3_gate", "nt", dgpre, Wgate, M=T, N=D, K=FB, nb=N_DEV, a_lay="blk", b_lay="blk",
                   red_block=True, after=[dWup])
    c_gate = to_chips("gate", s_gate, after=[dxn3])
    dxn3 = mm("mm_dxn3_up", "nt", dup, Wup, M=T, N=D, K=FB, nb=N_DEV, a_lay="blk", b_lay="blk",
                   red_block=True, res=dxn3)
    c_up = to_chips("up", s_up, after=[dxn3])
    dh2, dh2b, dg_ffn = _rms_bwd("rms_bwd_ffn", dxn3, h2, g_ffn, dh3)

    dWo = mm("mm_dw_o", "tn", o, dh2b, M=D, N=D, K=T, out_dtype=BF16)
    s_o = to_sibling("o", [("w_o", row_blocks(dWo))])
    do = mm("mm_do", "nt", dh2b, Wo, M=T, N=D, K=D, out_dtype=BF16, after=[dWo])
    finish(c_down, after=[do])
    dq, dk, dv = _attn_bwd(q, kk, vv, do)
    c_o = to_chips("o", s_o, after=[dq])
    dWq = mm("mm_dw_q", "tn", xn2, dq, M=D, N=D, K=T, out_dtype=BF16)
    s_q = to_sibling("q", [("w_q", row_blocks(dWq))])
    dxn2 = mm("mm_dxn2", "nt", dq, Wq, M=T, N=D, K=D, after=[dWq])
    finish(c_gate, after=[dxn2])
    c_q = to_chips("q", s_q, after=[dxn2])
    dh1, dh1b, dg_xattn = _rms_bwd("rms_bwd_xattn", dxn2, h1, g_xattn, dh2)
    dWk = mm("mm_dw_k", "tn", memn, dk, M=D, N=D, K=Mm, out_dtype=BF16, after=[dh1b])
    dWv = mm("mm_dw_v", "tn", memn, dv, M=D, N=D, K=Mm, out_dtype=BF16, after=[dh1b])
    s_kv = to_sibling("kv", [("w_k", row_blocks(dWk)), ("w_v", row_blocks(dWv))])
    dmemn = mm("mm_dmem_k", "nt", dk, Wk, M=Mm, N=D, K=D, after=[dWk, dWv])
    dmemn = mm("mm_dmem_v", "nt", dv, Wv, M=Mm, N=D, K=D, res=dmemn)
    dg_mem = _rms_bwd("rms_bwd_mem", dmemn, mem2, g_mem2)
    finish(c_up, after=[dg_mem])

    dWout = mm("mm_dw_out", "tn", mix, dh1b, M=CW, N=D, K=T, nb=2, a_lay="blk", o_lay="blk", out_dtype=BF16,
                    after=[dg_mem])
    s_out = to_sibling("out", [("w_out", row_blocks(dWout.reshape(D, D)))])
    dmix = mm("mm_dmix", "nt", dh1b, Wout, M=T, N=D, K=D, after=[dWout])
    c_kv = to_chips("kv", s_kv, after=[dmix])
    du1, dcv, dbg, dcaw, dcab, dlng, dlnb, dcbw = _mixer_bwd1(dmix, proj, u1, caw, ln_a_g, ln_a_b, cbw, T, CW)
    c_out = to_chips("out", s_out, after=[du1])
    finish(c_o, after=[du1])
    finish(c_q, after=[du1])
    dav, dag, dcg, dbh = _mixer_bwd2(du1, dcv, proj, caw, cbw, T, CW)
    dproj = jnp.concatenate([dav, dag, dbg, dcg, dbh], axis=1)
    dWin = mm("mm_dw_in", "tn", xn1, dproj, M=D, N=INB, K=T, nb=N_DEV, b_lay="col", o_lay="blk",
                   out_dtype=BF16, tn=INB)
    s_in = to_sibling("in", [("w_in", dWin), ("conv_a_w", conv_blocks(dcaw, KA)),
                             ("conv_b_w", conv_blocks(dcbw, KB))])
    finish(c_kv, after=[dWin])
    c_in = to_chips("in", s_in, after=list(pending))
    dxn1 = mm("mm_dxn1", "nt", dproj, Win, M=T, N=D, K=INB, nb=N_DEV, a_lay="col", b_lay="blk",
                   red_block=True, after=[dWin])
    finish(c_out, after=[dxn1])
    dx, _, dg_mix = _rms_bwd("rms_bwd_mix", dxn1, x2, g_mix, dh1, after=list(pending))

    def pair(a, b):
        return jnp.concatenate([a, b], axis=1)

    zeros_half = jnp.zeros((1, CW), F32)
    small_g = jnp.concatenate([
        dg_mix, pair(dcab, dlng), pair(dlnb, zeros_half), dg_xattn, dg_mem, dg_ffn, dg_final,
        jnp.broadcast_to(loss_part[:, :1], (1, D))], axis=0)
    small_sum = _all_reduce_rows("ar_small", small_g)
    loss = small_sum[7, 0]
    finish(c_in, after=[small_sum])

    def pack(a_mix, a_cab, a_lng, a_lnb, a_xattn, a_mem, a_ffn, a_final):
        return jnp.concatenate([a_mix, pair(a_cab, a_lng), pair(a_lnb, zeros_half), a_xattn, a_mem.reshape(1, D),
                                a_ffn, a_final.reshape(1, D), jnp.zeros((1, D), F32)], axis=0)

    small = _adamw("adamw_small", small_sum[None],
                   pack(g_mix, conv_a_b, ln_a_g, ln_a_b, g_xattn, g_mem, g_ffn, g_final),
                   pack(m_g_mix, m_conv_a_b, m_ln_a_g, m_ln_a_b, m_g_xattn, m_g_mem, m_g_ffn, m_g_final),
                   pack(v_g_mix, v_conv_a_b, v_ln_a_g, v_ln_a_b, v_g_xattn, v_g_mem, v_g_ffn, v_g_final))

    def unpack(a):
        return {"g_mix": a[0:1], "conv_a_b": a[1:2, :CW], "ln_a_g": a[1:2, CW:], "ln_a_b": a[2:3, :CW],
                "g_xattn": a[3:4], "g_mem": a[4], "g_ffn": a[5:6], "g_final": a[6]}

    small = [unpack(a) for a in small]
    order = ["g_mix", "w_in", "conv_a_w", "conv_a_b", "ln_a_g", "ln_a_b", "conv_b_w", "w_out", "g_xattn", "g_mem",
             "w_q", "w_k", "w_v", "w_o", "g_ffn", "w_gate", "w_up", "conv_f_w", "w_down", "g_final"]
    outs = [loss, dx[None]]
    for kind in range(4):
        for n in order:
            outs.append(res[n][kind][None] if n in res else small[kind][n])
    return tuple(outs)
```

```python
import functools

import jax
import jax.numpy as jnp
from jax import lax
from jax.experimental import pallas as pl
from jax.experimental.pallas import tpu as pltpu
from jax.experimental.pallas import tpu_sc as plsc

F32 = jnp.float32
BF16 = jnp.bfloat16

N_DEV = 8
EPS = 1e-6
GROUP_DIM = 128
N_XATTN_HEADS = 4
ADAM_LR = 0.001
ADAM_B1 = 0.9
ADAM_B2 = 0.999
ADAM_EPS = 1e-08
ADAM_WD = 0.01
ADAM_STEP = 10

AG_ID, SIBLING_ID, CHIPS_ID = 1, 2, 3

HALO = 32
VMEM_V7X_BYTES = 64 * 1024 * 1024
VMEM_TEMP_ALLOWANCE = 12 * 1024 * 1024

VMEM_SPEC = pl.BlockSpec(memory_space=pltpu.VMEM)
MESH = pl.DeviceIdType.MESH


def _tile(n, pref, align):
    if n <= pref:
        return n
    t = (pref // align) * align
    while t >= align:
        if n % t == 0:
            return t
        t -= align
    return n


def _nbytes(shape, dtype):
    n = 1
    for d in shape:
        if d is not None:
            n *= d
    return n * jnp.dtype(dtype).itemsize


def _call(body, name, grid, in_specs, out_specs, out_shape, operands, scratch=(), sem=None, after=()):
    outs = out_shape if isinstance(out_shape, (tuple, list)) else (out_shape,)
    ospecs = out_specs if isinstance(out_specs, (tuple, list)) else (out_specs,)
    est = 0
    for spec, arr in list(zip(in_specs, operands)) + list(zip(ospecs, outs)):
        est += 2 * _nbytes(spec.block_shape, arr.dtype)
    for s in scratch:
        if hasattr(s, "shape") and hasattr(s, "dtype"):
            est += _nbytes(s.shape, s.dtype)
    limit = min(est + VMEM_TEMP_ALLOWANCE, VMEM_V7X_BYTES - 4 * 1024 * 1024)
    if sem is None:
        sem = ("arbitrary",) * len(grid)
    n_in, n_after = len(operands), len(after)
    operands = [pltpu.with_memory_space_constraint(o, pltpu.HBM) for o in operands]
    after = [pltpu.with_memory_space_constraint(o, pltpu.HBM) for o in after]
    in_hbm = [pltpu.HBM(o.shape, o.dtype) for o in outs]
    out_shape = in_hbm if isinstance(out_shape, (tuple, list)) else in_hbm[0]

    def ordered_body(*refs):
        body(*refs[:n_in], *refs[n_in + n_after:])

    return pl.pallas_call(
        ordered_body if n_after else body, name=name, grid=grid,
        in_specs=list(in_specs) + [pl.BlockSpec(memory_space=pl.ANY)] * n_after,
        out_specs=out_specs, out_shape=out_shape, scratch_shapes=list(scratch),
        compiler_params=pltpu.CompilerParams(dimension_semantics=sem, vmem_limit_bytes=int(limit)),
    )(*operands, *after)


_DOT_DIMS = {"nn": (((1,), (0,)), ((), ())), "nt": (((1,), (1,)), ((), ())), "tn": (((0,), (0,)), ((), ()))}


def _operand_spec(layout, tr, tc, cols_per_block, pick, group=None):
    if layout == "plain":
        return pl.BlockSpec((tr, tc), lambda *g: pick(*g)[1:])
    if layout == "blk":
        return pl.BlockSpec((group, tr, tc), lambda *g: pick(*g))
    assert layout == "col"
    if group:
        assert tc == cols_per_block
        return pl.BlockSpec((tr, group * tc), lambda *g: (pick(*g)[1], pick(*g)[0]))
    per = cols_per_block // tc
    return pl.BlockSpec((tr, tc), lambda *g: (pick(*g)[1], pick(*g)[0] * per + pick(*g)[2]))


def _matmul(name, dims, a, b, *, M, N, K, nb=1, a_lay="plain", b_lay="plain", o_lay="plain",
            red_block=False, group=None, out_dtype=F32, res=None, tm=1024, tn=None, after=()):
    tm = _tile(M, tm, 128 if dims == "tn" else 16)
    tn = _tile(N, tn or (1024 if red_block else 512), 128)
    tk = K
    gi, gj, gk = M // tm, N // tn, K // tk
    if red_block:
        grid = (gi, gj, nb // (group or 1), gk)
        unpack = lambda i, j, bb, k: (bb, i, j, k)
        red_axes, sem = (2, 3), ("parallel", "parallel", "arbitrary", "arbitrary")
    else:
        grid = (nb, gi, gj, gk)
        unpack = lambda bb, i, j, k: (bb, i, j, k)
        red_axes, sem = (3,), ("parallel", "parallel", "parallel", "arbitrary")

    def picker(f):
        return lambda *g: f(*unpack(*g))

    if dims == "tn":
        a_spec = _operand_spec(a_lay, tk, tm, M, picker(lambda bb, i, j, k: (bb, k, i)), group)
    else:
        a_spec = _operand_spec(a_lay, tm, tk, K, picker(lambda bb, i, j, k: (bb, i, k)), group)
    if dims == "nt":
        b_spec = _operand_spec(b_lay, tn, tk, K, picker(lambda bb, i, j, k: (bb, j, k)), group)
    else:
        b_spec = _operand_spec(b_lay, tk, tn, N, picker(lambda bb, i, j, k: (bb, k, j)), group)
    o_spec = _operand_spec(o_lay, tm, tn, N, picker(lambda bb, i, j, k: (bb, i, j)))
    if o_lay == "plain":
        out_shape = jax.ShapeDtypeStruct((M, N), out_dtype)
    elif o_lay == "blk":
        out_shape = jax.ShapeDtypeStruct((nb, M, N), out_dtype)
    else:
        out_shape = jax.ShapeDtypeStruct((M, nb * N), out_dtype)
    n_red = [grid[ax] for ax in red_axes]
    has_res = res is not None
    one_step = all(n == 1 for n in n_red)

    def contract(a_ref, b_ref):
        if group:
            parts = [(a_ref[p] if a_lay == "blk" else a_ref[:, p * K:(p + 1) * K], b_ref[p]) for p in range(group)]
        else:
            parts = [(a_ref[...], b_ref[...])]
        r = None
        for a_part, b_part in parts:
            d = lax.dot_general(a_part, b_part, _DOT_DIMS[dims], preferred_element_type=F32)
            r = d if r is None else r + d
        return r

    def body_one_step(*refs):
        a_ref, b_ref = refs[:2]
        o_ref = refs[-1]
        r = contract(a_ref, b_ref)
        if has_res:
            r = r + refs[2][...]
        o_ref[...] = r.astype(o_ref.dtype)

    def body(*refs):
        if has_res:
            a_ref, b_ref, r_ref, o_ref, acc = refs
        else:
            a_ref, b_ref, o_ref, acc = refs
        first = functools.reduce(jnp.logical_and, [pl.program_id(ax) == 0 for ax in red_axes])
        last = functools.reduce(jnp.logical_and, [pl.program_id(ax) == n - 1 for ax, n in zip(red_axes, n_red)])

        @pl.when(first)
        def _():
            acc[...] = jnp.zeros_like(acc)

        acc[...] += contract(a_ref, b_ref)

        @pl.when(last)
        def _():
            r = acc[...]
            if has_res:
                r = r + r_ref[...]
            o_ref[...] = r.astype(o_ref.dtype)

    in_specs = [a_spec, b_spec]
    operands = [a, b]
    if has_res:
        in_specs.append(_operand_spec("plain", tm, tn, N, picker(lambda bb, i, j, k: (bb, i, j))))
        operands.append(res)
    if one_step:
        return _call(body_one_step, name, grid, in_specs, o_spec, out_shape, operands, sem=sem, after=after)
    return _call(body, name, grid, in_specs, o_spec, out_shape, operands,
                 scratch=[pltpu.VMEM((tm, tn), F32)], sem=sem, after=after)


def _rows_spec(arr, tr):
    lead = arr.ndim - 2
    return pl.BlockSpec((None,) * lead + (tr, arr.shape[-1]), lambda i: (0,) * lead + (i, 0))


def _cast_bf16(name, w):
    R, C = w.shape[-2:]
    tr = _tile(R, max(8, (3 << 20) // C), 16)

    def body(w_ref, o_ref):
        o_ref[...] = w_ref[...].astype(BF16)

    return _call(body, name, (R // tr,), [_rows_spec(w, tr)],
                 pl.BlockSpec((tr, C), lambda i: (i, 0)), jax.ShapeDtypeStruct((R, C), BF16), [w],
                 sem=("parallel",))


def _rms_fwd(name, x, g, after=()):
    T, D = x.shape
    tm = _tile(T, 128, 16)

    def body(x_ref, g_ref, o_ref):
        xv = x_ref[...]
        r = lax.rsqrt(jnp.mean(xv * xv, axis=-1, keepdims=True) + EPS)
        o_ref[...] = (xv * r * g_ref[...]).astype(BF16)

    return _call(body, name, (T // tm,),
                 [pl.BlockSpec((tm, D), lambda i: (i, 0)), pl.BlockSpec((1, D), lambda i: (0, 0))],
                 pl.BlockSpec((tm, D), lambda i: (i, 0)), jax.ShapeDtypeStruct((T, D), BF16), [x, g],
                 sem=("parallel",), after=after)


def _rms_bwd(name, dxn, x, g, dh=None, after=()):
    T, D = x.shape
    tm = _tile(T, 128, 16)
    with_dx = dh is not None

    def body(*refs):
        if with_dx:
            dxn_ref, x_ref, g_ref, dh_ref, o_ref, ob_ref, dg_ref = refs
        else:
            dxn_ref, x_ref, g_ref, dg_ref = refs
        xv = x_ref[...]
        r = lax.rsqrt(jnp.mean(xv * xv, axis=-1, keepdims=True) + EPS)
        xh = xv * r
        dy = dxn_ref[...].astype(F32)

        @pl.when(pl.program_id(0) == 0)
        def _():
            dg_ref[...] = jnp.zeros_like(dg_ref)

        dg_ref[...] += jnp.sum(dy * xh, axis=0, keepdims=True)
        if with_dx:
            dyg = dy * g_ref[...]
            tot = dh_ref[...] + r * (dyg - xh * jnp.mean(dyg * xh, axis=-1, keepdims=True))
            o_ref[...] = tot
            ob_ref[...] = tot.astype(BF16)

    row = pl.BlockSpec((tm, D), lambda i: (i, 0))
    vec = pl.BlockSpec((1, D), lambda i: (0, 0))
    if with_dx:
        return _call(body, name, (T // tm,), [row, row, vec, row], (row, row, vec),
                     (jax.ShapeDtypeStruct((T, D), F32), jax.ShapeDtypeStruct((T, D), BF16),
                      jax.ShapeDtypeStruct((1, D), F32)), [dxn, x, g, dh], after=after)
    return _call(body, name, (T // tm,), [row, row, vec], vec, jax.ShapeDtypeStruct((1, D), F32), [dxn, x, g])


def _loss_head(h, target, g):
    T, D = h.shape
    tm = _tile(T, 128, 16)

    def body(h_ref, t_ref, g_ref, o_ref, ob_ref, loss_ref, dg_ref):
        xv = h_ref[...]
        gv = g_ref[...]
        r = lax.rsqrt(jnp.mean(xv * xv, axis=-1, keepdims=True) + EPS)
        xh = xv * r
        e = xh * gv - t_ref[...]

        @pl.when(pl.program_id(0) == 0)
        def _():
            dg_ref[...] = jnp.zeros_like(dg_ref)
            loss_ref[...] = jnp.zeros_like(loss_ref)

        loss_ref[...] += 0.5 * jnp.sum(jnp.mean(e * e, axis=-1, keepdims=True), axis=0, keepdims=True)
        dy = e * (1.0 / D)
        dg_ref[...] += jnp.sum(dy * xh, axis=0, keepdims=True)
        dyg = dy * gv
        dx = r * (dyg - xh * jnp.mean(dyg * xh, axis=-1, keepdims=True))
        o_ref[...] = dx
        ob_ref[...] = dx.astype(BF16)

    row = pl.BlockSpec((tm, D), lambda i: (i, 0))
    vec = pl.BlockSpec((1, D), lambda i: (0, 0))
    return _call(body, "loss_head", (T // tm,), [row, row, vec],
                 (row, row, pl.BlockSpec((1, 128), lambda i: (0, 0)), vec),
                 (jax.ShapeDtypeStruct((T, D), F32), jax.ShapeDtypeStruct((T, D), BF16),
                  jax.ShapeDtypeStruct((1, 128), F32), jax.ShapeDtypeStruct((1, D), F32)), [h, target, g])


ROW_CHUNK = 64
SUBLANES = 8


def _col_chunks(width):
    return [slice(c0, min(c0 + GROUP_DIM, width)) for c0 in range(0, width, GROUP_DIM)]


def _row_chunks(n_rows):
    return [(r0, min(ROW_CHUNK, n_rows - r0)) for r0 in range(0, n_rows, ROW_CHUNK)]


def _pad_rows(K):
    return -(-(K - 1) // SUBLANES) * SUBLANES


def _shifted_back(buf, K, r0, nr, cs):
    pad = _pad_rows(K)
    win = buf[pl.ds(HALO + r0 - pad, nr + pad), cs]
    for b in range(min(SUBLANES, K)):
        rolled = win if b == 0 else pltpu.roll(win, b, axis=0)
        for a in range((K - 1 - b) // SUBLANES + 1):
            yield K - 1 - (SUBLANES * a + b), rolled[pad - SUBLANES * a:pad - SUBLANES * a + nr]


def _conv_fwd(buf, w_ref, K, r0, nr, cs):
    y = None
    for k, xs in _shifted_back(buf, K, r0, nr, cs):
        term = xs * w_ref[pl.ds(k, 1), cs]
        y = term if y is None else y + term
    return y


def _conv_bwd_input(buf, w_ref, K, r0, nr, cs):
    pad = _pad_rows(K)
    win = buf[pl.ds(r0, nr + pad), cs]
    dx = None
    for b in range(min(SUBLANES, K)):
        rolled = win if b == 0 else pltpu.roll(win, nr + pad - b, axis=0)
        for a in range((K - 1 - b) // SUBLANES + 1):
            k = K - 1 - (SUBLANES * a + b)
            term = rolled[SUBLANES * a:SUBLANES * a + nr] * w_ref[pl.ds(k, 1), cs]
            dx = term if dx is None else dx + term
    return dx


def _fold_rows(v):
    nr, lanes = v.shape
    if nr % SUBLANES:
        return jnp.sum(v, axis=0, keepdims=True)
    return jnp.sum(v.reshape(nr // SUBLANES, SUBLANES, lanes), axis=0)


def _conv_bwd_weight(accs, dy, buf, K, r0, nr, cs):
    accs = list(accs)
    for k, xs in _shifted_back(buf, K, r0, nr, cs):
        accs[k] = accs[k] + _fold_rows(dy * xs)
    return accs


def _add_row(ref, row, cs, acc):
    ref[pl.ds(row, 1), cs] += jnp.sum(acc, axis=0, keepdims=True)


def _sigmoid(z):
    return 0.5 * jnp.tanh(0.5 * z) + 0.5


def _silu_grad(z, sig):
    return sig * (1.0 + z * (1.0 - sig))


def _group_norm(xg):
    xc = xg - jnp.mean(xg, axis=-1, keepdims=True)
    rstd = lax.rsqrt(jnp.mean(xc * xc, axis=-1, keepdims=True) + EPS)
    return xc * rstd, rstd


def _mixer_tiles(T, CW):
    tm = _tile(T, 512, HALO)
    tc = _tile(CW, 256, GROUP_DIM)
    return tm, tc, tm // HALO, CW // tc


def _mixer_fwd(proj, caw, cab, lng, lnb, cbw, T, CW):
    KA, KB = caw.shape[0], cbw.shape[0]
    tm, tc, hb, nc = _mixer_tiles(T, CW)

    def sec(s):
        return pl.BlockSpec((tm, tc), lambda i, c: (i, s * nc + c))

    def sec_prev(s):
        return pl.BlockSpec((HALO, tc), lambda i, c: (jnp.maximum(i * hb - 1, 0), s * nc + c))

    def chan(rows):
        return pl.BlockSpec((rows, tc), lambda i, c: (0, c))

    def body(av, ag, bg, cg, bh, avh, agh, cgh, bhh, caw_ref, cab_ref, lng_ref, lnb_ref, cbw_ref,
             mix_ref, u1_ref, bufa, bufb):
        first = pl.program_id(0) == 0
        bufa[pl.ds(0, HALO), :] = jnp.where(first, 0.0, avh[...].astype(F32) * _sigmoid(agh[...].astype(F32)))
        bufb[pl.ds(0, HALO), :] = jnp.where(first, 0.0, cgh[...].astype(F32) * bhh[...].astype(F32))
        for cs in _col_chunks(tc):
            for r0, nr in _row_chunks(tm):
                rows = pl.ds(r0, nr)
                bufa[pl.ds(HALO + r0, nr), cs] = av[rows, cs].astype(F32) * _sigmoid(ag[rows, cs].astype(F32))
                bufb[pl.ds(HALO + r0, nr), cs] = cg[rows, cs].astype(F32) * bh[rows, cs].astype(F32)
        for cs in _col_chunks(tc):
            for r0, nr in _row_chunks(tm):
                rows = pl.ds(r0, nr)
                u1 = _conv_fwd(bufa, caw_ref, KA, r0, nr, cs) + cab_ref[:, cs]
                u1_ref[rows, cs] = u1
                y, _ = _group_norm(u1)
                z = y * lng_ref[:, cs] + lnb_ref[:, cs]
                mix_ref[0, rows, cs] = (z * _sigmoid(z)).astype(BF16)
                mix_ref[1, rows, cs] = (bg[rows, cs].astype(F32) * _conv_fwd(bufb, cbw_ref, KB, r0, nr, cs)).astype(BF16)

    in_specs = [sec(0), sec(1), sec(2), sec(3), sec(4), sec_prev(0), sec_prev(1), sec_prev(3), sec_prev(4),
                chan(KA), chan(1), chan(1), chan(1), chan(KB)]
    operands = [proj] * 9 + [caw, cab, lng, lnb, cbw]
    return _call(body, "mixer_fwd", (T // tm, nc), in_specs,
                 (pl.BlockSpec((2, tm, tc), lambda i, c: (0, i, c)), pl.BlockSpec((tm, tc), lambda i, c: (i, c))),
                 (jax.ShapeDtypeStruct((2, T, CW), BF16), jax.ShapeDtypeStruct((T, CW), F32)), operands,
                 scratch=[pltpu.VMEM((HALO + tm, tc), F32), pltpu.VMEM((HALO + tm, tc), F32)],
                 sem=("parallel", "parallel"))


def _mixer_bwd1(dmix, proj, u1, caw, lng, lnb, cbw, T, CW):
    KA, KB = caw.shape[0], cbw.shape[0]
    tm, tc, hb, nc = _mixer_tiles(T, CW)

    def sec(s):
        return pl.BlockSpec((tm, tc), lambda c, i: (i, s * nc + c))

    def sec_prev(s):
        return pl.BlockSpec((HALO, tc), lambda c, i: (jnp.maximum(i * hb - 1, 0), s * nc + c))

    def chan(rows):
        return pl.BlockSpec((rows, tc), lambda c, i: (0, c))

    tile = pl.BlockSpec((tm, tc), lambda c, i: (i, c))

    def body(du, dv, u1_ref, av, ag, bg, cg, bh, avh, agh, cgh, bhh, lng_ref, lnb_ref, cbw_ref,
             du1_ref, dcv_ref, dbg_ref, dcaw_ref, dcab_ref, dlng_ref, dlnb_ref, dcbw_ref, bufa, bufb):
        first = pl.program_id(1) == 0

        @pl.when(first)
        def _():
            for r in (dcaw_ref, dcab_ref, dlng_ref, dlnb_ref, dcbw_ref):
                r[...] = jnp.zeros_like(r)

        bufa[pl.ds(0, HALO), :] = jnp.where(first, 0.0, avh[...].astype(F32) * _sigmoid(agh[...].astype(F32)))
        bufb[pl.ds(0, HALO), :] = jnp.where(first, 0.0, cgh[...].astype(F32) * bhh[...].astype(F32))
        for cs in _col_chunks(tc):
            for r0, nr in _row_chunks(tm):
                rows = pl.ds(r0, nr)
                bufa[pl.ds(HALO + r0, nr), cs] = av[rows, cs].astype(F32) * _sigmoid(ag[rows, cs].astype(F32))
                bufb[pl.ds(HALO + r0, nr), cs] = cg[rows, cs].astype(F32) * bh[rows, cs].astype(F32)
        for cs in _col_chunks(tc):
            lanes = cs.stop - cs.start
            zero = jnp.zeros((SUBLANES, lanes), F32)
            a_lng, a_lnb, a_cab = zero, zero, zero
            a_caw, a_cbw = [zero] * KA, [zero] * KB
            gamma, beta = lng_ref[:, cs], lnb_ref[:, cs]
            for r0, nr in _row_chunks(tm):
                rows = pl.ds(r0, nr)
                y, rstd = _group_norm(u1_ref[rows, cs])
                z = y * gamma + beta
                dz = du[rows, cs].astype(F32) * _silu_grad(z, _sigmoid(z))
                a_lng = a_lng + _fold_rows(dz * y)
                a_lnb = a_lnb + _fold_rows(dz)
                dy = dz * gamma
                du1 = rstd * (dy - jnp.mean(dy, axis=-1, keepdims=True)
                              - y * jnp.mean(dy * y, axis=-1, keepdims=True))
                du1_ref[rows, cs] = du1
                a_cab = a_cab + _fold_rows(du1)
                a_caw = _conv_bwd_weight(a_caw, du1, bufa, KA, r0, nr, cs)

                dvv = dv[rows, cs].astype(F32)
                dbg_ref[rows, cs] = (dvv * _conv_fwd(bufb, cbw_ref, KB, r0, nr, cs)).astype(BF16)
                dcv = dvv * bg[rows, cs].astype(F32)
                dcv_ref[rows, cs] = dcv
                a_cbw = _conv_bwd_weight(a_cbw, dcv, bufb, KB, r0, nr, cs)
            _add_row(dlng_ref, 0, cs, a_lng)
            _add_row(dlnb_ref, 0, cs, a_lnb)
            _add_row(dcab_ref, 0, cs, a_cab)
            for k in range(KA):
                _add_row(dcaw_ref, k, cs, a_caw[k])
            for k in range(KB):
                _add_row(dcbw_ref, k, cs, a_cbw[k])

    in_specs = [sec(0), sec(1), tile, sec(0), sec(1), sec(2), sec(3), sec(4),
                sec_prev(0), sec_prev(1), sec_prev(3), sec_prev(4), chan(1), chan(1), chan(KB)]
    operands = [dmix, dmix, u1] + [proj] * 9 + [lng, lnb, cbw]
    return _call(body, "mixer_bwd1", (nc, T // tm), in_specs,
                 (tile, tile, tile, chan(KA), chan(1), chan(1), chan(1), chan(KB)),
                 (jax.ShapeDtypeStruct((T, CW), F32), jax.ShapeDtypeStruct((T, CW), F32),
                  jax.ShapeDtypeStruct((T, CW), BF16), jax.ShapeDtypeStruct((KA, CW), F32),
                  jax.ShapeDtypeStruct((1, CW), F32), jax.ShapeDtypeStruct((1, CW), F32),
                  jax.ShapeDtypeStruct((1, CW), F32), jax.ShapeDtypeStruct((KB, CW), F32)), operands,
                 scratch=[pltpu.VMEM((HALO + tm, tc), F32), pltpu.VMEM((HALO + tm, tc), F32)],
                 sem=("parallel", "arbitrary"))


def _mixer_bwd2(du1, dcv, proj, caw, cbw, T, CW):
    KA, KB = caw.shape[0], cbw.shape[0]
    tm, tc, hb, nc = _mixer_tiles(T, CW)
    n_i = T // tm

    def sec(s):
        return pl.BlockSpec((tm, tc), lambda i, c: (i, s * nc + c))

    def chan(rows):
        return pl.BlockSpec((rows, tc), lambda i, c: (0, c))

    tile = pl.BlockSpec((tm, tc), lambda i, c: (i, c))
    nxt = pl.BlockSpec((HALO, tc), lambda i, c: (jnp.minimum((i + 1) * hb, n_i * hb - 1), c))

    def body(du1_ref, du1n, dcv_ref, dcvn, av, ag, cg, bh, caw_ref, cbw_ref, dav, dag, dcg, dbh, bufa, bufb):
        last = pl.program_id(0) == n_i - 1
        bufa[pl.ds(0, tm), :] = du1_ref[...]
        bufa[pl.ds(tm, HALO), :] = jnp.where(last, 0.0, du1n[...])
        bufb[pl.ds(0, tm), :] = dcv_ref[...]
        bufb[pl.ds(tm, HALO), :] = jnp.where(last, 0.0, dcvn[...])
        for cs in _col_chunks(tc):
            for r0, nr in _row_chunks(tm):
                rows = pl.ds(r0, nr)
                du0 = _conv_bwd_input(bufa, caw_ref, KA, r0, nr, cs)
                sig = _sigmoid(ag[rows, cs].astype(F32))
                dav[rows, cs] = (du0 * sig).astype(BF16)
                dag[rows, cs] = (du0 * av[rows, cs].astype(F32) * (sig * (1.0 - sig))).astype(BF16)
                dch = _conv_bwd_input(bufb, cbw_ref, KB, r0, nr, cs)
                dcg[rows, cs] = (dch * bh[rows, cs].astype(F32)).astype(BF16)
                dbh[rows, cs] = (dch * cg[rows, cs].astype(F32)).astype(BF16)

    in_specs = [tile, nxt, tile, nxt, sec(0), sec(1), sec(3), sec(4), chan(KA), chan(KB)]
    operands = [du1, du1, dcv, dcv, proj, proj, proj, proj, caw, cbw]
    out = jax.ShapeDtypeStruct((T, CW), BF16)
    return _call(body, "mixer_bwd2", (n_i, nc), in_specs, (tile, tile, tile, tile), (out, out, out, out),
                 operands, scratch=[pltpu.VMEM((HALO + tm, tc), F32), pltpu.VMEM((HALO + tm, tc), F32)],
                 sem=("parallel", "parallel"))


def _ffn_tiles(T):
    tm = _tile(T, 512, HALO)
    return tm, tm // HALO, T // tm


def _ffn_act_fwd(gpre, up, cfw):
    nb, T, F = gpre.shape
    KF = cfw.shape[1]
    tm, hb, n_i = _ffn_tiles(T)
    tile = pl.BlockSpec((None, tm, F), lambda b, i: (b, i, 0))
    prev = pl.BlockSpec((None, HALO, F), lambda b, i: (b, jnp.maximum(i * hb - 1, 0), 0))
    wspec = pl.BlockSpec((None, KF, F), lambda b, i: (b, 0, 0))

    def body(g_ref, gh_ref, up_ref, w_ref, f_ref, buf):
        buf[pl.ds(HALO, tm), :] = g_ref[...].astype(F32)
        buf[pl.ds(0, HALO), :] = jnp.where(pl.program_id(1) == 0, 0.0, gh_ref[...].astype(F32))
        for cs in _col_chunks(F):
            for r0, nr in _row_chunks(tm):
                rows = pl.ds(r0, nr)
                g = _conv_fwd(buf, w_ref, KF, r0, nr, cs)
                f_ref[rows, cs] = (g * _sigmoid(g) * up_ref[rows, cs].astype(F32)).astype(BF16)

    return _call(body, "ffn_act_fwd", (nb, n_i), [tile, prev, tile, wspec], tile,
                 jax.ShapeDtypeStruct((nb, T, F), BF16), [gpre, gpre, up, cfw],
                 scratch=[pltpu.VMEM((HALO + tm, F), F32)], sem=("parallel", "parallel"))


def _ffn_act_bwd1(df, gpre, up, cfw):
    nb, T, F = gpre.shape
    KF = cfw.shape[1]
    tm, hb, n_i = _ffn_tiles(T)
    tile = pl.BlockSpec((None, tm, F), lambda b, i: (b, i, 0))
    prev = pl.BlockSpec((None, HALO, F), lambda b, i: (b, jnp.maximum(i * hb - 1, 0), 0))
    wspec = pl.BlockSpec((None, KF, F), lambda b, i: (b, 0, 0))

    def body(df_ref, g_ref, gh_ref, up_ref, w_ref, dg_ref, dup_ref, dw_ref, buf):
        first = pl.program_id(1) == 0

        @pl.when(first)
        def _():
            dw_ref[...] = jnp.zeros_like(dw_ref)

        buf[pl.ds(HALO, tm), :] = g_ref[...].astype(F32)
        buf[pl.ds(0, HALO), :] = jnp.where(first, 0.0, gh_ref[...].astype(F32))
        for cs in _col_chunks(F):
            accs = [jnp.zeros((SUBLANES, cs.stop - cs.start), F32)] * KF
            for r0, nr in _row_chunks(tm):
                rows = pl.ds(r0, nr)
                g = _conv_fwd(buf, w_ref, KF, r0, nr, cs)
                sig = _sigmoid(g)
                dfv = df_ref[rows, cs].astype(F32)
                dup_ref[rows, cs] = (dfv * (g * sig)).astype(BF16)
                dg = dfv * up_ref[rows, cs].astype(F32) * _silu_grad(g, sig)
                dg_ref[rows, cs] = dg.astype(BF16)
                accs = _conv_bwd_weight(accs, dg, buf, KF, r0, nr, cs)
            for k in range(KF):
                _add_row(dw_ref, k, cs, accs[k])

    return _call(body, "ffn_act_bwd1", (nb, n_i), [tile, tile, prev, tile, wspec], (tile, tile, wspec),
                 (jax.ShapeDtypeStruct((nb, T, F), BF16), jax.ShapeDtypeStruct((nb, T, F), BF16),
                  jax.ShapeDtypeStruct((nb, KF, F), F32)), [df, gpre, gpre, up, cfw],
                 scratch=[pltpu.VMEM((HALO + tm, F), F32)], sem=("parallel", "arbitrary"))


def _ffn_act_bwd2(dg, cfw):
    nb, T, F = dg.shape
    KF = cfw.shape[1]
    tm, hb, n_i = _ffn_tiles(T)
    tile = pl.BlockSpec((None, tm, F), lambda b, i: (b, i, 0))
    nxt = pl.BlockSpec((None, HALO, F), lambda b, i: (b, jnp.minimum((i + 1) * hb, n_i * hb - 1), 0))
    wspec = pl.BlockSpec((None, KF, F), lambda b, i: (b, 0, 0))

    def body(dg_ref, dgn_ref, w_ref, o_ref, buf):
        buf[pl.ds(0, tm), :] = dg_ref[...].astype(F32)
        buf[pl.ds(tm, HALO), :] = jnp.where(pl.program_id(1) == n_i - 1, 0.0, dgn_ref[...].astype(F32))
        for cs in _col_chunks(F):
            for r0, nr in _row_chunks(tm):
                o_ref[pl.ds(r0, nr), cs] = _conv_bwd_input(buf, w_ref, KF, r0, nr, cs).astype(BF16)

    return _call(body, "ffn_act_bwd2", (nb, n_i), [tile, nxt, wspec], tile,
                 jax.ShapeDtypeStruct((nb, T, F), BF16), [dg, dg, cfw],
                 scratch=[pltpu.VMEM((HALO + tm, F), F32)], sem=("parallel", "parallel"))


def _softmax_rows(s):
    e = jnp.exp(s - jnp.max(s, axis=-1, keepdims=True))
    return e / jnp.sum(e, axis=-1, keepdims=True)


def _attn_fwd(q, k, v):
    T, D = q.shape
    Mm = k.shape[0]
    hd = D // N_XATTN_HEADS
    scale = hd ** -0.5
    tm = _tile(T, 256, 16)

    def body(q_ref, k_ref, v_ref, o_ref):
        for h in range(N_XATTN_HEADS):
            sl = slice(h * hd, (h + 1) * hd)
            s = lax.dot_general(q_ref[:, sl], k_ref[:, sl], _DOT_DIMS["nt"], preferred_element_type=F32) * scale
            p = _softmax_rows(s).astype(BF16)
            o_ref[:, sl] = jnp.dot(p, v_ref[:, sl], preferred_element_type=F32).astype(BF16)

    row = pl.BlockSpec((tm, D), lambda i: (i, 0))
    full = pl.BlockSpec((Mm, D), lambda i: (0, 0))
    return _call(body, "attn_fwd", (T // tm,), [row, full, full], row, jax.ShapeDtypeStruct((T, D), BF16),
                 [q, k, v], sem=("parallel",))


def _attn_bwd(q, k, v, do):
    T, D = q.shape
    Mm = k.shape[0]
    hd = D // N_XATTN_HEADS
    scale = hd ** -0.5
    tm = _tile(T, 256, 16)
    n_i = T // tm

    def body(q_ref, do_ref, k_ref, v_ref, dq_ref, dk_ref, dv_ref, dk_acc, dv_acc):
        @pl.when(pl.program_id(0) == 0)
        def _():
            dk_acc[...] = jnp.zeros_like(dk_acc)
            dv_acc[...] = jnp.zeros_like(dv_acc)

        for h in range(N_XATTN_HEADS):
            sl = slice(h * hd, (h + 1) * hd)
            qh, kh, doh = q_ref[:, sl], k_ref[:, sl], do_ref[:, sl]
            s = lax.dot_general(qh, kh, _DOT_DIMS["nt"], preferred_element_type=F32) * scale
            p = _softmax_rows(s)
            dv_acc[:, sl] += lax.dot_general(p.astype(BF16), doh, _DOT_DIMS["tn"], preferred_element_type=F32)
            dp = lax.dot_general(doh, v_ref[:, sl], _DOT_DIMS["nt"], preferred_element_type=F32)
            ds = (p * (dp - jnp.sum(dp * p, axis=-1, keepdims=True)) * scale).astype(BF16)
            dq_ref[:, sl] = jnp.dot(ds, kh, preferred_element_type=F32).astype(BF16)
            dk_acc[:, sl] += lax.dot_general(ds, qh, _DOT_DIMS["tn"], preferred_element_type=F32)

        @pl.when(pl.program_id(0) == n_i - 1)
        def _():
            dk_ref[...] = dk_acc[...].astype(BF16)
            dv_ref[...] = dv_acc[...].astype(BF16)

    row = pl.BlockSpec((tm, D), lambda i: (i, 0))
    full = pl.BlockSpec((Mm, D), lambda i: (0, 0))
    return _call(body, "attn_bwd", (n_i,), [row, row, full, full], (row, full, full),
                 (jax.ShapeDtypeStruct((T, D), BF16), jax.ShapeDtypeStruct((Mm, D), BF16),
                  jax.ShapeDtypeStruct((Mm, D), BF16)), [q, do, k, v],
                 scratch=[pltpu.VMEM((Mm, D), F32), pltpu.VMEM((Mm, D), F32)])


def _position():
    x, y, c = lax.axis_index("x"), lax.axis_index("y"), lax.axis_index("c")
    return x, y, c


def _peer(pos, k):
    x, y, c = pos
    return (1 - x if k & 4 else x, 1 - y if k & 2 else y, 1 - c if k & 1 else c)


def _index(pos):
    x, y, c = pos
    return 4 * x + 2 * y + c


def _sequencer_kernel(body, name, collective_id, out_type, operands):
    return pl.kernel(
        body, name=name, out_type=out_type,
        mesh=plsc.ScalarSubcoreMesh(axis_name="sequencer", num_cores=1),
        scratch_types=[pltpu.SemaphoreType.DMA, pltpu.SemaphoreType.DMA((7,)), pltpu.SemaphoreType.DMA],
        compiler_params=pltpu.CompilerParams(collective_id=collective_id),
    )(*operands)


def _handshake(peers):
    barrier = pltpu.get_barrier_semaphore()
    for peer in peers:
        pl.semaphore_signal(barrier, inc=1, device_id=peer, device_id_type=MESH)
    pl.semaphore_wait(barrier, len(peers))


def _sequencer_all_gather(name, collective_id, shards):
    n = len(shards)

    def body(*refs):
        x_refs, out_refs = refs[:n], refs[n:2 * n]
        send_sem, recv_sems, local_sem = refs[2 * n:]
        me = _position()
        x, y, c = me
        sibling = _peer(me, 1)
        first = (x + (1 - c) - 2 * x * (1 - c), y + c - 2 * y * c, c)
        second = (x + c - 2 * x * c, y + (1 - c) - 2 * y * (1 - c), c)
        diagonal = _peer(me, 6)
        _handshake([sibling, first, second])

        def copy(a, k, block, to, own=False):
            dst = out_refs[a].at[_index(block)]
            return pltpu.make_async_remote_copy(
                src_ref=x_refs[a] if own else dst, dst_ref=dst, send_sem=send_sem, recv_sem=recv_sems.at[k],
                device_id=to, device_id_type=MESH)

        local = [pltpu.make_async_copy(x_refs[a], out_refs[a].at[_index(me)], local_sem) for a in range(n)]
        started = [copy(a, 1 + j, me, peer, own=True) for a in range(n) for j, peer in enumerate((first, second))]
        started += [copy(a, 0, me, sibling, own=True) for a in range(n)]
        for cp in started + local:
            cp.start()
        for k, origin in ((1, first), (2, second), (3, diagonal)):
            for a in range(n):
                copy(a, k, origin, me).wait_recv()
            passed = [copy(a, 3 + k, origin, sibling) for a in range(n)]
            if k == 1:
                passed = [copy(a, 3, origin, second) for a in range(n)] + passed
            for cp in passed:
                cp.start()
            started += passed
        for k in (0, 4, 5, 6):
            for a in range(n):
                copy(a, k, sibling, me).wait_recv()
        for cp in started:
            cp.wait_send()
        for cp in local:
            cp.wait()

    return _sequencer_kernel(body, name, collective_id,
                             [jax.ShapeDtypeStruct((N_DEV,) + s.shape, s.dtype) for s in shards], shards)


def _chip_index(pos):
    return 2 * pos[0] + pos[1]


def _sequencer_to_sibling(name, collective_id, parts):
    n = len(parts)

    def body(*refs):
        p_refs, out_refs = refs[:n], refs[n:2 * n]
        send_sem, recv_sems, _ = refs[2 * n:]
        me = _position()
        sibling = _peer(me, 1)
        _handshake([sibling])
        copies = [pltpu.make_async_remote_copy(
            src_ref=p_refs[a].at[2 * q + sibling[2]], dst_ref=out_refs[a].at[q], send_sem=send_sem,
            recv_sem=recv_sems.at[0], device_id=sibling, device_id_type=MESH)
            for a in range(n) for q in range(N_DEV // 2)]
        for cp in copies:
            cp.start()
        for cp in copies:
            cp.wait_recv()
        for cp in copies:
            cp.wait_send()

    return _sequencer_kernel(body, name, collective_id,
                             [jax.ShapeDtypeStruct((N_DEV // 2,) + p.shape[1:], p.dtype) for p in parts], parts)


def _sequencer_to_chips(name, collective_id, sums):
    n = len(sums)

    def body(*refs):
        s_refs, out_refs = refs[:n], refs[n:2 * n]
        send_sem, recv_sems, local_sem = refs[2 * n:]
        me = _position()
        my_chip = _chip_index(me)
        peers = [_peer(me, 4), _peer(me, 2), _peer(me, 6)]
        _handshake(peers)
        local = [pltpu.make_async_copy(s_refs[a].at[my_chip], out_refs[a].at[my_chip], local_sem) for a in range(n)]
        sends = [pltpu.make_async_remote_copy(
            src_ref=s_refs[a].at[_chip_index(peer)], dst_ref=out_refs[a].at[my_chip], send_sem=send_sem,
            recv_sem=recv_sems.at[1 + j], device_id=peer, device_id_type=MESH)
            for a in range(n) for j, peer in enumerate(peers)]
        for cp in sends + local:
            cp.start()
        for j, peer in enumerate(peers):
            for a in range(n):
                pltpu.make_async_remote_copy(
                    src_ref=s_refs[a].at[my_chip], dst_ref=out_refs[a].at[_chip_index(peer)], send_sem=send_sem,
                    recv_sem=recv_sems.at[1 + j], device_id=peer, device_id_type=MESH).wait_recv()
        for cp in sends:
            cp.wait_send()
        for cp in local:
            cp.wait()

    return _sequencer_kernel(body, name, collective_id,
                             [jax.ShapeDtypeStruct(s.shape, s.dtype) for s in sums], sums)


def _chip_sum(name, parts, got, after=()):
    _, R, C = parts.shape
    tr = _tile(R, max(8, (3 << 20) // C), 16)
    n_after = len(after)
    limit = 6 * _nbytes((tr, C), parts.dtype) + VMEM_TEMP_ALLOWANCE

    def body(c_ref, p_ref, g_ref, *rest):
        o_ref = rest[n_after]
        o_ref[...] = (p_ref[...].astype(F32) + g_ref[...].astype(F32)).astype(o_ref.dtype)

    blk = pl.BlockSpec((None, tr, C), lambda q, i, c_ref: (q, i, 0))
    mine = pl.BlockSpec((None, tr, C), lambda q, i, c_ref: (2 * q + c_ref[0], i, 0))
    core = lax.axis_index("c").astype(jnp.int32).reshape(1)
    parts, got = [pltpu.with_memory_space_constraint(o, pltpu.HBM) for o in (parts, got)]
    after = [pltpu.with_memory_space_constraint(o, pltpu.HBM) for o in after]
    return pl.pallas_call(
        body, name=name, out_shape=pltpu.HBM((N_DEV // 2, R, C), parts.dtype),
        grid_spec=pltpu.PrefetchScalarGridSpec(
            num_scalar_prefetch=1, grid=(N_DEV // 2, R // tr),
            in_specs=[mine, blk] + [pl.BlockSpec(memory_space=pl.ANY)] * n_after, out_specs=blk),
        compiler_params=pltpu.CompilerParams(dimension_semantics=("parallel", "parallel"),
                                             vmem_limit_bytes=int(limit)),
    )(core, parts, got, *after)


def _all_reduce_rows(name, v):
    R, C = v.shape

    def body(v_ref, out_ref, gath, send_sems, recv_sems):
        me = _position()
        gath[_index(me)] = v_ref[...]
        sends = []
        for k in range(1, N_DEV):
            peer = _peer(me, k)
            sends.append(pltpu.make_async_remote_copy(
                src_ref=v_ref, dst_ref=gath.at[_index(me)], send_sem=send_sems.at[k - 1],
                recv_sem=recv_sems.at[k - 1], device_id=peer, device_id_type=MESH))
        for cp in sends:
            cp.start()
        for k in range(1, N_DEV):
            peer = _peer(me, k)
            pltpu.make_async_remote_copy(
                src_ref=v_ref, dst_ref=gath.at[_index(peer)], send_sem=send_sems.at[k - 1],
                recv_sem=recv_sems.at[k - 1], device_id=peer, device_id_type=MESH).wait_recv()
        for cp in sends:
            cp.wait_send()
        tot = gath[0]
        for s in range(1, N_DEV):
            tot = tot + gath[s]
        out_ref[...] = tot

    return pl.pallas_call(
        body, name=name, out_shape=jax.ShapeDtypeStruct((R, C), F32),
        in_specs=[VMEM_SPEC], out_specs=VMEM_SPEC,
        scratch_shapes=[pltpu.VMEM((N_DEV, R, C), F32), pltpu.SemaphoreType.DMA((7,)),
                        pltpu.SemaphoreType.DMA((7,))],
    )(v)


def _adamw_math(g, w, m, v):
    m = ADAM_B1 * m + (1.0 - ADAM_B1) * g
    v = ADAM_B2 * v + (1.0 - ADAM_B2) * (g * g)
    m_hat = m / (1.0 - ADAM_B1 ** ADAM_STEP)
    v_hat = v / (1.0 - ADAM_B2 ** ADAM_STEP)
    delta = -ADAM_LR * (m_hat / (jnp.sqrt(v_hat) + ADAM_EPS) + ADAM_WD * w)
    return delta, m, v


def _adamw(name, parts, w, m, v, after=()):
    n, R, C = parts.shape
    tr = _tile(R, max(8, (1 << 18) // C), 16)

    def body(p_ref, w_ref, m_ref, v_ref, g_ref, d_ref, nm_ref, nv_ref):
        g = p_ref[0].astype(F32)
        for s in range(1, n):
            g = g + p_ref[s].astype(F32)
        g_ref[...] = g
        d_ref[...], nm_ref[...], nv_ref[...] = _adamw_math(g, w_ref[...], m_ref[...], v_ref[...])

    blk = _rows_spec(w, tr)
    out = jax.ShapeDtypeStruct(w.shape, F32)
    return _call(body, name, (R // tr,), [pl.BlockSpec((n, tr, C), lambda i: (0, i, 0)), blk, blk, blk],
                 (blk, blk, blk, blk), (out, out, out, out), [parts, w, m, v], sem=("parallel",), after=after)


def kernel(x, mem, g_mix, w_in, conv_a_w, conv_a_b, ln_a_g, ln_a_b, conv_b_w, w_out, g_xattn, g_mem, w_q, w_k, w_v, w_o, g_ffn, w_gate, w_up, conv_f_w, w_down, g_final, loss_target, m_g_mix, m_w_in, m_conv_a_w, m_conv_a_b, m_ln_a_g, m_ln_a_b, m_conv_b_w, m_w_out, m_g_xattn, m_g_mem, m_w_q, m_w_k, m_w_v, m_w_o, m_g_ffn, m_w_gate, m_w_up, m_conv_f_w, m_w_down, m_g_final, v_g_mix, v_w_in, v_conv_a_w, v_conv_a_b, v_ln_a_g, v_ln_a_b, v_conv_b_w, v_w_out, v_g_xattn, v_g_mem, v_w_q, v_w_k, v_w_v, v_w_o, v_g_ffn, v_w_gate, v_w_up, v_conv_f_w, v_w_down, v_g_final):
    T, D = x.shape[1], x.shape[2]
    Mm = mem.shape[1]
    CW = conv_a_b.shape[1]
    INB = w_in.shape[2]
    FB = w_gate.shape[2]
    KA, KB, KF = conv_a_w.shape[1], conv_b_w.shape[1], conv_f_w.shape[1]
    DB = D // N_DEV
    assert 5 * CW == N_DEV * INB and 2 * CW == D

    x2, mem2, tgt = x[0], mem[0], loss_target[0]
    g_mem2, g_final2 = g_mem.reshape(1, D), g_final.reshape(1, D)

    def bf16(name, w):
        return _cast_bf16("cast_" + name, w)

    Win, caw, cbw = _sequencer_all_gather(
        "ag_in", AG_ID, [bf16("w_in", w_in), conv_a_w[0], conv_b_w[0]])
    Wout, = _sequencer_all_gather("ag_out", AG_ID, [bf16("w_out", w_out)])
    Wq, Wk, Wv, Wo = _sequencer_all_gather(
        "ag_attn", AG_ID, [bf16("w_q", w_q), bf16("w_k", w_k), bf16("w_v", w_v), bf16("w_o", w_o)])
    def transposed(w):
        return jnp.transpose(w[0])

    gate_t = [transposed(a) for a in (w_gate, m_w_gate, v_w_gate)]
    up_t = [transposed(a) for a in (w_up, m_w_up, v_w_up)]
    WgateT, cfw = _sequencer_all_gather("ag_gate", AG_ID, [bf16("w_gate", gate_t[0]), conv_f_w[0]])
    WupT, = _sequencer_all_gather("ag_up", AG_ID, [bf16("w_up", up_t[0])])
    Wdown, = _sequencer_all_gather("ag_down", AG_ID, [bf16("w_down", w_down)])
    Wout, Wq, Wk, Wv, Wo = [w.reshape(D, D) for w in (Wout, Wq, Wk, Wv, Wo)]
    caw = jnp.transpose(caw, (1, 0, 2)).reshape(KA, CW)
    cbw = jnp.transpose(cbw, (1, 0, 2)).reshape(KB, CW)

    xn1 = _rms_fwd("rms_mix", x2, g_mix)
    proj = _matmul("mm_proj", "nn", xn1, Win, M=T, N=INB, K=D, nb=N_DEV, b_lay="blk", o_lay="col", tn=INB,
                   out_dtype=BF16)
    mix, u1 = _mixer_fwd(proj, caw, conv_a_b, ln_a_g, ln_a_b, cbw, T, CW)
    h1 = _matmul("mm_h1", "nn", mix, Wout.reshape(2, CW, D), M=T, N=D, K=CW, nb=2, a_lay="blk", b_lay="blk",
                 red_block=True, res=x2)
    xn2 = _rms_fwd("rms_xattn", h1, g_xattn)
    q = _matmul("mm_q", "nn", xn2, Wq, M=T, N=D, K=D, out_dtype=BF16)
    memn = _rms_fwd("rms_mem", mem2, g_mem2, after=[q])
    kk = _matmul("mm_k", "nn", memn, Wk, M=Mm, N=D, K=D, out_dtype=BF16)
    vv = _matmul("mm_v", "nn", memn, Wv, M=Mm, N=D, K=D, out_dtype=BF16)
    o = _attn_fwd(q, kk, vv)
    h2 = _matmul("mm_h2", "nn", o, Wo, M=T, N=D, K=D, res=h1)
    xn3 = _rms_fwd("rms_ffn", h2, g_ffn)
    gpre = _matmul("mm_gate", "nt", xn3, WgateT, M=T, N=FB, K=D, nb=N_DEV, b_lay="blk", o_lay="blk", tn=FB,
                   out_dtype=BF16)
    up = _matmul("mm_up", "nt", xn3, WupT, M=T, N=FB, K=D, nb=N_DEV, b_lay="blk", o_lay="blk", tn=FB,
                 out_dtype=BF16)
    f = _ffn_act_fwd(gpre, up, cfw)
    h3 = _matmul("mm_h3", "nn", f, Wdown, M=T, N=D, K=FB, nb=N_DEV, a_lay="blk", b_lay="blk", red_block=True,
                 group=2, res=h2)
    dh3, dh3b, loss_part, dg_final = _loss_head(h3, tgt, g_final2)

    wmv = {"w_in": (w_in, m_w_in, v_w_in), "conv_a_w": (conv_a_w, m_conv_a_w, v_conv_a_w),
           "conv_b_w": (conv_b_w, m_conv_b_w, v_conv_b_w), "w_out": (w_out, m_w_out, v_w_out),
           "w_q": (w_q, m_w_q, v_w_q), "w_k": (w_k, m_w_k, v_w_k), "w_v": (w_v, m_w_v, v_w_v),
           "w_o": (w_o, m_w_o, v_w_o), "w_gate": gate_t, "w_up": up_t,
           "conv_f_w": (conv_f_w, m_conv_f_w, v_conv_f_w), "w_down": (w_down, m_w_down, v_w_down)}
    res = {}
    pending = []

    def mm(*args, after=(), **kwargs):
        behind = list(after) + pending
        pending.clear()
        return _matmul(*args, after=behind, **kwargs)

    def to_sibling(tag, named_parts):
        got = _sequencer_to_sibling("rs1_" + tag, SIBLING_ID, [p for _, p in named_parts])
        return named_parts, got

    def to_chips(tag, stage1, after):
        named_parts, got = stage1
        sums = [_chip_sum("sum_" + n, p, g, after=after) for (n, p), g in zip(named_parts, got)]
        pending.extend(sums)
        return [n for n, _ in named_parts], _sequencer_to_chips("rs2_" + tag, CHIPS_ID, sums)

    def finish(stage2, after):
        names, got = stage2
        for n, g in zip(names, got):
            w, m, v = wmv[n]
            res[n] = _adamw("adamw_" + n, g, w, m, v, after=after)
            pending.append(res[n][0])

    def row_blocks(dw):
        return dw.reshape(N_DEV, DB, D)

    def conv_blocks(dw, K):
        return jnp.transpose(dw.reshape(K, N_DEV, CW // N_DEV), (1, 0, 2))

    dWdown = mm("mm_dw_down", "tn", f, dh3b, M=FB, N=D, K=T, nb=N_DEV, a_lay="blk", o_lay="blk",
                     out_dtype=BF16, tm=FB)
    s_down = to_sibling("down", [("w_down", dWdown)])
    df = mm("mm_df", "nt", dh3b, Wdown, M=T, N=FB, K=D, nb=N_DEV, b_lay="blk", o_lay="blk", tn=FB, out_dtype=BF16,
                 after=[dWdown])
    dg, dup, dcfw = _ffn_act_bwd1(df, gpre, up, cfw)
    dgpre = _ffn_act_bwd2(dg, cfw)
    c_down = to_chips("down", s_down, after=[dgpre])
    dWgate = mm("mm_dw_gate", "tn", dgpre, xn3, M=FB, N=D, K=T, nb=N_DEV, a_lay="blk", o_lay="blk",
                     out_dtype=BF16, tm=FB)
    s_gate = to_sibling("gate", [("w_gate", dWgate), ("conv_f_w", dcfw)])
    dWup = mm("mm_dw_up", "tn", dup, xn3, M=FB, N=D, K=T, nb=N_DEV, a_lay="blk", o_lay="blk",
                   out_dtype=BF16, tm=FB, after=[dWgate])
    s_up = to_sibling("up", [("w_up", dWup)])
    dxn3 = mm("mm_dxn3_gate", "nn", dgpre, WgateT, M=T, N=D, K=FB, nb=N_DEV, a_lay="blk", b_lay="blk",
                   red_block=True, group=2, out_dtype=BF16, after=[dWup])
    c_gate = to_chips("gate", s_gate, after=[dxn3])
    dxn3 = mm("mm_dxn3_up", "nn", dup, WupT, M=T, N=D, K=FB, nb=N_DEV, a_lay="blk", b_lay="blk",
                   red_block=True, group=2, out_dtype=BF16, res=dxn3)
    c_up = to_chips("up", s_up, after=[dxn3])
    dh2, dh2b, dg_ffn = _rms_bwd("rms_bwd_ffn", dxn3, h2, g_ffn, dh3)

    dWo = mm("mm_dw_o", "tn", o, dh2b, M=D, N=D, K=T, out_dtype=BF16)
    s_o = to_sibling("o", [("w_o", row_blocks(dWo))])
    do = mm("mm_do", "nt", dh2b, Wo, M=T, N=D, K=D, out_dtype=BF16, after=[dWo])
    finish(c_down, after=[do])
    dq, dk, dv = _attn_bwd(q, kk, vv, do)
    c_o = to_chips("o", s_o, after=[dq])
    dWq = mm("mm_dw_q", "tn", xn2, dq, M=D, N=D, K=T, out_dtype=BF16)
    s_q = to_sibling("q", [("w_q", row_blocks(dWq))])
    dxn2 = mm("mm_dxn2", "nt", dq, Wq, M=T, N=D, K=D, out_dtype=BF16, after=[dWq])
    finish(c_gate, after=[dxn2])
    c_q = to_chips("q", s_q, after=[dxn2])
    dh1, dh1b, dg_xattn = _rms_bwd("rms_bwd_xattn", dxn2, h1, g_xattn, dh2)
    dWk = mm("mm_dw_k", "tn", memn, dk, M=D, N=D, K=Mm, out_dtype=BF16, after=[dh1b])
    dWv = mm("mm_dw_v", "tn", memn, dv, M=D, N=D, K=Mm, out_dtype=BF16, after=[dh1b])
    s_kv = to_sibling("kv", [("w_k", row_blocks(dWk)), ("w_v", row_blocks(dWv))])
    dmemn = mm("mm_dmem_k", "nt", dk, Wk, M=Mm, N=D, K=D, after=[dWk, dWv])
    dmemn = mm("mm_dmem_v", "nt", dv, Wv, M=Mm, N=D, K=D, res=dmemn)
    dg_mem = _rms_bwd("rms_bwd_mem", dmemn, mem2, g_mem2)

    dWout = mm("mm_dw_out", "tn", mix, dh1b, M=CW, N=D, K=T, nb=2, a_lay="blk", o_lay="blk", out_dtype=BF16,
                    after=[dg_mem])
    s_out = to_sibling("out", [("w_out", row_blocks(dWout.reshape(D, D)))])
    dmix = mm("mm_dmix", "nt", dh1b, Wout, M=T, N=D, K=D, out_dtype=BF16, after=[dWout])
    c_kv = to_chips("kv", s_kv, after=[dmix])
    finish(c_up, after=[dmix])
    du1, dcv, dbg, dcaw, dcab, dlng, dlnb, dcbw = _mixer_bwd1(dmix, proj, u1, caw, ln_a_g, ln_a_b, cbw, T, CW)
    c_out = to_chips("out", s_out, after=[du1])
    finish(c_o, after=[du1])
    finish(c_q, after=[du1])
    dav, dag, dcg, dbh = _mixer_bwd2(du1, dcv, proj, caw, cbw, T, CW)
    dproj = jnp.concatenate([dav, dag, dbg, dcg, dbh], axis=1)
    dWin = mm("mm_dw_in", "tn", xn1, dproj, M=D, N=INB, K=T, nb=N_DEV, b_lay="col", o_lay="blk",
                   out_dtype=BF16, tn=INB)
    s_in = to_sibling("in", [("w_in", dWin), ("conv_a_w", conv_blocks(dcaw, KA)),
                             ("conv_b_w", conv_blocks(dcbw, KB))])
    finish(c_kv, after=[dWin])
    c_in = to_chips("in", s_in, after=list(pending))
    dxn1 = mm("mm_dxn1", "nt", dproj, Win, M=T, N=D, K=INB, nb=N_DEV, a_lay="col", b_lay="blk",
                   red_block=True, group=2, out_dtype=BF16, after=[dWin])
    finish(c_out, after=[dxn1])
    dx, _, dg_mix = _rms_bwd("rms_bwd_mix", dxn1, x2, g_mix, dh1, after=list(pending))

    def pair(a, b):
        return jnp.concatenate([a, b], axis=1)

    zeros_half = jnp.zeros((1, CW), F32)
    small_g = jnp.concatenate([
        dg_mix, pair(dcab, dlng), pair(dlnb, zeros_half), dg_xattn, dg_mem, dg_ffn, dg_final,
        jnp.broadcast_to(loss_part[:, :1], (1, D))], axis=0)
    small_sum = _all_reduce_rows("ar_small", small_g)
    loss = small_sum[7, 0]
    finish(c_in, after=[small_sum])

    def pack(a_mix, a_cab, a_lng, a_lnb, a_xattn, a_mem, a_ffn, a_final):
        return jnp.concatenate([a_mix, pair(a_cab, a_lng), pair(a_lnb, zeros_half), a_xattn, a_mem.reshape(1, D),
                                a_ffn, a_final.reshape(1, D), jnp.zeros((1, D), F32)], axis=0)

    small = _adamw("adamw_small", small_sum[None],
                   pack(g_mix, conv_a_b, ln_a_g, ln_a_b, g_xattn, g_mem, g_ffn, g_final),
                   pack(m_g_mix, m_conv_a_b, m_ln_a_g, m_ln_a_b, m_g_xattn, m_g_mem, m_g_ffn, m_g_final),
                   pack(v_g_mix, v_conv_a_b, v_ln_a_g, v_ln_a_b, v_g_xattn, v_g_mem, v_g_ffn, v_g_final))

    def unpack(a):
        return {"g_mix": a[0:1], "conv_a_b": a[1:2, :CW], "ln_a_g": a[1:2, CW:], "ln_a_b": a[2:3, :CW],
                "g_xattn": a[3:4], "g_mem": a[4], "g_ffn": a[5:6], "g_final": a[6]}

    small = [unpack(a) for a in small]
    order = ["g_mix", "w_in", "conv_a_w", "conv_a_b", "ln_a_g", "ln_a_b", "conv_b_w", "w_out", "g_xattn", "g_mem",
             "w_q", "w_k", "w_v", "w_o", "g_ffn", "w_gate", "w_up", "conv_f_w", "w_down", "g_final"]
    outs = [loss, dx[None]]
    for kind in range(4):
        for n in order:
            if n in ("w_gate", "w_up"):
                outs.append(jnp.transpose(res[n][kind])[None])
            else:
                outs.append(res[n][kind] if n in res else small[kind][n])
    return tuple(outs)
```

```python
import functools
from typing import NamedTuple

import jax
import jax.numpy as jnp
from jax import lax
from jax.experimental import pallas as pl
from jax.experimental.pallas import tpu as pltpu
from jax.experimental.pallas import tpu_sc as plsc

F32 = jnp.float32
BF16 = jnp.bfloat16

N_DEV = 8
EPS = 1e-6
GROUP_DIM = 128
N_XATTN_HEADS = 4
ADAM_LR = 0.001
ADAM_B1 = 0.9
ADAM_B2 = 0.999
ADAM_EPS = 1e-08
ADAM_WD = 0.01
ADAM_STEP = 10

AG_ID, SIBLING_ID, CHIPS_ID = 1, 2, 3

SIDE_TILES = 32
HALO = 32
VMEM_V7X_BYTES = 64 * 1024 * 1024
VMEM_TEMP_ALLOWANCE = 12 * 1024 * 1024

VMEM_SPEC = pl.BlockSpec(memory_space=pltpu.VMEM)
MESH = pl.DeviceIdType.MESH


def _tile(n, pref, align):
    if n <= pref:
        return n
    t = (pref // align) * align
    while t >= align:
        if n % t == 0:
            return t
        t -= align
    return n


def _nbytes(shape, dtype):
    n = 1
    for d in shape:
        if d is not None:
            n *= d
    return n * jnp.dtype(dtype).itemsize


def _call(body, name, grid, in_specs, out_specs, out_shape, operands, scratch=(), sem=None, after=()):
    outs = out_shape if isinstance(out_shape, (tuple, list)) else (out_shape,)
    ospecs = out_specs if isinstance(out_specs, (tuple, list)) else (out_specs,)
    est = 0
    for spec, arr in list(zip(in_specs, operands)) + list(zip(ospecs, outs)):
        est += 2 * _nbytes(spec.block_shape, arr.dtype)
    for s in scratch:
        if hasattr(s, "shape") and hasattr(s, "dtype"):
            est += _nbytes(s.shape, s.dtype)
    limit = min(est + VMEM_TEMP_ALLOWANCE, VMEM_V7X_BYTES - 4 * 1024 * 1024)
    if sem is None:
        sem = ("arbitrary",) * len(grid)
    n_in, n_after = len(operands), len(after)
    operands = [pltpu.with_memory_space_constraint(o, pltpu.HBM) for o in operands]
    after = [pltpu.with_memory_space_constraint(o, pltpu.HBM) for o in after]
    in_hbm = [pltpu.HBM(o.shape, o.dtype) for o in outs]
    out_shape = in_hbm if isinstance(out_shape, (tuple, list)) else in_hbm[0]

    def ordered_body(*refs):
        body(*refs[:n_in], *refs[n_in + n_after:])

    return pl.pallas_call(
        ordered_body if n_after else body, name=name, grid=grid,
        in_specs=list(in_specs) + [pl.BlockSpec(memory_space=pl.ANY)] * n_after,
        out_specs=out_specs, out_shape=out_shape, scratch_shapes=list(scratch),
        compiler_params=pltpu.CompilerParams(dimension_semantics=sem, vmem_limit_bytes=int(limit)),
    )(*operands, *after)


_DOT_DIMS = {"nn": (((1,), (0,)), ((), ())), "nt": (((1,), (1,)), ((), ())), "tn": (((0,), (0,)), ((), ()))}


def _operand_spec(layout, tr, tc, cols_per_block, pick, group=None):
    if layout == "plain":
        return pl.BlockSpec((tr, tc), lambda *g: pick(*g)[1:])
    if layout == "blk":
        return pl.BlockSpec((group, tr, tc), lambda *g: pick(*g))
    assert layout == "col"
    if group:
        assert tc == cols_per_block
        return pl.BlockSpec((tr, group * tc), lambda *g: (pick(*g)[1], pick(*g)[0]))
    per = cols_per_block // tc
    return pl.BlockSpec((tr, tc), lambda *g: (pick(*g)[1], pick(*g)[0] * per + pick(*g)[2]))


def _matmul(name, dims, a, b, *, M, N, K, nb=1, a_lay="plain", b_lay="plain", o_lay="plain",
            red_block=False, group=None, out_dtype=F32, res=None, tm=1024, tn=None, after=(), side=()):
    tm = _tile(M, tm, 128 if dims == "tn" else 16)
    tn = _tile(N, tn or (1024 if red_block else 512), 128)
    tk = K
    gi, gj, gk = M // tm, N // tn, K // tk
    if red_block:
        grid = (gi, gj, nb // (group or 1), gk)
        unpack = lambda i, j, bb, k: (bb, i, j, k)
        red_axes, sem = (2, 3), ("parallel", "parallel", "arbitrary", "arbitrary")
    else:
        grid = (nb, gi, gj, gk)
        unpack = lambda bb, i, j, k: (bb, i, j, k)
        red_axes, sem = (3,), ("parallel", "parallel", "parallel", "arbitrary")

    def picker(f):
        return lambda *g: f(*unpack(*g))

    if dims == "tn":
        a_spec = _operand_spec(a_lay, tk, tm, M, picker(lambda bb, i, j, k: (bb, k, i)), group)
    else:
        a_spec = _operand_spec(a_lay, tm, tk, K, picker(lambda bb, i, j, k: (bb, i, k)), group)
    if dims == "nt":
        b_spec = _operand_spec(b_lay, tn, tk, K, picker(lambda bb, i, j, k: (bb, j, k)), group)
    else:
        b_spec = _operand_spec(b_lay, tk, tn, N, picker(lambda bb, i, j, k: (bb, k, j)), group)
    o_spec = _operand_spec(o_lay, tm, tn, N, picker(lambda bb, i, j, k: (bb, i, j)))
    if o_lay == "plain":
        out_shape = jax.ShapeDtypeStruct((M, N), out_dtype)
    elif o_lay == "blk":
        out_shape = jax.ShapeDtypeStruct((nb, M, N), out_dtype)
    else:
        out_shape = jax.ShapeDtypeStruct((M, nb * N), out_dtype)
    n_red = [grid[ax] for ax in red_axes]
    has_res = res is not None
    one_step = all(n == 1 for n in n_red)

    def contract(a_ref, b_ref):
        if group:
            parts = [(a_ref[p] if a_lay == "blk" else a_ref[:, p * K:(p + 1) * K], b_ref[p]) for p in range(group)]
        else:
            parts = [(a_ref[...], b_ref[...])]
        r = None
        for a_part, b_part in parts:
            d = lax.dot_general(a_part, b_part, _DOT_DIMS[dims], preferred_element_type=F32)
            r = d if r is None else r + d
        return r

    def step_index(*g):
        s = g[0]
        for ax in range(1, len(grid)):
            s = s * grid[ax] + g[ax]
        return s

    def side_spec(block):
        shape, index = block
        return pl.BlockSpec(shape, lambda *g: index(jnp.minimum(step_index(*g), SIDE_TILES - 1)))

    asked_to_carry = bool(side)
    if functools.reduce(lambda p, q: p * q, grid) < SIDE_TILES:
        side = ()
    n_main_in = 3 if has_res else 2
    n_side_in = sum(len(job.operands) for job in side)
    n_side_out = sum(len(job.out_shapes) for job in side)

    def side_work(side_in, side_out):
        @pl.when(step_index(*[pl.program_id(ax) for ax in range(len(grid))]) < SIDE_TILES)
        def _():
            i0 = o0 = 0
            for job in side:
                n_i, n_o = len(job.operands), len(job.out_shapes)
                job.body(*side_in[i0:i0 + n_i], *side_out[o0:o0 + n_o])
                i0, o0 = i0 + n_i, o0 + n_o

    def body(*refs):
        a_ref, b_ref = refs[:2]
        side_in = refs[n_main_in:n_main_in + n_side_in]
        o_ref = refs[n_main_in + n_side_in]
        side_out = refs[n_main_in + n_side_in + 1:n_main_in + n_side_in + 1 + n_side_out]
        if one_step:
            r = contract(a_ref, b_ref)
            if has_res:
                r = r + refs[2][...]
            o_ref[...] = r.astype(o_ref.dtype)
        else:
            acc = refs[-1]
            first = functools.reduce(jnp.logical_and, [pl.program_id(ax) == 0 for ax in red_axes])
            last = functools.reduce(jnp.logical_and,
                                    [pl.program_id(ax) == n - 1 for ax, n in zip(red_axes, n_red)])

            @pl.when(first)
            def _():
                acc[...] = jnp.zeros_like(acc)

            acc[...] += contract(a_ref, b_ref)

            @pl.when(last)
            def _():
                r = acc[...]
                if has_res:
                    r = r + refs[2][...]
                o_ref[...] = r.astype(o_ref.dtype)
        if side:
            side_work(side_in, side_out)

    in_specs = [a_spec, b_spec]
    operands = [a, b]
    if has_res:
        in_specs.append(_operand_spec("plain", tm, tn, N, picker(lambda bb, i, j, k: (bb, i, j))))
        operands.append(res)
    out_specs, out_shapes = [o_spec], [out_shape]
    for job in side:
        in_specs += [side_spec(blk) for blk in job.in_blocks]
        operands += job.operands
        out_specs += [side_spec(blk) for blk in job.out_blocks]
        out_shapes += job.out_shapes
    scratch = [] if one_step else [pltpu.VMEM((tm, tn), F32)]
    if not side:
        out = _call(body, name, grid, in_specs, o_spec, out_shape, operands, scratch=scratch, sem=sem, after=after)
        return (out, None) if asked_to_carry else out
    outs = _call(body, name, grid, in_specs, tuple(out_specs), tuple(out_shapes), operands, scratch=scratch,
                 sem=("arbitrary",) * len(grid), after=after)
    results, pos = [], 1
    for job in side:
        results.append(tuple(outs[pos:pos + len(job.out_shapes)]))
        pos += len(job.out_shapes)
    return outs[0], results


def _rows_spec(arr, tr):
    lead = arr.ndim - 2
    return pl.BlockSpec((None,) * lead + (tr, arr.shape[-1]), lambda i: (0,) * lead + (i, 0))


def _cast_bf16(name, w):
    R, C = w.shape[-2:]
    tr = _tile(R, max(8, (3 << 20) // C), 16)

    def body(w_ref, o_ref):
        o_ref[...] = w_ref[...].astype(BF16)

    return _call(body, name, (R // tr,), [_rows_spec(w, tr)],
                 pl.BlockSpec((tr, C), lambda i: (i, 0)), jax.ShapeDtypeStruct((R, C), BF16), [w],
                 sem=("parallel",))


def _rms_fwd(name, x, g, after=()):
    T, D = x.shape
    tm = _tile(T, 128, 16)

    def body(x_ref, g_ref, o_ref):
        xv = x_ref[...]
        r = lax.rsqrt(jnp.mean(xv * xv, axis=-1, keepdims=True) + EPS)
        o_ref[...] = (xv * r * g_ref[...]).astype(BF16)

    return _call(body, name, (T // tm,),
                 [pl.BlockSpec((tm, D), lambda i: (i, 0)), pl.BlockSpec((1, D), lambda i: (0, 0))],
                 pl.BlockSpec((tm, D), lambda i: (i, 0)), jax.ShapeDtypeStruct((T, D), BF16), [x, g],
                 sem=("parallel",), after=after)


def _rms_bwd(name, dxn, x, g, dh=None, after=()):
    T, D = x.shape
    tm = _tile(T, 128, 16)
    with_dx = dh is not None

    def body(*refs):
        if with_dx:
            dxn_ref, x_ref, g_ref, dh_ref, o_ref, ob_ref, dg_ref = refs
        else:
            dxn_ref, x_ref, g_ref, dg_ref = refs
        xv = x_ref[...]
        r = lax.rsqrt(jnp.mean(xv * xv, axis=-1, keepdims=True) + EPS)
        xh = xv * r
        dy = dxn_ref[...].astype(F32)

        @pl.when(pl.program_id(0) == 0)
        def _():
            dg_ref[...] = jnp.zeros_like(dg_ref)

        dg_ref[...] += jnp.sum(dy * xh, axis=0, keepdims=True)
        if with_dx:
            dyg = dy * g_ref[...]
            tot = dh_ref[...] + r * (dyg - xh * jnp.mean(dyg * xh, axis=-1, keepdims=True))
            o_ref[...] = tot
            ob_ref[...] = tot.astype(BF16)

    row = pl.BlockSpec((tm, D), lambda i: (i, 0))
    vec = pl.BlockSpec((1, D), lambda i: (0, 0))
    if with_dx:
        return _call(body, name, (T // tm,), [row, row, vec, row], (row, row, vec),
                     (jax.ShapeDtypeStruct((T, D), F32), jax.ShapeDtypeStruct((T, D), BF16),
                      jax.ShapeDtypeStruct((1, D), F32)), [dxn, x, g, dh], after=after)
    return _call(body, name, (T // tm,), [row, row, vec], vec, jax.ShapeDtypeStruct((1, D), F32), [dxn, x, g])


def _loss_head(h, target, g):
    T, D = h.shape
    tm = _tile(T, 128, 16)

    def body(h_ref, t_ref, g_ref, o_ref, ob_ref, loss_ref, dg_ref):
        xv = h_ref[...]
        gv = g_ref[...]
        r = lax.rsqrt(jnp.mean(xv * xv, axis=-1, keepdims=True) + EPS)
        xh = xv * r
        e = xh * gv - t_ref[...]

        @pl.when(pl.program_id(0) == 0)
        def _():
            dg_ref[...] = jnp.zeros_like(dg_ref)
            loss_ref[...] = jnp.zeros_like(loss_ref)

        loss_ref[...] += 0.5 * jnp.sum(jnp.mean(e * e, axis=-1, keepdims=True), axis=0, keepdims=True)
        dy = e * (1.0 / D)
        dg_ref[...] += jnp.sum(dy * xh, axis=0, keepdims=True)
        dyg = dy * gv
        dx = r * (dyg - xh * jnp.mean(dyg * xh, axis=-1, keepdims=True))
        o_ref[...] = dx
        ob_ref[...] = dx.astype(BF16)

    row = pl.BlockSpec((tm, D), lambda i: (i, 0))
    vec = pl.BlockSpec((1, D), lambda i: (0, 0))
    return _call(body, "loss_head", (T // tm,), [row, row, vec],
                 (row, row, pl.BlockSpec((1, 128), lambda i: (0, 0)), vec),
                 (jax.ShapeDtypeStruct((T, D), F32), jax.ShapeDtypeStruct((T, D), BF16),
                  jax.ShapeDtypeStruct((1, 128), F32), jax.ShapeDtypeStruct((1, D), F32)), [h, target, g])


ROW_CHUNK = 64
SUBLANES = 8


def _col_chunks(width):
    return [slice(c0, min(c0 + GROUP_DIM, width)) for c0 in range(0, width, GROUP_DIM)]


def _row_chunks(n_rows):
    return [(r0, min(ROW_CHUNK, n_rows - r0)) for r0 in range(0, n_rows, ROW_CHUNK)]


def _pad_rows(K):
    return -(-(K - 1) // SUBLANES) * SUBLANES


def _shifted_back(buf, K, r0, nr, cs):
    pad = _pad_rows(K)
    win = buf[pl.ds(HALO + r0 - pad, nr + pad), cs]
    for b in range(min(SUBLANES, K)):
        rolled = win if b == 0 else pltpu.roll(win, b, axis=0)
        for a in range((K - 1 - b) // SUBLANES + 1):
            yield K - 1 - (SUBLANES * a + b), rolled[pad - SUBLANES * a:pad - SUBLANES * a + nr]


def _conv_fwd(buf, w_ref, K, r0, nr, cs):
    y = None
    for k, xs in _shifted_back(buf, K, r0, nr, cs):
        term = xs * w_ref[pl.ds(k, 1), cs]
        y = term if y is None else y + term
    return y


def _conv_bwd_input(buf, w_ref, K, r0, nr, cs):
    pad = _pad_rows(K)
    win = buf[pl.ds(r0, nr + pad), cs]
    dx = None
    for b in range(min(SUBLANES, K)):
        rolled = win if b == 0 else pltpu.roll(win, nr + pad - b, axis=0)
        for a in range((K - 1 - b) // SUBLANES + 1):
            k = K - 1 - (SUBLANES * a + b)
            term = rolled[SUBLANES * a:SUBLANES * a + nr] * w_ref[pl.ds(k, 1), cs]
            dx = term if dx is None else dx + term
    return dx


def _fold_rows(v):
    nr, lanes = v.shape
    if nr % SUBLANES:
        return jnp.sum(v, axis=0, keepdims=True)
    return jnp.sum(v.reshape(nr // SUBLANES, SUBLANES, lanes), axis=0)


def _conv_bwd_weight(accs, dy, buf, K, r0, nr, cs):
    accs = list(accs)
    for k, xs in _shifted_back(buf, K, r0, nr, cs):
        accs[k] = accs[k] + _fold_rows(dy * xs)
    return accs


def _add_row(ref, row, cs, acc):
    ref[pl.ds(row, 1), cs] += jnp.sum(acc, axis=0, keepdims=True)


def _sigmoid(z):
    return 0.5 * jnp.tanh(0.5 * z) + 0.5


def _silu_grad(z, sig):
    return sig * (1.0 + z * (1.0 - sig))


def _group_norm(xg):
    xc = xg - jnp.mean(xg, axis=-1, keepdims=True)
    rstd = lax.rsqrt(jnp.mean(xc * xc, axis=-1, keepdims=True) + EPS)
    return xc * rstd, rstd


def _mixer_tiles(T, CW):
    tm = _tile(T, 512, HALO)
    tc = _tile(CW, 256, GROUP_DIM)
    return tm, tc, tm // HALO, CW // tc


def _mixer_fwd(proj, caw, cab, lng, lnb, cbw, T, CW):
    KA, KB = caw.shape[0], cbw.shape[0]
    tm, tc, hb, nc = _mixer_tiles(T, CW)

    def sec(s):
        return pl.BlockSpec((tm, tc), lambda i, c: (i, s * nc + c))

    def sec_prev(s):
        return pl.BlockSpec((HALO, tc), lambda i, c: (jnp.maximum(i * hb - 1, 0), s * nc + c))

    def chan(rows):
        return pl.BlockSpec((rows, tc), lambda i, c: (0, c))

    def body(av, ag, bg, cg, bh, avh, agh, cgh, bhh, caw_ref, cab_ref, lng_ref, lnb_ref, cbw_ref,
             mix_ref, u1_ref, bufa, bufb):
        first = pl.program_id(0) == 0
        bufa[pl.ds(0, HALO), :] = jnp.where(first, 0.0, avh[...].astype(F32) * _sigmoid(agh[...].astype(F32)))
        bufb[pl.ds(0, HALO), :] = jnp.where(first, 0.0, cgh[...].astype(F32) * bhh[...].astype(F32))
        for cs in _col_chunks(tc):
            for r0, nr in _row_chunks(tm):
                rows = pl.ds(r0, nr)
                bufa[pl.ds(HALO + r0, nr), cs] = av[rows, cs].astype(F32) * _sigmoid(ag[rows, cs].astype(F32))
                bufb[pl.ds(HALO + r0, nr), cs] = cg[rows, cs].astype(F32) * bh[rows, cs].astype(F32)
        for cs in _col_chunks(tc):
            for r0, nr in _row_chunks(tm):
                rows = pl.ds(r0, nr)
                u1 = _conv_fwd(bufa, caw_ref, KA, r0, nr, cs) + cab_ref[:, cs]
                u1_ref[rows, cs] = u1
                y, _ = _group_norm(u1)
                z = y * lng_ref[:, cs] + lnb_ref[:, cs]
                mix_ref[0, rows, cs] = (z * _sigmoid(z)).astype(BF16)
                mix_ref[1, rows, cs] = (bg[rows, cs].astype(F32) * _conv_fwd(bufb, cbw_ref, KB, r0, nr, cs)).astype(BF16)

    in_specs = [sec(0), sec(1), sec(2), sec(3), sec(4), sec_prev(0), sec_prev(1), sec_prev(3), sec_prev(4),
                chan(KA), chan(1), chan(1), chan(1), chan(KB)]
    operands = [proj] * 9 + [caw, cab, lng, lnb, cbw]
    return _call(body, "mixer_fwd", (T // tm, nc), in_specs,
                 (pl.BlockSpec((2, tm, tc), lambda i, c: (0, i, c)), pl.BlockSpec((tm, tc), lambda i, c: (i, c))),
                 (jax.ShapeDtypeStruct((2, T, CW), BF16), jax.ShapeDtypeStruct((T, CW), F32)), operands,
                 scratch=[pltpu.VMEM((HALO + tm, tc), F32), pltpu.VMEM((HALO + tm, tc), F32)],
                 sem=("parallel", "parallel"))


def _mixer_bwd1(dmix, proj, u1, caw, lng, lnb, cbw, T, CW):
    KA, KB = caw.shape[0], cbw.shape[0]
    tm, tc, hb, nc = _mixer_tiles(T, CW)

    def sec(s):
        return pl.BlockSpec((tm, tc), lambda c, i: (i, s * nc + c))

    def sec_prev(s):
        return pl.BlockSpec((HALO, tc), lambda c, i: (jnp.maximum(i * hb - 1, 0), s * nc + c))

    def chan(rows):
        return pl.BlockSpec((rows, tc), lambda c, i: (0, c))

    tile = pl.BlockSpec((tm, tc), lambda c, i: (i, c))

    def body(du, dv, u1_ref, av, ag, bg, cg, bh, avh, agh, cgh, bhh, lng_ref, lnb_ref, cbw_ref,
             du1_ref, dcv_ref, dbg_ref, dcaw_ref, dcab_ref, dlng_ref, dlnb_ref, dcbw_ref, bufa, bufb):
        first = pl.program_id(1) == 0

        @pl.when(first)
        def _():
            for r in (dcaw_ref, dcab_ref, dlng_ref, dlnb_ref, dcbw_ref):
                r[...] = jnp.zeros_like(r)

        bufa[pl.ds(0, HALO), :] = jnp.where(first, 0.0, avh[...].astype(F32) * _sigmoid(agh[...].astype(F32)))
        bufb[pl.ds(0, HALO), :] = jnp.where(first, 0.0, cgh[...].astype(F32) * bhh[...].astype(F32))
        for cs in _col_chunks(tc):
            for r0, nr in _row_chunks(tm):
                rows = pl.ds(r0, nr)
                bufa[pl.ds(HALO + r0, nr), cs] = av[rows, cs].astype(F32) * _sigmoid(ag[rows, cs].astype(F32))
                bufb[pl.ds(HALO + r0, nr), cs] = cg[rows, cs].astype(F32) * bh[rows, cs].astype(F32)
        for cs in _col_chunks(tc):
            lanes = cs.stop - cs.start
            zero = jnp.zeros((SUBLANES, lanes), F32)
            a_lng, a_lnb, a_cab = zero, zero, zero
            a_caw, a_cbw = [zero] * KA, [zero] * KB
            gamma, beta = lng_ref[:, cs], lnb_ref[:, cs]
            for r0, nr in _row_chunks(tm):
                rows = pl.ds(r0, nr)
                y, rstd = _group_norm(u1_ref[rows, cs])
                z = y * gamma + beta
                dz = du[rows, cs].astype(F32) * _silu_grad(z, _sigmoid(z))
                a_lng = a_lng + _fold_rows(dz * y)
                a_lnb = a_lnb + _fold_rows(dz)
                dy = dz * gamma
                du1 = rstd * (dy - jnp.mean(dy, axis=-1, keepdims=True)
                              - y * jnp.mean(dy * y, axis=-1, keepdims=True))
                du1_ref[rows, cs] = du1
                a_cab = a_cab + _fold_rows(du1)
                a_caw = _conv_bwd_weight(a_caw, du1, bufa, KA, r0, nr, cs)

                dvv = dv[rows, cs].astype(F32)
                dbg_ref[rows, cs] = (dvv * _conv_fwd(bufb, cbw_ref, KB, r0, nr, cs)).astype(BF16)
                dcv = dvv * bg[rows, cs].astype(F32)
                dcv_ref[rows, cs] = dcv
                a_cbw = _conv_bwd_weight(a_cbw, dcv, bufb, KB, r0, nr, cs)
            _add_row(dlng_ref, 0, cs, a_lng)
            _add_row(dlnb_ref, 0, cs, a_lnb)
            _add_row(dcab_ref, 0, cs, a_cab)
            for k in range(KA):
                _add_row(dcaw_ref, k, cs, a_caw[k])
            for k in range(KB):
                _add_row(dcbw_ref, k, cs, a_cbw[k])

    in_specs = [sec(0), sec(1), tile, sec(0), sec(1), sec(2), sec(3), sec(4),
                sec_prev(0), sec_prev(1), sec_prev(3), sec_prev(4), chan(1), chan(1), chan(KB)]
    operands = [dmix, dmix, u1] + [proj] * 9 + [lng, lnb, cbw]
    return _call(body, "mixer_bwd1", (nc, T // tm), in_specs,
                 (tile, tile, tile, chan(KA), chan(1), chan(1), chan(1), chan(KB)),
                 (jax.ShapeDtypeStruct((T, CW), F32), jax.ShapeDtypeStruct((T, CW), F32),
                  jax.ShapeDtypeStruct((T, CW), BF16), jax.ShapeDtypeStruct((KA, CW), F32),
                  jax.ShapeDtypeStruct((1, CW), F32), jax.ShapeDtypeStruct((1, CW), F32),
                  jax.ShapeDtypeStruct((1, CW), F32), jax.ShapeDtypeStruct((KB, CW), F32)), operands,
                 scratch=[pltpu.VMEM((HALO + tm, tc), F32), pltpu.VMEM((HALO + tm, tc), F32)],
                 sem=("parallel", "arbitrary"))


def _mixer_bwd2(du1, dcv, proj, caw, cbw, T, CW):
    KA, KB = caw.shape[0], cbw.shape[0]
    tm, tc, hb, nc = _mixer_tiles(T, CW)
    n_i = T // tm

    def sec(s):
        return pl.BlockSpec((tm, tc), lambda i, c: (i, s * nc + c))

    def chan(rows):
        return pl.BlockSpec((rows, tc), lambda i, c: (0, c))

    tile = pl.BlockSpec((tm, tc), lambda i, c: (i, c))
    nxt = pl.BlockSpec((HALO, tc), lambda i, c: (jnp.minimum((i + 1) * hb, n_i * hb - 1), c))

    def body(du1_ref, du1n, dcv_ref, dcvn, av, ag, cg, bh, caw_ref, cbw_ref, dav, dag, dcg, dbh, bufa, bufb):
        last = pl.program_id(0) == n_i - 1
        bufa[pl.ds(0, tm), :] = du1_ref[...]
        bufa[pl.ds(tm, HALO), :] = jnp.where(last, 0.0, du1n[...])
        bufb[pl.ds(0, tm), :] = dcv_ref[...]
        bufb[pl.ds(tm, HALO), :] = jnp.where(last, 0.0, dcvn[...])
        for cs in _col_chunks(tc):
            for r0, nr in _row_chunks(tm):
                rows = pl.ds(r0, nr)
                du0 = _conv_bwd_input(bufa, caw_ref, KA, r0, nr, cs)
                sig = _sigmoid(ag[rows, cs].astype(F32))
                dav[rows, cs] = (du0 * sig).astype(BF16)
                dag[rows, cs] = (du0 * av[rows, cs].astype(F32) * (sig * (1.0 - sig))).astype(BF16)
                dch = _conv_bwd_input(bufb, cbw_ref, KB, r0, nr, cs)
                dcg[rows, cs] = (dch * bh[rows, cs].astype(F32)).astype(BF16)
                dbh[rows, cs] = (dch * cg[rows, cs].astype(F32)).astype(BF16)

    in_specs = [tile, nxt, tile, nxt, sec(0), sec(1), sec(3), sec(4), chan(KA), chan(KB)]
    operands = [du1, du1, dcv, dcv, proj, proj, proj, proj, caw, cbw]
    out = jax.ShapeDtypeStruct((T, CW), BF16)
    return _call(body, "mixer_bwd2", (n_i, nc), in_specs, (tile, tile, tile, tile), (out, out, out, out),
                 operands, scratch=[pltpu.VMEM((HALO + tm, tc), F32), pltpu.VMEM((HALO + tm, tc), F32)],
                 sem=("parallel", "parallel"))


def _ffn_tiles(T):
    tm = _tile(T, 512, HALO)
    return tm, tm // HALO, T // tm


def _ffn_act_fwd(gpre, up, cfw):
    nb, T, F = gpre.shape
    KF = cfw.shape[1]
    tm, hb, n_i = _ffn_tiles(T)
    tile = pl.BlockSpec((None, tm, F), lambda b, i: (b, i, 0))
    prev = pl.BlockSpec((None, HALO, F), lambda b, i: (b, jnp.maximum(i * hb - 1, 0), 0))
    wspec = pl.BlockSpec((None, KF, F), lambda b, i: (b, 0, 0))

    def body(g_ref, gh_ref, up_ref, w_ref, f_ref, buf):
        buf[pl.ds(HALO, tm), :] = g_ref[...].astype(F32)
        buf[pl.ds(0, HALO), :] = jnp.where(pl.program_id(1) == 0, 0.0, gh_ref[...].astype(F32))
        for cs in _col_chunks(F):
            for r0, nr in _row_chunks(tm):
                rows = pl.ds(r0, nr)
                g = _conv_fwd(buf, w_ref, KF, r0, nr, cs)
                f_ref[rows, cs] = (g * _sigmoid(g) * up_ref[rows, cs].astype(F32)).astype(BF16)

    return _call(body, "ffn_act_fwd", (nb, n_i), [tile, prev, tile, wspec], tile,
                 jax.ShapeDtypeStruct((nb, T, F), BF16), [gpre, gpre, up, cfw],
                 scratch=[pltpu.VMEM((HALO + tm, F), F32)], sem=("parallel", "parallel"))


def _ffn_act_bwd1(df, gpre, up, cfw):
    nb, T, F = gpre.shape
    KF = cfw.shape[1]
    tm, hb, n_i = _ffn_tiles(T)
    tile = pl.BlockSpec((None, tm, F), lambda b, i: (b, i, 0))
    prev = pl.BlockSpec((None, HALO, F), lambda b, i: (b, jnp.maximum(i * hb - 1, 0), 0))
    wspec = pl.BlockSpec((None, KF, F), lambda b, i: (b, 0, 0))

    def body(df_ref, g_ref, gh_ref, up_ref, w_ref, dg_ref, dup_ref, dw_ref, buf):
        first = pl.program_id(1) == 0

        @pl.when(first)
        def _():
            dw_ref[...] = jnp.zeros_like(dw_ref)

        buf[pl.ds(HALO, tm), :] = g_ref[...].astype(F32)
        buf[pl.ds(0, HALO), :] = jnp.where(first, 0.0, gh_ref[...].astype(F32))
        for cs in _col_chunks(F):
            accs = [jnp.zeros((SUBLANES, cs.stop - cs.start), F32)] * KF
            for r0, nr in _row_chunks(tm):
                rows = pl.ds(r0, nr)
                g = _conv_fwd(buf, w_ref, KF, r0, nr, cs)
                sig = _sigmoid(g)
                dfv = df_ref[rows, cs].astype(F32)
                dup_ref[rows, cs] = (dfv * (g * sig)).astype(BF16)
                dg = dfv * up_ref[rows, cs].astype(F32) * _silu_grad(g, sig)
                dg_ref[rows, cs] = dg.astype(BF16)
                accs = _conv_bwd_weight(accs, dg, buf, KF, r0, nr, cs)
            for k in range(KF):
                _add_row(dw_ref, k, cs, accs[k])

    return _call(body, "ffn_act_bwd1", (nb, n_i), [tile, tile, prev, tile, wspec], (tile, tile, wspec),
                 (jax.ShapeDtypeStruct((nb, T, F), BF16), jax.ShapeDtypeStruct((nb, T, F), BF16),
                  jax.ShapeDtypeStruct((nb, KF, F), F32)), [df, gpre, gpre, up, cfw],
                 scratch=[pltpu.VMEM((HALO + tm, F), F32)], sem=("parallel", "arbitrary"))


def _ffn_act_bwd2(dg, cfw):
    nb, T, F = dg.shape
    KF = cfw.shape[1]
    tm, hb, n_i = _ffn_tiles(T)
    tile = pl.BlockSpec((None, tm, F), lambda b, i: (b, i, 0))
    nxt = pl.BlockSpec((None, HALO, F), lambda b, i: (b, jnp.minimum((i + 1) * hb, n_i * hb - 1), 0))
    wspec = pl.BlockSpec((None, KF, F), lambda b, i: (b, 0, 0))

    def body(dg_ref, dgn_ref, w_ref, o_ref, buf):
        buf[pl.ds(0, tm), :] = dg_ref[...].astype(F32)
        buf[pl.ds(tm, HALO), :] = jnp.where(pl.program_id(1) == n_i - 1, 0.0, dgn_ref[...].astype(F32))
        for cs in _col_chunks(F):
            for r0, nr in _row_chunks(tm):
                o_ref[pl.ds(r0, nr), cs] = _conv_bwd_input(buf, w_ref, KF, r0, nr, cs).astype(BF16)

    return _call(body, "ffn_act_bwd2", (nb, n_i), [tile, nxt, wspec], tile,
                 jax.ShapeDtypeStruct((nb, T, F), BF16), [dg, dg, cfw],
                 scratch=[pltpu.VMEM((HALO + tm, F), F32)], sem=("parallel", "parallel"))


def _softmax_rows(s):
    e = jnp.exp(s - jnp.max(s, axis=-1, keepdims=True))
    return e / jnp.sum(e, axis=-1, keepdims=True)


def _attn_fwd(q, k, v):
    T, D = q.shape
    Mm = k.shape[0]
    hd = D // N_XATTN_HEADS
    scale = hd ** -0.5
    tm = _tile(T, 256, 16)

    def body(q_ref, k_ref, v_ref, o_ref):
        for h in range(N_XATTN_HEADS):
            sl = slice(h * hd, (h + 1) * hd)
            s = lax.dot_general(q_ref[:, sl], k_ref[:, sl], _DOT_DIMS["nt"], preferred_element_type=F32) * scale
            p = _softmax_rows(s).astype(BF16)
            o_ref[:, sl] = jnp.dot(p, v_ref[:, sl], preferred_element_type=F32).astype(BF16)

    row = pl.BlockSpec((tm, D), lambda i: (i, 0))
    full = pl.BlockSpec((Mm, D), lambda i: (0, 0))
    return _call(body, "attn_fwd", (T // tm,), [row, full, full], row, jax.ShapeDtypeStruct((T, D), BF16),
                 [q, k, v], sem=("parallel",))


def _attn_bwd(q, k, v, do):
    T, D = q.shape
    Mm = k.shape[0]
    hd = D // N_XATTN_HEADS
    scale = hd ** -0.5
    tm = _tile(T, 256, 16)
    n_i = T // tm

    def body(q_ref, do_ref, k_ref, v_ref, dq_ref, dk_ref, dv_ref, dk_acc, dv_acc):
        @pl.when(pl.program_id(0) == 0)
        def _():
            dk_acc[...] = jnp.zeros_like(dk_acc)
            dv_acc[...] = jnp.zeros_like(dv_acc)

        for h in range(N_XATTN_HEADS):
            sl = slice(h * hd, (h + 1) * hd)
            qh, kh, doh = q_ref[:, sl], k_ref[:, sl], do_ref[:, sl]
            s = lax.dot_general(qh, kh, _DOT_DIMS["nt"], preferred_element_type=F32) * scale
            p = _softmax_rows(s)
            dv_acc[:, sl] += lax.dot_general(p.astype(BF16), doh, _DOT_DIMS["tn"], preferred_element_type=F32)
            dp = lax.dot_general(doh, v_ref[:, sl], _DOT_DIMS["nt"], preferred_element_type=F32)
            ds = (p * (dp - jnp.sum(dp * p, axis=-1, keepdims=True)) * scale).astype(BF16)
            dq_ref[:, sl] = jnp.dot(ds, kh, preferred_element_type=F32).astype(BF16)
            dk_acc[:, sl] += lax.dot_general(ds, qh, _DOT_DIMS["tn"], preferred_element_type=F32)

        @pl.when(pl.program_id(0) == n_i - 1)
        def _():
            dk_ref[...] = dk_acc[...].astype(BF16)
            dv_ref[...] = dv_acc[...].astype(BF16)

    row = pl.BlockSpec((tm, D), lambda i: (i, 0))
    full = pl.BlockSpec((Mm, D), lambda i: (0, 0))
    return _call(body, "attn_bwd", (n_i,), [row, row, full, full], (row, full, full),
                 (jax.ShapeDtypeStruct((T, D), BF16), jax.ShapeDtypeStruct((Mm, D), BF16),
                  jax.ShapeDtypeStruct((Mm, D), BF16)), [q, do, k, v],
                 scratch=[pltpu.VMEM((Mm, D), F32), pltpu.VMEM((Mm, D), F32)])


def _position():
    x, y, c = lax.axis_index("x"), lax.axis_index("y"), lax.axis_index("c")
    return x, y, c


def _peer(pos, k):
    x, y, c = pos
    return (1 - x if k & 4 else x, 1 - y if k & 2 else y, 1 - c if k & 1 else c)


def _index(pos):
    x, y, c = pos
    return 4 * x + 2 * y + c


def _sequencer_kernel(body, name, collective_id, out_type, operands):
    return pl.kernel(
        body, name=name, out_type=out_type,
        mesh=plsc.ScalarSubcoreMesh(axis_name="sequencer", num_cores=1),
        scratch_types=[pltpu.SemaphoreType.DMA, pltpu.SemaphoreType.DMA((7,)), pltpu.SemaphoreType.DMA],
        compiler_params=pltpu.CompilerParams(collective_id=collective_id),
    )(*operands)


def _handshake(peers):
    barrier = pltpu.get_barrier_semaphore()
    for peer in peers:
        pl.semaphore_signal(barrier, inc=1, device_id=peer, device_id_type=MESH)
    pl.semaphore_wait(barrier, len(peers))


def _sequencer_all_gather(name, collective_id, shards):
    n = len(shards)

    def body(*refs):
        x_refs, out_refs = refs[:n], refs[n:2 * n]
        send_sem, recv_sems, local_sem = refs[2 * n:]
        me = _position()
        x, y, c = me
        sibling = _peer(me, 1)
        first = (x + (1 - c) - 2 * x * (1 - c), y + c - 2 * y * c, c)
        second = (x + c - 2 * x * c, y + (1 - c) - 2 * y * (1 - c), c)
        diagonal = _peer(me, 6)
        _handshake([sibling, first, second])

        def copy(a, k, block, to, own=False):
            dst = out_refs[a].at[_index(block)]
            return pltpu.make_async_remote_copy(
                src_ref=x_refs[a] if own else dst, dst_ref=dst, send_sem=send_sem, recv_sem=recv_sems.at[k],
                device_id=to, device_id_type=MESH)

        local = [pltpu.make_async_copy(x_refs[a], out_refs[a].at[_index(me)], local_sem) for a in range(n)]
        started = [copy(a, 1 + j, me, peer, own=True) for a in range(n) for j, peer in enumerate((first, second))]
        started += [copy(a, 0, me, sibling, own=True) for a in range(n)]
        for cp in started + local:
            cp.start()
        for k, origin in ((1, first), (2, second), (3, diagonal)):
            for a in range(n):
                copy(a, k, origin, me).wait_recv()
            passed = [copy(a, 3 + k, origin, sibling) for a in range(n)]
            if k == 1:
                passed = [copy(a, 3, origin, second) for a in range(n)] + passed
            for cp in passed:
                cp.start()
            started += passed
        for k in (0, 4, 5, 6):
            for a in range(n):
                copy(a, k, sibling, me).wait_recv()
        for cp in started:
            cp.wait_send()
        for cp in local:
            cp.wait()

    return _sequencer_kernel(body, name, collective_id,
                             [jax.ShapeDtypeStruct((N_DEV,) + s.shape, s.dtype) for s in shards], shards)


def _chip_index(pos):
    return 2 * pos[0] + pos[1]


def _sequencer_to_sibling(name, collective_id, parts):
    n = len(parts)

    def body(*refs):
        p_refs, out_refs = refs[:n], refs[n:2 * n]
        send_sem, recv_sems, _ = refs[2 * n:]
        me = _position()
        sibling = _peer(me, 1)
        _handshake([sibling])
        copies = [pltpu.make_async_remote_copy(
            src_ref=p_refs[a].at[2 * q + sibling[2]], dst_ref=out_refs[a].at[q], send_sem=send_sem,
            recv_sem=recv_sems.at[0], device_id=sibling, device_id_type=MESH)
            for a in range(n) for q in range(N_DEV // 2)]
        for cp in copies:
            cp.start()
        for cp in copies:
            cp.wait_recv()
        for cp in copies:
            cp.wait_send()

    return _sequencer_kernel(body, name, collective_id,
                             [jax.ShapeDtypeStruct((N_DEV // 2,) + p.shape[1:], p.dtype) for p in parts], parts)


def _sequencer_to_chips(name, collective_id, sums):
    n = len(sums)

    def body(*refs):
        s_refs, out_refs = refs[:n], refs[n:2 * n]
        send_sem, recv_sems, local_sem = refs[2 * n:]
        me = _position()
        my_chip = _chip_index(me)
        peers = [_peer(me, 4), _peer(me, 2), _peer(me, 6)]
        _handshake(peers)
        local = [pltpu.make_async_copy(s_refs[a].at[my_chip], out_refs[a].at[my_chip], local_sem) for a in range(n)]
        sends = [pltpu.make_async_remote_copy(
            src_ref=s_refs[a].at[_chip_index(peer)], dst_ref=out_refs[a].at[my_chip], send_sem=send_sem,
            recv_sem=recv_sems.at[1 + j], device_id=peer, device_id_type=MESH)
            for a in range(n) for j, peer in enumerate(peers)]
        for cp in sends + local:
            cp.start()
        for j, peer in enumerate(peers):
            for a in range(n):
                pltpu.make_async_remote_copy(
                    src_ref=s_refs[a].at[my_chip], dst_ref=out_refs[a].at[_chip_index(peer)], send_sem=send_sem,
                    recv_sem=recv_sems.at[1 + j], device_id=peer, device_id_type=MESH).wait_recv()
        for cp in sends:
            cp.wait_send()
        for cp in local:
            cp.wait()

    return _sequencer_kernel(body, name, collective_id,
                             [jax.ShapeDtypeStruct(s.shape, s.dtype) for s in sums], sums)


def _chip_sum(name, parts, got, after=()):
    _, R, C = parts.shape
    tr = _tile(R, max(8, (3 << 20) // C), 16)
    n_after = len(after)
    limit = 6 * _nbytes((tr, C), parts.dtype) + VMEM_TEMP_ALLOWANCE

    def body(c_ref, p_ref, g_ref, *rest):
        o_ref = rest[n_after]
        o_ref[...] = (p_ref[...].astype(F32) + g_ref[...].astype(F32)).astype(o_ref.dtype)

    blk = pl.BlockSpec((None, tr, C), lambda q, i, c_ref: (q, i, 0))
    mine = pl.BlockSpec((None, tr, C), lambda q, i, c_ref: (2 * q + c_ref[0], i, 0))
    core = lax.axis_index("c").astype(jnp.int32).reshape(1)
    parts, got = [pltpu.with_memory_space_constraint(o, pltpu.HBM) for o in (parts, got)]
    after = [pltpu.with_memory_space_constraint(o, pltpu.HBM) for o in after]
    return pl.pallas_call(
        body, name=name, out_shape=pltpu.HBM((N_DEV // 2, R, C), parts.dtype),
        grid_spec=pltpu.PrefetchScalarGridSpec(
            num_scalar_prefetch=1, grid=(N_DEV // 2, R // tr),
            in_specs=[mine, blk] + [pl.BlockSpec(memory_space=pl.ANY)] * n_after, out_specs=blk),
        compiler_params=pltpu.CompilerParams(dimension_semantics=("parallel", "parallel"),
                                             vmem_limit_bytes=int(limit)),
    )(core, parts, got, *after)


def _all_reduce_rows(name, v):
    R, C = v.shape

    def body(v_ref, out_ref, gath, send_sems, recv_sems):
        me = _position()
        gath[_index(me)] = v_ref[...]
        sends = []
        for k in range(1, N_DEV):
            peer = _peer(me, k)
            sends.append(pltpu.make_async_remote_copy(
                src_ref=v_ref, dst_ref=gath.at[_index(me)], send_sem=send_sems.at[k - 1],
                recv_sem=recv_sems.at[k - 1], device_id=peer, device_id_type=MESH))
        for cp in sends:
            cp.start()
        for k in range(1, N_DEV):
            peer = _peer(me, k)
            pltpu.make_async_remote_copy(
                src_ref=v_ref, dst_ref=gath.at[_index(peer)], send_sem=send_sems.at[k - 1],
                recv_sem=recv_sems.at[k - 1], device_id=peer, device_id_type=MESH).wait_recv()
        for cp in sends:
            cp.wait_send()
        tot = gath[0]
        for s in range(1, N_DEV):
            tot = tot + gath[s]
        out_ref[...] = tot

    return pl.pallas_call(
        body, name=name, out_shape=jax.ShapeDtypeStruct((R, C), F32),
        in_specs=[VMEM_SPEC], out_specs=VMEM_SPEC,
        scratch_shapes=[pltpu.VMEM((N_DEV, R, C), F32), pltpu.SemaphoreType.DMA((7,)),
                        pltpu.SemaphoreType.DMA((7,))],
    )(v)


def _adamw_math(g, w, m, v):
    m = ADAM_B1 * m + (1.0 - ADAM_B1) * g
    v = ADAM_B2 * v + (1.0 - ADAM_B2) * (g * g)
    m_hat = m / (1.0 - ADAM_B1 ** ADAM_STEP)
    v_hat = v / (1.0 - ADAM_B2 ** ADAM_STEP)
    delta = -ADAM_LR * (m_hat / (jnp.sqrt(v_hat) + ADAM_EPS) + ADAM_WD * w)
    return delta, m, v


def _adamw_tile(n):
    def body(p_ref, w_ref, m_ref, v_ref, g_ref, d_ref, nm_ref, nv_ref):
        g = p_ref[0].astype(F32)
        for s in range(1, n):
            g = g + p_ref[s].astype(F32)
        g_ref[...] = g
        d_ref[...], nm_ref[...], nv_ref[...] = _adamw_math(g, w_ref[...], m_ref[...], v_ref[...])
    return body


class _SideJob(NamedTuple):
    operands: list
    in_blocks: list
    out_blocks: list
    out_shapes: list
    body: object


def _adamw_job(parts, w, m, v):
    n, R, C = parts.shape
    lead = w.ndim - 2
    if R % (SIDE_TILES * 16) == 0:
        tile, index = (R // SIDE_TILES, C), lambda t: (t, 0)
    elif C % (SIDE_TILES * 128) == 0:
        tile, index = (R, C // SIDE_TILES), lambda t: (0, t)
    else:
        return None
    p_blk = ((n,) + tile, lambda t: (0,) + index(t))
    w_blk = ((None,) * lead + tile, lambda t: (0,) * lead + index(t))
    return _SideJob([parts, w, m, v], [p_blk, w_blk, w_blk, w_blk], [w_blk] * 4,
                    [jax.ShapeDtypeStruct(w.shape, F32)] * 4, _adamw_tile(n))


def _adamw(name, parts, w, m, v, after=()):
    n, R, C = parts.shape
    tr = _tile(R, max(8, (1 << 18) // C), 16)
    blk = _rows_spec(w, tr)
    out = jax.ShapeDtypeStruct(w.shape, F32)
    return _call(_adamw_tile(n), name, (R // tr,), [pl.BlockSpec((n, tr, C), lambda i: (0, i, 0)), blk, blk, blk],
                 (blk, blk, blk, blk), (out, out, out, out), [parts, w, m, v], sem=("parallel",), after=after)


def kernel(x, mem, g_mix, w_in, conv_a_w, conv_a_b, ln_a_g, ln_a_b, conv_b_w, w_out, g_xattn, g_mem, w_q, w_k, w_v, w_o, g_ffn, w_gate, w_up, conv_f_w, w_down, g_final, loss_target, m_g_mix, m_w_in, m_conv_a_w, m_conv_a_b, m_ln_a_g, m_ln_a_b, m_conv_b_w, m_w_out, m_g_xattn, m_g_mem, m_w_q, m_w_k, m_w_v, m_w_o, m_g_ffn, m_w_gate, m_w_up, m_conv_f_w, m_w_down, m_g_final, v_g_mix, v_w_in, v_conv_a_w, v_conv_a_b, v_ln_a_g, v_ln_a_b, v_conv_b_w, v_w_out, v_g_xattn, v_g_mem, v_w_q, v_w_k, v_w_v, v_w_o, v_g_ffn, v_w_gate, v_w_up, v_conv_f_w, v_w_down, v_g_final):
    T, D = x.shape[1], x.shape[2]
    Mm = mem.shape[1]
    CW = conv_a_b.shape[1]
    INB = w_in.shape[2]
    FB = w_gate.shape[2]
    KA, KB, KF = conv_a_w.shape[1], conv_b_w.shape[1], conv_f_w.shape[1]
    DB = D // N_DEV
    assert 5 * CW == N_DEV * INB and 2 * CW == D

    x2, mem2, tgt = x[0], mem[0], loss_target[0]
    g_mem2, g_final2 = g_mem.reshape(1, D), g_final.reshape(1, D)

    def bf16(name, w):
        return _cast_bf16("cast_" + name, w)

    Win, caw, cbw = _sequencer_all_gather(
        "ag_in", AG_ID, [bf16("w_in", w_in), conv_a_w[0], conv_b_w[0]])
    Wout, = _sequencer_all_gather("ag_out", AG_ID, [bf16("w_out", w_out)])
    Wq, Wk, Wv, Wo = _sequencer_all_gather(
        "ag_attn", AG_ID, [bf16("w_q", w_q), bf16("w_k", w_k), bf16("w_v", w_v), bf16("w_o", w_o)])
    def transposed(w):
        return jnp.transpose(w[0])

    gate_t = [transposed(a) for a in (w_gate, m_w_gate, v_w_gate)]
    up_t = [transposed(a) for a in (w_up, m_w_up, v_w_up)]
    WgateT, cfw = _sequencer_all_gather("ag_gate", AG_ID, [bf16("w_gate", gate_t[0]), conv_f_w[0]])
    WupT, = _sequencer_all_gather("ag_up", AG_ID, [bf16("w_up", up_t[0])])
    Wdown, = _sequencer_all_gather("ag_down", AG_ID, [bf16("w_down", w_down)])
    Wout, Wq, Wk, Wv, Wo = [w.reshape(D, D) for w in (Wout, Wq, Wk, Wv, Wo)]
    caw = jnp.transpose(caw, (1, 0, 2)).reshape(KA, CW)
    cbw = jnp.transpose(cbw, (1, 0, 2)).reshape(KB, CW)

    xn1 = _rms_fwd("rms_mix", x2, g_mix)
    proj = _matmul("mm_proj", "nn", xn1, Win, M=T, N=INB, K=D, nb=N_DEV, b_lay="blk", o_lay="col", tn=INB,
                   out_dtype=BF16)
    mix, u1 = _mixer_fwd(proj, caw, conv_a_b, ln_a_g, ln_a_b, cbw, T, CW)
    h1 = _matmul("mm_h1", "nn", mix, Wout.reshape(2, CW, D), M=T, N=D, K=CW, nb=2, a_lay="blk", b_lay="blk",
                 red_block=True, res=x2)
    xn2 = _rms_fwd("rms_xattn", h1, g_xattn)
    q = _matmul("mm_q", "nn", xn2, Wq, M=T, N=D, K=D, out_dtype=BF16)
    memn = _rms_fwd("rms_mem", mem2, g_mem2, after=[q])
    kk = _matmul("mm_k", "nn", memn, Wk, M=Mm, N=D, K=D, out_dtype=BF16)
    vv = _matmul("mm_v", "nn", memn, Wv, M=Mm, N=D, K=D, out_dtype=BF16)
    o = _attn_fwd(q, kk, vv)
    h2 = _matmul("mm_h2", "nn", o, Wo, M=T, N=D, K=D, res=h1)
    xn3 = _rms_fwd("rms_ffn", h2, g_ffn)
    gpre = _matmul("mm_gate", "nt", xn3, WgateT, M=T, N=FB, K=D, nb=N_DEV, b_lay="blk", o_lay="blk", tn=FB,
                   out_dtype=BF16)
    up = _matmul("mm_up", "nt", xn3, WupT, M=T, N=FB, K=D, nb=N_DEV, b_lay="blk", o_lay="blk", tn=FB,
                 out_dtype=BF16)
    f = _ffn_act_fwd(gpre, up, cfw)
    h3 = _matmul("mm_h3", "nn", f, Wdown, M=T, N=D, K=FB, nb=N_DEV, a_lay="blk", b_lay="blk", red_block=True,
                 group=2, res=h2)
    dh3, dh3b, loss_part, dg_final = _loss_head(h3, tgt, g_final2)

    wmv = {"w_in": (w_in, m_w_in, v_w_in), "conv_a_w": (conv_a_w, m_conv_a_w, v_conv_a_w),
           "conv_b_w": (conv_b_w, m_conv_b_w, v_conv_b_w), "w_out": (w_out, m_w_out, v_w_out),
           "w_q": (w_q, m_w_q, v_w_q), "w_k": (w_k, m_w_k, v_w_k), "w_v": (w_v, m_w_v, v_w_v),
           "w_o": (w_o, m_w_o, v_w_o), "w_gate": gate_t, "w_up": up_t,
           "conv_f_w": (conv_f_w, m_conv_f_w, v_conv_f_w), "w_down": (w_down, m_w_down, v_w_down)}
    res = {}
    pending = []

    def mm(*args, after=(), carry=(), **kwargs):
        behind = list(after) + pending
        pending.clear()
        jobs, alone = [], []
        for names, got in carry:
            for n, g in zip(names, got):
                job = _adamw_job(g, *wmv[n])
                (alone if job is None else jobs).append((n, g, job))
        results = None
        if jobs:
            out, results = _matmul(*args, after=behind, side=[job for _, _, job in jobs], **kwargs)
        else:
            out = _matmul(*args, after=behind, **kwargs)
        if results is None:
            alone, jobs = alone + jobs, []
        for (n, _, _), r in zip(jobs, results or []):
            res[n] = r
            pending.append(r[0])
        for n, g, _ in alone:
            res[n] = _adamw("adamw_" + n, g, *wmv[n], after=[out])
            pending.append(res[n][0])
        return out

    def to_sibling(tag, named_parts):
        got = _sequencer_to_sibling("rs1_" + tag, SIBLING_ID, [p for _, p in named_parts])
        return named_parts, got

    def to_chips(tag, stage1, after):
        named_parts, got = stage1
        sums = [_chip_sum("sum_" + n, p, g, after=after) for (n, p), g in zip(named_parts, got)]
        pending.extend(sums)
        return [n for n, _ in named_parts], _sequencer_to_chips("rs2_" + tag, CHIPS_ID, sums)

    def finish(stage2, after):
        names, got = stage2
        for n, g in zip(names, got):
            w, m, v = wmv[n]
            res[n] = _adamw("adamw_" + n, g, w, m, v, after=after)
            pending.append(res[n][0])

    def row_blocks(dw):
        return dw.reshape(N_DEV, DB, D)

    def conv_blocks(dw, K):
        return jnp.transpose(dw.reshape(K, N_DEV, CW // N_DEV), (1, 0, 2))

    dWdown = mm("mm_dw_down", "tn", f, dh3b, M=FB, N=D, K=T, nb=N_DEV, a_lay="blk", o_lay="blk",
                     out_dtype=BF16, tm=FB)
    s_down = to_sibling("down", [("w_down", dWdown)])
    df = mm("mm_df", "nt", dh3b, Wdown, M=T, N=FB, K=D, nb=N_DEV, b_lay="blk", o_lay="blk", tn=FB, out_dtype=BF16,
                 after=[dWdown])
    dg, dup, dcfw = _ffn_act_bwd1(df, gpre, up, cfw)
    dgpre = _ffn_act_bwd2(dg, cfw)
    c_down = to_chips("down", s_down, after=[dgpre])
    dWgate = mm("mm_dw_gate", "tn", dgpre, xn3, M=FB, N=D, K=T, nb=N_DEV, a_lay="blk", o_lay="blk",
                     out_dtype=BF16, tm=FB)
    s_gate = to_sibling("gate", [("w_gate", dWgate), ("conv_f_w", dcfw)])
    dWup = mm("mm_dw_up", "tn", dup, xn3, M=FB, N=D, K=T, nb=N_DEV, a_lay="blk", o_lay="blk",
                   out_dtype=BF16, tm=FB, after=[dWgate])
    s_up = to_sibling("up", [("w_up", dWup)])
    dxn3 = mm("mm_dxn3_gate", "nn", dgpre, WgateT, M=T, N=D, K=FB, nb=N_DEV, a_lay="blk", b_lay="blk",
                   red_block=True, group=2, out_dtype=BF16, after=[dWup], carry=[c_down])
    c_gate = to_chips("gate", s_gate, after=[dxn3])
    dxn3 = mm("mm_dxn3_up", "nn", dup, WupT, M=T, N=D, K=FB, nb=N_DEV, a_lay="blk", b_lay="blk",
                   red_block=True, group=2, out_dtype=BF16, res=dxn3)
    c_up = to_chips("up", s_up, after=[dxn3])
    dh2, dh2b, dg_ffn = _rms_bwd("rms_bwd_ffn", dxn3, h2, g_ffn, dh3)

    dWo = mm("mm_dw_o", "tn", o, dh2b, M=D, N=D, K=T, out_dtype=BF16)
    s_o = to_sibling("o", [("w_o", row_blocks(dWo))])
    do = mm("mm_do", "nt", dh2b, Wo, M=T, N=D, K=D, out_dtype=BF16, after=[dWo])
    dq, dk, dv = _attn_bwd(q, kk, vv, do)
    dWq = mm("mm_dw_q", "tn", xn2, dq, M=D, N=D, K=T, out_dtype=BF16, carry=[c_gate])
    s_q = to_sibling("q", [("w_q", row_blocks(dWq))])
    dxn2 = mm("mm_dxn2", "nt", dq, Wq, M=T, N=D, K=D, out_dtype=BF16, after=[dWq], carry=[c_up])
    c_o = to_chips("o", s_o, after=[dxn2])
    c_q = to_chips("q", s_q, after=[dxn2])
    dh1, dh1b, dg_xattn = _rms_bwd("rms_bwd_xattn", dxn2, h1, g_xattn, dh2)
    dWk = mm("mm_dw_k", "tn", memn, dk, M=D, N=D, K=Mm, out_dtype=BF16, after=[dh1b])
    dWv = mm("mm_dw_v", "tn", memn, dv, M=D, N=D, K=Mm, out_dtype=BF16, after=[dh1b])
    s_kv = to_sibling("kv", [("w_k", row_blocks(dWk)), ("w_v", row_blocks(dWv))])
    dmemn = mm("mm_dmem_k", "nt", dk, Wk, M=Mm, N=D, K=D, after=[dWk, dWv])
    dmemn = mm("mm_dmem_v", "nt", dv, Wv, M=Mm, N=D, K=D, res=dmemn)
    dg_mem = _rms_bwd("rms_bwd_mem", dmemn, mem2, g_mem2)

    dWout = mm("mm_dw_out", "tn", mix, dh1b, M=CW, N=D, K=T, nb=2, a_lay="blk", o_lay="blk", out_dtype=BF16,
                    after=[dg_mem], carry=[c_o])
    s_out = to_sibling("out", [("w_out", row_blocks(dWout.reshape(D, D)))])
    dmix = mm("mm_dmix", "nt", dh1b, Wout, M=T, N=D, K=D, out_dtype=BF16, after=[dWout], carry=[c_q])
    c_kv = to_chips("kv", s_kv, after=[dmix])
    du1, dcv, dbg, dcaw, dcab, dlng, dlnb, dcbw = _mixer_bwd1(dmix, proj, u1, caw, ln_a_g, ln_a_b, cbw, T, CW)
    c_out = to_chips("out", s_out, after=[du1])
    dav, dag, dcg, dbh = _mixer_bwd2(du1, dcv, proj, caw, cbw, T, CW)
    dproj = jnp.concatenate([dav, dag, dbg, dcg, dbh], axis=1)
    dWin = mm("mm_dw_in", "tn", xn1, dproj, M=D, N=INB, K=T, nb=N_DEV, b_lay="col", o_lay="blk",
                   out_dtype=BF16, tm=512, tn=INB, carry=[c_kv])
    s_in = to_sibling("in", [("w_in", dWin), ("conv_a_w", conv_blocks(dcaw, KA)),
                             ("conv_b_w", conv_blocks(dcbw, KB))])
    c_in = to_chips("in", s_in, after=list(pending))
    dxn1 = mm("mm_dxn1", "nt", dproj, Win, M=T, N=D, K=INB, nb=N_DEV, a_lay="col", b_lay="blk",
                   red_block=True, group=2, out_dtype=BF16, after=[dWin])
    finish(c_out, after=[dxn1])
    dx, _, dg_mix = _rms_bwd("rms_bwd_mix", dxn1, x2, g_mix, dh1, after=list(pending))

    def pair(a, b):
        return jnp.concatenate([a, b], axis=1)

    zeros_half = jnp.zeros((1, CW), F32)
    small_g = jnp.concatenate([
        dg_mix, pair(dcab, dlng), pair(dlnb, zeros_half), dg_xattn, dg_mem, dg_ffn, dg_final,
        jnp.broadcast_to(loss_part[:, :1], (1, D))], axis=0)
    small_sum = _all_reduce_rows("ar_small", small_g)
    loss = small_sum[7, 0]
    finish(c_in, after=[small_sum])

    def pack(a_mix, a_cab, a_lng, a_lnb, a_xattn, a_mem, a_ffn, a_final):
        return jnp.concatenate([a_mix, pair(a_cab, a_lng), pair(a_lnb, zeros_half), a_xattn, a_mem.reshape(1, D),
                                a_ffn, a_final.reshape(1, D), jnp.zeros((1, D), F32)], axis=0)

    small = _adamw("adamw_small", small_sum[None],
                   pack(g_mix, conv_a_b, ln_a_g, ln_a_b, g_xattn, g_mem, g_ffn, g_final),
                   pack(m_g_mix, m_conv_a_b, m_ln_a_g, m_ln_a_b, m_g_xattn, m_g_mem, m_g_ffn, m_g_final),
                   pack(v_g_mix, v_conv_a_b, v_ln_a_g, v_ln_a_b, v_g_xattn, v_g_mem, v_g_ffn, v_g_final))

    def unpack(a):
        return {"g_mix": a[0:1], "conv_a_b": a[1:2, :CW], "ln_a_g": a[1:2, CW:], "ln_a_b": a[2:3, :CW],
                "g_xattn": a[3:4], "g_mem": a[4], "g_ffn": a[5:6], "g_final": a[6]}

    small = [unpack(a) for a in small]
    order = ["g_mix", "w_in", "conv_a_w", "conv_a_b", "ln_a_g", "ln_a_b", "conv_b_w", "w_out", "g_xattn", "g_mem",
             "w_q", "w_k", "w_v", "w_o", "g_ffn", "w_gate", "w_up", "conv_f_w", "w_down", "g_final"]
    outs = [loss, dx[None]]
    for kind in range(4):
        for n in order:
            if n in ("w_gate", "w_up"):
                outs.append(jnp.transpose(res[n][kind])[None])
            else:
                outs.append(res[n][kind] if n in res else small[kind][n])
    return tuple(outs)
```

```python
import functools
from typing import NamedTuple

import jax
import jax.numpy as jnp
from jax import lax
from jax.experimental import pallas as pl
from jax.experimental.pallas import tpu as pltpu
from jax.experimental.pallas import tpu_sc as plsc

F32 = jnp.float32
BF16 = jnp.bfloat16

N_DEV = 8
EPS = 1e-6
GROUP_DIM = 128
N_XATTN_HEADS = 4
ADAM_LR = 0.001
ADAM_B1 = 0.9
ADAM_B2 = 0.999
ADAM_EPS = 1e-08
ADAM_WD = 0.01
ADAM_STEP = 10

AG_ID, SIBLING_ID, CHIPS_ID = 1, 2, 3

SIDE_TILES = 32
HALO = 32
VMEM_V7X_BYTES = 64 * 1024 * 1024
VMEM_TEMP_ALLOWANCE = 12 * 1024 * 1024

VMEM_SPEC = pl.BlockSpec(memory_space=pltpu.VMEM)
MESH = pl.DeviceIdType.MESH


def _tile(n, pref, align):
    if n <= pref:
        return n
    t = (pref // align) * align
    while t >= align:
        if n % t == 0:
            return t
        t -= align
    return n


def _nbytes(shape, dtype):
    n = 1
    for d in shape:
        if d is not None:
            n *= d
    return n * jnp.dtype(dtype).itemsize


def _call(body, name, grid, in_specs, out_specs, out_shape, operands, scratch=(), sem=None, after=()):
    outs = out_shape if isinstance(out_shape, (tuple, list)) else (out_shape,)
    ospecs = out_specs if isinstance(out_specs, (tuple, list)) else (out_specs,)
    est = 0
    for spec, arr in list(zip(in_specs, operands)) + list(zip(ospecs, outs)):
        est += 2 * _nbytes(spec.block_shape, arr.dtype)
    for s in scratch:
        if hasattr(s, "shape") and hasattr(s, "dtype"):
            est += _nbytes(s.shape, s.dtype)
    limit = min(est + VMEM_TEMP_ALLOWANCE, VMEM_V7X_BYTES - 4 * 1024 * 1024)
    if sem is None:
        sem = ("arbitrary",) * len(grid)
    n_in, n_after = len(operands), len(after)
    operands = [pltpu.with_memory_space_constraint(o, pltpu.HBM) for o in operands]
    after = [pltpu.with_memory_space_constraint(o, pltpu.HBM) for o in after]
    in_hbm = [pltpu.HBM(o.shape, o.dtype) for o in outs]
    out_shape = in_hbm if isinstance(out_shape, (tuple, list)) else in_hbm[0]

    def ordered_body(*refs):
        body(*refs[:n_in], *refs[n_in + n_after:])

    return pl.pallas_call(
        ordered_body if n_after else body, name=name, grid=grid,
        in_specs=list(in_specs) + [pl.BlockSpec(memory_space=pl.ANY)] * n_after,
        out_specs=out_specs, out_shape=out_shape, scratch_shapes=list(scratch),
        compiler_params=pltpu.CompilerParams(dimension_semantics=sem, vmem_limit_bytes=int(limit)),
    )(*operands, *after)


_DOT_DIMS = {"nn": (((1,), (0,)), ((), ())), "nt": (((1,), (1,)), ((), ())), "tn": (((0,), (0,)), ((), ()))}


def _operand_spec(layout, tr, tc, cols_per_block, pick, group=None):
    if layout == "plain":
        return pl.BlockSpec((tr, tc), lambda *g: pick(*g)[1:])
    if layout == "blk":
        return pl.BlockSpec((group, tr, tc), lambda *g: pick(*g))
    assert layout == "col"
    if group:
        assert tc == cols_per_block
        return pl.BlockSpec((tr, group * tc), lambda *g: (pick(*g)[1], pick(*g)[0]))
    per = cols_per_block // tc
    return pl.BlockSpec((tr, tc), lambda *g: (pick(*g)[1], pick(*g)[0] * per + pick(*g)[2]))


def _matmul(name, dims, a, b, *, M, N, K, nb=1, a_lay="plain", b_lay="plain", o_lay="plain",
            red_block=False, group=None, out_dtype=F32, res=None, tm=1024, tn=None, after=(), side=()):
    tm = _tile(M, tm, 128 if dims == "tn" else 16)
    tn = _tile(N, tn or (1024 if red_block else 512), 128)
    tk = K
    gi, gj, gk = M // tm, N // tn, K // tk
    if red_block:
        grid = (gi, gj, nb // (group or 1), gk)
        unpack = lambda i, j, bb, k: (bb, i, j, k)
        red_axes, sem = (2, 3), ("parallel", "parallel", "arbitrary", "arbitrary")
    else:
        grid = (nb, gi, gj, gk)
        unpack = lambda bb, i, j, k: (bb, i, j, k)
        red_axes, sem = (3,), ("parallel", "parallel", "parallel", "arbitrary")

    def picker(f):
        return lambda *g: f(*unpack(*g))

    if dims == "tn":
        a_spec = _operand_spec(a_lay, tk, tm, M, picker(lambda bb, i, j, k: (bb, k, i)), group)
    else:
        a_spec = _operand_spec(a_lay, tm, tk, K, picker(lambda bb, i, j, k: (bb, i, k)), group)
    if dims == "nt":
        b_spec = _operand_spec(b_lay, tn, tk, K, picker(lambda bb, i, j, k: (bb, j, k)), group)
    else:
        b_spec = _operand_spec(b_lay, tk, tn, N, picker(lambda bb, i, j, k: (bb, k, j)), group)
    o_spec = _operand_spec(o_lay, tm, tn, N, picker(lambda bb, i, j, k: (bb, i, j)))
    if o_lay == "plain":
        out_shape = jax.ShapeDtypeStruct((M, N), out_dtype)
    elif o_lay == "blk":
        out_shape = jax.ShapeDtypeStruct((nb, M, N), out_dtype)
    else:
        out_shape = jax.ShapeDtypeStruct((M, nb * N), out_dtype)
    n_red = [grid[ax] for ax in red_axes]
    has_res = res is not None
    one_step = all(n == 1 for n in n_red)

    def contract(a_ref, b_ref):
        if group:
            parts = [(a_ref[p] if a_lay == "blk" else a_ref[:, p * K:(p + 1) * K], b_ref[p]) for p in range(group)]
        else:
            parts = [(a_ref[...], b_ref[...])]
        r = None
        for a_part, b_part in parts:
            d = lax.dot_general(a_part, b_part, _DOT_DIMS[dims], preferred_element_type=F32)
            r = d if r is None else r + d
        return r

    def step_index(*g):
        s = g[0]
        for ax in range(1, len(grid)):
            s = s * grid[ax] + g[ax]
        return s

    def side_spec(block):
        shape, index = block
        return pl.BlockSpec(shape, lambda *g: index(jnp.minimum(step_index(*g), SIDE_TILES - 1)))

    asked_to_carry = bool(side)
    if functools.reduce(lambda p, q: p * q, grid) < SIDE_TILES:
        side = ()
    n_main_in = 3 if has_res else 2
    n_side_in = sum(len(job.operands) for job in side)
    n_side_out = sum(len(job.out_shapes) for job in side)

    def side_work(side_in, side_out):
        @pl.when(step_index(*[pl.program_id(ax) for ax in range(len(grid))]) < SIDE_TILES)
        def _():
            i0 = o0 = 0
            for job in side:
                n_i, n_o = len(job.operands), len(job.out_shapes)
                job.body(*side_in[i0:i0 + n_i], *side_out[o0:o0 + n_o])
                i0, o0 = i0 + n_i, o0 + n_o

    def body(*refs):
        a_ref, b_ref = refs[:2]
        side_in = refs[n_main_in:n_main_in + n_side_in]
        o_ref = refs[n_main_in + n_side_in]
        side_out = refs[n_main_in + n_side_in + 1:n_main_in + n_side_in + 1 + n_side_out]
        if one_step:
            r = contract(a_ref, b_ref)
            if has_res:
                r = r + refs[2][...]
            o_ref[...] = r.astype(o_ref.dtype)
        else:
            acc = refs[-1]
            first = functools.reduce(jnp.logical_and, [pl.program_id(ax) == 0 for ax in red_axes])
            last = functools.reduce(jnp.logical_and,
                                    [pl.program_id(ax) == n - 1 for ax, n in zip(red_axes, n_red)])

            @pl.when(first)
            def _():
                acc[...] = jnp.zeros_like(acc)

            acc[...] += contract(a_ref, b_ref)

            @pl.when(last)
            def _():
                r = acc[...]
                if has_res:
                    r = r + refs[2][...]
                o_ref[...] = r.astype(o_ref.dtype)
        if side:
            side_work(side_in, side_out)

    in_specs = [a_spec, b_spec]
    operands = [a, b]
    if has_res:
        in_specs.append(_operand_spec("plain", tm, tn, N, picker(lambda bb, i, j, k: (bb, i, j))))
        operands.append(res)
    out_specs, out_shapes = [o_spec], [out_shape]
    for job in side:
        in_specs += [side_spec(blk) for blk in job.in_blocks]
        operands += job.operands
        out_specs += [side_spec(blk) for blk in job.out_blocks]
        out_shapes += job.out_shapes
    scratch = [] if one_step else [pltpu.VMEM((tm, tn), F32)]
    if not side:
        out = _call(body, name, grid, in_specs, o_spec, out_shape, operands, scratch=scratch, sem=sem, after=after)
        return (out, None) if asked_to_carry else out
    outs = _call(body, name, grid, in_specs, tuple(out_specs), tuple(out_shapes), operands, scratch=scratch,
                 sem=("arbitrary",) * len(grid), after=after)
    results, pos = [], 1
    for job in side:
        results.append(tuple(outs[pos:pos + len(job.out_shapes)]))
        pos += len(job.out_shapes)
    return outs[0], results


def _rows_spec(arr, tr):
    lead = arr.ndim - 2
    return pl.BlockSpec((None,) * lead + (tr, arr.shape[-1]), lambda i: (0,) * lead + (i, 0))


def _cast_bf16(name, w):
    R, C = w.shape[-2:]
    tr = _tile(R, max(8, (3 << 20) // C), 16)

    def body(w_ref, o_ref):
        o_ref[...] = w_ref[...].astype(BF16)

    return _call(body, name, (R // tr,), [_rows_spec(w, tr)],
                 pl.BlockSpec((tr, C), lambda i: (i, 0)), jax.ShapeDtypeStruct((R, C), BF16), [w],
                 sem=("parallel",))


def _rms_fwd(name, x, g, after=()):
    T, D = x.shape
    tm = _tile(T, 128, 16)

    def body(x_ref, g_ref, o_ref):
        xv = x_ref[...]
        r = lax.rsqrt(jnp.mean(xv * xv, axis=-1, keepdims=True) + EPS)
        o_ref[...] = (xv * r * g_ref[...]).astype(BF16)

    return _call(body, name, (T // tm,),
                 [pl.BlockSpec((tm, D), lambda i: (i, 0)), pl.BlockSpec((1, D), lambda i: (0, 0))],
                 pl.BlockSpec((tm, D), lambda i: (i, 0)), jax.ShapeDtypeStruct((T, D), BF16), [x, g],
                 sem=("parallel",), after=after)


def _rms_bwd(name, dxn, x, g, dh=None, after=()):
    T, D = x.shape
    tm = _tile(T, 128, 16)
    with_dx = dh is not None

    def body(*refs):
        if with_dx:
            dxn_ref, x_ref, g_ref, dh_ref, o_ref, ob_ref, dg_ref = refs
        else:
            dxn_ref, x_ref, g_ref, dg_ref = refs
        xv = x_ref[...]
        r = lax.rsqrt(jnp.mean(xv * xv, axis=-1, keepdims=True) + EPS)
        xh = xv * r
        dy = dxn_ref[...].astype(F32)

        @pl.when(pl.program_id(0) == 0)
        def _():
            dg_ref[...] = jnp.zeros_like(dg_ref)

        dg_ref[...] += jnp.sum(dy * xh, axis=0, keepdims=True)
        if with_dx:
            dyg = dy * g_ref[...]
            tot = dh_ref[...] + r * (dyg - xh * jnp.mean(dyg * xh, axis=-1, keepdims=True))
            o_ref[...] = tot
            ob_ref[...] = tot.astype(BF16)

    row = pl.BlockSpec((tm, D), lambda i: (i, 0))
    vec = pl.BlockSpec((1, D), lambda i: (0, 0))
    if with_dx:
        return _call(body, name, (T // tm,), [row, row, vec, row], (row, row, vec),
                     (jax.ShapeDtypeStruct((T, D), F32), jax.ShapeDtypeStruct((T, D), BF16),
                      jax.ShapeDtypeStruct((1, D), F32)), [dxn, x, g, dh], after=after)
    return _call(body, name, (T // tm,), [row, row, vec], vec, jax.ShapeDtypeStruct((1, D), F32), [dxn, x, g])


def _loss_head(h, target, g):
    T, D = h.shape
    tm = _tile(T, 128, 16)

    def body(h_ref, t_ref, g_ref, o_ref, ob_ref, loss_ref, dg_ref):
        xv = h_ref[...]
        gv = g_ref[...]
        r = lax.rsqrt(jnp.mean(xv * xv, axis=-1, keepdims=True) + EPS)
        xh = xv * r
        e = xh * gv - t_ref[...]

        @pl.when(pl.program_id(0) == 0)
        def _():
            dg_ref[...] = jnp.zeros_like(dg_ref)
            loss_ref[...] = jnp.zeros_like(loss_ref)

        loss_ref[...] += 0.5 * jnp.sum(jnp.mean(e * e, axis=-1, keepdims=True), axis=0, keepdims=True)
        dy = e * (1.0 / D)
        dg_ref[...] += jnp.sum(dy * xh, axis=0, keepdims=True)
        dyg = dy * gv
        dx = r * (dyg - xh * jnp.mean(dyg * xh, axis=-1, keepdims=True))
        o_ref[...] = dx
        ob_ref[...] = dx.astype(BF16)

    row = pl.BlockSpec((tm, D), lambda i: (i, 0))
    vec = pl.BlockSpec((1, D), lambda i: (0, 0))
    return _call(body, "loss_head", (T // tm,), [row, row, vec],
                 (row, row, pl.BlockSpec((1, 128), lambda i: (0, 0)), vec),
                 (jax.ShapeDtypeStruct((T, D), F32), jax.ShapeDtypeStruct((T, D), BF16),
                  jax.ShapeDtypeStruct((1, 128), F32), jax.ShapeDtypeStruct((1, D), F32)), [h, target, g])


ROW_CHUNK = 64
SUBLANES = 8


def _col_chunks(width):
    return [slice(c0, min(c0 + GROUP_DIM, width)) for c0 in range(0, width, GROUP_DIM)]


def _row_chunks(n_rows):
    return [(r0, min(ROW_CHUNK, n_rows - r0)) for r0 in range(0, n_rows, ROW_CHUNK)]


def _pad_rows(K):
    return -(-(K - 1) // SUBLANES) * SUBLANES


def _shifted_back(buf, K, r0, nr, cs):
    pad = _pad_rows(K)
    win = buf[pl.ds(HALO + r0 - pad, nr + pad), cs]
    for b in range(min(SUBLANES, K)):
        rolled = win if b == 0 else pltpu.roll(win, b, axis=0)
        for a in range((K - 1 - b) // SUBLANES + 1):
            yield K - 1 - (SUBLANES * a + b), rolled[pad - SUBLANES * a:pad - SUBLANES * a + nr]


def _conv_fwd(buf, w_ref, K, r0, nr, cs):
    y = None
    for k, xs in _shifted_back(buf, K, r0, nr, cs):
        term = xs * w_ref[pl.ds(k, 1), cs]
        y = term if y is None else y + term
    return y


def _conv_bwd_input(buf, w_ref, K, r0, nr, cs):
    pad = _pad_rows(K)
    win = buf[pl.ds(r0, nr + pad), cs]
    dx = None
    for b in range(min(SUBLANES, K)):
        rolled = win if b == 0 else pltpu.roll(win, nr + pad - b, axis=0)
        for a in range((K - 1 - b) // SUBLANES + 1):
            k = K - 1 - (SUBLANES * a + b)
            term = rolled[SUBLANES * a:SUBLANES * a + nr] * w_ref[pl.ds(k, 1), cs]
            dx = term if dx is None else dx + term
    return dx


def _fold_rows(v):
    nr, lanes = v.shape
    if nr % SUBLANES:
        return jnp.sum(v, axis=0, keepdims=True)
    return jnp.sum(v.reshape(nr // SUBLANES, SUBLANES, lanes), axis=0)


def _conv_bwd_weight(accs, dy, buf, K, r0, nr, cs):
    accs = list(accs)
    for k, xs in _shifted_back(buf, K, r0, nr, cs):
        accs[k] = accs[k] + _fold_rows(dy * xs)
    return accs


def _add_row(ref, row, cs, acc):
    ref[pl.ds(row, 1), cs] += jnp.sum(acc, axis=0, keepdims=True)


def _sigmoid(z):
    return 0.5 * jnp.tanh(0.5 * z) + 0.5


def _silu_grad(z, sig):
    return sig * (1.0 + z * (1.0 - sig))


def _group_norm(xg):
    xc = xg - jnp.mean(xg, axis=-1, keepdims=True)
    rstd = lax.rsqrt(jnp.mean(xc * xc, axis=-1, keepdims=True) + EPS)
    return xc * rstd, rstd


def _mixer_tiles(T, CW):
    tm = _tile(T, 512, HALO)
    tc = _tile(CW, 256, GROUP_DIM)
    return tm, tc, tm // HALO, CW // tc


def _mixer_fwd(proj, caw, cab, lng, lnb, cbw, T, CW):
    KA, KB = caw.shape[0], cbw.shape[0]
    tm, tc, hb, nc = _mixer_tiles(T, CW)

    def sec(s):
        return pl.BlockSpec((tm, tc), lambda i, c: (i, s * nc + c))

    def sec_prev(s):
        return pl.BlockSpec((HALO, tc), lambda i, c: (jnp.maximum(i * hb - 1, 0), s * nc + c))

    def chan(rows):
        return pl.BlockSpec((rows, tc), lambda i, c: (0, c))

    def body(av, ag, bg, cg, bh, avh, agh, cgh, bhh, caw_ref, cab_ref, lng_ref, lnb_ref, cbw_ref,
             mix_ref, u1_ref, bufa, bufb):
        first = pl.program_id(0) == 0
        bufa[pl.ds(0, HALO), :] = jnp.where(first, 0.0, avh[...].astype(F32) * _sigmoid(agh[...].astype(F32)))
        bufb[pl.ds(0, HALO), :] = jnp.where(first, 0.0, cgh[...].astype(F32) * bhh[...].astype(F32))
        for cs in _col_chunks(tc):
            for r0, nr in _row_chunks(tm):
                rows = pl.ds(r0, nr)
                bufa[pl.ds(HALO + r0, nr), cs] = av[rows, cs].astype(F32) * _sigmoid(ag[rows, cs].astype(F32))
                bufb[pl.ds(HALO + r0, nr), cs] = cg[rows, cs].astype(F32) * bh[rows, cs].astype(F32)
        for cs in _col_chunks(tc):
            for r0, nr in _row_chunks(tm):
                rows = pl.ds(r0, nr)
                u1 = _conv_fwd(bufa, caw_ref, KA, r0, nr, cs) + cab_ref[:, cs]
                u1_ref[rows, cs] = u1
                y, _ = _group_norm(u1)
                z = y * lng_ref[:, cs] + lnb_ref[:, cs]
                mix_ref[0, rows, cs] = (z * _sigmoid(z)).astype(BF16)
                mix_ref[1, rows, cs] = (bg[rows, cs].astype(F32) * _conv_fwd(bufb, cbw_ref, KB, r0, nr, cs)).astype(BF16)

    in_specs = [sec(0), sec(1), sec(2), sec(3), sec(4), sec_prev(0), sec_prev(1), sec_prev(3), sec_prev(4),
                chan(KA), chan(1), chan(1), chan(1), chan(KB)]
    operands = [proj] * 9 + [caw, cab, lng, lnb, cbw]
    return _call(body, "mixer_fwd", (T // tm, nc), in_specs,
                 (pl.BlockSpec((2, tm, tc), lambda i, c: (0, i, c)), pl.BlockSpec((tm, tc), lambda i, c: (i, c))),
                 (jax.ShapeDtypeStruct((2, T, CW), BF16), jax.ShapeDtypeStruct((T, CW), F32)), operands,
                 scratch=[pltpu.VMEM((HALO + tm, tc), F32), pltpu.VMEM((HALO + tm, tc), F32)],
                 sem=("parallel", "parallel"))


def _mixer_bwd1(dmix, proj, u1, caw, lng, lnb, cbw, T, CW):
    KA, KB = caw.shape[0], cbw.shape[0]
    tm, tc, hb, nc = _mixer_tiles(T, CW)

    def sec(s):
        return pl.BlockSpec((tm, tc), lambda c, i: (i, s * nc + c))

    def sec_prev(s):
        return pl.BlockSpec((HALO, tc), lambda c, i: (jnp.maximum(i * hb - 1, 0), s * nc + c))

    def chan(rows):
        return pl.BlockSpec((rows, tc), lambda c, i: (0, c))

    tile = pl.BlockSpec((tm, tc), lambda c, i: (i, c))

    def body(du, dv, u1_ref, av, ag, bg, cg, bh, avh, agh, cgh, bhh, lng_ref, lnb_ref, cbw_ref,
             du1_ref, dcv_ref, dbg_ref, dcaw_ref, dcab_ref, dlng_ref, dlnb_ref, dcbw_ref, bufa, bufb):
        first = pl.program_id(1) == 0

        @pl.when(first)
        def _():
            for r in (dcaw_ref, dcab_ref, dlng_ref, dlnb_ref, dcbw_ref):
                r[...] = jnp.zeros_like(r)

        bufa[pl.ds(0, HALO), :] = jnp.where(first, 0.0, avh[...].astype(F32) * _sigmoid(agh[...].astype(F32)))
        bufb[pl.ds(0, HALO), :] = jnp.where(first, 0.0, cgh[...].astype(F32) * bhh[...].astype(F32))
        for cs in _col_chunks(tc):
            for r0, nr in _row_chunks(tm):
                rows = pl.ds(r0, nr)
                bufa[pl.ds(HALO + r0, nr), cs] = av[rows, cs].astype(F32) * _sigmoid(ag[rows, cs].astype(F32))
                bufb[pl.ds(HALO + r0, nr), cs] = cg[rows, cs].astype(F32) * bh[rows, cs].astype(F32)
        for cs in _col_chunks(tc):
            lanes = cs.stop - cs.start
            zero = jnp.zeros((SUBLANES, lanes), F32)
            a_lng, a_lnb, a_cab = zero, zero, zero
            a_caw, a_cbw = [zero] * KA, [zero] * KB
            gamma, beta = lng_ref[:, cs], lnb_ref[:, cs]
            for r0, nr in _row_chunks(tm):
                rows = pl.ds(r0, nr)
                y, rstd = _group_norm(u1_ref[rows, cs])
                z = y * gamma + beta
                dz = du[rows, cs].astype(F32) * _silu_grad(z, _sigmoid(z))
                a_lng = a_lng + _fold_rows(dz * y)
                a_lnb = a_lnb + _fold_rows(dz)
                dy = dz * gamma
                du1 = rstd * (dy - jnp.mean(dy, axis=-1, keepdims=True)
                              - y * jnp.mean(dy * y, axis=-1, keepdims=True))
                du1_ref[rows, cs] = du1
                a_cab = a_cab + _fold_rows(du1)
                a_caw = _conv_bwd_weight(a_caw, du1, bufa, KA, r0, nr, cs)

                dvv = dv[rows, cs].astype(F32)
                dbg_ref[rows, cs] = (dvv * _conv_fwd(bufb, cbw_ref, KB, r0, nr, cs)).astype(BF16)
                dcv = dvv * bg[rows, cs].astype(F32)
                dcv_ref[rows, cs] = dcv
                a_cbw = _conv_bwd_weight(a_cbw, dcv, bufb, KB, r0, nr, cs)
            _add_row(dlng_ref, 0, cs, a_lng)
            _add_row(dlnb_ref, 0, cs, a_lnb)
            _add_row(dcab_ref, 0, cs, a_cab)
            for k in range(KA):
                _add_row(dcaw_ref, k, cs, a_caw[k])
            for k in range(KB):
                _add_row(dcbw_ref, k, cs, a_cbw[k])

    in_specs = [sec(0), sec(1), tile, sec(0), sec(1), sec(2), sec(3), sec(4),
                sec_prev(0), sec_prev(1), sec_prev(3), sec_prev(4), chan(1), chan(1), chan(KB)]
    operands = [dmix, dmix, u1] + [proj] * 9 + [lng, lnb, cbw]
    return _call(body, "mixer_bwd1", (nc, T // tm), in_specs,
                 (tile, tile, tile, chan(KA), chan(1), chan(1), chan(1), chan(KB)),
                 (jax.ShapeDtypeStruct((T, CW), F32), jax.ShapeDtypeStruct((T, CW), F32),
                  jax.ShapeDtypeStruct((T, CW), BF16), jax.ShapeDtypeStruct((KA, CW), F32),
                  jax.ShapeDtypeStruct((1, CW), F32), jax.ShapeDtypeStruct((1, CW), F32),
                  jax.ShapeDtypeStruct((1, CW), F32), jax.ShapeDtypeStruct((KB, CW), F32)), operands,
                 scratch=[pltpu.VMEM((HALO + tm, tc), F32), pltpu.VMEM((HALO + tm, tc), F32)],
                 sem=("parallel", "arbitrary"))


def _mixer_bwd2(du1, dcv, proj, caw, cbw, T, CW):
    KA, KB = caw.shape[0], cbw.shape[0]
    tm, tc, hb, nc = _mixer_tiles(T, CW)
    n_i = T // tm

    def sec(s):
        return pl.BlockSpec((tm, tc), lambda i, c: (i, s * nc + c))

    def chan(rows):
        return pl.BlockSpec((rows, tc), lambda i, c: (0, c))

    tile = pl.BlockSpec((tm, tc), lambda i, c: (i, c))
    nxt = pl.BlockSpec((HALO, tc), lambda i, c: (jnp.minimum((i + 1) * hb, n_i * hb - 1), c))

    def body(du1_ref, du1n, dcv_ref, dcvn, av, ag, cg, bh, caw_ref, cbw_ref, dav, dag, dcg, dbh, bufa, bufb):
        last = pl.program_id(0) == n_i - 1
        bufa[pl.ds(0, tm), :] = du1_ref[...]
        bufa[pl.ds(tm, HALO), :] = jnp.where(last, 0.0, du1n[...])
        bufb[pl.ds(0, tm), :] = dcv_ref[...]
        bufb[pl.ds(tm, HALO), :] = jnp.where(last, 0.0, dcvn[...])
        for cs in _col_chunks(tc):
            for r0, nr in _row_chunks(tm):
                rows = pl.ds(r0, nr)
                du0 = _conv_bwd_input(bufa, caw_ref, KA, r0, nr, cs)
                sig = _sigmoid(ag[rows, cs].astype(F32))
                dav[rows, cs] = (du0 * sig).astype(BF16)
                dag[rows, cs] = (du0 * av[rows, cs].astype(F32) * (sig * (1.0 - sig))).astype(BF16)
                dch = _conv_bwd_input(bufb, cbw_ref, KB, r0, nr, cs)
                dcg[rows, cs] = (dch * bh[rows, cs].astype(F32)).astype(BF16)
                dbh[rows, cs] = (dch * cg[rows, cs].astype(F32)).astype(BF16)

    in_specs = [tile, nxt, tile, nxt, sec(0), sec(1), sec(3), sec(4), chan(KA), chan(KB)]
    operands = [du1, du1, dcv, dcv, proj, proj, proj, proj, caw, cbw]
    out = jax.ShapeDtypeStruct((T, CW), BF16)
    return _call(body, "mixer_bwd2", (n_i, nc), in_specs, (tile, tile, tile, tile), (out, out, out, out),
                 operands, scratch=[pltpu.VMEM((HALO + tm, tc), F32), pltpu.VMEM((HALO + tm, tc), F32)],
                 sem=("parallel", "parallel"))


def _ffn_tiles(T):
    tm = _tile(T, 512, HALO)
    return tm, tm // HALO, T // tm


def _ffn_act_fwd(gpre, up, cfw):
    nb, T, F = gpre.shape
    KF = cfw.shape[1]
    tm, hb, n_i = _ffn_tiles(T)
    tile = pl.BlockSpec((None, tm, F), lambda b, i: (b, i, 0))
    prev = pl.BlockSpec((None, HALO, F), lambda b, i: (b, jnp.maximum(i * hb - 1, 0), 0))
    wspec = pl.BlockSpec((None, KF, F), lambda b, i: (b, 0, 0))

    def body(g_ref, gh_ref, up_ref, w_ref, f_ref, buf):
        buf[pl.ds(HALO, tm), :] = g_ref[...].astype(F32)
        buf[pl.ds(0, HALO), :] = jnp.where(pl.program_id(1) == 0, 0.0, gh_ref[...].astype(F32))
        for cs in _col_chunks(F):
            for r0, nr in _row_chunks(tm):
                rows = pl.ds(r0, nr)
                g = _conv_fwd(buf, w_ref, KF, r0, nr, cs)
                f_ref[rows, cs] = (g * _sigmoid(g) * up_ref[rows, cs].astype(F32)).astype(BF16)

    return _call(body, "ffn_act_fwd", (nb, n_i), [tile, prev, tile, wspec], tile,
                 jax.ShapeDtypeStruct((nb, T, F), BF16), [gpre, gpre, up, cfw],
                 scratch=[pltpu.VMEM((HALO + tm, F), F32)], sem=("parallel", "parallel"))


def _ffn_act_bwd1(df, gpre, up, cfw):
    nb, T, F = gpre.shape
    KF = cfw.shape[1]
    tm, hb, n_i = _ffn_tiles(T)
    tile = pl.BlockSpec((None, tm, F), lambda b, i: (b, i, 0))
    prev = pl.BlockSpec((None, HALO, F), lambda b, i: (b, jnp.maximum(i * hb - 1, 0), 0))
    wspec = pl.BlockSpec((None, KF, F), lambda b, i: (b, 0, 0))

    def body(df_ref, g_ref, gh_ref, up_ref, w_ref, dg_ref, dup_ref, dw_ref, buf):
        first = pl.program_id(1) == 0

        @pl.when(first)
        def _():
            dw_ref[...] = jnp.zeros_like(dw_ref)

        buf[pl.ds(HALO, tm), :] = g_ref[...].astype(F32)
        buf[pl.ds(0, HALO), :] = jnp.where(first, 0.0, gh_ref[...].astype(F32))
        for cs in _col_chunks(F):
            accs = [jnp.zeros((SUBLANES, cs.stop - cs.start), F32)] * KF
            for r0, nr in _row_chunks(tm):
                rows = pl.ds(r0, nr)
                g = _conv_fwd(buf, w_ref, KF, r0, nr, cs)
                sig = _sigmoid(g)
                dfv = df_ref[rows, cs].astype(F32)
                dup_ref[rows, cs] = (dfv * (g * sig)).astype(BF16)
                dg = dfv * up_ref[rows, cs].astype(F32) * _silu_grad(g, sig)
                dg_ref[rows, cs] = dg.astype(BF16)
                accs = _conv_bwd_weight(accs, dg, buf, KF, r0, nr, cs)
            for k in range(KF):
                _add_row(dw_ref, k, cs, accs[k])

    return _call(body, "ffn_act_bwd1", (nb, n_i), [tile, tile, prev, tile, wspec], (tile, tile, wspec),
                 (jax.ShapeDtypeStruct((nb, T, F), BF16), jax.ShapeDtypeStruct((nb, T, F), BF16),
                  jax.ShapeDtypeStruct((nb, KF, F), F32)), [df, gpre, gpre, up, cfw],
                 scratch=[pltpu.VMEM((HALO + tm, F), F32)], sem=("parallel", "arbitrary"))


def _ffn_act_bwd2(dg, cfw):
    nb, T, F = dg.shape
    KF = cfw.shape[1]
    tm, hb, n_i = _ffn_tiles(T)
    tile = pl.BlockSpec((None, tm, F), lambda b, i: (b, i, 0))
    nxt = pl.BlockSpec((None, HALO, F), lambda b, i: (b, jnp.minimum((i + 1) * hb, n_i * hb - 1), 0))
    wspec = pl.BlockSpec((None, KF, F), lambda b, i: (b, 0, 0))

    def body(dg_ref, dgn_ref, w_ref, o_ref, buf):
        buf[pl.ds(0, tm), :] = dg_ref[...].astype(F32)
        buf[pl.ds(tm, HALO), :] = jnp.where(pl.program_id(1) == n_i - 1, 0.0, dgn_ref[...].astype(F32))
        for cs in _col_chunks(F):
            for r0, nr in _row_chunks(tm):
                o_ref[pl.ds(r0, nr), cs] = _conv_bwd_input(buf, w_ref, KF, r0, nr, cs).astype(BF16)

    return _call(body, "ffn_act_bwd2", (nb, n_i), [tile, nxt, wspec], tile,
                 jax.ShapeDtypeStruct((nb, T, F), BF16), [dg, dg, cfw],
                 scratch=[pltpu.VMEM((HALO + tm, F), F32)], sem=("parallel", "parallel"))


def _softmax_rows(s):
    e = jnp.exp(s - jnp.max(s, axis=-1, keepdims=True))
    return e / jnp.sum(e, axis=-1, keepdims=True)


def _attn_fwd(q, k, v):
    T, D = q.shape
    Mm = k.shape[0]
    hd = D // N_XATTN_HEADS
    scale = hd ** -0.5
    tm = _tile(T, 256, 16)

    def body(q_ref, k_ref, v_ref, o_ref):
        for h in range(N_XATTN_HEADS):
            sl = slice(h * hd, (h + 1) * hd)
            s = lax.dot_general(q_ref[:, sl], k_ref[:, sl], _DOT_DIMS["nt"], preferred_element_type=F32) * scale
            p = _softmax_rows(s).astype(BF16)
            o_ref[:, sl] = jnp.dot(p, v_ref[:, sl], preferred_element_type=F32).astype(BF16)

    row = pl.BlockSpec((tm, D), lambda i: (i, 0))
    full = pl.BlockSpec((Mm, D), lambda i: (0, 0))
    return _call(body, "attn_fwd", (T // tm,), [row, full, full], row, jax.ShapeDtypeStruct((T, D), BF16),
                 [q, k, v], sem=("parallel",))


def _attn_bwd(q, k, v, do):
    T, D = q.shape
    Mm = k.shape[0]
    hd = D // N_XATTN_HEADS
    scale = hd ** -0.5
    tm = _tile(T, 256, 16)
    n_i = T // tm

    def body(q_ref, do_ref, k_ref, v_ref, dq_ref, dk_ref, dv_ref, dk_acc, dv_acc):
        @pl.when(pl.program_id(0) == 0)
        def _():
            dk_acc[...] = jnp.zeros_like(dk_acc)
            dv_acc[...] = jnp.zeros_like(dv_acc)

        for h in range(N_XATTN_HEADS):
            sl = slice(h * hd, (h + 1) * hd)
            qh, kh, doh = q_ref[:, sl], k_ref[:, sl], do_ref[:, sl]
            s = lax.dot_general(qh, kh, _DOT_DIMS["nt"], preferred_element_type=F32) * scale
            p = _softmax_rows(s)
            dv_acc[:, sl] += lax.dot_general(p.astype(BF16), doh, _DOT_DIMS["tn"], preferred_element_type=F32)
            dp = lax.dot_general(doh, v_ref[:, sl], _DOT_DIMS["nt"], preferred_element_type=F32)
            ds = (p * (dp - jnp.sum(dp * p, axis=-1, keepdims=True)) * scale).astype(BF16)
            dq_ref[:, sl] = jnp.dot(ds, kh, preferred_element_type=F32).astype(BF16)
            dk_acc[:, sl] += lax.dot_general(ds, qh, _DOT_DIMS["tn"], preferred_element_type=F32)

        @pl.when(pl.program_id(0) == n_i - 1)
        def _():
            dk_ref[...] = dk_acc[...].astype(BF16)
            dv_ref[...] = dv_acc[...].astype(BF16)

    row = pl.BlockSpec((tm, D), lambda i: (i, 0))
    full = pl.BlockSpec((Mm, D), lambda i: (0, 0))
    return _call(body, "attn_bwd", (n_i,), [row, row, full, full], (row, full, full),
                 (jax.ShapeDtypeStruct((T, D), BF16), jax.ShapeDtypeStruct((Mm, D), BF16),
                  jax.ShapeDtypeStruct((Mm, D), BF16)), [q, do, k, v],
                 scratch=[pltpu.VMEM((Mm, D), F32), pltpu.VMEM((Mm, D), F32)])


def _position():
    x, y, c = lax.axis_index("x"), lax.axis_index("y"), lax.axis_index("c")
    return x, y, c


def _peer(pos, k):
    x, y, c = pos
    return (1 - x if k & 4 else x, 1 - y if k & 2 else y, 1 - c if k & 1 else c)


def _index(pos):
    x, y, c = pos
    return 4 * x + 2 * y + c


def _sequencer_kernel(body, name, collective_id, out_type, operands):
    return pl.kernel(
        body, name=name, out_type=out_type,
        mesh=plsc.ScalarSubcoreMesh(axis_name="sequencer", num_cores=1),
        scratch_types=[pltpu.SemaphoreType.DMA, pltpu.SemaphoreType.DMA((7,)), pltpu.SemaphoreType.DMA],
        compiler_params=pltpu.CompilerParams(collective_id=collective_id),
    )(*operands)


def _handshake(peers):
    barrier = pltpu.get_barrier_semaphore()
    for peer in peers:
        pl.semaphore_signal(barrier, inc=1, device_id=peer, device_id_type=MESH)
    pl.semaphore_wait(barrier, len(peers))


def _sequencer_all_gather(name, collective_id, shards):
    n = len(shards)

    def body(*refs):
        x_refs, out_refs = refs[:n], refs[n:2 * n]
        send_sem, recv_sems, local_sem = refs[2 * n:]
        me = _position()
        x, y, c = me
        sibling = _peer(me, 1)
        first = (x + (1 - c) - 2 * x * (1 - c), y + c - 2 * y * c, c)
        second = (x + c - 2 * x * c, y + (1 - c) - 2 * y * (1 - c), c)
        diagonal = _peer(me, 6)
        _handshake([sibling, first, second])

        def copy(a, k, block, to, own=False):
            dst = out_refs[a].at[_index(block)]
            return pltpu.make_async_remote_copy(
                src_ref=x_refs[a] if own else dst, dst_ref=dst, send_sem=send_sem, recv_sem=recv_sems.at[k],
                device_id=to, device_id_type=MESH)

        local = [pltpu.make_async_copy(x_refs[a], out_refs[a].at[_index(me)], local_sem) for a in range(n)]
        started = [copy(a, 1 + j, me, peer, own=True) for a in range(n) for j, peer in enumerate((first, second))]
        started += [copy(a, 0, me, sibling, own=True) for a in range(n)]
        for cp in started + local:
            cp.start()
        for k, origin in ((1, first), (2, second), (3, diagonal)):
            for a in range(n):
                copy(a, k, origin, me).wait_recv()
            passed = [copy(a, 3 + k, origin, sibling) for a in range(n)]
            if k == 1:
                passed = [copy(a, 3, origin, second) for a in range(n)] + passed
            for cp in passed:
                cp.start()
            started += passed
        for k in (0, 4, 5, 6):
            for a in range(n):
                copy(a, k, sibling, me).wait_recv()
        for cp in started:
            cp.wait_send()
        for cp in local:
            cp.wait()

    return _sequencer_kernel(body, name, collective_id,
                             [jax.ShapeDtypeStruct((N_DEV,) + s.shape, s.dtype) for s in shards], shards)


def _chip_index(pos):
    return 2 * pos[0] + pos[1]


def _sequencer_to_sibling(name, collective_id, parts):
    n = len(parts)

    def body(*refs):
        p_refs, out_refs = refs[:n], refs[n:2 * n]
        send_sem, recv_sems, _ = refs[2 * n:]
        me = _position()
        sibling = _peer(me, 1)
        _handshake([sibling])
        copies = [pltpu.make_async_remote_copy(
            src_ref=p_refs[a].at[2 * q + sibling[2]], dst_ref=out_refs[a].at[q], send_sem=send_sem,
            recv_sem=recv_sems.at[0], device_id=sibling, device_id_type=MESH)
            for a in range(n) for q in range(N_DEV // 2)]
        for cp in copies:
            cp.start()
        for cp in copies:
            cp.wait_recv()
        for cp in copies:
            cp.wait_send()

    return _sequencer_kernel(body, name, collective_id,
                             [jax.ShapeDtypeStruct((N_DEV // 2,) + p.shape[1:], p.dtype) for p in parts], parts)


def _sequencer_to_chips(name, collective_id, sums):
    n = len(sums)

    def body(*refs):
        s_refs, out_refs = refs[:n], refs[n:2 * n]
        send_sem, recv_sems, local_sem = refs[2 * n:]
        me = _position()
        my_chip = _chip_index(me)
        peers = [_peer(me, 4), _peer(me, 2), _peer(me, 6)]
        _handshake(peers)
        local = [pltpu.make_async_copy(s_refs[a].at[my_chip], out_refs[a].at[my_chip], local_sem) for a in range(n)]
        sends = [pltpu.make_async_remote_copy(
            src_ref=s_refs[a].at[_chip_index(peer)], dst_ref=out_refs[a].at[my_chip], send_sem=send_sem,
            recv_sem=recv_sems.at[1 + j], device_id=peer, device_id_type=MESH)
            for a in range(n) for j, peer in enumerate(peers)]
        for cp in sends + local:
            cp.start()
        for j, peer in enumerate(peers):
            for a in range(n):
                pltpu.make_async_remote_copy(
                    src_ref=s_refs[a].at[my_chip], dst_ref=out_refs[a].at[_chip_index(peer)], send_sem=send_sem,
                    recv_sem=recv_sems.at[1 + j], device_id=peer, device_id_type=MESH).wait_recv()
        for cp in sends:
            cp.wait_send()
        for cp in local:
            cp.wait()

    return _sequencer_kernel(body, name, collective_id,
                             [jax.ShapeDtypeStruct(s.shape, s.dtype) for s in sums], sums)


def _chip_sum(name, parts, got, after=()):
    _, R, C = parts.shape
    tr = _tile(R, max(8, (3 << 20) // C), 16)
    n_after = len(after)
    limit = 6 * _nbytes((tr, C), parts.dtype) + VMEM_TEMP_ALLOWANCE

    def body(c_ref, p_ref, g_ref, *rest):
        o_ref = rest[n_after]
        o_ref[...] = (p_ref[...].astype(F32) + g_ref[...].astype(F32)).astype(o_ref.dtype)

    blk = pl.BlockSpec((None, tr, C), lambda q, i, c_ref: (q, i, 0))
    mine = pl.BlockSpec((None, tr, C), lambda q, i, c_ref: (2 * q + c_ref[0], i, 0))
    core = lax.axis_index("c").astype(jnp.int32).reshape(1)
    parts, got = [pltpu.with_memory_space_constraint(o, pltpu.HBM) for o in (parts, got)]
    after = [pltpu.with_memory_space_constraint(o, pltpu.HBM) for o in after]
    return pl.pallas_call(
        body, name=name, out_shape=pltpu.HBM((N_DEV // 2, R, C), parts.dtype),
        grid_spec=pltpu.PrefetchScalarGridSpec(
            num_scalar_prefetch=1, grid=(N_DEV // 2, R // tr),
            in_specs=[mine, blk] + [pl.BlockSpec(memory_space=pl.ANY)] * n_after, out_specs=blk),
        compiler_params=pltpu.CompilerParams(dimension_semantics=("parallel", "parallel"),
                                             vmem_limit_bytes=int(limit)),
    )(core, parts, got, *after)


def _all_reduce_rows(name, v):
    R, C = v.shape

    def body(v_ref, out_ref, gath, send_sems, recv_sems):
        me = _position()
        gath[_index(me)] = v_ref[...]
        sends = []
        for k in range(1, N_DEV):
            peer = _peer(me, k)
            sends.append(pltpu.make_async_remote_copy(
                src_ref=v_ref, dst_ref=gath.at[_index(me)], send_sem=send_sems.at[k - 1],
                recv_sem=recv_sems.at[k - 1], device_id=peer, device_id_type=MESH))
        for cp in sends:
            cp.start()
        for k in range(1, N_DEV):
            peer = _peer(me, k)
            pltpu.make_async_remote_copy(
                src_ref=v_ref, dst_ref=gath.at[_index(peer)], send_sem=send_sems.at[k - 1],
                recv_sem=recv_sems.at[k - 1], device_id=peer, device_id_type=MESH).wait_recv()
        for cp in sends:
            cp.wait_send()
        tot = gath[0]
        for s in range(1, N_DEV):
            tot = tot + gath[s]
        out_ref[...] = tot

    return pl.pallas_call(
        body, name=name, out_shape=jax.ShapeDtypeStruct((R, C), F32),
        in_specs=[VMEM_SPEC], out_specs=VMEM_SPEC,
        scratch_shapes=[pltpu.VMEM((N_DEV, R, C), F32), pltpu.SemaphoreType.DMA((7,)),
                        pltpu.SemaphoreType.DMA((7,))],
    )(v)


def _adamw_math(g, w, m, v):
    m = ADAM_B1 * m + (1.0 - ADAM_B1) * g
    v = ADAM_B2 * v + (1.0 - ADAM_B2) * (g * g)
    m_hat = m / (1.0 - ADAM_B1 ** ADAM_STEP)
    v_hat = v / (1.0 - ADAM_B2 ** ADAM_STEP)
    delta = -ADAM_LR * (m_hat / (jnp.sqrt(v_hat) + ADAM_EPS) + ADAM_WD * w)
    return delta, m, v


def _adamw_tile(n):
    def body(p_ref, w_ref, m_ref, v_ref, g_ref, d_ref, nm_ref, nv_ref):
        g = p_ref[0].astype(F32)
        for s in range(1, n):
            g = g + p_ref[s].astype(F32)
        g_ref[...] = g
        d_ref[...], nm_ref[...], nv_ref[...] = _adamw_math(g, w_ref[...], m_ref[...], v_ref[...])
    return body


class _SideJob(NamedTuple):
    operands: list
    in_blocks: list
    out_blocks: list
    out_shapes: list
    body: object


def _adamw_job(parts, w, m, v):
    n, R, C = parts.shape
    lead = w.ndim - 2
    if R % (SIDE_TILES * 16) == 0:
        tile, index = (R // SIDE_TILES, C), lambda t: (t, 0)
    elif C % (SIDE_TILES * 128) == 0:
        tile, index = (R, C // SIDE_TILES), lambda t: (0, t)
    else:
        return None
    p_blk = ((n,) + tile, lambda t: (0,) + index(t))
    w_blk = ((None,) * lead + tile, lambda t: (0,) * lead + index(t))
    return _SideJob([parts, w, m, v], [p_blk, w_blk, w_blk, w_blk], [w_blk] * 4,
                    [jax.ShapeDtypeStruct(w.shape, F32)] * 4, _adamw_tile(n))


def _adamw(name, parts, w, m, v, after=()):
    n, R, C = parts.shape
    tr = _tile(R, max(8, (1 << 18) // C), 16)
    blk = _rows_spec(w, tr)
    out = jax.ShapeDtypeStruct(w.shape, F32)
    return _call(_adamw_tile(n), name, (R // tr,), [pl.BlockSpec((n, tr, C), lambda i: (0, i, 0)), blk, blk, blk],
                 (blk, blk, blk, blk), (out, out, out, out), [parts, w, m, v], sem=("parallel",), after=after)


def kernel(x, mem, g_mix, w_in, conv_a_w, conv_a_b, ln_a_g, ln_a_b, conv_b_w, w_out, g_xattn, g_mem, w_q, w_k, w_v, w_o, g_ffn, w_gate, w_up, conv_f_w, w_down, g_final, loss_target, m_g_mix, m_w_in, m_conv_a_w, m_conv_a_b, m_ln_a_g, m_ln_a_b, m_conv_b_w, m_w_out, m_g_xattn, m_g_mem, m_w_q, m_w_k, m_w_v, m_w_o, m_g_ffn, m_w_gate, m_w_up, m_conv_f_w, m_w_down, m_g_final, v_g_mix, v_w_in, v_conv_a_w, v_conv_a_b, v_ln_a_g, v_ln_a_b, v_conv_b_w, v_w_out, v_g_xattn, v_g_mem, v_w_q, v_w_k, v_w_v, v_w_o, v_g_ffn, v_w_gate, v_w_up, v_conv_f_w, v_w_down, v_g_final):
    T, D = x.shape[1], x.shape[2]
    Mm = mem.shape[1]
    CW = conv_a_b.shape[1]
    INB = w_in.shape[2]
    FB = w_gate.shape[2]
    KA, KB, KF = conv_a_w.shape[1], conv_b_w.shape[1], conv_f_w.shape[1]
    DB = D // N_DEV
    assert 5 * CW == N_DEV * INB and 2 * CW == D

    x2, mem2, tgt = x[0], mem[0], loss_target[0]
    g_mem2, g_final2 = g_mem.reshape(1, D), g_final.reshape(1, D)

    def bf16(name, w):
        return _cast_bf16("cast_" + name, w)

    Win, caw, cbw = _sequencer_all_gather(
        "ag_in", AG_ID, [bf16("w_in", w_in), conv_a_w[0], conv_b_w[0]])
    Wout, = _sequencer_all_gather("ag_out", AG_ID, [bf16("w_out", w_out)])
    Wq, Wk, Wv, Wo = _sequencer_all_gather(
        "ag_attn", AG_ID, [bf16("w_q", w_q), bf16("w_k", w_k), bf16("w_v", w_v), bf16("w_o", w_o)])
    def transposed(w):
        return jnp.transpose(w[0])

    gate_t = [transposed(a) for a in (w_gate, m_w_gate, v_w_gate)]
    up_t = [transposed(a) for a in (w_up, m_w_up, v_w_up)]
    WgateT, cfw = _sequencer_all_gather("ag_gate", AG_ID, [bf16("w_gate", gate_t[0]), conv_f_w[0]])
    WupT, = _sequencer_all_gather("ag_up", AG_ID, [bf16("w_up", up_t[0])])
    Wdown, = _sequencer_all_gather("ag_down", AG_ID, [bf16("w_down", w_down)])
    Wout, Wq, Wk, Wv, Wo = [w.reshape(D, D) for w in (Wout, Wq, Wk, Wv, Wo)]
    caw = jnp.transpose(caw, (1, 0, 2)).reshape(KA, CW)
    cbw = jnp.transpose(cbw, (1, 0, 2)).reshape(KB, CW)

    xn1 = _rms_fwd("rms_mix", x2, g_mix)
    proj = _matmul("mm_proj", "nn", xn1, Win, M=T, N=INB, K=D, nb=N_DEV, b_lay="blk", o_lay="col", tn=INB,
                   out_dtype=BF16)
    mix, u1 = _mixer_fwd(proj, caw, conv_a_b, ln_a_g, ln_a_b, cbw, T, CW)
    h1 = _matmul("mm_h1", "nn", mix, Wout.reshape(2, CW, D), M=T, N=D, K=CW, nb=2, a_lay="blk", b_lay="blk",
                 red_block=True, res=x2)
    xn2 = _rms_fwd("rms_xattn", h1, g_xattn)
    q = _matmul("mm_q", "nn", xn2, Wq, M=T, N=D, K=D, out_dtype=BF16)
    memn = _rms_fwd("rms_mem", mem2, g_mem2, after=[q])
    kk = _matmul("mm_k", "nn", memn, Wk, M=Mm, N=D, K=D, out_dtype=BF16)
    vv = _matmul("mm_v", "nn", memn, Wv, M=Mm, N=D, K=D, out_dtype=BF16)
    o = _attn_fwd(q, kk, vv)
    h2 = _matmul("mm_h2", "nn", o, Wo, M=T, N=D, K=D, res=h1)
    xn3 = _rms_fwd("rms_ffn", h2, g_ffn)
    gpre = _matmul("mm_gate", "nt", xn3, WgateT, M=T, N=FB, K=D, nb=N_DEV, b_lay="blk", o_lay="blk", tn=FB,
                   out_dtype=BF16)
    up = _matmul("mm_up", "nt", xn3, WupT, M=T, N=FB, K=D, nb=N_DEV, b_lay="blk", o_lay="blk", tn=FB,
                 out_dtype=BF16)
    f = _ffn_act_fwd(gpre, up, cfw)
    h3 = _matmul("mm_h3", "nn", f, Wdown, M=T, N=D, K=FB, nb=N_DEV, a_lay="blk", b_lay="blk", red_block=True,
                 group=2, res=h2)
    dh3, dh3b, loss_part, dg_final = _loss_head(h3, tgt, g_final2)

    wmv = {"w_in": (w_in, m_w_in, v_w_in), "conv_a_w": (conv_a_w, m_conv_a_w, v_conv_a_w),
           "conv_b_w": (conv_b_w, m_conv_b_w, v_conv_b_w), "w_out": (w_out, m_w_out, v_w_out),
           "w_q": (w_q, m_w_q, v_w_q), "w_k": (w_k, m_w_k, v_w_k), "w_v": (w_v, m_w_v, v_w_v),
           "w_o": (w_o, m_w_o, v_w_o), "w_gate": gate_t, "w_up": up_t,
           "conv_f_w": (conv_f_w, m_conv_f_w, v_conv_f_w), "w_down": (w_down, m_w_down, v_w_down)}
    res = {}
    pending = []

    def mm(*args, after=(), carry=(), **kwargs):
        behind = list(after) + pending
        pending.clear()
        jobs, alone = [], []
        for names, got in carry:
            for n, g in zip(names, got):
                job = _adamw_job(g, *wmv[n])
                (alone if job is None else jobs).append((n, g, job))
        results = None
        if jobs:
            out, results = _matmul(*args, after=behind, side=[job for _, _, job in jobs], **kwargs)
        else:
            out = _matmul(*args, after=behind, **kwargs)
        if results is None:
            alone, jobs = alone + jobs, []
        for (n, _, _), r in zip(jobs, results or []):
            res[n] = r
            pending.append(r[0])
        for n, g, _ in alone:
            res[n] = _adamw("adamw_" + n, g, *wmv[n], after=[out])
            pending.append(res[n][0])
        return out

    def to_sibling(tag, named_parts):
        got = _sequencer_to_sibling("rs1_" + tag, SIBLING_ID, [p for _, p in named_parts])
        return named_parts, got

    def to_chips(tag, stage1, after):
        named_parts, got = stage1
        sums = [_chip_sum("sum_" + n, p, g, after=after) for (n, p), g in zip(named_parts, got)]
        pending.extend(sums)
        return [n for n, _ in named_parts], _sequencer_to_chips("rs2_" + tag, CHIPS_ID, sums)

    def finish(stage2, after):
        names, got = stage2
        for n, g in zip(names, got):
            w, m, v = wmv[n]
            res[n] = _adamw("adamw_" + n, g, w, m, v, after=after)
            pending.append(res[n][0])

    def row_blocks(dw):
        return dw.reshape(N_DEV, DB, D)

    def conv_blocks(dw, K):
        return jnp.transpose(dw.reshape(K, N_DEV, CW // N_DEV), (1, 0, 2))

    dWdown = mm("mm_dw_down", "tn", f, dh3b, M=FB, N=D, K=T, nb=N_DEV, a_lay="blk", o_lay="blk",
                     out_dtype=BF16, tm=FB)
    s_down = to_sibling("down", [("w_down", dWdown)])
    df = mm("mm_df", "nt", dh3b, Wdown, M=T, N=FB, K=D, nb=N_DEV, b_lay="blk", o_lay="blk", tn=FB, out_dtype=BF16,
                 after=[dWdown])
    dg, dup, dcfw = _ffn_act_bwd1(df, gpre, up, cfw)
    dgpre = _ffn_act_bwd2(dg, cfw)
    c_down = to_chips("down", s_down, after=[dgpre])
    dWgate = mm("mm_dw_gate", "tn", dgpre, xn3, M=FB, N=D, K=T, nb=N_DEV, a_lay="blk", o_lay="blk",
                     out_dtype=BF16, tm=FB)
    s_gate = to_sibling("gate", [("w_gate", dWgate), ("conv_f_w", dcfw)])
    dWup = mm("mm_dw_up", "tn", dup, xn3, M=FB, N=D, K=T, nb=N_DEV, a_lay="blk", o_lay="blk",
                   out_dtype=BF16, tm=FB, after=[dWgate])
    s_up = to_sibling("up", [("w_up", dWup)])
    dxn3 = mm("mm_dxn3_gate", "nn", dgpre, WgateT, M=T, N=D, K=FB, nb=N_DEV, a_lay="blk", b_lay="blk",
                   red_block=True, group=2, out_dtype=BF16, after=[dWup], carry=[c_down])
    c_gate = to_chips("gate", s_gate, after=[dxn3])
    dxn3 = mm("mm_dxn3_up", "nn", dup, WupT, M=T, N=D, K=FB, nb=N_DEV, a_lay="blk", b_lay="blk",
                   red_block=True, group=2, out_dtype=BF16, res=dxn3)
    c_up = to_chips("up", s_up, after=[dxn3])
    dh2, dh2b, dg_ffn = _rms_bwd("rms_bwd_ffn", dxn3, h2, g_ffn, dh3)

    dWo = mm("mm_dw_o", "tn", o, dh2b, M=D, N=D, K=T, out_dtype=BF16)
    s_o = to_sibling("o", [("w_o", row_blocks(dWo))])
    do = mm("mm_do", "nt", dh2b, Wo, M=T, N=D, K=D, out_dtype=BF16, after=[dWo])
    dq, dk, dv = _attn_bwd(q, kk, vv, do)
    dWq = mm("mm_dw_q", "tn", xn2, dq, M=D, N=D, K=T, out_dtype=BF16, carry=[c_gate])
    s_q = to_sibling("q", [("w_q", row_blocks(dWq))])
    dxn2 = mm("mm_dxn2", "nt", dq, Wq, M=T, N=D, K=D, out_dtype=BF16, after=[dWq], carry=[c_up])
    c_o = to_chips("o", s_o, after=[dxn2])
    c_q = to_chips("q", s_q, after=[dxn2])
    dh1, dh1b, dg_xattn = _rms_bwd("rms_bwd_xattn", dxn2, h1, g_xattn, dh2)
    dWk = mm("mm_dw_k", "tn", memn, dk, M=D, N=D, K=Mm, out_dtype=BF16, after=[dh1b])
    dWv = mm("mm_dw_v", "tn", memn, dv, M=D, N=D, K=Mm, out_dtype=BF16, after=[dh1b])
    s_kv = to_sibling("kv", [("w_k", row_blocks(dWk)), ("w_v", row_blocks(dWv))])
    dmemn = mm("mm_dmem_k", "nt", dk, Wk, M=Mm, N=D, K=D, after=[dWk, dWv])
    dmemn = mm("mm_dmem_v", "nt", dv, Wv, M=Mm, N=D, K=D, res=dmemn)
    dg_mem = _rms_bwd("rms_bwd_mem", dmemn, mem2, g_mem2)

    dWout = mm("mm_dw_out", "tn", mix, dh1b, M=CW, N=D, K=T, nb=2, a_lay="blk", o_lay="blk", out_dtype=BF16,
                    after=[dg_mem], carry=[c_o])
    s_out = to_sibling("out", [("w_out", row_blocks(dWout.reshape(D, D)))])
    dmix = mm("mm_dmix", "nt", dh1b, Wout, M=T, N=D, K=D, out_dtype=BF16, after=[dWout], carry=[c_q])
    c_kv = to_chips("kv", s_kv, after=[dmix])
    du1, dcv, dbg, dcaw, dcab, dlng, dlnb, dcbw = _mixer_bwd1(dmix, proj, u1, caw, ln_a_g, ln_a_b, cbw, T, CW)
    c_out = to_chips("out", s_out, after=[du1])
    dav, dag, dcg, dbh = _mixer_bwd2(du1, dcv, proj, caw, cbw, T, CW)
    dproj = jnp.concatenate([dav, dag, dbg, dcg, dbh], axis=1)
    dWin = mm("mm_dw_in", "tn", xn1, dproj, M=D, N=INB, K=T, nb=N_DEV, b_lay="col", o_lay="blk",
                   out_dtype=BF16, tm=512, tn=INB, carry=[c_kv])
    s_in = to_sibling("in", [("w_in", dWin), ("conv_a_w", conv_blocks(dcaw, KA)),
                             ("conv_b_w", conv_blocks(dcbw, KB))])
    finish(c_out, after=[dWin])
    c_in = to_chips("in", s_in, after=list(pending))
    dxn1 = mm("mm_dxn1", "nt", dproj, Win, M=T, N=D, K=INB, nb=N_DEV, a_lay="col", b_lay="blk",
                   red_block=True, group=2, out_dtype=BF16, after=[dWin])
    dx, _, dg_mix = _rms_bwd("rms_bwd_mix", dxn1, x2, g_mix, dh1, after=list(pending))

    def pair(a, b):
        return jnp.concatenate([a, b], axis=1)

    zeros_half = jnp.zeros((1, CW), F32)
    small_g = jnp.concatenate([
        dg_mix, pair(dcab, dlng), pair(dlnb, zeros_half), dg_xattn, dg_mem, dg_ffn, dg_final,
        jnp.broadcast_to(loss_part[:, :1], (1, D))], axis=0)
    small_sum = _all_reduce_rows("ar_small", small_g)
    loss = small_sum[7, 0]
    finish(c_in, after=[small_sum])

    def pack(a_mix, a_cab, a_lng, a_lnb, a_xattn, a_mem, a_ffn, a_final):
        return jnp.concatenate([a_mix, pair(a_cab, a_lng), pair(a_lnb, zeros_half), a_xattn, a_mem.reshape(1, D),
                                a_ffn, a_final.reshape(1, D), jnp.zeros((1, D), F32)], axis=0)

    small = _adamw("adamw_small", small_sum[None],
                   pack(g_mix, conv_a_b, ln_a_g, ln_a_b, g_xattn, g_mem, g_ffn, g_final),
                   pack(m_g_mix, m_conv_a_b, m_ln_a_g, m_ln_a_b, m_g_xattn, m_g_mem, m_g_ffn, m_g_final),
                   pack(v_g_mix, v_conv_a_b, v_ln_a_g, v_ln_a_b, v_g_xattn, v_g_mem, v_g_ffn, v_g_final))

    def unpack(a):
        return {"g_mix": a[0:1], "conv_a_b": a[1:2, :CW], "ln_a_g": a[1:2, CW:], "ln_a_b": a[2:3, :CW],
                "g_xattn": a[3:4], "g_mem": a[4], "g_ffn": a[5:6], "g_final": a[6]}

    small = [unpack(a) for a in small]
    order = ["g_mix", "w_in", "conv_a_w", "conv_a_b", "ln_a_g", "ln_a_b", "conv_b_w", "w_out", "g_xattn", "g_mem",
             "w_q", "w_k", "w_v", "w_o", "g_ffn", "w_gate", "w_up", "conv_f_w", "w_down", "g_final"]
    outs = [loss, dx[None]]
    for kind in range(4):
        for n in order:
            if n in ("w_gate", "w_up"):
                outs.append(jnp.transpose(res[n][kind])[None])
            else:
                outs.append(res[n][kind] if n in res else small[kind][n])
    return tuple(outs)
```

```python
import functools
from typing import NamedTuple

import jax
import jax.numpy as jnp
from jax import lax
from jax.experimental import pallas as pl
from jax.experimental.pallas import tpu as pltpu
from jax.experimental.pallas import tpu_sc as plsc

F32 = jnp.float32
BF16 = jnp.bfloat16

N_DEV = 8
EPS = 1e-6
GROUP_DIM = 128
N_XATTN_HEADS = 4
ADAM_LR = 0.001
ADAM_B1 = 0.9
ADAM_B2 = 0.999
ADAM_EPS = 1e-08
ADAM_WD = 0.01
ADAM_STEP = 10

AG_ID, SIBLING_ID, CHIPS_ID = 1, 2, 3

SIDE_TILES = 32
HALO = 32
VMEM_V7X_BYTES = 64 * 1024 * 1024
VMEM_TEMP_ALLOWANCE = 12 * 1024 * 1024

VMEM_SPEC = pl.BlockSpec(memory_space=pltpu.VMEM)
MESH = pl.DeviceIdType.MESH


def _tile(n, pref, align):
    if n <= pref:
        return n
    t = (pref // align) * align
    while t >= align:
        if n % t == 0:
            return t
        t -= align
    return n


def _nbytes(shape, dtype):
    n = 1
    for d in shape:
        if d is not None:
            n *= d
    return n * jnp.dtype(dtype).itemsize


def _call(body, name, grid, in_specs, out_specs, out_shape, operands, scratch=(), sem=None, after=()):
    outs = out_shape if isinstance(out_shape, (tuple, list)) else (out_shape,)
    ospecs = out_specs if isinstance(out_specs, (tuple, list)) else (out_specs,)
    est = 0
    for spec, arr in list(zip(in_specs, operands)) + list(zip(ospecs, outs)):
        est += 2 * _nbytes(spec.block_shape, arr.dtype)
    for s in scratch:
        if hasattr(s, "shape") and hasattr(s, "dtype"):
            est += _nbytes(s.shape, s.dtype)
    limit = min(est + VMEM_TEMP_ALLOWANCE, VMEM_V7X_BYTES - 4 * 1024 * 1024)
    if sem is None:
        sem = ("arbitrary",) * len(grid)
    n_in, n_after = len(operands), len(after)
    operands = [pltpu.with_memory_space_constraint(o, pltpu.HBM) for o in operands]
    after = [pltpu.with_memory_space_constraint(o, pltpu.HBM) for o in after]
    in_hbm = [pltpu.HBM(o.shape, o.dtype) for o in outs]
    out_shape = in_hbm if isinstance(out_shape, (tuple, list)) else in_hbm[0]

    def ordered_body(*refs):
        body(*refs[:n_in], *refs[n_in + n_after:])

    return pl.pallas_call(
        ordered_body if n_after else body, name=name, grid=grid,
        in_specs=list(in_specs) + [pl.BlockSpec(memory_space=pl.ANY)] * n_after,
        out_specs=out_specs, out_shape=out_shape, scratch_shapes=list(scratch),
        compiler_params=pltpu.CompilerParams(dimension_semantics=sem, vmem_limit_bytes=int(limit)),
    )(*operands, *after)


_DOT_DIMS = {"nn": (((1,), (0,)), ((), ())), "nt": (((1,), (1,)), ((), ())), "tn": (((0,), (0,)), ((), ()))}


def _operand_spec(layout, tr, tc, cols_per_block, pick, group=None):
    if layout == "plain":
        return pl.BlockSpec((tr, tc), lambda *g: pick(*g)[1:])
    if layout == "blk":
        return pl.BlockSpec((group, tr, tc), lambda *g: pick(*g))
    assert layout == "col"
    if group:
        assert tc == cols_per_block
        return pl.BlockSpec((tr, group * tc), lambda *g: (pick(*g)[1], pick(*g)[0]))
    per = cols_per_block // tc
    return pl.BlockSpec((tr, tc), lambda *g: (pick(*g)[1], pick(*g)[0] * per + pick(*g)[2]))


def _matmul(name, dims, a, b, *, M, N, K, nb=1, a_lay="plain", b_lay="plain", o_lay="plain",
            red_block=False, group=None, out_dtype=F32, res=None, tm=1024, tn=None, after=(), side=()):
    tm = _tile(M, tm, 128 if dims == "tn" else 16)
    tn = _tile(N, tn or (1024 if red_block else 512), 128)
    tk = K
    gi, gj, gk = M // tm, N // tn, K // tk
    if red_block:
        grid = (gi, gj, nb // (group or 1), gk)
        unpack = lambda i, j, bb, k: (bb, i, j, k)
        red_axes, sem = (2, 3), ("parallel", "parallel", "arbitrary", "arbitrary")
    else:
        grid = (nb, gi, gj, gk)
        unpack = lambda bb, i, j, k: (bb, i, j, k)
        red_axes, sem = (3,), ("parallel", "parallel", "parallel", "arbitrary")

    def picker(f):
        return lambda *g: f(*unpack(*g))

    if dims == "tn":
        a_spec = _operand_spec(a_lay, tk, tm, M, picker(lambda bb, i, j, k: (bb, k, i)), group)
    else:
        a_spec = _operand_spec(a_lay, tm, tk, K, picker(lambda bb, i, j, k: (bb, i, k)), group)
    if dims == "nt":
        b_spec = _operand_spec(b_lay, tn, tk, K, picker(lambda bb, i, j, k: (bb, j, k)), group)
    else:
        b_spec = _operand_spec(b_lay, tk, tn, N, picker(lambda bb, i, j, k: (bb, k, j)), group)
    o_spec = _operand_spec(o_lay, tm, tn, N, picker(lambda bb, i, j, k: (bb, i, j)))
    if o_lay == "plain":
        out_shape = jax.ShapeDtypeStruct((M, N), out_dtype)
    elif o_lay == "blk":
        out_shape = jax.ShapeDtypeStruct((nb, M, N), out_dtype)
    else:
        out_shape = jax.ShapeDtypeStruct((M, nb * N), out_dtype)
    n_red = [grid[ax] for ax in red_axes]
    has_res = res is not None
    one_step = all(n == 1 for n in n_red)

    def contract(a_ref, b_ref):
        if group:
            parts = [(a_ref[p] if a_lay == "blk" else a_ref[:, p * K:(p + 1) * K], b_ref[p]) for p in range(group)]
        else:
            parts = [(a_ref[...], b_ref[...])]
        r = None
        for a_part, b_part in parts:
            d = lax.dot_general(a_part, b_part, _DOT_DIMS[dims], preferred_element_type=F32)
            r = d if r is None else r + d
        return r

    def step_index(*g):
        s = g[0]
        for ax in range(1, len(grid)):
            s = s * grid[ax] + g[ax]
        return s

    def side_spec(block):
        shape, index = block
        return pl.BlockSpec(shape, lambda *g: index(jnp.minimum(step_index(*g), SIDE_TILES - 1)))

    asked_to_carry = bool(side)
    if functools.reduce(lambda p, q: p * q, grid) < SIDE_TILES:
        side = ()
    n_main_in = 3 if has_res else 2
    n_side_in = sum(len(job.operands) for job in side)
    n_side_out = sum(len(job.out_shapes) for job in side)

    def side_work(side_in, side_out):
        @pl.when(step_index(*[pl.program_id(ax) for ax in range(len(grid))]) < SIDE_TILES)
        def _():
            i0 = o0 = 0
            for job in side:
                n_i, n_o = len(job.operands), len(job.out_shapes)
                job.body(*side_in[i0:i0 + n_i], *side_out[o0:o0 + n_o])
                i0, o0 = i0 + n_i, o0 + n_o

    def body(*refs):
        a_ref, b_ref = refs[:2]
        side_in = refs[n_main_in:n_main_in + n_side_in]
        o_ref = refs[n_main_in + n_side_in]
        side_out = refs[n_main_in + n_side_in + 1:n_main_in + n_side_in + 1 + n_side_out]
        if one_step:
            r = contract(a_ref, b_ref)
            if has_res:
                r = r + refs[2][...]
            o_ref[...] = r.astype(o_ref.dtype)
        else:
            acc = refs[-1]
            first = functools.reduce(jnp.logical_and, [pl.program_id(ax) == 0 for ax in red_axes])
            last = functools.reduce(jnp.logical_and,
                                    [pl.program_id(ax) == n - 1 for ax, n in zip(red_axes, n_red)])

            @pl.when(first)
            def _():
                acc[...] = jnp.zeros_like(acc)

            acc[...] += contract(a_ref, b_ref)

            @pl.when(last)
            def _():
                r = acc[...]
                if has_res:
                    r = r + refs[2][...]
                o_ref[...] = r.astype(o_ref.dtype)
        if side:
            side_work(side_in, side_out)

    in_specs = [a_spec, b_spec]
    operands = [a, b]
    if has_res:
        in_specs.append(_operand_spec("plain", tm, tn, N, picker(lambda bb, i, j, k: (bb, i, j))))
        operands.append(res)
    out_specs, out_shapes = [o_spec], [out_shape]
    for job in side:
        in_specs += [side_spec(blk) for blk in job.in_blocks]
        operands += job.operands
        out_specs += [side_spec(blk) for blk in job.out_blocks]
        out_shapes += job.out_shapes
    scratch = [] if one_step else [pltpu.VMEM((tm, tn), F32)]
    if not side:
        out = _call(body, name, grid, in_specs, o_spec, out_shape, operands, scratch=scratch, sem=sem, after=after)
        return (out, None) if asked_to_carry else out
    outs = _call(body, name, grid, in_specs, tuple(out_specs), tuple(out_shapes), operands, scratch=scratch,
                 sem=("arbitrary",) * len(grid), after=after)
    results, pos = [], 1
    for job in side:
        results.append(tuple(outs[pos:pos + len(job.out_shapes)]))
        pos += len(job.out_shapes)
    return outs[0], results


def _rows_spec(arr, tr):
    lead = arr.ndim - 2
    return pl.BlockSpec((None,) * lead + (tr, arr.shape[-1]), lambda i: (0,) * lead + (i, 0))


def _cast_bf16(name, w):
    R, C = w.shape[-2:]
    tr = _tile(R, max(8, (3 << 20) // C), 16)

    def body(w_ref, o_ref):
        o_ref[...] = w_ref[...].astype(BF16)

    return _call(body, name, (R // tr,), [_rows_spec(w, tr)],
                 pl.BlockSpec((tr, C), lambda i: (i, 0)), jax.ShapeDtypeStruct((R, C), BF16), [w],
                 sem=("parallel",))


def _rms_fwd(name, x, g, after=()):
    T, D = x.shape
    tm = _tile(T, 128, 16)

    def body(x_ref, g_ref, o_ref):
        xv = x_ref[...]
        r = lax.rsqrt(jnp.mean(xv * xv, axis=-1, keepdims=True) + EPS)
        o_ref[...] = (xv * r * g_ref[...]).astype(BF16)

    return _call(body, name, (T // tm,),
                 [pl.BlockSpec((tm, D), lambda i: (i, 0)), pl.BlockSpec((1, D), lambda i: (0, 0))],
                 pl.BlockSpec((tm, D), lambda i: (i, 0)), jax.ShapeDtypeStruct((T, D), BF16), [x, g],
                 sem=("parallel",), after=after)


def _rms_bwd(name, dxn, x, g, dh=None, after=()):
    T, D = x.shape
    tm = _tile(T, 128, 16)
    with_dx = dh is not None

    def body(*refs):
        if with_dx:
            dxn_ref, x_ref, g_ref, dh_ref, o_ref, ob_ref, dg_ref = refs
        else:
            dxn_ref, x_ref, g_ref, dg_ref = refs
        xv = x_ref[...]
        r = lax.rsqrt(jnp.mean(xv * xv, axis=-1, keepdims=True) + EPS)
        xh = xv * r
        dy = dxn_ref[...].astype(F32)

        @pl.when(pl.program_id(0) == 0)
        def _():
            dg_ref[...] = jnp.zeros_like(dg_ref)

        dg_ref[...] += jnp.sum(dy * xh, axis=0, keepdims=True)
        if with_dx:
            dyg = dy * g_ref[...]
            tot = dh_ref[...] + r * (dyg - xh * jnp.mean(dyg * xh, axis=-1, keepdims=True))
            o_ref[...] = tot
            ob_ref[...] = tot.astype(BF16)

    row = pl.BlockSpec((tm, D), lambda i: (i, 0))
    vec = pl.BlockSpec((1, D), lambda i: (0, 0))
    if with_dx:
        return _call(body, name, (T // tm,), [row, row, vec, row], (row, row, vec),
                     (jax.ShapeDtypeStruct((T, D), F32), jax.ShapeDtypeStruct((T, D), BF16),
                      jax.ShapeDtypeStruct((1, D), F32)), [dxn, x, g, dh], after=after)
    return _call(body, name, (T // tm,), [row, row, vec], vec, jax.ShapeDtypeStruct((1, D), F32), [dxn, x, g])


def _loss_head(h, target, g):
    T, D = h.shape
    tm = _tile(T, 128, 16)

    def body(h_ref, t_ref, g_ref, o_ref, ob_ref, loss_ref, dg_ref):
        xv = h_ref[...]
        gv = g_ref[...]
        r = lax.rsqrt(jnp.mean(xv * xv, axis=-1, keepdims=True) + EPS)
        xh = xv * r
        e = xh * gv - t_ref[...]

        @pl.when(pl.program_id(0) == 0)
        def _():
            dg_ref[...] = jnp.zeros_like(dg_ref)
            loss_ref[...] = jnp.zeros_like(loss_ref)

        loss_ref[...] += 0.5 * jnp.sum(jnp.mean(e * e, axis=-1, keepdims=True), axis=0, keepdims=True)
        dy = e * (1.0 / D)
        dg_ref[...] += jnp.sum(dy * xh, axis=0, keepdims=True)
        dyg = dy * gv
        dx = r * (dyg - xh * jnp.mean(dyg * xh, axis=-1, keepdims=True))
        o_ref[...] = dx
        ob_ref[...] = dx.astype(BF16)

    row = pl.BlockSpec((tm, D), lambda i: (i, 0))
    vec = pl.BlockSpec((1, D), lambda i: (0, 0))
    return _call(body, "loss_head", (T // tm,), [row, row, vec],
                 (row, row, pl.BlockSpec((1, 128), lambda i: (0, 0)), vec),
                 (jax.ShapeDtypeStruct((T, D), F32), jax.ShapeDtypeStruct((T, D), BF16),
                  jax.ShapeDtypeStruct((1, 128), F32), jax.ShapeDtypeStruct((1, D), F32)), [h, target, g])


ROW_CHUNK = 64
SUBLANES = 8


def _col_chunks(width):
    return [slice(c0, min(c0 + GROUP_DIM, width)) for c0 in range(0, width, GROUP_DIM)]


def _row_chunks(n_rows):
    return [(r0, min(ROW_CHUNK, n_rows - r0)) for r0 in range(0, n_rows, ROW_CHUNK)]


def _pad_rows(K):
    return -(-(K - 1) // SUBLANES) * SUBLANES


def _shifted_back(buf, K, r0, nr, cs):
    pad = _pad_rows(K)
    win = buf[pl.ds(HALO + r0 - pad, nr + pad), cs]
    for b in range(min(SUBLANES, K)):
        rolled = win if b == 0 else pltpu.roll(win, b, axis=0)
        for a in range((K - 1 - b) // SUBLANES + 1):
            yield K - 1 - (SUBLANES * a + b), rolled[pad - SUBLANES * a:pad - SUBLANES * a + nr]


def _conv_fwd(buf, w_ref, K, r0, nr, cs):
    y = None
    for k, xs in _shifted_back(buf, K, r0, nr, cs):
        term = xs * w_ref[pl.ds(k, 1), cs]
        y = term if y is None else y + term
    return y


def _conv_bwd_input(buf, w_ref, K, r0, nr, cs):
    pad = _pad_rows(K)
    win = buf[pl.ds(r0, nr + pad), cs]
    dx = None
    for b in range(min(SUBLANES, K)):
        rolled = win if b == 0 else pltpu.roll(win, nr + pad - b, axis=0)
        for a in range((K - 1 - b) // SUBLANES + 1):
            k = K - 1 - (SUBLANES * a + b)
            term = rolled[SUBLANES * a:SUBLANES * a + nr] * w_ref[pl.ds(k, 1), cs]
            dx = term if dx is None else dx + term
    return dx


def _fold_rows(v):
    nr, lanes = v.shape
    if nr % SUBLANES:
        return jnp.sum(v, axis=0, keepdims=True)
    return jnp.sum(v.reshape(nr // SUBLANES, SUBLANES, lanes), axis=0)


def _conv_bwd_weight(accs, dy, buf, K, r0, nr, cs):
    accs = list(accs)
    for k, xs in _shifted_back(buf, K, r0, nr, cs):
        accs[k] = accs[k] + _fold_rows(dy * xs)
    return accs


def _add_row(ref, row, cs, acc):
    ref[pl.ds(row, 1), cs] += jnp.sum(acc, axis=0, keepdims=True)


def _sigmoid(z):
    return 0.5 * jnp.tanh(0.5 * z) + 0.5


def _silu_grad(z, sig):
    return sig * (1.0 + z * (1.0 - sig))


def _group_norm(xg):
    xc = xg - jnp.mean(xg, axis=-1, keepdims=True)
    rstd = lax.rsqrt(jnp.mean(xc * xc, axis=-1, keepdims=True) + EPS)
    return xc * rstd, rstd


def _mixer_tiles(T, CW):
    tm = _tile(T, 512, HALO)
    tc = _tile(CW, 256, GROUP_DIM)
    return tm, tc, tm // HALO, CW // tc


def _mixer_fwd(proj, caw, cab, lng, lnb, cbw, T, CW):
    KA, KB = caw.shape[0], cbw.shape[0]
    tm, tc, hb, nc = _mixer_tiles(T, CW)

    def sec(s):
        return pl.BlockSpec((tm, tc), lambda i, c: (i, s * nc + c))

    def sec_prev(s):
        return pl.BlockSpec((HALO, tc), lambda i, c: (jnp.maximum(i * hb - 1, 0), s * nc + c))

    def chan(rows):
        return pl.BlockSpec((rows, tc), lambda i, c: (0, c))

    def body(av, ag, bg, cg, bh, avh, agh, cgh, bhh, caw_ref, cab_ref, lng_ref, lnb_ref, cbw_ref,
             mix_ref, u1_ref, bufa, bufb):
        first = pl.program_id(0) == 0
        bufa[pl.ds(0, HALO), :] = jnp.where(first, 0.0, avh[...].astype(F32) * _sigmoid(agh[...].astype(F32)))
        bufb[pl.ds(0, HALO), :] = jnp.where(first, 0.0, cgh[...].astype(F32) * bhh[...].astype(F32))
        for cs in _col_chunks(tc):
            for r0, nr in _row_chunks(tm):
                rows = pl.ds(r0, nr)
                bufa[pl.ds(HALO + r0, nr), cs] = av[rows, cs].astype(F32) * _sigmoid(ag[rows, cs].astype(F32))
                bufb[pl.ds(HALO + r0, nr), cs] = cg[rows, cs].astype(F32) * bh[rows, cs].astype(F32)
        for cs in _col_chunks(tc):
            for r0, nr in _row_chunks(tm):
                rows = pl.ds(r0, nr)
                u1 = _conv_fwd(bufa, caw_ref, KA, r0, nr, cs) + cab_ref[:, cs]
                u1_ref[rows, cs] = u1
                y, _ = _group_norm(u1)
                z = y * lng_ref[:, cs] + lnb_ref[:, cs]
                mix_ref[0, rows, cs] = (z * _sigmoid(z)).astype(BF16)
                mix_ref[1, rows, cs] = (bg[rows, cs].astype(F32) * _conv_fwd(bufb, cbw_ref, KB, r0, nr, cs)).astype(BF16)

    in_specs = [sec(0), sec(1), sec(2), sec(3), sec(4), sec_prev(0), sec_prev(1), sec_prev(3), sec_prev(4),
                chan(KA), chan(1), chan(1), chan(1), chan(KB)]
    operands = [proj] * 9 + [caw, cab, lng, lnb, cbw]
    return _call(body, "mixer_fwd", (T // tm, nc), in_specs,
                 (pl.BlockSpec((2, tm, tc), lambda i, c: (0, i, c)), pl.BlockSpec((tm, tc), lambda i, c: (i, c))),
                 (jax.ShapeDtypeStruct((2, T, CW), BF16), jax.ShapeDtypeStruct((T, CW), F32)), operands,
                 scratch=[pltpu.VMEM((HALO + tm, tc), F32), pltpu.VMEM((HALO + tm, tc), F32)],
                 sem=("parallel", "parallel"))


def _mixer_bwd1(dmix, proj, u1, caw, lng, lnb, cbw, T, CW):
    KA, KB = caw.shape[0], cbw.shape[0]
    tm, tc, hb, nc = _mixer_tiles(T, CW)

    def sec(s):
        return pl.BlockSpec((tm, tc), lambda c, i: (i, s * nc + c))

    def sec_prev(s):
        return pl.BlockSpec((HALO, tc), lambda c, i: (jnp.maximum(i * hb - 1, 0), s * nc + c))

    def chan(rows):
        return pl.BlockSpec((rows, tc), lambda c, i: (0, c))

    tile = pl.BlockSpec((tm, tc), lambda c, i: (i, c))

    def body(du, dv, u1_ref, av, ag, bg, cg, bh, avh, agh, cgh, bhh, lng_ref, lnb_ref, cbw_ref,
             du1_ref, dcv_ref, dbg_ref, dcaw_ref, dcab_ref, dlng_ref, dlnb_ref, dcbw_ref, bufa, bufb):
        first = pl.program_id(1) == 0

        @pl.when(first)
        def _():
            for r in (dcaw_ref, dcab_ref, dlng_ref, dlnb_ref, dcbw_ref):
                r[...] = jnp.zeros_like(r)

        bufa[pl.ds(0, HALO), :] = jnp.where(first, 0.0, avh[...].astype(F32) * _sigmoid(agh[...].astype(F32)))
        bufb[pl.ds(0, HALO), :] = jnp.where(first, 0.0, cgh[...].astype(F32) * bhh[...].astype(F32))
        for cs in _col_chunks(tc):
            for r0, nr in _row_chunks(tm):
                rows = pl.ds(r0, nr)
                bufa[pl.ds(HALO + r0, nr), cs] = av[rows, cs].astype(F32) * _sigmoid(ag[rows, cs].astype(F32))
                bufb[pl.ds(HALO + r0, nr), cs] = cg[rows, cs].astype(F32) * bh[rows, cs].astype(F32)
        for cs in _col_chunks(tc):
            lanes = cs.stop - cs.start
            zero = jnp.zeros((SUBLANES, lanes), F32)
            a_lng, a_lnb, a_cab = zero, zero, zero
            a_caw, a_cbw = [zero] * KA, [zero] * KB
            gamma, beta = lng_ref[:, cs], lnb_ref[:, cs]
            for r0, nr in _row_chunks(tm):
                rows = pl.ds(r0, nr)
                y, rstd = _group_norm(u1_ref[rows, cs])
                z = y * gamma + beta
                dz = du[rows, cs].astype(F32) * _silu_grad(z, _sigmoid(z))
                a_lng = a_lng + _fold_rows(dz * y)
                a_lnb = a_lnb + _fold_rows(dz)
                dy = dz * gamma
                du1 = rstd * (dy - jnp.mean(dy, axis=-1, keepdims=True)
                              - y * jnp.mean(dy * y, axis=-1, keepdims=True))
                du1_ref[rows, cs] = du1
                a_cab = a_cab + _fold_rows(du1)
                a_caw = _conv_bwd_weight(a_caw, du1, bufa, KA, r0, nr, cs)

                dvv = dv[rows, cs].astype(F32)
                dbg_ref[rows, cs] = (dvv * _conv_fwd(bufb, cbw_ref, KB, r0, nr, cs)).astype(BF16)
                dcv = dvv * bg[rows, cs].astype(F32)
                dcv_ref[rows, cs] = dcv
                a_cbw = _conv_bwd_weight(a_cbw, dcv, bufb, KB, r0, nr, cs)
            _add_row(dlng_ref, 0, cs, a_lng)
            _add_row(dlnb_ref, 0, cs, a_lnb)
            _add_row(dcab_ref, 0, cs, a_cab)
            for k in range(KA):
                _add_row(dcaw_ref, k, cs, a_caw[k])
            for k in range(KB):
                _add_row(dcbw_ref, k, cs, a_cbw[k])

    in_specs = [sec(0), sec(1), tile, sec(0), sec(1), sec(2), sec(3), sec(4),
                sec_prev(0), sec_prev(1), sec_prev(3), sec_prev(4), chan(1), chan(1), chan(KB)]
    operands = [dmix, dmix, u1] + [proj] * 9 + [lng, lnb, cbw]
    return _call(body, "mixer_bwd1", (nc, T // tm), in_specs,
                 (tile, tile, tile, chan(KA), chan(1), chan(1), chan(1), chan(KB)),
                 (jax.ShapeDtypeStruct((T, CW), F32), jax.ShapeDtypeStruct((T, CW), F32),
                  jax.ShapeDtypeStruct((T, CW), BF16), jax.ShapeDtypeStruct((KA, CW), F32),
                  jax.ShapeDtypeStruct((1, CW), F32), jax.ShapeDtypeStruct((1, CW), F32),
                  jax.ShapeDtypeStruct((1, CW), F32), jax.ShapeDtypeStruct((KB, CW), F32)), operands,
                 scratch=[pltpu.VMEM((HALO + tm, tc), F32), pltpu.VMEM((HALO + tm, tc), F32)],
                 sem=("parallel", "arbitrary"))


def _mixer_bwd2(du1, dcv, proj, caw, cbw, T, CW):
    KA, KB = caw.shape[0], cbw.shape[0]
    tm, tc, hb, nc = _mixer_tiles(T, CW)
    n_i = T // tm

    def sec(s):
        return pl.BlockSpec((tm, tc), lambda i, c: (i, s * nc + c))

    def chan(rows):
        return pl.BlockSpec((rows, tc), lambda i, c: (0, c))

    tile = pl.BlockSpec((tm, tc), lambda i, c: (i, c))
    nxt = pl.BlockSpec((HALO, tc), lambda i, c: (jnp.minimum((i + 1) * hb, n_i * hb - 1), c))

    def body(du1_ref, du1n, dcv_ref, dcvn, av, ag, cg, bh, caw_ref, cbw_ref, dav, dag, dcg, dbh, bufa, bufb):
        last = pl.program_id(0) == n_i - 1
        bufa[pl.ds(0, tm), :] = du1_ref[...]
        bufa[pl.ds(tm, HALO), :] = jnp.where(last, 0.0, du1n[...])
        bufb[pl.ds(0, tm), :] = dcv_ref[...]
        bufb[pl.ds(tm, HALO), :] = jnp.where(last, 0.0, dcvn[...])
        for cs in _col_chunks(tc):
            for r0, nr in _row_chunks(tm):
                rows = pl.ds(r0, nr)
                du0 = _conv_bwd_input(bufa, caw_ref, KA, r0, nr, cs)
                sig = _sigmoid(ag[rows, cs].astype(F32))
                dav[rows, cs] = (du0 * sig).astype(BF16)
                dag[rows, cs] = (du0 * av[rows, cs].astype(F32) * (sig * (1.0 - sig))).astype(BF16)
                dch = _conv_bwd_input(bufb, cbw_ref, KB, r0, nr, cs)
                dcg[rows, cs] = (dch * bh[rows, cs].astype(F32)).astype(BF16)
                dbh[rows, cs] = (dch * cg[rows, cs].astype(F32)).astype(BF16)

    in_specs = [tile, nxt, tile, nxt, sec(0), sec(1), sec(3), sec(4), chan(KA), chan(KB)]
    operands = [du1, du1, dcv, dcv, proj, proj, proj, proj, caw, cbw]
    out = jax.ShapeDtypeStruct((T, CW), BF16)
    return _call(body, "mixer_bwd2", (n_i, nc), in_specs, (tile, tile, tile, tile), (out, out, out, out),
                 operands, scratch=[pltpu.VMEM((HALO + tm, tc), F32), pltpu.VMEM((HALO + tm, tc), F32)],
                 sem=("parallel", "parallel"))


def _ffn_tiles(T):
    tm = _tile(T, 512, HALO)
    return tm, tm // HALO, T // tm


def _ffn_act_fwd(gpre, up, cfw):
    nb, T, F = gpre.shape
    KF = cfw.shape[1]
    tm, hb, n_i = _ffn_tiles(T)
    tile = pl.BlockSpec((None, tm, F), lambda b, i: (b, i, 0))
    prev = pl.BlockSpec((None, HALO, F), lambda b, i: (b, jnp.maximum(i * hb - 1, 0), 0))
    wspec = pl.BlockSpec((None, KF, F), lambda b, i: (b, 0, 0))

    def body(g_ref, gh_ref, up_ref, w_ref, f_ref, buf):
        buf[pl.ds(HALO, tm), :] = g_ref[...].astype(F32)
        buf[pl.ds(0, HALO), :] = jnp.where(pl.program_id(1) == 0, 0.0, gh_ref[...].astype(F32))
        for cs in _col_chunks(F):
            for r0, nr in _row_chunks(tm):
                rows = pl.ds(r0, nr)
                g = _conv_fwd(buf, w_ref, KF, r0, nr, cs)
                f_ref[rows, cs] = (g * _sigmoid(g) * up_ref[rows, cs].astype(F32)).astype(BF16)

    return _call(body, "ffn_act_fwd", (nb, n_i), [tile, prev, tile, wspec], tile,
                 jax.ShapeDtypeStruct((nb, T, F), BF16), [gpre, gpre, up, cfw],
                 scratch=[pltpu.VMEM((HALO + tm, F), F32)], sem=("parallel", "parallel"))


def _ffn_act_bwd(df, gpre, up, cfw):
    nb, T, F = gpre.shape
    KF = cfw.shape[1]
    tm, hb, n_i = _ffn_tiles(T)
    extra = 2 * SUBLANES
    assert KF - 1 <= SUBLANES and HALO >= extra
    tile = pl.BlockSpec((None, tm, F), lambda b, i: (b, i, 0))
    prev = pl.BlockSpec((None, HALO, F), lambda b, i: (b, jnp.maximum(i * hb - 1, 0), 0))
    nxt = pl.BlockSpec((None, HALO, F), lambda b, i: (b, jnp.minimum((i + 1) * hb, n_i * hb - 1), 0))
    wspec = pl.BlockSpec((None, KF, F), lambda b, i: (b, 0, 0))

    def body(df_ref, g_ref, gh_ref, up_ref, dfn_ref, gn_ref, upn_ref, w_ref, dgpre_ref, dup_ref, dw_ref, gbuf, dgbuf):
        first = pl.program_id(1) == 0
        last = pl.program_id(1) == n_i - 1

        @pl.when(first)
        def _():
            dw_ref[...] = jnp.zeros_like(dw_ref)

        gbuf[pl.ds(0, HALO), :] = jnp.where(first, 0.0, gh_ref[...].astype(F32))
        gbuf[pl.ds(HALO, tm), :] = g_ref[...].astype(F32)
        gbuf[pl.ds(HALO + tm, HALO), :] = gn_ref[...].astype(F32)
        for cs in _col_chunks(F):
            accs = [jnp.zeros((SUBLANES, cs.stop - cs.start), F32)] * KF
            for r0, nr in _row_chunks(tm):
                rows = pl.ds(r0, nr)
                g = _conv_fwd(gbuf, w_ref, KF, r0, nr, cs)
                sig = _sigmoid(g)
                dfv = df_ref[rows, cs].astype(F32)
                dup_ref[rows, cs] = (dfv * (g * sig)).astype(BF16)
                dg = dfv * up_ref[rows, cs].astype(F32) * _silu_grad(g, sig)
                dgbuf[rows, cs] = dg
                accs = _conv_bwd_weight(accs, dg, gbuf, KF, r0, nr, cs)
            for k in range(KF):
                _add_row(dw_ref, k, cs, accs[k])
            g = _conv_fwd(gbuf, w_ref, KF, tm, extra, cs)
            dg_next = (dfn_ref[pl.ds(0, extra), cs].astype(F32) * upn_ref[pl.ds(0, extra), cs].astype(F32)
                       * _silu_grad(g, _sigmoid(g)))
            dgbuf[pl.ds(tm, extra), cs] = jnp.where(last, 0.0, dg_next)
        for cs in _col_chunks(F):
            for r0, nr in _row_chunks(tm):
                dgpre_ref[pl.ds(r0, nr), cs] = _conv_bwd_input(dgbuf, w_ref, KF, r0, nr, cs).astype(BF16)

    out = jax.ShapeDtypeStruct((nb, T, F), BF16)
    return _call(body, "ffn_act_bwd", (nb, n_i), [tile, tile, prev, tile, nxt, nxt, nxt, wspec], (tile, tile, wspec),
                 (out, out, jax.ShapeDtypeStruct((nb, KF, F), F32)), [df, gpre, gpre, up, df, gpre, up, cfw],
                 scratch=[pltpu.VMEM((2 * HALO + tm, F), F32), pltpu.VMEM((tm + extra, F), F32)],
                 sem=("parallel", "arbitrary"))


def _softmax_rows(s):
    e = jnp.exp(s - jnp.max(s, axis=-1, keepdims=True))
    return e / jnp.sum(e, axis=-1, keepdims=True)


def _attn_fwd(q, k, v):
    T, D = q.shape
    Mm = k.shape[0]
    hd = D // N_XATTN_HEADS
    scale = hd ** -0.5
    tm = _tile(T, 256, 16)

    def body(q_ref, k_ref, v_ref, o_ref):
        for h in range(N_XATTN_HEADS):
            sl = slice(h * hd, (h + 1) * hd)
            s = lax.dot_general(q_ref[:, sl], k_ref[:, sl], _DOT_DIMS["nt"], preferred_element_type=F32) * scale
            p = _softmax_rows(s).astype(BF16)
            o_ref[:, sl] = jnp.dot(p, v_ref[:, sl], preferred_element_type=F32).astype(BF16)

    row = pl.BlockSpec((tm, D), lambda i: (i, 0))
    full = pl.BlockSpec((Mm, D), lambda i: (0, 0))
    return _call(body, "attn_fwd", (T // tm,), [row, full, full], row, jax.ShapeDtypeStruct((T, D), BF16),
                 [q, k, v], sem=("parallel",))


def _attn_bwd(q, k, v, do):
    T, D = q.shape
    Mm = k.shape[0]
    hd = D // N_XATTN_HEADS
    scale = hd ** -0.5
    tm = _tile(T, 256, 16)
    n_i = T // tm

    def body(q_ref, do_ref, k_ref, v_ref, dq_ref, dk_ref, dv_ref, dk_acc, dv_acc):
        @pl.when(pl.program_id(0) == 0)
        def _():
            dk_acc[...] = jnp.zeros_like(dk_acc)
            dv_acc[...] = jnp.zeros_like(dv_acc)

        for h in range(N_XATTN_HEADS):
            sl = slice(h * hd, (h + 1) * hd)
            qh, kh, doh = q_ref[:, sl], k_ref[:, sl], do_ref[:, sl]
            s = lax.dot_general(qh, kh, _DOT_DIMS["nt"], preferred_element_type=F32) * scale
            p = _softmax_rows(s)
            dv_acc[:, sl] += lax.dot_general(p.astype(BF16), doh, _DOT_DIMS["tn"], preferred_element_type=F32)
            dp = lax.dot_general(doh, v_ref[:, sl], _DOT_DIMS["nt"], preferred_element_type=F32)
            ds = (p * (dp - jnp.sum(dp * p, axis=-1, keepdims=True)) * scale).astype(BF16)
            dq_ref[:, sl] = jnp.dot(ds, kh, preferred_element_type=F32).astype(BF16)
            dk_acc[:, sl] += lax.dot_general(ds, qh, _DOT_DIMS["tn"], preferred_element_type=F32)

        @pl.when(pl.program_id(0) == n_i - 1)
        def _():
            dk_ref[...] = dk_acc[...].astype(BF16)
            dv_ref[...] = dv_acc[...].astype(BF16)

    row = pl.BlockSpec((tm, D), lambda i: (i, 0))
    full = pl.BlockSpec((Mm, D), lambda i: (0, 0))
    return _call(body, "attn_bwd", (n_i,), [row, row, full, full], (row, full, full),
                 (jax.ShapeDtypeStruct((T, D), BF16), jax.ShapeDtypeStruct((Mm, D), BF16),
                  jax.ShapeDtypeStruct((Mm, D), BF16)), [q, do, k, v],
                 scratch=[pltpu.VMEM((Mm, D), F32), pltpu.VMEM((Mm, D), F32)])


def _position():
    x, y, c = lax.axis_index("x"), lax.axis_index("y"), lax.axis_index("c")
    return x, y, c


def _peer(pos, k):
    x, y, c = pos
    return (1 - x if k & 4 else x, 1 - y if k & 2 else y, 1 - c if k & 1 else c)


def _index(pos):
    x, y, c = pos
    return 4 * x + 2 * y + c


def _sequencer_kernel(body, name, collective_id, out_type, operands):
    return pl.kernel(
        body, name=name, out_type=out_type,
        mesh=plsc.ScalarSubcoreMesh(axis_name="sequencer", num_cores=1),
        scratch_types=[pltpu.SemaphoreType.DMA, pltpu.SemaphoreType.DMA((7,)), pltpu.SemaphoreType.DMA],
        compiler_params=pltpu.CompilerParams(collective_id=collective_id),
    )(*operands)


def _handshake(peers):
    barrier = pltpu.get_barrier_semaphore()
    for peer in peers:
        pl.semaphore_signal(barrier, inc=1, device_id=peer, device_id_type=MESH)
    pl.semaphore_wait(barrier, len(peers))


def _sequencer_all_gather(name, collective_id, shards):
    n = len(shards)

    def body(*refs):
        x_refs, out_refs = refs[:n], refs[n:2 * n]
        send_sem, recv_sems, local_sem = refs[2 * n:]
        me = _position()
        x, y, c = me
        sibling = _peer(me, 1)
        first = (x + (1 - c) - 2 * x * (1 - c), y + c - 2 * y * c, c)
        second = (x + c - 2 * x * c, y + (1 - c) - 2 * y * (1 - c), c)
        diagonal = _peer(me, 6)
        _handshake([sibling, first, second])

        def copy(a, k, block, to, own=False):
            dst = out_refs[a].at[_index(block)]
            return pltpu.make_async_remote_copy(
                src_ref=x_refs[a] if own else dst, dst_ref=dst, send_sem=send_sem, recv_sem=recv_sems.at[k],
                device_id=to, device_id_type=MESH)

        local = [pltpu.make_async_copy(x_refs[a], out_refs[a].at[_index(me)], local_sem) for a in range(n)]
        started = [copy(a, 1 + j, me, peer, own=True) for a in range(n) for j, peer in enumerate((first, second))]
        started += [copy(a, 0, me, sibling, own=True) for a in range(n)]
        for cp in started + local:
            cp.start()
        for k, origin in ((1, first), (2, second), (3, diagonal)):
            for a in range(n):
                copy(a, k, origin, me).wait_recv()
            passed = [copy(a, 3 + k, origin, sibling) for a in range(n)]
            if k == 1:
                passed = [copy(a, 3, origin, second) for a in range(n)] + passed
            for cp in passed:
                cp.start()
            started += passed
        for k in (0, 4, 5, 6):
            for a in range(n):
                copy(a, k, sibling, me).wait_recv()
        for cp in started:
            cp.wait_send()
        for cp in local:
            cp.wait()

    return _sequencer_kernel(body, name, collective_id,
                             [jax.ShapeDtypeStruct((N_DEV,) + s.shape, s.dtype) for s in shards], shards)


def _chip_index(pos):
    return 2 * pos[0] + pos[1]


def _sequencer_to_sibling(name, collective_id, parts):
    n = len(parts)

    def body(*refs):
        p_refs, out_refs = refs[:n], refs[n:2 * n]
        send_sem, recv_sems, _ = refs[2 * n:]
        me = _position()
        sibling = _peer(me, 1)
        _handshake([sibling])
        copies = [pltpu.make_async_remote_copy(
            src_ref=p_refs[a].at[2 * q + sibling[2]], dst_ref=out_refs[a].at[q], send_sem=send_sem,
            recv_sem=recv_sems.at[0], device_id=sibling, device_id_type=MESH)
            for a in range(n) for q in range(N_DEV // 2)]
        for cp in copies:
            cp.start()
        for cp in copies:
            cp.wait_recv()
        for cp in copies:
            cp.wait_send()

    return _sequencer_kernel(body, name, collective_id,
                             [jax.ShapeDtypeStruct((N_DEV // 2,) + p.shape[1:], p.dtype) for p in parts], parts)


def _sequencer_to_chips(name, collective_id, sums):
    n = len(sums)

    def body(*refs):
        s_refs, out_refs = refs[:n], refs[n:2 * n]
        send_sem, recv_sems, local_sem = refs[2 * n:]
        me = _position()
        my_chip = _chip_index(me)
        peers = [_peer(me, 4), _peer(me, 2), _peer(me, 6)]
        _handshake(peers)
        local = [pltpu.make_async_copy(s_refs[a].at[my_chip], out_refs[a].at[my_chip], local_sem) for a in range(n)]
        sends = [pltpu.make_async_remote_copy(
            src_ref=s_refs[a].at[_chip_index(peer)], dst_ref=out_refs[a].at[my_chip], send_sem=send_sem,
            recv_sem=recv_sems.at[1 + j], device_id=peer, device_id_type=MESH)
            for a in range(n) for j, peer in enumerate(peers)]
        for cp in sends + local:
            cp.start()
        for j, peer in enumerate(peers):
            for a in range(n):
                pltpu.make_async_remote_copy(
                    src_ref=s_refs[a].at[my_chip], dst_ref=out_refs[a].at[_chip_index(peer)], send_sem=send_sem,
                    recv_sem=recv_sems.at[1 + j], device_id=peer, device_id_type=MESH).wait_recv()
        for cp in sends:
            cp.wait_send()
        for cp in local:
            cp.wait()

    return _sequencer_kernel(body, name, collective_id,
                             [jax.ShapeDtypeStruct(s.shape, s.dtype) for s in sums], sums)


def _chip_sum(name, parts, got, after=()):
    _, R, C = parts.shape
    tr = _tile(R, max(8, (3 << 20) // C), 16)
    n_after = len(after)
    limit = 6 * _nbytes((tr, C), parts.dtype) + VMEM_TEMP_ALLOWANCE

    def body(c_ref, p_ref, g_ref, *rest):
        o_ref = rest[n_after]
        o_ref[...] = (p_ref[...].astype(F32) + g_ref[...].astype(F32)).astype(o_ref.dtype)

    blk = pl.BlockSpec((None, tr, C), lambda q, i, c_ref: (q, i, 0))
    mine = pl.BlockSpec((None, tr, C), lambda q, i, c_ref: (2 * q + c_ref[0], i, 0))
    core = lax.axis_index("c").astype(jnp.int32).reshape(1)
    parts, got = [pltpu.with_memory_space_constraint(o, pltpu.HBM) for o in (parts, got)]
    after = [pltpu.with_memory_space_constraint(o, pltpu.HBM) for o in after]
    return pl.pallas_call(
        body, name=name, out_shape=pltpu.HBM((N_DEV // 2, R, C), parts.dtype),
        grid_spec=pltpu.PrefetchScalarGridSpec(
            num_scalar_prefetch=1, grid=(N_DEV // 2, R // tr),
            in_specs=[mine, blk] + [pl.BlockSpec(memory_space=pl.ANY)] * n_after, out_specs=blk),
        compiler_params=pltpu.CompilerParams(dimension_semantics=("parallel", "parallel"),
                                             vmem_limit_bytes=int(limit)),
    )(core, parts, got, *after)


def _all_reduce_rows(name, v):
    R, C = v.shape

    def body(v_ref, out_ref, gath, send_sems, recv_sems):
        me = _position()
        gath[_index(me)] = v_ref[...]
        sends = []
        for k in range(1, N_DEV):
            peer = _peer(me, k)
            sends.append(pltpu.make_async_remote_copy(
                src_ref=v_ref, dst_ref=gath.at[_index(me)], send_sem=send_sems.at[k - 1],
                recv_sem=recv_sems.at[k - 1], device_id=peer, device_id_type=MESH))
        for cp in sends:
            cp.start()
        for k in range(1, N_DEV):
            peer = _peer(me, k)
            pltpu.make_async_remote_copy(
                src_ref=v_ref, dst_ref=gath.at[_index(peer)], send_sem=send_sems.at[k - 1],
                recv_sem=recv_sems.at[k - 1], device_id=peer, device_id_type=MESH).wait_recv()
        for cp in sends:
            cp.wait_send()
        tot = gath[0]
        for s in range(1, N_DEV):
            tot = tot + gath[s]
        out_ref[...] = tot

    return pl.pallas_call(
        body, name=name, out_shape=jax.ShapeDtypeStruct((R, C), F32),
        in_specs=[VMEM_SPEC], out_specs=VMEM_SPEC,
        scratch_shapes=[pltpu.VMEM((N_DEV, R, C), F32), pltpu.SemaphoreType.DMA((7,)),
                        pltpu.SemaphoreType.DMA((7,))],
    )(v)


def _adamw_math(g, w, m, v):
    m = ADAM_B1 * m + (1.0 - ADAM_B1) * g
    v = ADAM_B2 * v + (1.0 - ADAM_B2) * (g * g)
    m_hat = m / (1.0 - ADAM_B1 ** ADAM_STEP)
    v_hat = v / (1.0 - ADAM_B2 ** ADAM_STEP)
    delta = -ADAM_LR * (m_hat / (jnp.sqrt(v_hat) + ADAM_EPS) + ADAM_WD * w)
    return delta, m, v


def _adamw_tile(n):
    def body(p_ref, w_ref, m_ref, v_ref, g_ref, d_ref, nm_ref, nv_ref):
        g = p_ref[0].astype(F32)
        for s in range(1, n):
            g = g + p_ref[s].astype(F32)
        g_ref[...] = g
        d_ref[...], nm_ref[...], nv_ref[...] = _adamw_math(g, w_ref[...], m_ref[...], v_ref[...])
    return body


class _SideJob(NamedTuple):
    operands: list
    in_blocks: list
    out_blocks: list
    out_shapes: list
    body: object


def _adamw_job(parts, w, m, v):
    n, R, C = parts.shape
    lead = w.ndim - 2
    if R % (SIDE_TILES * 16) == 0:
        tile, index = (R // SIDE_TILES, C), lambda t: (t, 0)
    elif C % (SIDE_TILES * 128) == 0:
        tile, index = (R, C // SIDE_TILES), lambda t: (0, t)
    else:
        return None
    p_blk = ((n,) + tile, lambda t: (0,) + index(t))
    w_blk = ((None,) * lead + tile, lambda t: (0,) * lead + index(t))
    return _SideJob([parts, w, m, v], [p_blk, w_blk, w_blk, w_blk], [w_blk] * 4,
                    [jax.ShapeDtypeStruct(w.shape, F32)] * 4, _adamw_tile(n))


def _adamw(name, parts, w, m, v, after=()):
    n, R, C = parts.shape
    tr = _tile(R, max(8, (1 << 18) // C), 16)
    blk = _rows_spec(w, tr)
    out = jax.ShapeDtypeStruct(w.shape, F32)
    return _call(_adamw_tile(n), name, (R // tr,), [pl.BlockSpec((n, tr, C), lambda i: (0, i, 0)), blk, blk, blk],
                 (blk, blk, blk, blk), (out, out, out, out), [parts, w, m, v], sem=("parallel",), after=after)


def kernel(x, mem, g_mix, w_in, conv_a_w, conv_a_b, ln_a_g, ln_a_b, conv_b_w, w_out, g_xattn, g_mem, w_q, w_k, w_v, w_o, g_ffn, w_gate, w_up, conv_f_w, w_down, g_final, loss_target, m_g_mix, m_w_in, m_conv_a_w, m_conv_a_b, m_ln_a_g, m_ln_a_b, m_conv_b_w, m_w_out, m_g_xattn, m_g_mem, m_w_q, m_w_k, m_w_v, m_w_o, m_g_ffn, m_w_gate, m_w_up, m_conv_f_w, m_w_down, m_g_final, v_g_mix, v_w_in, v_conv_a_w, v_conv_a_b, v_ln_a_g, v_ln_a_b, v_conv_b_w, v_w_out, v_g_xattn, v_g_mem, v_w_q, v_w_k, v_w_v, v_w_o, v_g_ffn, v_w_gate, v_w_up, v_conv_f_w, v_w_down, v_g_final):
    T, D = x.shape[1], x.shape[2]
    Mm = mem.shape[1]
    CW = conv_a_b.shape[1]
    INB = w_in.shape[2]
    FB = w_gate.shape[2]
    KA, KB, KF = conv_a_w.shape[1], conv_b_w.shape[1], conv_f_w.shape[1]
    DB = D // N_DEV
    assert 5 * CW == N_DEV * INB and 2 * CW == D

    x2, mem2, tgt = x[0], mem[0], loss_target[0]
    g_mem2, g_final2 = g_mem.reshape(1, D), g_final.reshape(1, D)

    def bf16(name, w):
        return _cast_bf16("cast_" + name, w)

    Win, caw, cbw = _sequencer_all_gather(
        "ag_in", AG_ID, [bf16("w_in", w_in), conv_a_w[0], conv_b_w[0]])
    Wout, = _sequencer_all_gather("ag_out", AG_ID, [bf16("w_out", w_out)])
    Wq, Wk, Wv, Wo = _sequencer_all_gather(
        "ag_attn", AG_ID, [bf16("w_q", w_q), bf16("w_k", w_k), bf16("w_v", w_v), bf16("w_o", w_o)])
    def transposed(w):
        return jnp.transpose(w[0])

    gate_t = [transposed(a) for a in (w_gate, m_w_gate, v_w_gate)]
    up_t = [transposed(a) for a in (w_up, m_w_up, v_w_up)]
    WgateT, cfw = _sequencer_all_gather("ag_gate", AG_ID, [bf16("w_gate", gate_t[0]), conv_f_w[0]])
    WupT, = _sequencer_all_gather("ag_up", AG_ID, [bf16("w_up", up_t[0])])
    Wdown, = _sequencer_all_gather("ag_down", AG_ID, [bf16("w_down", w_down)])
    Wout, Wq, Wk, Wv, Wo = [w.reshape(D, D) for w in (Wout, Wq, Wk, Wv, Wo)]
    caw = jnp.transpose(caw, (1, 0, 2)).reshape(KA, CW)
    cbw = jnp.transpose(cbw, (1, 0, 2)).reshape(KB, CW)

    xn1 = _rms_fwd("rms_mix", x2, g_mix)
    proj = _matmul("mm_proj", "nn", xn1, Win, M=T, N=INB, K=D, nb=N_DEV, b_lay="blk", o_lay="col", tn=INB,
                   out_dtype=BF16)
    mix, u1 = _mixer_fwd(proj, caw, conv_a_b, ln_a_g, ln_a_b, cbw, T, CW)
    h1 = _matmul("mm_h1", "nn", mix, Wout.reshape(2, CW, D), M=T, N=D, K=CW, nb=2, a_lay="blk", b_lay="blk",
                 red_block=True, res=x2)
    xn2 = _rms_fwd("rms_xattn", h1, g_xattn)
    q = _matmul("mm_q", "nn", xn2, Wq, M=T, N=D, K=D, out_dtype=BF16)
    memn = _rms_fwd("rms_mem", mem2, g_mem2, after=[q])
    kk = _matmul("mm_k", "nn", memn, Wk, M=Mm, N=D, K=D, out_dtype=BF16)
    vv = _matmul("mm_v", "nn", memn, Wv, M=Mm, N=D, K=D, out_dtype=BF16)
    o = _attn_fwd(q, kk, vv)
    h2 = _matmul("mm_h2", "nn", o, Wo, M=T, N=D, K=D, res=h1)
    xn3 = _rms_fwd("rms_ffn", h2, g_ffn)
    gpre = _matmul("mm_gate", "nt", xn3, WgateT, M=T, N=FB, K=D, nb=N_DEV, b_lay="blk", o_lay="blk", tn=FB,
                   out_dtype=BF16)
    up = _matmul("mm_up", "nt", xn3, WupT, M=T, N=FB, K=D, nb=N_DEV, b_lay="blk", o_lay="blk", tn=FB,
                 out_dtype=BF16)
    f = _ffn_act_fwd(gpre, up, cfw)
    h3 = _matmul("mm_h3", "nn", f, Wdown, M=T, N=D, K=FB, nb=N_DEV, a_lay="blk", b_lay="blk", red_block=True,
                 group=2, res=h2)
    dh3, dh3b, loss_part, dg_final = _loss_head(h3, tgt, g_final2)

    wmv = {"w_in": (w_in, m_w_in, v_w_in), "conv_a_w": (conv_a_w, m_conv_a_w, v_conv_a_w),
           "conv_b_w": (conv_b_w, m_conv_b_w, v_conv_b_w), "w_out": (w_out, m_w_out, v_w_out),
           "w_q": (w_q, m_w_q, v_w_q), "w_k": (w_k, m_w_k, v_w_k), "w_v": (w_v, m_w_v, v_w_v),
           "w_o": (w_o, m_w_o, v_w_o), "w_gate": gate_t, "w_up": up_t,
           "conv_f_w": (conv_f_w, m_conv_f_w, v_conv_f_w), "w_down": (w_down, m_w_down, v_w_down)}
    res = {}
    pending = []

    def mm(*args, after=(), carry=(), **kwargs):
        behind = list(after) + pending
        pending.clear()
        jobs, alone = [], []
        for names, got in carry:
            for n, g in zip(names, got):
                job = _adamw_job(g, *wmv[n])
                (alone if job is None else jobs).append((n, g, job))
        results = None
        if jobs:
            out, results = _matmul(*args, after=behind, side=[job for _, _, job in jobs], **kwargs)
        else:
            out = _matmul(*args, after=behind, **kwargs)
        if results is None:
            alone, jobs = alone + jobs, []
        for (n, _, _), r in zip(jobs, results or []):
            res[n] = r
            pending.append(r[0])
        for n, g, _ in alone:
            res[n] = _adamw("adamw_" + n, g, *wmv[n], after=[out])
            pending.append(res[n][0])
        return out

    def to_sibling(tag, named_parts):
        got = _sequencer_to_sibling("rs1_" + tag, SIBLING_ID, [p for _, p in named_parts])
        return named_parts, got

    def to_chips(tag, stage1, after):
        named_parts, got = stage1
        sums = [_chip_sum("sum_" + n, p, g, after=after) for (n, p), g in zip(named_parts, got)]
        pending.extend(sums)
        return [n for n, _ in named_parts], _sequencer_to_chips("rs2_" + tag, CHIPS_ID, sums)

    def finish(stage2, after):
        names, got = stage2
        for n, g in zip(names, got):
            w, m, v = wmv[n]
            res[n] = _adamw("adamw_" + n, g, w, m, v, after=after)
            pending.append(res[n][0])

    def row_blocks(dw):
        return dw.reshape(N_DEV, DB, D)

    def conv_blocks(dw, K):
        return jnp.transpose(dw.reshape(K, N_DEV, CW // N_DEV), (1, 0, 2))

    dWdown = mm("mm_dw_down", "tn", f, dh3b, M=FB, N=D, K=T, nb=N_DEV, a_lay="blk", o_lay="blk",
                     out_dtype=BF16, tm=FB)
    s_down = to_sibling("down", [("w_down", dWdown)])
    df = mm("mm_df", "nt", dh3b, Wdown, M=T, N=FB, K=D, nb=N_DEV, b_lay="blk", o_lay="blk", tn=FB, out_dtype=BF16,
                 after=[dWdown])
    dgpre, dup, dcfw = _ffn_act_bwd(df, gpre, up, cfw)
    c_down = to_chips("down", s_down, after=[dgpre])
    dWgate = mm("mm_dw_gate", "tn", dgpre, xn3, M=FB, N=D, K=T, nb=N_DEV, a_lay="blk", o_lay="blk",
                     out_dtype=BF16, tm=FB)
    s_gate = to_sibling("gate", [("w_gate", dWgate), ("conv_f_w", dcfw)])
    dWup = mm("mm_dw_up", "tn", dup, xn3, M=FB, N=D, K=T, nb=N_DEV, a_lay="blk", o_lay="blk",
                   out_dtype=BF16, tm=FB, after=[dWgate])
    s_up = to_sibling("up", [("w_up", dWup)])
    dxn3 = mm("mm_dxn3_gate", "nn", dgpre, WgateT, M=T, N=D, K=FB, nb=N_DEV, a_lay="blk", b_lay="blk",
                   red_block=True, group=2, out_dtype=BF16, after=[dWup], carry=[c_down])
    c_gate = to_chips("gate", s_gate, after=[dxn3])
    dxn3 = mm("mm_dxn3_up", "nn", dup, WupT, M=T, N=D, K=FB, nb=N_DEV, a_lay="blk", b_lay="blk",
                   red_block=True, group=2, out_dtype=BF16, res=dxn3)
    c_up = to_chips("up", s_up, after=[dxn3])
    dh2, dh2b, dg_ffn = _rms_bwd("rms_bwd_ffn", dxn3, h2, g_ffn, dh3)

    dWo = mm("mm_dw_o", "tn", o, dh2b, M=D, N=D, K=T, out_dtype=BF16)
    s_o = to_sibling("o", [("w_o", row_blocks(dWo))])
    do = mm("mm_do", "nt", dh2b, Wo, M=T, N=D, K=D, out_dtype=BF16, after=[dWo])
    dq, dk, dv = _attn_bwd(q, kk, vv, do)
    dWq = mm("mm_dw_q", "tn", xn2, dq, M=D, N=D, K=T, out_dtype=BF16, carry=[c_gate])
    s_q = to_sibling("q", [("w_q", row_blocks(dWq))])
    dxn2 = mm("mm_dxn2", "nt", dq, Wq, M=T, N=D, K=D, out_dtype=BF16, after=[dWq], carry=[c_up])
    c_o = to_chips("o", s_o, after=[dxn2])
    c_q = to_chips("q", s_q, after=[dxn2])
    dh1, dh1b, dg_xattn = _rms_bwd("rms_bwd_xattn", dxn2, h1, g_xattn, dh2)
    dWk = mm("mm_dw_k", "tn", memn, dk, M=D, N=D, K=Mm, out_dtype=BF16, after=[dh1b])
    dWv = mm("mm_dw_v", "tn", memn, dv, M=D, N=D, K=Mm, out_dtype=BF16, after=[dh1b])
    s_kv = to_sibling("kv", [("w_k", row_blocks(dWk)), ("w_v", row_blocks(dWv))])
    dmemn = mm("mm_dmem_k", "nt", dk, Wk, M=Mm, N=D, K=D, after=[dWk, dWv])
    dmemn = mm("mm_dmem_v", "nt", dv, Wv, M=Mm, N=D, K=D, res=dmemn)
    dg_mem = _rms_bwd("rms_bwd_mem", dmemn, mem2, g_mem2)

    dWout = mm("mm_dw_out", "tn", mix, dh1b, M=CW, N=D, K=T, nb=2, a_lay="blk", o_lay="blk", out_dtype=BF16,
                    after=[dg_mem], carry=[c_o])
    s_out = to_sibling("out", [("w_out", row_blocks(dWout.reshape(D, D)))])
    dmix = mm("mm_dmix", "nt", dh1b, Wout, M=T, N=D, K=D, out_dtype=BF16, after=[dWout], carry=[c_q])
    c_kv = to_chips("kv", s_kv, after=[dmix])
    du1, dcv, dbg, dcaw, dcab, dlng, dlnb, dcbw = _mixer_bwd1(dmix, proj, u1, caw, ln_a_g, ln_a_b, cbw, T, CW)
    c_out = to_chips("out", s_out, after=[du1])
    dav, dag, dcg, dbh = _mixer_bwd2(du1, dcv, proj, caw, cbw, T, CW)
    dproj = jnp.concatenate([dav, dag, dbg, dcg, dbh], axis=1)
    dWin = mm("mm_dw_in", "tn", xn1, dproj, M=D, N=INB, K=T, nb=N_DEV, b_lay="col", o_lay="blk",
                   out_dtype=BF16, tm=512, tn=INB, carry=[c_kv])
    s_in = to_sibling("in", [("w_in", dWin), ("conv_a_w", conv_blocks(dcaw, KA)),
                             ("conv_b_w", conv_blocks(dcbw, KB))])
    finish(c_out, after=[dWin])
    c_in = to_chips("in", s_in, after=list(pending))
    dxn1 = mm("mm_dxn1", "nt", dproj, Win, M=T, N=D, K=INB, nb=N_DEV, a_lay="col", b_lay="blk",
                   red_block=True, group=2, out_dtype=BF16, after=[dWin])
    dx, _, dg_mix = _rms_bwd("rms_bwd_mix", dxn1, x2, g_mix, dh1, after=list(pending))

    def pair(a, b):
        return jnp.concatenate([a, b], axis=1)

    zeros_half = jnp.zeros((1, CW), F32)
    small_g = jnp.concatenate([
        dg_mix, pair(dcab, dlng), pair(dlnb, zeros_half), dg_xattn, dg_mem, dg_ffn, dg_final,
        jnp.broadcast_to(loss_part[:, :1], (1, D))], axis=0)
    small_sum = _all_reduce_rows("ar_small", small_g)
    loss = small_sum[7, 0]
    finish(c_in, after=[small_sum])

    def pack(a_mix, a_cab, a_lng, a_lnb, a_xattn, a_mem, a_ffn, a_final):
        return jnp.concatenate([a_mix, pair(a_cab, a_lng), pair(a_lnb, zeros_half), a_xattn, a_mem.reshape(1, D),
                                a_ffn, a_final.reshape(1, D), jnp.zeros((1, D), F32)], axis=0)

    small = _adamw("adamw_small", small_sum[None],
                   pack(g_mix, conv_a_b, ln_a_g, ln_a_b, g_xattn, g_mem, g_ffn, g_final),
                   pack(m_g_mix, m_conv_a_b, m_ln_a_g, m_ln_a_b, m_g_xattn, m_g_mem, m_g_ffn, m_g_final),
                   pack(v_g_mix, v_conv_a_b, v_ln_a_g, v_ln_a_b, v_g_xattn, v_g_mem, v_g_ffn, v_g_final))

    def unpack(a):
        return {"g_mix": a[0:1], "conv_a_b": a[1:2, :CW], "ln_a_g": a[1:2, CW:], "ln_a_b": a[2:3, :CW],
                "g_xattn": a[3:4], "g_mem": a[4], "g_ffn": a[5:6], "g_final": a[6]}

    small = [unpack(a) for a in small]
    order = ["g_mix", "w_in", "conv_a_w", "conv_a_b", "ln_a_g", "ln_a_b", "conv_b_w", "w_out", "g_xattn", "g_mem",
             "w_q", "w_k", "w_v", "w_o", "g_ffn", "w_gate", "w_up", "conv_f_w", "w_down", "g_final"]
    outs = [loss, dx[None]]
    for kind in range(4):
        for n in order:
            if n in ("w_gate", "w_up"):
                outs.append(jnp.transpose(res[n][kind])[None])
            else:
                outs.append(res[n][kind] if n in res else small[kind][n])
    return tuple(outs)
```

```python
import functools
from typing import NamedTuple

import jax
import jax.numpy as jnp
from jax import lax
from jax.experimental import pallas as pl
from jax.experimental.pallas import tpu as pltpu
from jax.experimental.pallas import tpu_sc as plsc

F32 = jnp.float32
BF16 = jnp.bfloat16

N_DEV = 8
EPS = 1e-6
GROUP_DIM = 128
N_XATTN_HEADS = 4
ADAM_LR = 0.001
ADAM_B1 = 0.9
ADAM_B2 = 0.999
ADAM_EPS = 1e-08
ADAM_WD = 0.01
ADAM_STEP = 10

AG_ID, SIBLING_ID, CHIPS_ID = 1, 2, 3

SIDE_TILES = 32
HALO = 32
VMEM_V7X_BYTES = 64 * 1024 * 1024
VMEM_TEMP_ALLOWANCE = 12 * 1024 * 1024

VMEM_SPEC = pl.BlockSpec(memory_space=pltpu.VMEM)
MESH = pl.DeviceIdType.MESH


def _tile(n, pref, align):
    if n <= pref:
        return n
    t = (pref // align) * align
    while t >= align:
        if n % t == 0:
            return t
        t -= align
    return n


def _nbytes(shape, dtype):
    n = 1
    for d in shape:
        if d is not None:
            n *= d
    return n * jnp.dtype(dtype).itemsize


def _call(body, name, grid, in_specs, out_specs, out_shape, operands, scratch=(), sem=None, after=()):
    outs = out_shape if isinstance(out_shape, (tuple, list)) else (out_shape,)
    ospecs = out_specs if isinstance(out_specs, (tuple, list)) else (out_specs,)
    est = 0
    for spec, arr in list(zip(in_specs, operands)) + list(zip(ospecs, outs)):
        est += 2 * _nbytes(spec.block_shape, arr.dtype)
    for s in scratch:
        if hasattr(s, "shape") and hasattr(s, "dtype"):
            est += _nbytes(s.shape, s.dtype)
    limit = min(est + VMEM_TEMP_ALLOWANCE, VMEM_V7X_BYTES - 4 * 1024 * 1024)
    if sem is None:
        sem = ("arbitrary",) * len(grid)
    n_in, n_after = len(operands), len(after)
    operands = [pltpu.with_memory_space_constraint(o, pltpu.HBM) for o in operands]
    after = [pltpu.with_memory_space_constraint(o, pltpu.HBM) for o in after]
    in_hbm = [pltpu.HBM(o.shape, o.dtype) for o in outs]
    out_shape = in_hbm if isinstance(out_shape, (tuple, list)) else in_hbm[0]

    def ordered_body(*refs):
        body(*refs[:n_in], *refs[n_in + n_after:])

    return pl.pallas_call(
        ordered_body if n_after else body, name=name, grid=grid,
        in_specs=list(in_specs) + [pl.BlockSpec(memory_space=pl.ANY)] * n_after,
        out_specs=out_specs, out_shape=out_shape, scratch_shapes=list(scratch),
        compiler_params=pltpu.CompilerParams(dimension_semantics=sem, vmem_limit_bytes=int(limit)),
    )(*operands, *after)


_DOT_DIMS = {"nn": (((1,), (0,)), ((), ())), "nt": (((1,), (1,)), ((), ())), "tn": (((0,), (0,)), ((), ()))}


def _operand_spec(layout, tr, tc, cols_per_block, pick, group=None):
    if layout == "plain":
        return pl.BlockSpec((tr, tc), lambda *g: pick(*g)[1:])
    if layout == "blk":
        return pl.BlockSpec((group, tr, tc), lambda *g: pick(*g))
    assert layout == "col"
    if group:
        assert tc == cols_per_block
        return pl.BlockSpec((tr, group * tc), lambda *g: (pick(*g)[1], pick(*g)[0]))
    per = cols_per_block // tc
    return pl.BlockSpec((tr, tc), lambda *g: (pick(*g)[1], pick(*g)[0] * per + pick(*g)[2]))


def _matmul(name, dims, a, b, *, M, N, K, nb=1, a_lay="plain", b_lay="plain", o_lay="plain",
            red_block=False, group=None, out_dtype=F32, res=None, tm=1024, tn=None, after=(), side=()):
    tm = _tile(M, tm, 128 if dims == "tn" else 16)
    tn = _tile(N, tn or (1024 if red_block else 512), 128)
    tk = K
    gi, gj, gk = M // tm, N // tn, K // tk
    if red_block:
        grid = (gi, gj, nb // (group or 1), gk)
        unpack = lambda i, j, bb, k: (bb, i, j, k)
        red_axes, sem = (2, 3), ("parallel", "parallel", "arbitrary", "arbitrary")
    else:
        grid = (nb, gi, gj, gk)
        unpack = lambda bb, i, j, k: (bb, i, j, k)
        red_axes, sem = (3,), ("parallel", "parallel", "parallel", "arbitrary")

    def picker(f):
        return lambda *g: f(*unpack(*g))

    if dims == "tn":
        a_spec = _operand_spec(a_lay, tk, tm, M, picker(lambda bb, i, j, k: (bb, k, i)), group)
    else:
        a_spec = _operand_spec(a_lay, tm, tk, K, picker(lambda bb, i, j, k: (bb, i, k)), group)
    if dims == "nt":
        b_spec = _operand_spec(b_lay, tn, tk, K, picker(lambda bb, i, j, k: (bb, j, k)), group)
    else:
        b_spec = _operand_spec(b_lay, tk, tn, N, picker(lambda bb, i, j, k: (bb, k, j)), group)
    o_spec = _operand_spec(o_lay, tm, tn, N, picker(lambda bb, i, j, k: (bb, i, j)))
    if o_lay == "plain":
        out_shape = jax.ShapeDtypeStruct((M, N), out_dtype)
    elif o_lay == "blk":
        out_shape = jax.ShapeDtypeStruct((nb, M, N), out_dtype)
    else:
        out_shape = jax.ShapeDtypeStruct((M, nb * N), out_dtype)
    n_red = [grid[ax] for ax in red_axes]
    has_res = res is not None
    one_step = all(n == 1 for n in n_red)

    def contract(a_ref, b_ref):
        if group:
            parts = [(a_ref[p] if a_lay == "blk" else a_ref[:, p * K:(p + 1) * K], b_ref[p]) for p in range(group)]
        else:
            parts = [(a_ref[...], b_ref[...])]
        r = None
        for a_part, b_part in parts:
            d = lax.dot_general(a_part, b_part, _DOT_DIMS[dims], preferred_element_type=F32)
            r = d if r is None else r + d
        return r

    def step_index(*g):
        s = g[0]
        for ax in range(1, len(grid)):
            s = s * grid[ax] + g[ax]
        return s

    def side_spec(block):
        shape, index = block
        return pl.BlockSpec(shape, lambda *g: index(jnp.minimum(step_index(*g), SIDE_TILES - 1)))

    asked_to_carry = bool(side)
    if functools.reduce(lambda p, q: p * q, grid) < SIDE_TILES:
        side = ()
    n_main_in = 3 if has_res else 2
    n_side_in = sum(len(job.operands) for job in side)
    n_side_out = sum(len(job.out_shapes) for job in side)

    def side_work(side_in, side_out):
        @pl.when(step_index(*[pl.program_id(ax) for ax in range(len(grid))]) < SIDE_TILES)
        def _():
            i0 = o0 = 0
            for job in side:
                n_i, n_o = len(job.operands), len(job.out_shapes)
                job.body(*side_in[i0:i0 + n_i], *side_out[o0:o0 + n_o])
                i0, o0 = i0 + n_i, o0 + n_o

    def body(*refs):
        a_ref, b_ref = refs[:2]
        side_in = refs[n_main_in:n_main_in + n_side_in]
        o_ref = refs[n_main_in + n_side_in]
        side_out = refs[n_main_in + n_side_in + 1:n_main_in + n_side_in + 1 + n_side_out]
        if one_step:
            r = contract(a_ref, b_ref)
            if has_res:
                r = r + refs[2][...]
            o_ref[...] = r.astype(o_ref.dtype)
        else:
            acc = refs[-1]
            first = functools.reduce(jnp.logical_and, [pl.program_id(ax) == 0 for ax in red_axes])
            last = functools.reduce(jnp.logical_and,
                                    [pl.program_id(ax) == n - 1 for ax, n in zip(red_axes, n_red)])

            @pl.when(first)
            def _():
                acc[...] = jnp.zeros_like(acc)

            acc[...] += contract(a_ref, b_ref)

            @pl.when(last)
            def _():
                r = acc[...]
                if has_res:
                    r = r + refs[2][...]
                o_ref[...] = r.astype(o_ref.dtype)
        if side:
            side_work(side_in, side_out)

    in_specs = [a_spec, b_spec]
    operands = [a, b]
    if has_res:
        in_specs.append(_operand_spec("plain", tm, tn, N, picker(lambda bb, i, j, k: (bb, i, j))))
        operands.append(res)
    out_specs, out_shapes = [o_spec], [out_shape]
    for job in side:
        in_specs += [side_spec(blk) for blk in job.in_blocks]
        operands += job.operands
        out_specs += [side_spec(blk) for blk in job.out_blocks]
        out_shapes += job.out_shapes
    scratch = [] if one_step else [pltpu.VMEM((tm, tn), F32)]
    if not side:
        out = _call(body, name, grid, in_specs, o_spec, out_shape, operands, scratch=scratch, sem=sem, after=after)
        return (out, None) if asked_to_carry else out
    outs = _call(body, name, grid, in_specs, tuple(out_specs), tuple(out_shapes), operands, scratch=scratch,
                 sem=("arbitrary",) * len(grid), after=after)
    results, pos = [], 1
    for job in side:
        results.append(tuple(outs[pos:pos + len(job.out_shapes)]))
        pos += len(job.out_shapes)
    return outs[0], results


def _rows_spec(arr, tr):
    lead = arr.ndim - 2
    return pl.BlockSpec((None,) * lead + (tr, arr.shape[-1]), lambda i: (0,) * lead + (i, 0))


def _cast_bf16(name, w):
    R, C = w.shape[-2:]
    tr = _tile(R, max(8, (3 << 20) // C), 16)

    def body(w_ref, o_ref):
        o_ref[...] = w_ref[...].astype(BF16)

    return _call(body, name, (R // tr,), [_rows_spec(w, tr)],
                 pl.BlockSpec((tr, C), lambda i: (i, 0)), jax.ShapeDtypeStruct((R, C), BF16), [w],
                 sem=("parallel",))


def _rms_fwd(name, x, g, after=()):
    T, D = x.shape
    tm = _tile(T, 128, 16)

    def body(x_ref, g_ref, o_ref):
        xv = x_ref[...]
        r = lax.rsqrt(jnp.mean(xv * xv, axis=-1, keepdims=True) + EPS)
        o_ref[...] = (xv * r * g_ref[...]).astype(BF16)

    return _call(body, name, (T // tm,),
                 [pl.BlockSpec((tm, D), lambda i: (i, 0)), pl.BlockSpec((1, D), lambda i: (0, 0))],
                 pl.BlockSpec((tm, D), lambda i: (i, 0)), jax.ShapeDtypeStruct((T, D), BF16), [x, g],
                 sem=("parallel",), after=after)


def _rms_bwd(name, dxn, x, g, dh=None, after=()):
    T, D = x.shape
    tm = _tile(T, 128, 16)
    with_dx = dh is not None

    def body(*refs):
        if with_dx:
            dxn_ref, x_ref, g_ref, dh_ref, o_ref, ob_ref, dg_ref = refs
        else:
            dxn_ref, x_ref, g_ref, dg_ref = refs
        xv = x_ref[...]
        r = lax.rsqrt(jnp.mean(xv * xv, axis=-1, keepdims=True) + EPS)
        xh = xv * r
        dy = dxn_ref[...].astype(F32)

        @pl.when(pl.program_id(0) == 0)
        def _():
            dg_ref[...] = jnp.zeros_like(dg_ref)

        dg_ref[...] += jnp.sum(dy * xh, axis=0, keepdims=True)
        if with_dx:
            dyg = dy * g_ref[...]
            tot = dh_ref[...] + r * (dyg - xh * jnp.mean(dyg * xh, axis=-1, keepdims=True))
            o_ref[...] = tot
            ob_ref[...] = tot.astype(BF16)

    row = pl.BlockSpec((tm, D), lambda i: (i, 0))
    vec = pl.BlockSpec((1, D), lambda i: (0, 0))
    if with_dx:
        return _call(body, name, (T // tm,), [row, row, vec, row], (row, row, vec),
                     (jax.ShapeDtypeStruct((T, D), F32), jax.ShapeDtypeStruct((T, D), BF16),
                      jax.ShapeDtypeStruct((1, D), F32)), [dxn, x, g, dh], after=after)
    return _call(body, name, (T // tm,), [row, row, vec], vec, jax.ShapeDtypeStruct((1, D), F32), [dxn, x, g])


def _loss_head(h, target, g):
    T, D = h.shape
    tm = _tile(T, 128, 16)

    def body(h_ref, t_ref, g_ref, o_ref, ob_ref, loss_ref, dg_ref):
        xv = h_ref[...]
        gv = g_ref[...]
        r = lax.rsqrt(jnp.mean(xv * xv, axis=-1, keepdims=True) + EPS)
        xh = xv * r
        e = xh * gv - t_ref[...]

        @pl.when(pl.program_id(0) == 0)
        def _():
            dg_ref[...] = jnp.zeros_like(dg_ref)
            loss_ref[...] = jnp.zeros_like(loss_ref)

        loss_ref[...] += 0.5 * jnp.sum(jnp.mean(e * e, axis=-1, keepdims=True), axis=0, keepdims=True)
        dy = e * (1.0 / D)
        dg_ref[...] += jnp.sum(dy * xh, axis=0, keepdims=True)
        dyg = dy * gv
        dx = r * (dyg - xh * jnp.mean(dyg * xh, axis=-1, keepdims=True))
        o_ref[...] = dx
        ob_ref[...] = dx.astype(BF16)

    row = pl.BlockSpec((tm, D), lambda i: (i, 0))
    vec = pl.BlockSpec((1, D), lambda i: (0, 0))
    return _call(body, "loss_head", (T // tm,), [row, row, vec],
                 (row, row, pl.BlockSpec((1, 128), lambda i: (0, 0)), vec),
                 (jax.ShapeDtypeStruct((T, D), F32), jax.ShapeDtypeStruct((T, D), BF16),
                  jax.ShapeDtypeStruct((1, 128), F32), jax.ShapeDtypeStruct((1, D), F32)), [h, target, g])


ROW_CHUNK = 64
SUBLANES = 8


def _col_chunks(width):
    return [slice(c0, min(c0 + GROUP_DIM, width)) for c0 in range(0, width, GROUP_DIM)]


def _row_chunks(n_rows):
    return [(r0, min(ROW_CHUNK, n_rows - r0)) for r0 in range(0, n_rows, ROW_CHUNK)]


def _pad_rows(K):
    return -(-(K - 1) // SUBLANES) * SUBLANES


def _shifted_back(buf, K, r0, nr, cs):
    pad = _pad_rows(K)
    win = buf[pl.ds(HALO + r0 - pad, nr + pad), cs]
    for b in range(min(SUBLANES, K)):
        rolled = win if b == 0 else pltpu.roll(win, b, axis=0)
        for a in range((K - 1 - b) // SUBLANES + 1):
            yield K - 1 - (SUBLANES * a + b), rolled[pad - SUBLANES * a:pad - SUBLANES * a + nr]


def _conv_fwd(buf, w_ref, K, r0, nr, cs):
    y = None
    for k, xs in _shifted_back(buf, K, r0, nr, cs):
        term = xs * w_ref[pl.ds(k, 1), cs]
        y = term if y is None else y + term
    return y


def _conv_bwd_input(buf, w_ref, K, r0, nr, cs):
    pad = _pad_rows(K)
    win = buf[pl.ds(r0, nr + pad), cs]
    dx = None
    for b in range(min(SUBLANES, K)):
        rolled = win if b == 0 else pltpu.roll(win, nr + pad - b, axis=0)
        for a in range((K - 1 - b) // SUBLANES + 1):
            k = K - 1 - (SUBLANES * a + b)
            term = rolled[SUBLANES * a:SUBLANES * a + nr] * w_ref[pl.ds(k, 1), cs]
            dx = term if dx is None else dx + term
    return dx


def _fold_rows(v):
    nr, lanes = v.shape
    if nr % SUBLANES:
        return jnp.sum(v, axis=0, keepdims=True)
    return jnp.sum(v.reshape(nr // SUBLANES, SUBLANES, lanes), axis=0)


def _conv_bwd_weight(accs, dy, buf, K, r0, nr, cs):
    accs = list(accs)
    for k, xs in _shifted_back(buf, K, r0, nr, cs):
        accs[k] = accs[k] + _fold_rows(dy * xs)
    return accs


def _add_row(ref, row, cs, acc):
    ref[pl.ds(row, 1), cs] += jnp.sum(acc, axis=0, keepdims=True)


def _sigmoid(z):
    return 0.5 * jnp.tanh(0.5 * z) + 0.5


def _silu_grad(z, sig):
    return sig * (1.0 + z * (1.0 - sig))


def _group_norm(xg):
    xc = xg - jnp.mean(xg, axis=-1, keepdims=True)
    rstd = lax.rsqrt(jnp.mean(xc * xc, axis=-1, keepdims=True) + EPS)
    return xc * rstd, rstd


def _mixer_tiles(T, CW):
    tm = _tile(T, 512, HALO)
    tc = _tile(CW, 256, GROUP_DIM)
    return tm, tc, tm // HALO, CW // tc


def _mixer_fwd(proj, caw, cab, lng, lnb, cbw, T, CW):
    KA, KB = caw.shape[0], cbw.shape[0]
    tm, tc, hb, nc = _mixer_tiles(T, CW)

    def sec(s):
        return pl.BlockSpec((tm, tc), lambda i, c: (i, s * nc + c))

    def sec_prev(s):
        return pl.BlockSpec((HALO, tc), lambda i, c: (jnp.maximum(i * hb - 1, 0), s * nc + c))

    def chan(rows):
        return pl.BlockSpec((rows, tc), lambda i, c: (0, c))

    def body(av, ag, bg, cg, bh, avh, agh, cgh, bhh, caw_ref, cab_ref, lng_ref, lnb_ref, cbw_ref,
             mix_ref, u1_ref, bufa, bufb):
        first = pl.program_id(0) == 0
        bufa[pl.ds(0, HALO), :] = jnp.where(first, 0.0, avh[...].astype(F32) * _sigmoid(agh[...].astype(F32)))
        bufb[pl.ds(0, HALO), :] = jnp.where(first, 0.0, cgh[...].astype(F32) * bhh[...].astype(F32))
        for cs in _col_chunks(tc):
            for r0, nr in _row_chunks(tm):
                rows = pl.ds(r0, nr)
                bufa[pl.ds(HALO + r0, nr), cs] = av[rows, cs].astype(F32) * _sigmoid(ag[rows, cs].astype(F32))
                bufb[pl.ds(HALO + r0, nr), cs] = cg[rows, cs].astype(F32) * bh[rows, cs].astype(F32)
        for cs in _col_chunks(tc):
            for r0, nr in _row_chunks(tm):
                rows = pl.ds(r0, nr)
                u1 = _conv_fwd(bufa, caw_ref, KA, r0, nr, cs) + cab_ref[:, cs]
                u1_ref[rows, cs] = u1
                y, _ = _group_norm(u1)
                z = y * lng_ref[:, cs] + lnb_ref[:, cs]
                mix_ref[0, rows, cs] = (z * _sigmoid(z)).astype(BF16)
                mix_ref[1, rows, cs] = (bg[rows, cs].astype(F32) * _conv_fwd(bufb, cbw_ref, KB, r0, nr, cs)).astype(BF16)

    in_specs = [sec(0), sec(1), sec(2), sec(3), sec(4), sec_prev(0), sec_prev(1), sec_prev(3), sec_prev(4),
                chan(KA), chan(1), chan(1), chan(1), chan(KB)]
    operands = [proj] * 9 + [caw, cab, lng, lnb, cbw]
    return _call(body, "mixer_fwd", (T // tm, nc), in_specs,
                 (pl.BlockSpec((2, tm, tc), lambda i, c: (0, i, c)), pl.BlockSpec((tm, tc), lambda i, c: (i, c))),
                 (jax.ShapeDtypeStruct((2, T, CW), BF16), jax.ShapeDtypeStruct((T, CW), F32)), operands,
                 scratch=[pltpu.VMEM((HALO + tm, tc), F32), pltpu.VMEM((HALO + tm, tc), F32)],
                 sem=("parallel", "parallel"))


def _mixer_bwd1(dmix, proj, u1, caw, lng, lnb, cbw, T, CW):
    KA, KB = caw.shape[0], cbw.shape[0]
    tm, tc, hb, nc = _mixer_tiles(T, CW)

    def sec(s):
        return pl.BlockSpec((tm, tc), lambda c, i: (i, s * nc + c))

    def sec_prev(s):
        return pl.BlockSpec((HALO, tc), lambda c, i: (jnp.maximum(i * hb - 1, 0), s * nc + c))

    def chan(rows):
        return pl.BlockSpec((rows, tc), lambda c, i: (0, c))

    tile = pl.BlockSpec((tm, tc), lambda c, i: (i, c))

    def body(du, dv, u1_ref, av, ag, bg, cg, bh, avh, agh, cgh, bhh, lng_ref, lnb_ref, cbw_ref,
             du1_ref, dcv_ref, dbg_ref, dcaw_ref, dcab_ref, dlng_ref, dlnb_ref, dcbw_ref, bufa, bufb):
        first = pl.program_id(1) == 0

        @pl.when(first)
        def _():
            for r in (dcaw_ref, dcab_ref, dlng_ref, dlnb_ref, dcbw_ref):
                r[...] = jnp.zeros_like(r)

        bufa[pl.ds(0, HALO), :] = jnp.where(first, 0.0, avh[...].astype(F32) * _sigmoid(agh[...].astype(F32)))
        bufb[pl.ds(0, HALO), :] = jnp.where(first, 0.0, cgh[...].astype(F32) * bhh[...].astype(F32))
        for cs in _col_chunks(tc):
            for r0, nr in _row_chunks(tm):
                rows = pl.ds(r0, nr)
                bufa[pl.ds(HALO + r0, nr), cs] = av[rows, cs].astype(F32) * _sigmoid(ag[rows, cs].astype(F32))
                bufb[pl.ds(HALO + r0, nr), cs] = cg[rows, cs].astype(F32) * bh[rows, cs].astype(F32)
        for cs in _col_chunks(tc):
            lanes = cs.stop - cs.start
            zero = jnp.zeros((SUBLANES, lanes), F32)
            a_lng, a_lnb, a_cab = zero, zero, zero
            a_caw, a_cbw = [zero] * KA, [zero] * KB
            gamma, beta = lng_ref[:, cs], lnb_ref[:, cs]
            for r0, nr in _row_chunks(tm):
                rows = pl.ds(r0, nr)
                y, rstd = _group_norm(u1_ref[rows, cs])
                z = y * gamma + beta
                dz = du[rows, cs].astype(F32) * _silu_grad(z, _sigmoid(z))
                a_lng = a_lng + _fold_rows(dz * y)
                a_lnb = a_lnb + _fold_rows(dz)
                dy = dz * gamma
                du1 = rstd * (dy - jnp.mean(dy, axis=-1, keepdims=True)
                              - y * jnp.mean(dy * y, axis=-1, keepdims=True))
                du1_ref[rows, cs] = du1
                a_cab = a_cab + _fold_rows(du1)
                a_caw = _conv_bwd_weight(a_caw, du1, bufa, KA, r0, nr, cs)

                dvv = dv[rows, cs].astype(F32)
                dbg_ref[rows, cs] = (dvv * _conv_fwd(bufb, cbw_ref, KB, r0, nr, cs)).astype(BF16)
                dcv = dvv * bg[rows, cs].astype(F32)
                dcv_ref[rows, cs] = dcv
                a_cbw = _conv_bwd_weight(a_cbw, dcv, bufb, KB, r0, nr, cs)
            _add_row(dlng_ref, 0, cs, a_lng)
            _add_row(dlnb_ref, 0, cs, a_lnb)
            _add_row(dcab_ref, 0, cs, a_cab)
            for k in range(KA):
                _add_row(dcaw_ref, k, cs, a_caw[k])
            for k in range(KB):
                _add_row(dcbw_ref, k, cs, a_cbw[k])

    in_specs = [sec(0), sec(1), tile, sec(0), sec(1), sec(2), sec(3), sec(4),
                sec_prev(0), sec_prev(1), sec_prev(3), sec_prev(4), chan(1), chan(1), chan(KB)]
    operands = [dmix, dmix, u1] + [proj] * 9 + [lng, lnb, cbw]
    return _call(body, "mixer_bwd1", (nc, T // tm), in_specs,
                 (tile, tile, tile, chan(KA), chan(1), chan(1), chan(1), chan(KB)),
                 (jax.ShapeDtypeStruct((T, CW), F32), jax.ShapeDtypeStruct((T, CW), F32),
                  jax.ShapeDtypeStruct((T, CW), BF16), jax.ShapeDtypeStruct((KA, CW), F32),
                  jax.ShapeDtypeStruct((1, CW), F32), jax.ShapeDtypeStruct((1, CW), F32),
                  jax.ShapeDtypeStruct((1, CW), F32), jax.ShapeDtypeStruct((KB, CW), F32)), operands,
                 scratch=[pltpu.VMEM((HALO + tm, tc), F32), pltpu.VMEM((HALO + tm, tc), F32)],
                 sem=("parallel", "arbitrary"))


def _mixer_bwd2(du1, dcv, proj, caw, cbw, T, CW):
    KA, KB = caw.shape[0], cbw.shape[0]
    tm, tc, hb, nc = _mixer_tiles(T, CW)
    n_i = T // tm

    def sec(s):
        return pl.BlockSpec((tm, tc), lambda i, c: (i, s * nc + c))

    def chan(rows):
        return pl.BlockSpec((rows, tc), lambda i, c: (0, c))

    tile = pl.BlockSpec((tm, tc), lambda i, c: (i, c))
    nxt = pl.BlockSpec((HALO, tc), lambda i, c: (jnp.minimum((i + 1) * hb, n_i * hb - 1), c))

    def body(du1_ref, du1n, dcv_ref, dcvn, av, ag, cg, bh, caw_ref, cbw_ref, dav, dag, dcg, dbh, bufa, bufb):
        last = pl.program_id(0) == n_i - 1
        bufa[pl.ds(0, tm), :] = du1_ref[...]
        bufa[pl.ds(tm, HALO), :] = jnp.where(last, 0.0, du1n[...])
        bufb[pl.ds(0, tm), :] = dcv_ref[...]
        bufb[pl.ds(tm, HALO), :] = jnp.where(last, 0.0, dcvn[...])
        for cs in _col_chunks(tc):
            for r0, nr in _row_chunks(tm):
                rows = pl.ds(r0, nr)
                du0 = _conv_bwd_input(bufa, caw_ref, KA, r0, nr, cs)
                sig = _sigmoid(ag[rows, cs].astype(F32))
                dav[rows, cs] = (du0 * sig).astype(BF16)
                dag[rows, cs] = (du0 * av[rows, cs].astype(F32) * (sig * (1.0 - sig))).astype(BF16)
                dch = _conv_bwd_input(bufb, cbw_ref, KB, r0, nr, cs)
                dcg[rows, cs] = (dch * bh[rows, cs].astype(F32)).astype(BF16)
                dbh[rows, cs] = (dch * cg[rows, cs].astype(F32)).astype(BF16)

    in_specs = [tile, nxt, tile, nxt, sec(0), sec(1), sec(3), sec(4), chan(KA), chan(KB)]
    operands = [du1, du1, dcv, dcv, proj, proj, proj, proj, caw, cbw]
    out = jax.ShapeDtypeStruct((T, CW), BF16)
    return _call(body, "mixer_bwd2", (n_i, nc), in_specs, (tile, tile, tile, tile), (out, out, out, out),
                 operands, scratch=[pltpu.VMEM((HALO + tm, tc), F32), pltpu.VMEM((HALO + tm, tc), F32)],
                 sem=("parallel", "parallel"))


def _ffn_tiles(T):
    tm = _tile(T, 512, HALO)
    return tm, tm // HALO, T // tm


def _ffn_act_fwd(gpre, up, cfw):
    nb, T, F = gpre.shape
    KF = cfw.shape[1]
    tm, hb, n_i = _ffn_tiles(T)
    tile = pl.BlockSpec((None, tm, F), lambda b, i: (b, i, 0))
    prev = pl.BlockSpec((None, HALO, F), lambda b, i: (b, jnp.maximum(i * hb - 1, 0), 0))
    wspec = pl.BlockSpec((None, KF, F), lambda b, i: (b, 0, 0))

    def body(g_ref, gh_ref, up_ref, w_ref, f_ref, buf):
        buf[pl.ds(HALO, tm), :] = g_ref[...].astype(F32)
        buf[pl.ds(0, HALO), :] = jnp.where(pl.program_id(1) == 0, 0.0, gh_ref[...].astype(F32))
        for cs in _col_chunks(F):
            for r0, nr in _row_chunks(tm):
                rows = pl.ds(r0, nr)
                g = _conv_fwd(buf, w_ref, KF, r0, nr, cs)
                f_ref[rows, cs] = (g * _sigmoid(g) * up_ref[rows, cs].astype(F32)).astype(BF16)

    return _call(body, "ffn_act_fwd", (nb, n_i), [tile, prev, tile, wspec], tile,
                 jax.ShapeDtypeStruct((nb, T, F), BF16), [gpre, gpre, up, cfw],
                 scratch=[pltpu.VMEM((HALO + tm, F), F32)], sem=("parallel", "parallel"))


def _ffn_act_bwd(df, gpre, up, cfw):
    nb, T, F = gpre.shape
    KF = cfw.shape[1]
    tm, hb, n_i = _ffn_tiles(T)
    extra = 2 * SUBLANES
    assert KF - 1 <= SUBLANES and HALO >= extra
    tile = pl.BlockSpec((None, tm, F), lambda b, i: (b, i, 0))
    prev = pl.BlockSpec((None, HALO, F), lambda b, i: (b, jnp.maximum(i * hb - 1, 0), 0))
    nxt = pl.BlockSpec((None, HALO, F), lambda b, i: (b, jnp.minimum((i + 1) * hb, n_i * hb - 1), 0))
    wspec = pl.BlockSpec((None, KF, F), lambda b, i: (b, 0, 0))

    def body(df_ref, g_ref, gh_ref, up_ref, dfn_ref, gn_ref, upn_ref, w_ref, dgpre_ref, dup_ref, dw_ref, gbuf, dgbuf):
        first = pl.program_id(1) == 0
        last = pl.program_id(1) == n_i - 1

        @pl.when(first)
        def _():
            dw_ref[...] = jnp.zeros_like(dw_ref)

        gbuf[pl.ds(0, HALO), :] = jnp.where(first, 0.0, gh_ref[...].astype(F32))
        gbuf[pl.ds(HALO, tm), :] = g_ref[...].astype(F32)
        gbuf[pl.ds(HALO + tm, HALO), :] = gn_ref[...].astype(F32)
        for cs in _col_chunks(F):
            accs = [jnp.zeros((SUBLANES, cs.stop - cs.start), F32)] * KF
            for r0, nr in _row_chunks(tm):
                rows = pl.ds(r0, nr)
                g = _conv_fwd(gbuf, w_ref, KF, r0, nr, cs)
                sig = _sigmoid(g)
                dfv = df_ref[rows, cs].astype(F32)
                dup_ref[rows, cs] = (dfv * (g * sig)).astype(BF16)
                dg = dfv * up_ref[rows, cs].astype(F32) * _silu_grad(g, sig)
                dgbuf[rows, cs] = dg
                accs = _conv_bwd_weight(accs, dg, gbuf, KF, r0, nr, cs)
            for k in range(KF):
                _add_row(dw_ref, k, cs, accs[k])
            g = _conv_fwd(gbuf, w_ref, KF, tm, extra, cs)
            dg_next = (dfn_ref[pl.ds(0, extra), cs].astype(F32) * upn_ref[pl.ds(0, extra), cs].astype(F32)
                       * _silu_grad(g, _sigmoid(g)))
            dgbuf[pl.ds(tm, extra), cs] = jnp.where(last, 0.0, dg_next)
        for cs in _col_chunks(F):
            for r0, nr in _row_chunks(tm):
                dgpre_ref[pl.ds(r0, nr), cs] = _conv_bwd_input(dgbuf, w_ref, KF, r0, nr, cs).astype(BF16)

    out = jax.ShapeDtypeStruct((nb, T, F), BF16)
    return _call(body, "ffn_act_bwd", (nb, n_i), [tile, tile, prev, tile, nxt, nxt, nxt, wspec], (tile, tile, wspec),
                 (out, out, jax.ShapeDtypeStruct((nb, KF, F), F32)), [df, gpre, gpre, up, df, gpre, up, cfw],
                 scratch=[pltpu.VMEM((2 * HALO + tm, F), F32), pltpu.VMEM((tm + extra, F), F32)],
                 sem=("parallel", "arbitrary"))


def _softmax_rows(s):
    e = jnp.exp(s - jnp.max(s, axis=-1, keepdims=True))
    return e / jnp.sum(e, axis=-1, keepdims=True)


def _attn_fwd(q, k, v):
    T, D = q.shape
    Mm = k.shape[0]
    hd = D // N_XATTN_HEADS
    scale = hd ** -0.5
    tm = _tile(T, 256, 16)

    def body(q_ref, k_ref, v_ref, o_ref):
        for h in range(N_XATTN_HEADS):
            sl = slice(h * hd, (h + 1) * hd)
            s = lax.dot_general(q_ref[:, sl], k_ref[:, sl], _DOT_DIMS["nt"], preferred_element_type=F32) * scale
            p = _softmax_rows(s).astype(BF16)
            o_ref[:, sl] = jnp.dot(p, v_ref[:, sl], preferred_element_type=F32).astype(BF16)

    row = pl.BlockSpec((tm, D), lambda i: (i, 0))
    full = pl.BlockSpec((Mm, D), lambda i: (0, 0))
    return _call(body, "attn_fwd", (T // tm,), [row, full, full], row, jax.ShapeDtypeStruct((T, D), BF16),
                 [q, k, v], sem=("parallel",))


def _attn_bwd(q, k, v, do):
    T, D = q.shape
    Mm = k.shape[0]
    hd = D // N_XATTN_HEADS
    scale = hd ** -0.5
    tm = _tile(T, 256, 16)
    n_i = T // tm

    def body(q_ref, do_ref, k_ref, v_ref, dq_ref, dk_ref, dv_ref, dk_acc, dv_acc):
        @pl.when(pl.program_id(0) == 0)
        def _():
            dk_acc[...] = jnp.zeros_like(dk_acc)
            dv_acc[...] = jnp.zeros_like(dv_acc)

        for h in range(N_XATTN_HEADS):
            sl = slice(h * hd, (h + 1) * hd)
            qh, kh, doh = q_ref[:, sl], k_ref[:, sl], do_ref[:, sl]
            s = lax.dot_general(qh, kh, _DOT_DIMS["nt"], preferred_element_type=F32) * scale
            p = _softmax_rows(s)
            dv_acc[:, sl] += lax.dot_general(p.astype(BF16), doh, _DOT_DIMS["tn"], preferred_element_type=F32)
            dp = lax.dot_general(doh, v_ref[:, sl], _DOT_DIMS["nt"], preferred_element_type=F32)
            ds = (p * (dp - jnp.sum(dp * p, axis=-1, keepdims=True)) * scale).astype(BF16)
            dq_ref[:, sl] = jnp.dot(ds, kh, preferred_element_type=F32).astype(BF16)
            dk_acc[:, sl] += lax.dot_general(ds, qh, _DOT_DIMS["tn"], preferred_element_type=F32)

        @pl.when(pl.program_id(0) == n_i - 1)
        def _():
            dk_ref[...] = dk_acc[...].astype(BF16)
            dv_ref[...] = dv_acc[...].astype(BF16)

    row = pl.BlockSpec((tm, D), lambda i: (i, 0))
    full = pl.BlockSpec((Mm, D), lambda i: (0, 0))
    return _call(body, "attn_bwd", (n_i,), [row, row, full, full], (row, full, full),
                 (jax.ShapeDtypeStruct((T, D), BF16), jax.ShapeDtypeStruct((Mm, D), BF16),
                  jax.ShapeDtypeStruct((Mm, D), BF16)), [q, do, k, v],
                 scratch=[pltpu.VMEM((Mm, D), F32), pltpu.VMEM((Mm, D), F32)])


def _position():
    x, y, c = lax.axis_index("x"), lax.axis_index("y"), lax.axis_index("c")
    return x, y, c


def _peer(pos, k):
    x, y, c = pos
    return (1 - x if k & 4 else x, 1 - y if k & 2 else y, 1 - c if k & 1 else c)


def _index(pos):
    x, y, c = pos
    return 4 * x + 2 * y + c


def _sequencer_kernel(body, name, collective_id, out_type, operands):
    return pl.kernel(
        body, name=name, out_type=out_type,
        mesh=plsc.ScalarSubcoreMesh(axis_name="sequencer", num_cores=1),
        scratch_types=[pltpu.SemaphoreType.DMA, pltpu.SemaphoreType.DMA((7,)), pltpu.SemaphoreType.DMA],
        compiler_params=pltpu.CompilerParams(collective_id=collective_id),
    )(*operands)


def _handshake(peers):
    barrier = pltpu.get_barrier_semaphore()
    for peer in peers:
        pl.semaphore_signal(barrier, inc=1, device_id=peer, device_id_type=MESH)
    pl.semaphore_wait(barrier, len(peers))


def _sequencer_all_gather(name, collective_id, shards):
    n = len(shards)

    def body(*refs):
        x_refs, out_refs = refs[:n], refs[n:2 * n]
        send_sem, recv_sems, local_sem = refs[2 * n:]
        me = _position()
        x, y, c = me
        sibling = _peer(me, 1)
        first = (x + (1 - c) - 2 * x * (1 - c), y + c - 2 * y * c, c)
        second = (x + c - 2 * x * c, y + (1 - c) - 2 * y * (1 - c), c)
        diagonal = _peer(me, 6)
        _handshake([sibling, first, second])

        def copy(a, k, block, to, own=False):
            dst = out_refs[a].at[_index(block)]
            return pltpu.make_async_remote_copy(
                src_ref=x_refs[a] if own else dst, dst_ref=dst, send_sem=send_sem, recv_sem=recv_sems.at[k],
                device_id=to, device_id_type=MESH)

        local = [pltpu.make_async_copy(x_refs[a], out_refs[a].at[_index(me)], local_sem) for a in range(n)]
        started = [copy(a, 1 + j, me, peer, own=True) for a in range(n) for j, peer in enumerate((first, second))]
        started += [copy(a, 0, me, sibling, own=True) for a in range(n)]
        for cp in started + local:
            cp.start()
        for k, origin in ((1, first), (2, second), (3, diagonal)):
            for a in range(n):
                copy(a, k, origin, me).wait_recv()
            passed = [copy(a, 3 + k, origin, sibling) for a in range(n)]
            if k == 1:
                passed = [copy(a, 3, origin, second) for a in range(n)] + passed
            for cp in passed:
                cp.start()
            started += passed
        for k in (0, 4, 5, 6):
            for a in range(n):
                copy(a, k, sibling, me).wait_recv()
        for cp in started:
            cp.wait_send()
        for cp in local:
            cp.wait()

    return _sequencer_kernel(body, name, collective_id,
                             [jax.ShapeDtypeStruct((N_DEV,) + s.shape, s.dtype) for s in shards], shards)


def _chip_index(pos):
    return 2 * pos[0] + pos[1]


def _sequencer_to_sibling(name, collective_id, parts):
    n = len(parts)

    def body(*refs):
        p_refs, out_refs = refs[:n], refs[n:2 * n]
        send_sem, recv_sems, _ = refs[2 * n:]
        me = _position()
        sibling = _peer(me, 1)
        _handshake([sibling])
        copies = [pltpu.make_async_remote_copy(
            src_ref=p_refs[a].at[2 * q + sibling[2]], dst_ref=out_refs[a].at[q], send_sem=send_sem,
            recv_sem=recv_sems.at[0], device_id=sibling, device_id_type=MESH)
            for a in range(n) for q in range(N_DEV // 2)]
        for cp in copies:
            cp.start()
        for cp in copies:
            cp.wait_recv()
        for cp in copies:
            cp.wait_send()

    return _sequencer_kernel(body, name, collective_id,
                             [jax.ShapeDtypeStruct((N_DEV // 2,) + p.shape[1:], p.dtype) for p in parts], parts)


def _sequencer_to_chips(name, collective_id, sums):
    n = len(sums)

    def body(*refs):
        s_refs, out_refs = refs[:n], refs[n:2 * n]
        send_sem, recv_sems, local_sem = refs[2 * n:]
        me = _position()
        my_chip = _chip_index(me)
        peers = [_peer(me, 4), _peer(me, 2), _peer(me, 6)]
        _handshake(peers)
        local = [pltpu.make_async_copy(s_refs[a].at[my_chip], out_refs[a].at[my_chip], local_sem) for a in range(n)]
        sends = [pltpu.make_async_remote_copy(
            src_ref=s_refs[a].at[_chip_index(peer)], dst_ref=out_refs[a].at[my_chip], send_sem=send_sem,
            recv_sem=recv_sems.at[1 + j], device_id=peer, device_id_type=MESH)
            for a in range(n) for j, peer in enumerate(peers)]
        for cp in sends + local:
            cp.start()
        for j, peer in enumerate(peers):
            for a in range(n):
                pltpu.make_async_remote_copy(
                    src_ref=s_refs[a].at[my_chip], dst_ref=out_refs[a].at[_chip_index(peer)], send_sem=send_sem,
                    recv_sem=recv_sems.at[1 + j], device_id=peer, device_id_type=MESH).wait_recv()
        for cp in sends:
            cp.wait_send()
        for cp in local:
            cp.wait()

    return _sequencer_kernel(body, name, collective_id,
                             [jax.ShapeDtypeStruct(s.shape, s.dtype) for s in sums], sums)


def _chip_sum(name, parts, got, after=()):
    _, R, C = parts.shape
    tr = _tile(R, max(8, (3 << 20) // C), 16)
    n_after = len(after)
    limit = 6 * _nbytes((tr, C), parts.dtype) + VMEM_TEMP_ALLOWANCE

    def body(c_ref, p_ref, g_ref, *rest):
        o_ref = rest[n_after]
        o_ref[...] = (p_ref[...].astype(F32) + g_ref[...].astype(F32)).astype(o_ref.dtype)

    blk = pl.BlockSpec((None, tr, C), lambda q, i, c_ref: (q, i, 0))
    mine = pl.BlockSpec((None, tr, C), lambda q, i, c_ref: (2 * q + c_ref[0], i, 0))
    core = lax.axis_index("c").astype(jnp.int32).reshape(1)
    parts, got = [pltpu.with_memory_space_constraint(o, pltpu.HBM) for o in (parts, got)]
    after = [pltpu.with_memory_space_constraint(o, pltpu.HBM) for o in after]
    return pl.pallas_call(
        body, name=name, out_shape=pltpu.HBM((N_DEV // 2, R, C), parts.dtype),
        grid_spec=pltpu.PrefetchScalarGridSpec(
            num_scalar_prefetch=1, grid=(N_DEV // 2, R // tr),
            in_specs=[mine, blk] + [pl.BlockSpec(memory_space=pl.ANY)] * n_after, out_specs=blk),
        compiler_params=pltpu.CompilerParams(dimension_semantics=("parallel", "parallel"),
                                             vmem_limit_bytes=int(limit)),
    )(core, parts, got, *after)


def _all_reduce_rows(name, v):
    R, C = v.shape

    def body(v_ref, out_ref, gath, send_sems, recv_sems):
        me = _position()
        gath[_index(me)] = v_ref[...]
        sends = []
        for k in range(1, N_DEV):
            peer = _peer(me, k)
            sends.append(pltpu.make_async_remote_copy(
                src_ref=v_ref, dst_ref=gath.at[_index(me)], send_sem=send_sems.at[k - 1],
                recv_sem=recv_sems.at[k - 1], device_id=peer, device_id_type=MESH))
        for cp in sends:
            cp.start()
        for k in range(1, N_DEV):
            peer = _peer(me, k)
            pltpu.make_async_remote_copy(
                src_ref=v_ref, dst_ref=gath.at[_index(peer)], send_sem=send_sems.at[k - 1],
                recv_sem=recv_sems.at[k - 1], device_id=peer, device_id_type=MESH).wait_recv()
        for cp in sends:
            cp.wait_send()
        tot = gath[0]
        for s in range(1, N_DEV):
            tot = tot + gath[s]
        out_ref[...] = tot

    return pl.pallas_call(
        body, name=name, out_shape=jax.ShapeDtypeStruct((R, C), F32),
        in_specs=[VMEM_SPEC], out_specs=VMEM_SPEC,
        scratch_shapes=[pltpu.VMEM((N_DEV, R, C), F32), pltpu.SemaphoreType.DMA((7,)),
                        pltpu.SemaphoreType.DMA((7,))],
    )(v)


def _adamw_math(g, w, m, v):
    m = ADAM_B1 * m + (1.0 - ADAM_B1) * g
    v = ADAM_B2 * v + (1.0 - ADAM_B2) * (g * g)
    m_hat = m / (1.0 - ADAM_B1 ** ADAM_STEP)
    v_hat = v / (1.0 - ADAM_B2 ** ADAM_STEP)
    delta = -ADAM_LR * (m_hat / (jnp.sqrt(v_hat) + ADAM_EPS) + ADAM_WD * w)
    return delta, m, v


def _adamw_tile(n):
    def body(p_ref, w_ref, m_ref, v_ref, g_ref, d_ref, nm_ref, nv_ref):
        g = p_ref[0].astype(F32)
        for s in range(1, n):
            g = g + p_ref[s].astype(F32)
        g_ref[...] = g
        d_ref[...], nm_ref[...], nv_ref[...] = _adamw_math(g, w_ref[...], m_ref[...], v_ref[...])
    return body


class _SideJob(NamedTuple):
    operands: list
    in_blocks: list
    out_blocks: list
    out_shapes: list
    body: object


def _adamw_job(parts, w, m, v):
    n, R, C = parts.shape
    lead = w.ndim - 2
    if R % (SIDE_TILES * 16) == 0:
        tile, index = (R // SIDE_TILES, C), lambda t: (t, 0)
    elif C % (SIDE_TILES * 128) == 0:
        tile, index = (R, C // SIDE_TILES), lambda t: (0, t)
    else:
        return None
    p_blk = ((n,) + tile, lambda t: (0,) + index(t))
    w_blk = ((None,) * lead + tile, lambda t: (0,) * lead + index(t))
    return _SideJob([parts, w, m, v], [p_blk, w_blk, w_blk, w_blk], [w_blk] * 4,
                    [jax.ShapeDtypeStruct(w.shape, F32)] * 4, _adamw_tile(n))


def _adamw(name, parts, w, m, v, after=()):
    n, R, C = parts.shape
    tr = _tile(R, max(8, (1 << 18) // C), 16)
    blk = _rows_spec(w, tr)
    out = jax.ShapeDtypeStruct(w.shape, F32)
    return _call(_adamw_tile(n), name, (R // tr,), [pl.BlockSpec((n, tr, C), lambda i: (0, i, 0)), blk, blk, blk],
                 (blk, blk, blk, blk), (out, out, out, out), [parts, w, m, v], sem=("parallel",), after=after)


def kernel(x, mem, g_mix, w_in, conv_a_w, conv_a_b, ln_a_g, ln_a_b, conv_b_w, w_out, g_xattn, g_mem, w_q, w_k, w_v, w_o, g_ffn, w_gate, w_up, conv_f_w, w_down, g_final, loss_target, m_g_mix, m_w_in, m_conv_a_w, m_conv_a_b, m_ln_a_g, m_ln_a_b, m_conv_b_w, m_w_out, m_g_xattn, m_g_mem, m_w_q, m_w_k, m_w_v, m_w_o, m_g_ffn, m_w_gate, m_w_up, m_conv_f_w, m_w_down, m_g_final, v_g_mix, v_w_in, v_conv_a_w, v_conv_a_b, v_ln_a_g, v_ln_a_b, v_conv_b_w, v_w_out, v_g_xattn, v_g_mem, v_w_q, v_w_k, v_w_v, v_w_o, v_g_ffn, v_w_gate, v_w_up, v_conv_f_w, v_w_down, v_g_final):
    T, D = x.shape[1], x.shape[2]
    Mm = mem.shape[1]
    CW = conv_a_b.shape[1]
    INB = w_in.shape[2]
    FB = w_gate.shape[2]
    KA, KB, KF = conv_a_w.shape[1], conv_b_w.shape[1], conv_f_w.shape[1]
    DB = D // N_DEV
    assert 5 * CW == N_DEV * INB and 2 * CW == D

    x2, mem2, tgt = x[0], mem[0], loss_target[0]
    g_mem2, g_final2 = g_mem.reshape(1, D), g_final.reshape(1, D)

    def bf16(name, w):
        return _cast_bf16("cast_" + name, w)

    Win, caw, cbw = _sequencer_all_gather(
        "ag_in", AG_ID, [bf16("w_in", w_in), conv_a_w[0], conv_b_w[0]])
    Wout, = _sequencer_all_gather("ag_out", AG_ID, [bf16("w_out", w_out)])
    Wq, Wk, Wv, Wo = _sequencer_all_gather(
        "ag_attn", AG_ID, [bf16("w_q", w_q), bf16("w_k", w_k), bf16("w_v", w_v), bf16("w_o", w_o)])
    def transposed(w):
        return jnp.transpose(w[0])

    gate_t = [transposed(a) for a in (w_gate, m_w_gate, v_w_gate)]
    up_t = [transposed(a) for a in (w_up, m_w_up, v_w_up)]
    WgateT, cfw = _sequencer_all_gather("ag_gate", AG_ID, [bf16("w_gate", gate_t[0]), conv_f_w[0]])
    WupT, = _sequencer_all_gather("ag_up", AG_ID, [bf16("w_up", up_t[0])])
    Wdown, = _sequencer_all_gather("ag_down", AG_ID, [bf16("w_down", w_down)])
    Wout, Wq, Wk, Wv, Wo = [w.reshape(D, D) for w in (Wout, Wq, Wk, Wv, Wo)]
    caw = jnp.transpose(caw, (1, 0, 2)).reshape(KA, CW)
    cbw = jnp.transpose(cbw, (1, 0, 2)).reshape(KB, CW)

    xn1 = _rms_fwd("rms_mix", x2, g_mix)
    proj = _matmul("mm_proj", "nn", xn1, Win, M=T, N=INB, K=D, nb=N_DEV, b_lay="blk", o_lay="col", tn=INB,
                   out_dtype=BF16)
    mix, u1 = _mixer_fwd(proj, caw, conv_a_b, ln_a_g, ln_a_b, cbw, T, CW)
    h1 = _matmul("mm_h1", "nn", mix, Wout.reshape(2, CW, D), M=T, N=D, K=CW, nb=2, a_lay="blk", b_lay="blk",
                 red_block=True, res=x2)
    xn2 = _rms_fwd("rms_xattn", h1, g_xattn)
    q = _matmul("mm_q", "nn", xn2, Wq, M=T, N=D, K=D, out_dtype=BF16, tn=1024)
    memn = _rms_fwd("rms_mem", mem2, g_mem2, after=[q])
    kk = _matmul("mm_k", "nn", memn, Wk, M=Mm, N=D, K=D, out_dtype=BF16)
    vv = _matmul("mm_v", "nn", memn, Wv, M=Mm, N=D, K=D, out_dtype=BF16)
    o = _attn_fwd(q, kk, vv)
    h2 = _matmul("mm_h2", "nn", o, Wo, M=T, N=D, K=D, res=h1)
    xn3 = _rms_fwd("rms_ffn", h2, g_ffn)
    gpre = _matmul("mm_gate", "nt", xn3, WgateT, M=T, N=FB, K=D, nb=N_DEV, b_lay="blk", o_lay="blk", tn=FB,
                   out_dtype=BF16)
    up = _matmul("mm_up", "nt", xn3, WupT, M=T, N=FB, K=D, nb=N_DEV, b_lay="blk", o_lay="blk", tn=FB,
                 out_dtype=BF16)
    f = _ffn_act_fwd(gpre, up, cfw)
    h3 = _matmul("mm_h3", "nn", f, Wdown, M=T, N=D, K=FB, nb=N_DEV, a_lay="blk", b_lay="blk", red_block=True,
                 group=2, res=h2)
    dh3, dh3b, loss_part, dg_final = _loss_head(h3, tgt, g_final2)

    wmv = {"w_in": (w_in, m_w_in, v_w_in), "conv_a_w": (conv_a_w, m_conv_a_w, v_conv_a_w),
           "conv_b_w": (conv_b_w, m_conv_b_w, v_conv_b_w), "w_out": (w_out, m_w_out, v_w_out),
           "w_q": (w_q, m_w_q, v_w_q), "w_k": (w_k, m_w_k, v_w_k), "w_v": (w_v, m_w_v, v_w_v),
           "w_o": (w_o, m_w_o, v_w_o), "w_gate": gate_t, "w_up": up_t,
           "conv_f_w": (conv_f_w, m_conv_f_w, v_conv_f_w), "w_down": (w_down, m_w_down, v_w_down)}
    res = {}
    pending = []

    def mm(*args, after=(), carry=(), **kwargs):
        behind = list(after) + pending
        pending.clear()
        jobs, alone = [], []
        for names, got in carry:
            for n, g in zip(names, got):
                job = _adamw_job(g, *wmv[n])
                (alone if job is None else jobs).append((n, g, job))
        results = None
        if jobs:
            out, results = _matmul(*args, after=behind, side=[job for _, _, job in jobs], **kwargs)
        else:
            out = _matmul(*args, after=behind, **kwargs)
        if results is None:
            alone, jobs = alone + jobs, []
        for (n, _, _), r in zip(jobs, results or []):
            res[n] = r
            pending.append(r[0])
        for n, g, _ in alone:
            res[n] = _adamw("adamw_" + n, g, *wmv[n], after=[out])
            pending.append(res[n][0])
        return out

    def to_sibling(tag, named_parts):
        got = _sequencer_to_sibling("rs1_" + tag, SIBLING_ID, [p for _, p in named_parts])
        return named_parts, got

    def to_chips(tag, stage1, after):
        named_parts, got = stage1
        sums = [_chip_sum("sum_" + n, p, g, after=after) for (n, p), g in zip(named_parts, got)]
        pending.extend(sums)
        return [n for n, _ in named_parts], _sequencer_to_chips("rs2_" + tag, CHIPS_ID, sums)

    def finish(stage2, after):
        names, got = stage2
        for n, g in zip(names, got):
            w, m, v = wmv[n]
            res[n] = _adamw("adamw_" + n, g, w, m, v, after=after)
            pending.append(res[n][0])

    def row_blocks(dw):
        return dw.reshape(N_DEV, DB, D)

    def conv_blocks(dw, K):
        return jnp.transpose(dw.reshape(K, N_DEV, CW // N_DEV), (1, 0, 2))

    dWdown = mm("mm_dw_down", "tn", f, dh3b, M=FB, N=D, K=T, nb=N_DEV, a_lay="blk", o_lay="blk",
                     out_dtype=BF16, tm=FB)
    s_down = to_sibling("down", [("w_down", dWdown)])
    df = mm("mm_df", "nt", dh3b, Wdown, M=T, N=FB, K=D, nb=N_DEV, b_lay="blk", o_lay="blk", tn=FB, out_dtype=BF16,
                 after=[dWdown])
    dgpre, dup, dcfw = _ffn_act_bwd(df, gpre, up, cfw)
    c_down = to_chips("down", s_down, after=[dgpre])
    dWgate = mm("mm_dw_gate", "tn", dgpre, xn3, M=FB, N=D, K=T, nb=N_DEV, a_lay="blk", o_lay="blk",
                     out_dtype=BF16, tm=FB)
    s_gate = to_sibling("gate", [("w_gate", dWgate), ("conv_f_w", dcfw)])
    dWup = mm("mm_dw_up", "tn", dup, xn3, M=FB, N=D, K=T, nb=N_DEV, a_lay="blk", o_lay="blk",
                   out_dtype=BF16, tm=FB, after=[dWgate])
    s_up = to_sibling("up", [("w_up", dWup)])
    dxn3 = mm("mm_dxn3_gate", "nn", dgpre, WgateT, M=T, N=D, K=FB, nb=N_DEV, a_lay="blk", b_lay="blk",
                   red_block=True, group=2, out_dtype=BF16, after=[dWup], carry=[c_down])
    c_gate = to_chips("gate", s_gate, after=[dxn3])
    dxn3 = mm("mm_dxn3_up", "nn", dup, WupT, M=T, N=D, K=FB, nb=N_DEV, a_lay="blk", b_lay="blk",
                   red_block=True, group=2, out_dtype=BF16, res=dxn3)
    c_up = to_chips("up", s_up, after=[dxn3])
    dh2, dh2b, dg_ffn = _rms_bwd("rms_bwd_ffn", dxn3, h2, g_ffn, dh3)

    dWo = mm("mm_dw_o", "tn", o, dh2b, M=D, N=D, K=T, out_dtype=BF16, tn=1024)
    s_o = to_sibling("o", [("w_o", row_blocks(dWo))])
    do = mm("mm_do", "nt", dh2b, Wo, M=T, N=D, K=D, out_dtype=BF16, tn=1024, after=[dWo])
    dq, dk, dv = _attn_bwd(q, kk, vv, do)
    dWq = mm("mm_dw_q", "tn", xn2, dq, M=D, N=D, K=T, out_dtype=BF16, carry=[c_gate])
    s_q = to_sibling("q", [("w_q", row_blocks(dWq))])
    dxn2 = mm("mm_dxn2", "nt", dq, Wq, M=T, N=D, K=D, out_dtype=BF16, after=[dWq], carry=[c_up])
    c_o = to_chips("o", s_o, after=[dxn2])
    c_q = to_chips("q", s_q, after=[dxn2])
    dh1, dh1b, dg_xattn = _rms_bwd("rms_bwd_xattn", dxn2, h1, g_xattn, dh2)
    dWk = mm("mm_dw_k", "tn", memn, dk, M=D, N=D, K=Mm, out_dtype=BF16, after=[dh1b])
    dWv = mm("mm_dw_v", "tn", memn, dv, M=D, N=D, K=Mm, out_dtype=BF16, after=[dh1b])
    s_kv = to_sibling("kv", [("w_k", row_blocks(dWk)), ("w_v", row_blocks(dWv))])
    dmemn = mm("mm_dmem_k", "nt", dk, Wk, M=Mm, N=D, K=D, after=[dWk, dWv])
    dmemn = mm("mm_dmem_v", "nt", dv, Wv, M=Mm, N=D, K=D, res=dmemn)
    dg_mem = _rms_bwd("rms_bwd_mem", dmemn, mem2, g_mem2)

    dWout = mm("mm_dw_out", "tn", mix, dh1b, M=CW, N=D, K=T, nb=2, a_lay="blk", o_lay="blk", out_dtype=BF16,
                    after=[dg_mem], carry=[c_o])
    s_out = to_sibling("out", [("w_out", row_blocks(dWout.reshape(D, D)))])
    dmix = mm("mm_dmix", "nt", dh1b, Wout, M=T, N=D, K=D, out_dtype=BF16, after=[dWout], carry=[c_q])
    c_kv = to_chips("kv", s_kv, after=[dmix])
    du1, dcv, dbg, dcaw, dcab, dlng, dlnb, dcbw = _mixer_bwd1(dmix, proj, u1, caw, ln_a_g, ln_a_b, cbw, T, CW)
    c_out = to_chips("out", s_out, after=[du1])
    dav, dag, dcg, dbh = _mixer_bwd2(du1, dcv, proj, caw, cbw, T, CW)
    dproj = jnp.concatenate([dav, dag, dbg, dcg, dbh], axis=1)
    dWin = mm("mm_dw_in", "tn", xn1, dproj, M=D, N=INB, K=T, nb=N_DEV, b_lay="col", o_lay="blk",
                   out_dtype=BF16, tm=512, tn=INB, carry=[c_kv])
    s_in = to_sibling("in", [("w_in", dWin), ("conv_a_w", conv_blocks(dcaw, KA)),
                             ("conv_b_w", conv_blocks(dcbw, KB))])
    finish(c_out, after=[dWin])
    c_in = to_chips("in", s_in, after=list(pending))
    dxn1 = mm("mm_dxn1", "nt", dproj, Win, M=T, N=D, K=INB, nb=N_DEV, a_lay="col", b_lay="blk",
                   red_block=True, group=2, out_dtype=BF16, after=[dWin])
    dx, _, dg_mix = _rms_bwd("rms_bwd_mix", dxn1, x2, g_mix, dh1, after=list(pending))

    def pair(a, b):
        return jnp.concatenate([a, b], axis=1)

    zeros_half = jnp.zeros((1, CW), F32)
    small_g = jnp.concatenate([
        dg_mix, pair(dcab, dlng), pair(dlnb, zeros_half), dg_xattn, dg_mem, dg_ffn, dg_final,
        jnp.broadcast_to(loss_part[:, :1], (1, D))], axis=0)
    small_sum = _all_reduce_rows("ar_small", small_g)
    loss = small_sum[7, 0]
    finish(c_in, after=[small_sum])

    def pack(a_mix, a_cab, a_lng, a_lnb, a_xattn, a_mem, a_ffn, a_final):
        return jnp.concatenate([a_mix, pair(a_cab, a_lng), pair(a_lnb, zeros_half), a_xattn, a_mem.reshape(1, D),
                                a_ffn, a_final.reshape(1, D), jnp.zeros((1, D), F32)], axis=0)

    small = _adamw("adamw_small", small_sum[None],
                   pack(g_mix, conv_a_b, ln_a_g, ln_a_b, g_xattn, g_mem, g_ffn, g_final),
                   pack(m_g_mix, m_conv_a_b, m_ln_a_g, m_ln_a_b, m_g_xattn, m_g_mem, m_g_ffn, m_g_final),
                   pack(v_g_mix, v_conv_a_b, v_ln_a_g, v_ln_a_b, v_g_xattn, v_g_mem, v_g_ffn, v_g_final))

    def unpack(a):
        return {"g_mix": a[0:1], "conv_a_b": a[1:2, :CW], "ln_a_g": a[1:2, CW:], "ln_a_b": a[2:3, :CW],
                "g_xattn": a[3:4], "g_mem": a[4], "g_ffn": a[5:6], "g_final": a[6]}

    small = [unpack(a) for a in small]
    order = ["g_mix", "w_in", "conv_a_w", "conv_a_b", "ln_a_g", "ln_a_b", "conv_b_w", "w_out", "g_xattn", "g_mem",
             "w_q", "w_k", "w_v", "w_o", "g_ffn", "w_gate", "w_up", "conv_f_w", "w_down", "g_final"]
    outs = [loss, dx[None]]
    for kind in range(4):
        for n in order:
            if n in ("w_gate", "w_up"):
                outs.append(jnp.transpose(res[n][kind])[None])
            else:
                outs.append(res[n][kind] if n in res else small[kind][n])
    return tuple(outs)
```

```python
import functools
from typing import NamedTuple

import jax
import jax.numpy as jnp
from jax import lax
from jax.experimental import pallas as pl
from jax.experimental.pallas import tpu as pltpu
from jax.experimental.pallas import tpu_sc as plsc

F32 = jnp.float32
BF16 = jnp.bfloat16

N_DEV = 8
EPS = 1e-6
GROUP_DIM = 128
N_XATTN_HEADS = 4
ADAM_LR = 0.001
ADAM_B1 = 0.9
ADAM_B2 = 0.999
ADAM_EPS = 1e-08
ADAM_WD = 0.01
ADAM_STEP = 10

AG_ID, SIBLING_ID, CHIPS_ID = 1, 2, 3

SIDE_TILES = 32
HALO = 32
VMEM_V7X_BYTES = 64 * 1024 * 1024
VMEM_TEMP_ALLOWANCE = 12 * 1024 * 1024

VMEM_SPEC = pl.BlockSpec(memory_space=pltpu.VMEM)
MESH = pl.DeviceIdType.MESH


def _tile(n, pref, align):
    if n <= pref:
        return n
    t = (pref // align) * align
    while t >= align:
        if n % t == 0:
            return t
        t -= align
    return n


def _nbytes(shape, dtype):
    n = 1
    for d in shape:
        if d is not None:
            n *= d
    return n * jnp.dtype(dtype).itemsize


def _call(body, name, grid, in_specs, out_specs, out_shape, operands, scratch=(), sem=None, after=()):
    outs = out_shape if isinstance(out_shape, (tuple, list)) else (out_shape,)
    ospecs = out_specs if isinstance(out_specs, (tuple, list)) else (out_specs,)
    est = 0
    for spec, arr in list(zip(in_specs, operands)) + list(zip(ospecs, outs)):
        est += 2 * _nbytes(spec.block_shape, arr.dtype)
    for s in scratch:
        if hasattr(s, "shape") and hasattr(s, "dtype"):
            est += _nbytes(s.shape, s.dtype)
    limit = min(est + VMEM_TEMP_ALLOWANCE, VMEM_V7X_BYTES - 4 * 1024 * 1024)
    if sem is None:
        sem = ("arbitrary",) * len(grid)
    n_in, n_after = len(operands), len(after)
    operands = [pltpu.with_memory_space_constraint(o, pltpu.HBM) for o in operands]
    after = [pltpu.with_memory_space_constraint(o, pltpu.HBM) for o in after]
    in_hbm = [pltpu.HBM(o.shape, o.dtype) for o in outs]
    out_shape = in_hbm if isinstance(out_shape, (tuple, list)) else in_hbm[0]

    def ordered_body(*refs):
        body(*refs[:n_in], *refs[n_in + n_after:])

    return pl.pallas_call(
        ordered_body if n_after else body, name=name, grid=grid,
        in_specs=list(in_specs) + [pl.BlockSpec(memory_space=pl.ANY)] * n_after,
        out_specs=out_specs, out_shape=out_shape, scratch_shapes=list(scratch),
        compiler_params=pltpu.CompilerParams(dimension_semantics=sem, vmem_limit_bytes=int(limit)),
    )(*operands, *after)


_DOT_DIMS = {"nn": (((1,), (0,)), ((), ())), "nt": (((1,), (1,)), ((), ())), "tn": (((0,), (0,)), ((), ()))}


def _operand_spec(layout, tr, tc, cols_per_block, pick, group=None):
    if layout == "plain":
        return pl.BlockSpec((tr, tc), lambda *g: pick(*g)[1:])
    if layout == "blk":
        return pl.BlockSpec((group, tr, tc), lambda *g: pick(*g))
    assert layout == "col"
    if group:
        assert tc == cols_per_block
        return pl.BlockSpec((tr, group * tc), lambda *g: (pick(*g)[1], pick(*g)[0]))
    per = cols_per_block // tc
    return pl.BlockSpec((tr, tc), lambda *g: (pick(*g)[1], pick(*g)[0] * per + pick(*g)[2]))


def _matmul(name, dims, a, b, *, M, N, K, nb=1, a_lay="plain", b_lay="plain", o_lay="plain",
            red_block=False, group=None, out_dtype=F32, res=None, tm=1024, tn=None, after=(), side=()):
    tm = _tile(M, tm, 128 if dims == "tn" else 16)
    tn = _tile(N, tn or (1024 if red_block else 512), 128)
    tk = K
    gi, gj, gk = M // tm, N // tn, K // tk
    if red_block:
        grid = (gi, gj, nb // (group or 1), gk)
        unpack = lambda i, j, bb, k: (bb, i, j, k)
        red_axes, sem = (2, 3), ("parallel", "parallel", "arbitrary", "arbitrary")
    else:
        grid = (nb, gi, gj, gk)
        unpack = lambda bb, i, j, k: (bb, i, j, k)
        red_axes, sem = (3,), ("parallel", "parallel", "parallel", "arbitrary")

    def picker(f):
        return lambda *g: f(*unpack(*g))

    if dims == "tn":
        a_spec = _operand_spec(a_lay, tk, tm, M, picker(lambda bb, i, j, k: (bb, k, i)), group)
    else:
        a_spec = _operand_spec(a_lay, tm, tk, K, picker(lambda bb, i, j, k: (bb, i, k)), group)
    if dims == "nt":
        b_spec = _operand_spec(b_lay, tn, tk, K, picker(lambda bb, i, j, k: (bb, j, k)), group)
    else:
        b_spec = _operand_spec(b_lay, tk, tn, N, picker(lambda bb, i, j, k: (bb, k, j)), group)
    o_spec = _operand_spec(o_lay, tm, tn, N, picker(lambda bb, i, j, k: (bb, i, j)))
    if o_lay == "plain":
        out_shape = jax.ShapeDtypeStruct((M, N), out_dtype)
    elif o_lay == "blk":
        out_shape = jax.ShapeDtypeStruct((nb, M, N), out_dtype)
    else:
        out_shape = jax.ShapeDtypeStruct((M, nb * N), out_dtype)
    n_red = [grid[ax] for ax in red_axes]
    has_res = res is not None
    one_step = all(n == 1 for n in n_red)

    def contract(a_ref, b_ref):
        if group:
            parts = [(a_ref[p] if a_lay == "blk" else a_ref[:, p * K:(p + 1) * K], b_ref[p]) for p in range(group)]
        else:
            parts = [(a_ref[...], b_ref[...])]
        r = None
        for a_part, b_part in parts:
            d = lax.dot_general(a_part, b_part, _DOT_DIMS[dims], preferred_element_type=F32)
            r = d if r is None else r + d
        return r

    def step_index(*g):
        s = g[0]
        for ax in range(1, len(grid)):
            s = s * grid[ax] + g[ax]
        return s

    def side_spec(block):
        shape, index = block
        return pl.BlockSpec(shape, lambda *g: index(jnp.minimum(step_index(*g), SIDE_TILES - 1)))

    asked_to_carry = bool(side)
    if functools.reduce(lambda p, q: p * q, grid) < SIDE_TILES:
        side = ()
    n_main_in = 3 if has_res else 2
    n_side_in = sum(len(job.operands) for job in side)
    n_side_out = sum(len(job.out_shapes) for job in side)

    def side_work(side_in, side_out):
        @pl.when(step_index(*[pl.program_id(ax) for ax in range(len(grid))]) < SIDE_TILES)
        def _():
            i0 = o0 = 0
            for job in side:
                n_i, n_o = len(job.operands), len(job.out_shapes)
                job.body(*side_in[i0:i0 + n_i], *side_out[o0:o0 + n_o])
                i0, o0 = i0 + n_i, o0 + n_o

    def body(*refs):
        a_ref, b_ref = refs[:2]
        side_in = refs[n_main_in:n_main_in + n_side_in]
        o_ref = refs[n_main_in + n_side_in]
        side_out = refs[n_main_in + n_side_in + 1:n_main_in + n_side_in + 1 + n_side_out]
        if one_step:
            r = contract(a_ref, b_ref)
            if has_res:
                r = r + refs[2][...]
            o_ref[...] = r.astype(o_ref.dtype)
        else:
            acc = refs[-1]
            first = functools.reduce(jnp.logical_and, [pl.program_id(ax) == 0 for ax in red_axes])
            last = functools.reduce(jnp.logical_and,
                                    [pl.program_id(ax) == n - 1 for ax, n in zip(red_axes, n_red)])

            @pl.when(first)
            def _():
                acc[...] = jnp.zeros_like(acc)

            acc[...] += contract(a_ref, b_ref)

            @pl.when(last)
            def _():
                r = acc[...]
                if has_res:
                    r = r + refs[2][...]
                o_ref[...] = r.astype(o_ref.dtype)
        if side:
            side_work(side_in, side_out)

    in_specs = [a_spec, b_spec]
    operands = [a, b]
    if has_res:
        in_specs.append(_operand_spec("plain", tm, tn, N, picker(lambda bb, i, j, k: (bb, i, j))))
        operands.append(res)
    out_specs, out_shapes = [o_spec], [out_shape]
    for job in side:
        in_specs += [side_spec(blk) for blk in job.in_blocks]
        operands += job.operands
        out_specs += [side_spec(blk) for blk in job.out_blocks]
        out_shapes += job.out_shapes
    scratch = [] if one_step else [pltpu.VMEM((tm, tn), F32)]
    if not side:
        out = _call(body, name, grid, in_specs, o_spec, out_shape, operands, scratch=scratch, sem=sem, after=after)
        return (out, None) if asked_to_carry else out
    outs = _call(body, name, grid, in_specs, tuple(out_specs), tuple(out_shapes), operands, scratch=scratch,
                 sem=("arbitrary",) * len(grid), after=after)
    results, pos = [], 1
    for job in side:
        results.append(tuple(outs[pos:pos + len(job.out_shapes)]))
        pos += len(job.out_shapes)
    return outs[0], results


def _rows_spec(arr, tr):
    lead = arr.ndim - 2
    return pl.BlockSpec((None,) * lead + (tr, arr.shape[-1]), lambda i: (0,) * lead + (i, 0))


def _cast_bf16(name, w):
    R, C = w.shape[-2:]
    tr = _tile(R, max(8, (3 << 20) // C), 16)

    def body(w_ref, o_ref):
        o_ref[...] = w_ref[...].astype(BF16)

    return _call(body, name, (R // tr,), [_rows_spec(w, tr)],
                 pl.BlockSpec((tr, C), lambda i: (i, 0)), jax.ShapeDtypeStruct((R, C), BF16), [w],
                 sem=("parallel",))


def _rms_fwd(name, x, g, after=()):
    T, D = x.shape
    tm = _tile(T, 128, 16)

    def body(x_ref, g_ref, o_ref):
        xv = x_ref[...]
        r = lax.rsqrt(jnp.mean(xv * xv, axis=-1, keepdims=True) + EPS)
        o_ref[...] = (xv * r * g_ref[...]).astype(BF16)

    return _call(body, name, (T // tm,),
                 [pl.BlockSpec((tm, D), lambda i: (i, 0)), pl.BlockSpec((1, D), lambda i: (0, 0))],
                 pl.BlockSpec((tm, D), lambda i: (i, 0)), jax.ShapeDtypeStruct((T, D), BF16), [x, g],
                 sem=("parallel",), after=after)


def _rms_bwd(name, dxn, x, g, dh=None, after=()):
    T, D = x.shape
    tm = _tile(T, 128, 16)
    with_dx = dh is not None

    def body(*refs):
        if with_dx:
            dxn_ref, x_ref, g_ref, dh_ref, o_ref, ob_ref, dg_ref = refs
        else:
            dxn_ref, x_ref, g_ref, dg_ref = refs
        xv = x_ref[...]
        r = lax.rsqrt(jnp.mean(xv * xv, axis=-1, keepdims=True) + EPS)
        xh = xv * r
        dy = dxn_ref[...].astype(F32)

        @pl.when(pl.program_id(0) == 0)
        def _():
            dg_ref[...] = jnp.zeros_like(dg_ref)

        dg_ref[...] += jnp.sum(dy * xh, axis=0, keepdims=True)
        if with_dx:
            dyg = dy * g_ref[...]
            tot = dh_ref[...] + r * (dyg - xh * jnp.mean(dyg * xh, axis=-1, keepdims=True))
            o_ref[...] = tot
            ob_ref[...] = tot.astype(BF16)

    row = pl.BlockSpec((tm, D), lambda i: (i, 0))
    vec = pl.BlockSpec((1, D), lambda i: (0, 0))
    if with_dx:
        return _call(body, name, (T // tm,), [row, row, vec, row], (row, row, vec),
                     (jax.ShapeDtypeStruct((T, D), F32), jax.ShapeDtypeStruct((T, D), BF16),
                      jax.ShapeDtypeStruct((1, D), F32)), [dxn, x, g, dh], after=after)
    return _call(body, name, (T // tm,), [row, row, vec], vec, jax.ShapeDtypeStruct((1, D), F32), [dxn, x, g])


def _loss_head(h, target, g):
    T, D = h.shape
    tm = _tile(T, 128, 16)

    def body(h_ref, t_ref, g_ref, o_ref, ob_ref, loss_ref, dg_ref):
        xv = h_ref[...]
        gv = g_ref[...]
        r = lax.rsqrt(jnp.mean(xv * xv, axis=-1, keepdims=True) + EPS)
        xh = xv * r
        e = xh * gv - t_ref[...]

        @pl.when(pl.program_id(0) == 0)
        def _():
            dg_ref[...] = jnp.zeros_like(dg_ref)
            loss_ref[...] = jnp.zeros_like(loss_ref)

        loss_ref[...] += 0.5 * jnp.sum(jnp.mean(e * e, axis=-1, keepdims=True), axis=0, keepdims=True)
        dy = e * (1.0 / D)
        dg_ref[...] += jnp.sum(dy * xh, axis=0, keepdims=True)
        dyg = dy * gv
        dx = r * (dyg - xh * jnp.mean(dyg * xh, axis=-1, keepdims=True))
        o_ref[...] = dx
        ob_ref[...] = dx.astype(BF16)

    row = pl.BlockSpec((tm, D), lambda i: (i, 0))
    vec = pl.BlockSpec((1, D), lambda i: (0, 0))
    return _call(body, "loss_head", (T // tm,), [row, row, vec],
                 (row, row, pl.BlockSpec((1, 128), lambda i: (0, 0)), vec),
                 (jax.ShapeDtypeStruct((T, D), F32), jax.ShapeDtypeStruct((T, D), BF16),
                  jax.ShapeDtypeStruct((1, 128), F32), jax.ShapeDtypeStruct((1, D), F32)), [h, target, g])


ROW_CHUNK = 64
SUBLANES = 8


def _col_chunks(width):
    return [slice(c0, min(c0 + GROUP_DIM, width)) for c0 in range(0, width, GROUP_DIM)]


def _row_chunks(n_rows):
    return [(r0, min(ROW_CHUNK, n_rows - r0)) for r0 in range(0, n_rows, ROW_CHUNK)]


def _pad_rows(K):
    return -(-(K - 1) // SUBLANES) * SUBLANES


def _shifted_back(buf, K, r0, nr, cs):
    pad = _pad_rows(K)
    win = buf[pl.ds(HALO + r0 - pad, nr + pad), cs]
    for b in range(min(SUBLANES, K)):
        rolled = win if b == 0 else pltpu.roll(win, b, axis=0)
        for a in range((K - 1 - b) // SUBLANES + 1):
            yield K - 1 - (SUBLANES * a + b), rolled[pad - SUBLANES * a:pad - SUBLANES * a + nr]


def _conv_fwd(buf, w_ref, K, r0, nr, cs):
    y = None
    for k, xs in _shifted_back(buf, K, r0, nr, cs):
        term = xs * w_ref[pl.ds(k, 1), cs]
        y = term if y is None else y + term
    return y


def _conv_bwd_input(buf, w_ref, K, r0, nr, cs):
    pad = _pad_rows(K)
    win = buf[pl.ds(r0, nr + pad), cs]
    dx = None
    for b in range(min(SUBLANES, K)):
        rolled = win if b == 0 else pltpu.roll(win, nr + pad - b, axis=0)
        for a in range((K - 1 - b) // SUBLANES + 1):
            k = K - 1 - (SUBLANES * a + b)
            term = rolled[SUBLANES * a:SUBLANES * a + nr] * w_ref[pl.ds(k, 1), cs]
            dx = term if dx is None else dx + term
    return dx


def _fold_rows(v):
    nr, lanes = v.shape
    if nr % SUBLANES:
        return jnp.sum(v, axis=0, keepdims=True)
    return jnp.sum(v.reshape(nr // SUBLANES, SUBLANES, lanes), axis=0)


def _conv_bwd_weight(accs, dy, buf, K, r0, nr, cs):
    accs = list(accs)
    for k, xs in _shifted_back(buf, K, r0, nr, cs):
        accs[k] = accs[k] + _fold_rows(dy * xs)
    return accs


def _add_row(ref, row, cs, acc):
    ref[pl.ds(row, 1), cs] += jnp.sum(acc, axis=0, keepdims=True)


def _sigmoid(z):
    return 0.5 * jnp.tanh(0.5 * z) + 0.5


def _silu_grad(z, sig):
    return sig * (1.0 + z * (1.0 - sig))


def _group_norm(xg):
    xc = xg - jnp.mean(xg, axis=-1, keepdims=True)
    rstd = lax.rsqrt(jnp.mean(xc * xc, axis=-1, keepdims=True) + EPS)
    return xc * rstd, rstd


def _mixer_tiles(T, CW):
    tm = _tile(T, 512, HALO)
    tc = _tile(CW, 256, GROUP_DIM)
    return tm, tc, tm // HALO, CW // tc


def _mixer_fwd(proj, caw, cab, lng, lnb, cbw, T, CW):
    KA, KB = caw.shape[0], cbw.shape[0]
    tm, tc, hb, nc = _mixer_tiles(T, CW)

    def sec(s):
        return pl.BlockSpec((tm, tc), lambda i, c: (i, s * nc + c))

    def sec_prev(s):
        return pl.BlockSpec((HALO, tc), lambda i, c: (jnp.maximum(i * hb - 1, 0), s * nc + c))

    def chan(rows):
        return pl.BlockSpec((rows, tc), lambda i, c: (0, c))

    def body(av, ag, bg, cg, bh, avh, agh, cgh, bhh, caw_ref, cab_ref, lng_ref, lnb_ref, cbw_ref,
             mix_ref, u1_ref, bufa, bufb):
        first = pl.program_id(0) == 0
        bufa[pl.ds(0, HALO), :] = jnp.where(first, 0.0, avh[...].astype(F32) * _sigmoid(agh[...].astype(F32)))
        bufb[pl.ds(0, HALO), :] = jnp.where(first, 0.0, cgh[...].astype(F32) * bhh[...].astype(F32))
        for cs in _col_chunks(tc):
            for r0, nr in _row_chunks(tm):
                rows = pl.ds(r0, nr)
                bufa[pl.ds(HALO + r0, nr), cs] = av[rows, cs].astype(F32) * _sigmoid(ag[rows, cs].astype(F32))
                bufb[pl.ds(HALO + r0, nr), cs] = cg[rows, cs].astype(F32) * bh[rows, cs].astype(F32)
        for cs in _col_chunks(tc):
            for r0, nr in _row_chunks(tm):
                rows = pl.ds(r0, nr)
                u1 = _conv_fwd(bufa, caw_ref, KA, r0, nr, cs) + cab_ref[:, cs]
                u1_ref[rows, cs] = u1
                y, _ = _group_norm(u1)
                z = y * lng_ref[:, cs] + lnb_ref[:, cs]
                mix_ref[0, rows, cs] = (z * _sigmoid(z)).astype(BF16)
                mix_ref[1, rows, cs] = (bg[rows, cs].astype(F32) * _conv_fwd(bufb, cbw_ref, KB, r0, nr, cs)).astype(BF16)

    in_specs = [sec(0), sec(1), sec(2), sec(3), sec(4), sec_prev(0), sec_prev(1), sec_prev(3), sec_prev(4),
                chan(KA), chan(1), chan(1), chan(1), chan(KB)]
    operands = [proj] * 9 + [caw, cab, lng, lnb, cbw]
    return _call(body, "mixer_fwd", (T // tm, nc), in_specs,
                 (pl.BlockSpec((2, tm, tc), lambda i, c: (0, i, c)), pl.BlockSpec((tm, tc), lambda i, c: (i, c))),
                 (jax.ShapeDtypeStruct((2, T, CW), BF16), jax.ShapeDtypeStruct((T, CW), F32)), operands,
                 scratch=[pltpu.VMEM((HALO + tm, tc), F32), pltpu.VMEM((HALO + tm, tc), F32)],
                 sem=("parallel", "parallel"))


def _mixer_bwd1(dmix, proj, u1, caw, lng, lnb, cbw, T, CW):
    KA, KB = caw.shape[0], cbw.shape[0]
    tm, tc, hb, nc = _mixer_tiles(T, CW)

    def sec(s):
        return pl.BlockSpec((tm, tc), lambda c, i: (i, s * nc + c))

    def sec_prev(s):
        return pl.BlockSpec((HALO, tc), lambda c, i: (jnp.maximum(i * hb - 1, 0), s * nc + c))

    def chan(rows):
        return pl.BlockSpec((rows, tc), lambda c, i: (0, c))

    tile = pl.BlockSpec((tm, tc), lambda c, i: (i, c))

    def body(du, dv, u1_ref, av, ag, bg, cg, bh, avh, agh, cgh, bhh, lng_ref, lnb_ref, cbw_ref,
             du1_ref, dcv_ref, dbg_ref, dcaw_ref, dcab_ref, dlng_ref, dlnb_ref, dcbw_ref, bufa, bufb):
        first = pl.program_id(1) == 0

        @pl.when(first)
        def _():
            for r in (dcaw_ref, dcab_ref, dlng_ref, dlnb_ref, dcbw_ref):
                r[...] = jnp.zeros_like(r)

        bufa[pl.ds(0, HALO), :] = jnp.where(first, 0.0, avh[...].astype(F32) * _sigmoid(agh[...].astype(F32)))
        bufb[pl.ds(0, HALO), :] = jnp.where(first, 0.0, cgh[...].astype(F32) * bhh[...].astype(F32))
        for cs in _col_chunks(tc):
            for r0, nr in _row_chunks(tm):
                rows = pl.ds(r0, nr)
                bufa[pl.ds(HALO + r0, nr), cs] = av[rows, cs].astype(F32) * _sigmoid(ag[rows, cs].astype(F32))
                bufb[pl.ds(HALO + r0, nr), cs] = cg[rows, cs].astype(F32) * bh[rows, cs].astype(F32)
        for cs in _col_chunks(tc):
            lanes = cs.stop - cs.start
            zero = jnp.zeros((SUBLANES, lanes), F32)
            a_lng, a_lnb, a_cab = zero, zero, zero
            a_caw, a_cbw = [zero] * KA, [zero] * KB
            gamma, beta = lng_ref[:, cs], lnb_ref[:, cs]
            for r0, nr in _row_chunks(tm):
                rows = pl.ds(r0, nr)
                y, rstd = _group_norm(u1_ref[rows, cs])
                z = y * gamma + beta
                dz = du[rows, cs].astype(F32) * _silu_grad(z, _sigmoid(z))
                a_lng = a_lng + _fold_rows(dz * y)
                a_lnb = a_lnb + _fold_rows(dz)
                dy = dz * gamma
                du1 = rstd * (dy - jnp.mean(dy, axis=-1, keepdims=True)
                              - y * jnp.mean(dy * y, axis=-1, keepdims=True))
                du1_ref[rows, cs] = du1
                a_cab = a_cab + _fold_rows(du1)
                a_caw = _conv_bwd_weight(a_caw, du1, bufa, KA, r0, nr, cs)

                dvv = dv[rows, cs].astype(F32)
                dbg_ref[rows, cs] = (dvv * _conv_fwd(bufb, cbw_ref, KB, r0, nr, cs)).astype(BF16)
                dcv = dvv * bg[rows, cs].astype(F32)
                dcv_ref[rows, cs] = dcv
                a_cbw = _conv_bwd_weight(a_cbw, dcv, bufb, KB, r0, nr, cs)
            _add_row(dlng_ref, 0, cs, a_lng)
            _add_row(dlnb_ref, 0, cs, a_lnb)
            _add_row(dcab_ref, 0, cs, a_cab)
            for k in range(KA):
                _add_row(dcaw_ref, k, cs, a_caw[k])
            for k in range(KB):
                _add_row(dcbw_ref, k, cs, a_cbw[k])

    in_specs = [sec(0), sec(1), tile, sec(0), sec(1), sec(2), sec(3), sec(4),
                sec_prev(0), sec_prev(1), sec_prev(3), sec_prev(4), chan(1), chan(1), chan(KB)]
    operands = [dmix, dmix, u1] + [proj] * 9 + [lng, lnb, cbw]
    return _call(body, "mixer_bwd1", (nc, T // tm), in_specs,
                 (tile, tile, tile, chan(KA), chan(1), chan(1), chan(1), chan(KB)),
                 (jax.ShapeDtypeStruct((T, CW), F32), jax.ShapeDtypeStruct((T, CW), F32),
                  jax.ShapeDtypeStruct((T, CW), BF16), jax.ShapeDtypeStruct((KA, CW), F32),
                  jax.ShapeDtypeStruct((1, CW), F32), jax.ShapeDtypeStruct((1, CW), F32),
                  jax.ShapeDtypeStruct((1, CW), F32), jax.ShapeDtypeStruct((KB, CW), F32)), operands,
                 scratch=[pltpu.VMEM((HALO + tm, tc), F32), pltpu.VMEM((HALO + tm, tc), F32)],
                 sem=("parallel", "arbitrary"))


def _mixer_bwd2(du1, dcv, proj, caw, cbw, T, CW):
    KA, KB = caw.shape[0], cbw.shape[0]
    tm, tc, hb, nc = _mixer_tiles(T, CW)
    n_i = T // tm

    def sec(s):
        return pl.BlockSpec((tm, tc), lambda i, c: (i, s * nc + c))

    def chan(rows):
        return pl.BlockSpec((rows, tc), lambda i, c: (0, c))

    tile = pl.BlockSpec((tm, tc), lambda i, c: (i, c))
    nxt = pl.BlockSpec((HALO, tc), lambda i, c: (jnp.minimum((i + 1) * hb, n_i * hb - 1), c))

    def body(du1_ref, du1n, dcv_ref, dcvn, av, ag, cg, bh, caw_ref, cbw_ref, dav, dag, dcg, dbh, bufa, bufb):
        last = pl.program_id(0) == n_i - 1
        bufa[pl.ds(0, tm), :] = du1_ref[...]
        bufa[pl.ds(tm, HALO), :] = jnp.where(last, 0.0, du1n[...])
        bufb[pl.ds(0, tm), :] = dcv_ref[...]
        bufb[pl.ds(tm, HALO), :] = jnp.where(last, 0.0, dcvn[...])
        for cs in _col_chunks(tc):
            for r0, nr in _row_chunks(tm):
                rows = pl.ds(r0, nr)
                du0 = _conv_bwd_input(bufa, caw_ref, KA, r0, nr, cs)
                sig = _sigmoid(ag[rows, cs].astype(F32))
                dav[rows, cs] = (du0 * sig).astype(BF16)
                dag[rows, cs] = (du0 * av[rows, cs].astype(F32) * (sig * (1.0 - sig))).astype(BF16)
                dch = _conv_bwd_input(bufb, cbw_ref, KB, r0, nr, cs)
                dcg[rows, cs] = (dch * bh[rows, cs].astype(F32)).astype(BF16)
                dbh[rows, cs] = (dch * cg[rows, cs].astype(F32)).astype(BF16)

    in_specs = [tile, nxt, tile, nxt, sec(0), sec(1), sec(3), sec(4), chan(KA), chan(KB)]
    operands = [du1, du1, dcv, dcv, proj, proj, proj, proj, caw, cbw]
    out = jax.ShapeDtypeStruct((T, CW), BF16)
    return _call(body, "mixer_bwd2", (n_i, nc), in_specs, (tile, tile, tile, tile), (out, out, out, out),
                 operands, scratch=[pltpu.VMEM((HALO + tm, tc), F32), pltpu.VMEM((HALO + tm, tc), F32)],
                 sem=("parallel", "parallel"))


def _ffn_tiles(T):
    tm = _tile(T, 512, HALO)
    return tm, tm // HALO, T // tm


def _ffn_act_fwd(gpre, up, cfw):
    nb, T, F = gpre.shape
    KF = cfw.shape[1]
    tm, hb, n_i = _ffn_tiles(T)
    tile = pl.BlockSpec((None, tm, F), lambda b, i: (b, i, 0))
    prev = pl.BlockSpec((None, HALO, F), lambda b, i: (b, jnp.maximum(i * hb - 1, 0), 0))
    wspec = pl.BlockSpec((None, KF, F), lambda b, i: (b, 0, 0))

    def body(g_ref, gh_ref, up_ref, w_ref, f_ref, buf):
        buf[pl.ds(HALO, tm), :] = g_ref[...].astype(F32)
        buf[pl.ds(0, HALO), :] = jnp.where(pl.program_id(1) == 0, 0.0, gh_ref[...].astype(F32))
        for cs in _col_chunks(F):
            for r0, nr in _row_chunks(tm):
                rows = pl.ds(r0, nr)
                g = _conv_fwd(buf, w_ref, KF, r0, nr, cs)
                f_ref[rows, cs] = (g * _sigmoid(g) * up_ref[rows, cs].astype(F32)).astype(BF16)

    return _call(body, "ffn_act_fwd", (nb, n_i), [tile, prev, tile, wspec], tile,
                 jax.ShapeDtypeStruct((nb, T, F), BF16), [gpre, gpre, up, cfw],
                 scratch=[pltpu.VMEM((HALO + tm, F), F32)], sem=("parallel", "parallel"))


def _ffn_act_bwd(df, gpre, up, cfw):
    nb, T, F = gpre.shape
    KF = cfw.shape[1]
    tm, hb, n_i = _ffn_tiles(T)
    extra = 2 * SUBLANES
    assert KF - 1 <= SUBLANES and HALO >= extra
    tile = pl.BlockSpec((None, tm, F), lambda b, i: (b, i, 0))
    prev = pl.BlockSpec((None, HALO, F), lambda b, i: (b, jnp.maximum(i * hb - 1, 0), 0))
    nxt = pl.BlockSpec((None, HALO, F), lambda b, i: (b, jnp.minimum((i + 1) * hb, n_i * hb - 1), 0))
    wspec = pl.BlockSpec((None, KF, F), lambda b, i: (b, 0, 0))

    def body(df_ref, g_ref, gh_ref, up_ref, dfn_ref, gn_ref, upn_ref, w_ref, dgpre_ref, dup_ref, dw_ref, gbuf, dgbuf):
        first = pl.program_id(1) == 0
        last = pl.program_id(1) == n_i - 1

        @pl.when(first)
        def _():
            dw_ref[...] = jnp.zeros_like(dw_ref)

        gbuf[pl.ds(0, HALO), :] = jnp.where(first, 0.0, gh_ref[...].astype(F32))
        gbuf[pl.ds(HALO, tm), :] = g_ref[...].astype(F32)
        gbuf[pl.ds(HALO + tm, HALO), :] = gn_ref[...].astype(F32)
        for cs in _col_chunks(F):
            accs = [jnp.zeros((SUBLANES, cs.stop - cs.start), F32)] * KF
            for r0, nr in _row_chunks(tm):
                rows = pl.ds(r0, nr)
                g = _conv_fwd(gbuf, w_ref, KF, r0, nr, cs)
                sig = _sigmoid(g)
                dfv = df_ref[rows, cs].astype(F32)
                dup_ref[rows, cs] = (dfv * (g * sig)).astype(BF16)
                dg = dfv * up_ref[rows, cs].astype(F32) * _silu_grad(g, sig)
                dgbuf[rows, cs] = dg
                accs = _conv_bwd_weight(accs, dg, gbuf, KF, r0, nr, cs)
            for k in range(KF):
                _add_row(dw_ref, k, cs, accs[k])
            g = _conv_fwd(gbuf, w_ref, KF, tm, extra, cs)
            dg_next = (dfn_ref[pl.ds(0, extra), cs].astype(F32) * upn_ref[pl.ds(0, extra), cs].astype(F32)
                       * _silu_grad(g, _sigmoid(g)))
            dgbuf[pl.ds(tm, extra), cs] = jnp.where(last, 0.0, dg_next)
        for cs in _col_chunks(F):
            for r0, nr in _row_chunks(tm):
                dgpre_ref[pl.ds(r0, nr), cs] = _conv_bwd_input(dgbuf, w_ref, KF, r0, nr, cs).astype(BF16)

    out = jax.ShapeDtypeStruct((nb, T, F), BF16)
    return _call(body, "ffn_act_bwd", (nb, n_i), [tile, tile, prev, tile, nxt, nxt, nxt, wspec], (tile, tile, wspec),
                 (out, out, jax.ShapeDtypeStruct((nb, KF, F), F32)), [df, gpre, gpre, up, df, gpre, up, cfw],
                 scratch=[pltpu.VMEM((2 * HALO + tm, F), F32), pltpu.VMEM((tm + extra, F), F32)],
                 sem=("parallel", "arbitrary"))


def _softmax_rows(s):
    e = jnp.exp(s - jnp.max(s, axis=-1, keepdims=True))
    return e / jnp.sum(e, axis=-1, keepdims=True)


def _attn_fwd(q, k, v):
    T, D = q.shape
    Mm = k.shape[0]
    hd = D // N_XATTN_HEADS
    scale = hd ** -0.5
    tm = _tile(T, 256, 16)

    def body(q_ref, k_ref, v_ref, o_ref):
        for h in range(N_XATTN_HEADS):
            sl = slice(h * hd, (h + 1) * hd)
            s = lax.dot_general(q_ref[:, sl], k_ref[:, sl], _DOT_DIMS["nt"], preferred_element_type=F32) * scale
            p = _softmax_rows(s).astype(BF16)
            o_ref[:, sl] = jnp.dot(p, v_ref[:, sl], preferred_element_type=F32).astype(BF16)

    row = pl.BlockSpec((tm, D), lambda i: (i, 0))
    full = pl.BlockSpec((Mm, D), lambda i: (0, 0))
    return _call(body, "attn_fwd", (T // tm,), [row, full, full], row, jax.ShapeDtypeStruct((T, D), BF16),
                 [q, k, v], sem=("parallel",))


def _attn_bwd(q, k, v, do):
    T, D = q.shape
    Mm = k.shape[0]
    hd = D // N_XATTN_HEADS
    scale = hd ** -0.5
    tm = _tile(T, 256, 16)
    n_i = T // tm

    def body(q_ref, do_ref, k_ref, v_ref, dq_ref, dk_ref, dv_ref, dk_acc, dv_acc):
        @pl.when(pl.program_id(0) == 0)
        def _():
            dk_acc[...] = jnp.zeros_like(dk_acc)
            dv_acc[...] = jnp.zeros_like(dv_acc)

        for h in range(N_XATTN_HEADS):
            sl = slice(h * hd, (h + 1) * hd)
            qh, kh, doh = q_ref[:, sl], k_ref[:, sl], do_ref[:, sl]
            s = lax.dot_general(qh, kh, _DOT_DIMS["nt"], preferred_element_type=F32) * scale
            p = _softmax_rows(s)
            dv_acc[:, sl] += lax.dot_general(p.astype(BF16), doh, _DOT_DIMS["tn"], preferred_element_type=F32)
            dp = lax.dot_general(doh, v_ref[:, sl], _DOT_DIMS["nt"], preferred_element_type=F32)
            ds = (p * (dp - jnp.sum(dp * p, axis=-1, keepdims=True)) * scale).astype(BF16)
            dq_ref[:, sl] = jnp.dot(ds, kh, preferred_element_type=F32).astype(BF16)
            dk_acc[:, sl] += lax.dot_general(ds, qh, _DOT_DIMS["tn"], preferred_element_type=F32)

        @pl.when(pl.program_id(0) == n_i - 1)
        def _():
            dk_ref[...] = dk_acc[...].astype(BF16)
            dv_ref[...] = dv_acc[...].astype(BF16)

    row = pl.BlockSpec((tm, D), lambda i: (i, 0))
    full = pl.BlockSpec((Mm, D), lambda i: (0, 0))
    return _call(body, "attn_bwd", (n_i,), [row, row, full, full], (row, full, full),
                 (jax.ShapeDtypeStruct((T, D), BF16), jax.ShapeDtypeStruct((Mm, D), BF16),
                  jax.ShapeDtypeStruct((Mm, D), BF16)), [q, do, k, v],
                 scratch=[pltpu.VMEM((Mm, D), F32), pltpu.VMEM((Mm, D), F32)])


def _position():
    x, y, c = lax.axis_index("x"), lax.axis_index("y"), lax.axis_index("c")
    return x, y, c


def _peer(pos, k):
    x, y, c = pos
    return (1 - x if k & 4 else x, 1 - y if k & 2 else y, 1 - c if k & 1 else c)


def _index(pos):
    x, y, c = pos
    return 4 * x + 2 * y + c


def _sequencer_kernel(body, name, collective_id, out_type, operands):
    return pl.kernel(
        body, name=name, out_type=out_type,
        mesh=plsc.ScalarSubcoreMesh(axis_name="sequencer", num_cores=1),
        scratch_types=[pltpu.SemaphoreType.DMA, pltpu.SemaphoreType.DMA((7,)), pltpu.SemaphoreType.DMA],
        compiler_params=pltpu.CompilerParams(collective_id=collective_id),
    )(*operands)


def _handshake(peers):
    barrier = pltpu.get_barrier_semaphore()
    for peer in peers:
        pl.semaphore_signal(barrier, inc=1, device_id=peer, device_id_type=MESH)
    pl.semaphore_wait(barrier, len(peers))


def _sequencer_all_gather(name, collective_id, shards):
    n = len(shards)

    def body(*refs):
        x_refs, out_refs = refs[:n], refs[n:2 * n]
        send_sem, recv_sems, local_sem = refs[2 * n:]
        me = _position()
        x, y, c = me
        sibling = _peer(me, 1)
        first = (x + (1 - c) - 2 * x * (1 - c), y + c - 2 * y * c, c)
        second = (x + c - 2 * x * c, y + (1 - c) - 2 * y * (1 - c), c)
        diagonal = _peer(me, 6)
        _handshake([sibling, first, second])

        def copy(a, k, block, to, own=False):
            dst = out_refs[a].at[_index(block)]
            return pltpu.make_async_remote_copy(
                src_ref=x_refs[a] if own else dst, dst_ref=dst, send_sem=send_sem, recv_sem=recv_sems.at[k],
                device_id=to, device_id_type=MESH)

        local = [pltpu.make_async_copy(x_refs[a], out_refs[a].at[_index(me)], local_sem) for a in range(n)]
        started = [copy(a, 1 + j, me, peer, own=True) for a in range(n) for j, peer in enumerate((first, second))]
        started += [copy(a, 0, me, sibling, own=True) for a in range(n)]
        for cp in started + local:
            cp.start()
        for k, origin in ((1, first), (2, second), (3, diagonal)):
            for a in range(n):
                copy(a, k, origin, me).wait_recv()
            passed = [copy(a, 3 + k, origin, sibling) for a in range(n)]
            if k == 1:
                passed = [copy(a, 3, origin, second) for a in range(n)] + passed
            for cp in passed:
                cp.start()
            started += passed
        for k in (0, 4, 5, 6):
            for a in range(n):
                copy(a, k, sibling, me).wait_recv()
        for cp in started:
            cp.wait_send()
        for cp in local:
            cp.wait()

    return _sequencer_kernel(body, name, collective_id,
                             [jax.ShapeDtypeStruct((N_DEV,) + s.shape, s.dtype) for s in shards], shards)


def _chip_index(pos):
    return 2 * pos[0] + pos[1]


def _sequencer_to_sibling(name, collective_id, parts):
    n = len(parts)

    def body(*refs):
        p_refs, out_refs = refs[:n], refs[n:2 * n]
        send_sem, recv_sems, _ = refs[2 * n:]
        me = _position()
        sibling = _peer(me, 1)
        _handshake([sibling])
        copies = [pltpu.make_async_remote_copy(
            src_ref=p_refs[a].at[2 * q + sibling[2]], dst_ref=out_refs[a].at[q], send_sem=send_sem,
            recv_sem=recv_sems.at[0], device_id=sibling, device_id_type=MESH)
            for a in range(n) for q in range(N_DEV // 2)]
        for cp in copies:
            cp.start()
        for cp in copies:
            cp.wait_recv()
        for cp in copies:
            cp.wait_send()

    return _sequencer_kernel(body, name, collective_id,
                             [jax.ShapeDtypeStruct((N_DEV // 2,) + p.shape[1:], p.dtype) for p in parts], parts)


def _sequencer_to_chips(name, collective_id, sums):
    n = len(sums)

    def body(*refs):
        s_refs, out_refs = refs[:n], refs[n:2 * n]
        send_sem, recv_sems, local_sem = refs[2 * n:]
        me = _position()
        my_chip = _chip_index(me)
        peers = [_peer(me, 4), _peer(me, 2), _peer(me, 6)]
        _handshake(peers)
        local = [pltpu.make_async_copy(s_refs[a].at[my_chip], out_refs[a].at[my_chip], local_sem) for a in range(n)]
        sends = [pltpu.make_async_remote_copy(
            src_ref=s_refs[a].at[_chip_index(peer)], dst_ref=out_refs[a].at[my_chip], send_sem=send_sem,
            recv_sem=recv_sems.at[1 + j], device_id=peer, device_id_type=MESH)
            for a in range(n) for j, peer in enumerate(peers)]
        for cp in sends + local:
            cp.start()
        for j, peer in enumerate(peers):
            for a in range(n):
                pltpu.make_async_remote_copy(
                    src_ref=s_refs[a].at[my_chip], dst_ref=out_refs[a].at[_chip_index(peer)], send_sem=send_sem,
                    recv_sem=recv_sems.at[1 + j], device_id=peer, device_id_type=MESH).wait_recv()
        for cp in sends:
            cp.wait_send()
        for cp in local:
            cp.wait()

    return _sequencer_kernel(body, name, collective_id,
                             [jax.ShapeDtypeStruct(s.shape, s.dtype) for s in sums], sums)


def _chip_sum(name, parts, got, after=()):
    _, R, C = parts.shape
    tr = _tile(R, max(8, (3 << 20) // C), 16)
    n_after = len(after)
    limit = 6 * _nbytes((tr, C), parts.dtype) + VMEM_TEMP_ALLOWANCE

    def body(c_ref, p_ref, g_ref, *rest):
        o_ref = rest[n_after]
        o_ref[...] = (p_ref[...].astype(F32) + g_ref[...].astype(F32)).astype(o_ref.dtype)

    blk = pl.BlockSpec((None, tr, C), lambda q, i, c_ref: (q, i, 0))
    mine = pl.BlockSpec((None, tr, C), lambda q, i, c_ref: (2 * q + c_ref[0], i, 0))
    core = lax.axis_index("c").astype(jnp.int32).reshape(1)
    parts, got = [pltpu.with_memory_space_constraint(o, pltpu.HBM) for o in (parts, got)]
    after = [pltpu.with_memory_space_constraint(o, pltpu.HBM) for o in after]
    return pl.pallas_call(
        body, name=name, out_shape=pltpu.HBM((N_DEV // 2, R, C), parts.dtype),
        grid_spec=pltpu.PrefetchScalarGridSpec(
            num_scalar_prefetch=1, grid=(N_DEV // 2, R // tr),
            in_specs=[mine, blk] + [pl.BlockSpec(memory_space=pl.ANY)] * n_after, out_specs=blk),
        compiler_params=pltpu.CompilerParams(dimension_semantics=("parallel", "parallel"),
                                             vmem_limit_bytes=int(limit)),
    )(core, parts, got, *after)


def _all_reduce_rows(name, v):
    R, C = v.shape

    def body(v_ref, out_ref, gath, send_sems, recv_sems):
        me = _position()
        gath[_index(me)] = v_ref[...]
        sends = []
        for k in range(1, N_DEV):
            peer = _peer(me, k)
            sends.append(pltpu.make_async_remote_copy(
                src_ref=v_ref, dst_ref=gath.at[_index(me)], send_sem=send_sems.at[k - 1],
                recv_sem=recv_sems.at[k - 1], device_id=peer, device_id_type=MESH))
        for cp in sends:
            cp.start()
        for k in range(1, N_DEV):
            peer = _peer(me, k)
            pltpu.make_async_remote_copy(
                src_ref=v_ref, dst_ref=gath.at[_index(peer)], send_sem=send_sems.at[k - 1],
                recv_sem=recv_sems.at[k - 1], device_id=peer, device_id_type=MESH).wait_recv()
        for cp in sends:
            cp.wait_send()
        tot = gath[0]
        for s in range(1, N_DEV):
            tot = tot + gath[s]
        out_ref[...] = tot

    return pl.pallas_call(
        body, name=name, out_shape=jax.ShapeDtypeStruct((R, C), F32),
        in_specs=[VMEM_SPEC], out_specs=VMEM_SPEC,
        scratch_shapes=[pltpu.VMEM((N_DEV, R, C), F32), pltpu.SemaphoreType.DMA((7,)),
                        pltpu.SemaphoreType.DMA((7,))],
    )(v)


def _adamw_math(g, w, m, v):
    m = ADAM_B1 * m + (1.0 - ADAM_B1) * g
    v = ADAM_B2 * v + (1.0 - ADAM_B2) * (g * g)
    m_hat = m / (1.0 - ADAM_B1 ** ADAM_STEP)
    v_hat = v / (1.0 - ADAM_B2 ** ADAM_STEP)
    delta = -ADAM_LR * (m_hat / (jnp.sqrt(v_hat) + ADAM_EPS) + ADAM_WD * w)
    return delta, m, v


def _adamw_tile(n):
    def body(p_ref, w_ref, m_ref, v_ref, g_ref, d_ref, nm_ref, nv_ref):
        g = p_ref[0].astype(F32)
        for s in range(1, n):
            g = g + p_ref[s].astype(F32)
        g_ref[...] = g
        d_ref[...], nm_ref[...], nv_ref[...] = _adamw_math(g, w_ref[...], m_ref[...], v_ref[...])
    return body


class _SideJob(NamedTuple):
    operands: list
    in_blocks: list
    out_blocks: list
    out_shapes: list
    body: object


def _adamw_job(parts, w, m, v):
    n, R, C = parts.shape
    lead = w.ndim - 2
    if R % (SIDE_TILES * 16) == 0:
        tile, index = (R // SIDE_TILES, C), lambda t: (t, 0)
    elif C % (SIDE_TILES * 128) == 0:
        tile, index = (R, C // SIDE_TILES), lambda t: (0, t)
    else:
        return None
    p_blk = ((n,) + tile, lambda t: (0,) + index(t))
    w_blk = ((None,) * lead + tile, lambda t: (0,) * lead + index(t))
    return _SideJob([parts, w, m, v], [p_blk, w_blk, w_blk, w_blk], [w_blk] * 4,
                    [jax.ShapeDtypeStruct(w.shape, F32)] * 4, _adamw_tile(n))


def _adamw(name, parts, w, m, v, after=()):
    n, R, C = parts.shape
    tr = _tile(R, max(8, (1 << 18) // C), 16)
    blk = _rows_spec(w, tr)
    out = jax.ShapeDtypeStruct(w.shape, F32)
    return _call(_adamw_tile(n), name, (R // tr,), [pl.BlockSpec((n, tr, C), lambda i: (0, i, 0)), blk, blk, blk],
                 (blk, blk, blk, blk), (out, out, out, out), [parts, w, m, v], sem=("parallel",), after=after)


def kernel(x, mem, g_mix, w_in, conv_a_w, conv_a_b, ln_a_g, ln_a_b, conv_b_w, w_out, g_xattn, g_mem, w_q, w_k, w_v, w_o, g_ffn, w_gate, w_up, conv_f_w, w_down, g_final, loss_target, m_g_mix, m_w_in, m_conv_a_w, m_conv_a_b, m_ln_a_g, m_ln_a_b, m_conv_b_w, m_w_out, m_g_xattn, m_g_mem, m_w_q, m_w_k, m_w_v, m_w_o, m_g_ffn, m_w_gate, m_w_up, m_conv_f_w, m_w_down, m_g_final, v_g_mix, v_w_in, v_conv_a_w, v_conv_a_b, v_ln_a_g, v_ln_a_b, v_conv_b_w, v_w_out, v_g_xattn, v_g_mem, v_w_q, v_w_k, v_w_v, v_w_o, v_g_ffn, v_w_gate, v_w_up, v_conv_f_w, v_w_down, v_g_final):
    T, D = x.shape[1], x.shape[2]
    Mm = mem.shape[1]
    CW = conv_a_b.shape[1]
    INB = w_in.shape[2]
    FB = w_gate.shape[2]
    KA, KB, KF = conv_a_w.shape[1], conv_b_w.shape[1], conv_f_w.shape[1]
    DB = D // N_DEV
    assert 5 * CW == N_DEV * INB and 2 * CW == D

    x2, mem2, tgt = x[0], mem[0], loss_target[0]
    g_mem2, g_final2 = g_mem.reshape(1, D), g_final.reshape(1, D)

    def bf16(name, w):
        return _cast_bf16("cast_" + name, w)

    Win, caw, cbw = _sequencer_all_gather(
        "ag_in", AG_ID, [bf16("w_in", w_in), conv_a_w[0], conv_b_w[0]])
    Wout, = _sequencer_all_gather("ag_out", AG_ID, [bf16("w_out", w_out)])
    Wq, Wk, Wv, Wo = _sequencer_all_gather(
        "ag_attn", AG_ID, [bf16("w_q", w_q), bf16("w_k", w_k), bf16("w_v", w_v), bf16("w_o", w_o)])
    def transposed(w):
        return jnp.transpose(w[0])

    gate_t = [transposed(a) for a in (w_gate, m_w_gate, v_w_gate)]
    up_t = [transposed(a) for a in (w_up, m_w_up, v_w_up)]
    WgateT, cfw = _sequencer_all_gather("ag_gate", AG_ID, [bf16("w_gate", gate_t[0]), conv_f_w[0]])
    WupT, = _sequencer_all_gather("ag_up", AG_ID, [bf16("w_up", up_t[0])])
    Wdown, = _sequencer_all_gather("ag_down", AG_ID, [bf16("w_down", w_down)])
    Wout, Wq, Wk, Wv, Wo = [w.reshape(D, D) for w in (Wout, Wq, Wk, Wv, Wo)]
    caw = jnp.transpose(caw, (1, 0, 2)).reshape(KA, CW)
    cbw = jnp.transpose(cbw, (1, 0, 2)).reshape(KB, CW)

    xn1 = _rms_fwd("rms_mix", x2, g_mix)
    proj = _matmul("mm_proj", "nn", xn1, Win, M=T, N=INB, K=D, nb=N_DEV, b_lay="blk", o_lay="col", tn=INB,
                   out_dtype=BF16)
    mix, u1 = _mixer_fwd(proj, caw, conv_a_b, ln_a_g, ln_a_b, cbw, T, CW)
    h1 = _matmul("mm_h1", "nn", mix, Wout.reshape(2, CW, D), M=T, N=D, K=CW, nb=2, a_lay="blk", b_lay="blk",
                 red_block=True, group=2, tn=512, res=x2)
    xn2 = _rms_fwd("rms_xattn", h1, g_xattn)
    q = _matmul("mm_q", "nn", xn2, Wq, M=T, N=D, K=D, out_dtype=BF16, tn=1024)
    memn = _rms_fwd("rms_mem", mem2, g_mem2, after=[q])
    kk = _matmul("mm_k", "nn", memn, Wk, M=Mm, N=D, K=D, out_dtype=BF16)
    vv = _matmul("mm_v", "nn", memn, Wv, M=Mm, N=D, K=D, out_dtype=BF16)
    o = _attn_fwd(q, kk, vv)
    h2 = _matmul("mm_h2", "nn", o, Wo, M=T, N=D, K=D, res=h1, tn=1024)
    xn3 = _rms_fwd("rms_ffn", h2, g_ffn)
    gpre = _matmul("mm_gate", "nt", xn3, WgateT, M=T, N=FB, K=D, nb=N_DEV, b_lay="blk", o_lay="blk", tn=FB,
                   out_dtype=BF16)
    up = _matmul("mm_up", "nt", xn3, WupT, M=T, N=FB, K=D, nb=N_DEV, b_lay="blk", o_lay="blk", tn=FB,
                 out_dtype=BF16)
    f = _ffn_act_fwd(gpre, up, cfw)
    h3 = _matmul("mm_h3", "nn", f, Wdown, M=T, N=D, K=FB, nb=N_DEV, a_lay="blk", b_lay="blk", red_block=True,
                 group=2, res=h2)
    dh3, dh3b, loss_part, dg_final = _loss_head(h3, tgt, g_final2)

    wmv = {"w_in": (w_in, m_w_in, v_w_in), "conv_a_w": (conv_a_w, m_conv_a_w, v_conv_a_w),
           "conv_b_w": (conv_b_w, m_conv_b_w, v_conv_b_w), "w_out": (w_out, m_w_out, v_w_out),
           "w_q": (w_q, m_w_q, v_w_q), "w_k": (w_k, m_w_k, v_w_k), "w_v": (w_v, m_w_v, v_w_v),
           "w_o": (w_o, m_w_o, v_w_o), "w_gate": gate_t, "w_up": up_t,
           "conv_f_w": (conv_f_w, m_conv_f_w, v_conv_f_w), "w_down": (w_down, m_w_down, v_w_down)}
    res = {}
    pending = []

    def mm(*args, after=(), carry=(), **kwargs):
        behind = list(after) + pending
        pending.clear()
        jobs, alone = [], []
        for names, got in carry:
            for n, g in zip(names, got):
                job = _adamw_job(g, *wmv[n])
                (alone if job is None else jobs).append((n, g, job))
        results = None
        if jobs:
            out, results = _matmul(*args, after=behind, side=[job for _, _, job in jobs], **kwargs)
        else:
            out = _matmul(*args, after=behind, **kwargs)
        if results is None:
            alone, jobs = alone + jobs, []
        for (n, _, _), r in zip(jobs, results or []):
            res[n] = r
            pending.append(r[0])
        for n, g, _ in alone:
            res[n] = _adamw("adamw_" + n, g, *wmv[n], after=[out])
            pending.append(res[n][0])
        return out

    def to_sibling(tag, named_parts):
        got = _sequencer_to_sibling("rs1_" + tag, SIBLING_ID, [p for _, p in named_parts])
        return named_parts, got

    def to_chips(tag, stage1, after):
        named_parts, got = stage1
        sums = [_chip_sum("sum_" + n, p, g, after=after) for (n, p), g in zip(named_parts, got)]
        pending.extend(sums)
        return [n for n, _ in named_parts], _sequencer_to_chips("rs2_" + tag, CHIPS_ID, sums)

    def finish(stage2, after):
        names, got = stage2
        for n, g in zip(names, got):
            w, m, v = wmv[n]
            res[n] = _adamw("adamw_" + n, g, w, m, v, after=after)
            pending.append(res[n][0])

    def row_blocks(dw):
        return dw.reshape(N_DEV, DB, D)

    def conv_blocks(dw, K):
        return jnp.transpose(dw.reshape(K, N_DEV, CW // N_DEV), (1, 0, 2))

    dWdown = mm("mm_dw_down", "tn", f, dh3b, M=FB, N=D, K=T, nb=N_DEV, a_lay="blk", o_lay="blk",
                     out_dtype=BF16, tm=FB)
    s_down = to_sibling("down", [("w_down", dWdown)])
    df = mm("mm_df", "nt", dh3b, Wdown, M=T, N=FB, K=D, nb=N_DEV, b_lay="blk", o_lay="blk", tn=FB, out_dtype=BF16,
                 after=[dWdown])
    dgpre, dup, dcfw = _ffn_act_bwd(df, gpre, up, cfw)
    c_down = to_chips("down", s_down, after=[dgpre])
    dWgate = mm("mm_dw_gate", "tn", dgpre, xn3, M=FB, N=D, K=T, nb=N_DEV, a_lay="blk", o_lay="blk",
                     out_dtype=BF16, tm=FB)
    s_gate = to_sibling("gate", [("w_gate", dWgate), ("conv_f_w", dcfw)])
    dWup = mm("mm_dw_up", "tn", dup, xn3, M=FB, N=D, K=T, nb=N_DEV, a_lay="blk", o_lay="blk",
                   out_dtype=BF16, tm=FB, after=[dWgate])
    s_up = to_sibling("up", [("w_up", dWup)])
    dxn3 = mm("mm_dxn3_gate", "nn", dgpre, WgateT, M=T, N=D, K=FB, nb=N_DEV, a_lay="blk", b_lay="blk",
                   red_block=True, group=2, out_dtype=BF16, after=[dWup], carry=[c_down])
    c_gate = to_chips("gate", s_gate, after=[dxn3])
    dxn3 = mm("mm_dxn3_up", "nn", dup, WupT, M=T, N=D, K=FB, nb=N_DEV, a_lay="blk", b_lay="blk",
                   red_block=True, group=2, out_dtype=BF16, res=dxn3)
    c_up = to_chips("up", s_up, after=[dxn3])
    dh2, dh2b, dg_ffn = _rms_bwd("rms_bwd_ffn", dxn3, h2, g_ffn, dh3)

    dWo = mm("mm_dw_o", "tn", o, dh2b, M=D, N=D, K=T, out_dtype=BF16, tn=1024)
    s_o = to_sibling("o", [("w_o", row_blocks(dWo))])
    do = mm("mm_do", "nt", dh2b, Wo, M=T, N=D, K=D, out_dtype=BF16, tn=1024, after=[dWo])
    dq, dk, dv = _attn_bwd(q, kk, vv, do)
    dWq = mm("mm_dw_q", "tn", xn2, dq, M=D, N=D, K=T, out_dtype=BF16, carry=[c_gate])
    s_q = to_sibling("q", [("w_q", row_blocks(dWq))])
    dxn2 = mm("mm_dxn2", "nt", dq, Wq, M=T, N=D, K=D, out_dtype=BF16, after=[dWq], carry=[c_up])
    c_o = to_chips("o", s_o, after=[dxn2])
    c_q = to_chips("q", s_q, after=[dxn2])
    dh1, dh1b, dg_xattn = _rms_bwd("rms_bwd_xattn", dxn2, h1, g_xattn, dh2)
    dWk = mm("mm_dw_k", "tn", memn, dk, M=D, N=D, K=Mm, out_dtype=BF16, after=[dh1b])
    dWv = mm("mm_dw_v", "tn", memn, dv, M=D, N=D, K=Mm, out_dtype=BF16, after=[dh1b])
    s_kv = to_sibling("kv", [("w_k", row_blocks(dWk)), ("w_v", row_blocks(dWv))])
    dmemn = mm("mm_dmem_k", "nt", dk, Wk, M=Mm, N=D, K=D, after=[dWk, dWv])
    dmemn = mm("mm_dmem_v", "nt", dv, Wv, M=Mm, N=D, K=D, res=dmemn)
    dg_mem = _rms_bwd("rms_bwd_mem", dmemn, mem2, g_mem2)

    dWout = mm("mm_dw_out", "tn", mix, dh1b, M=CW, N=D, K=T, nb=2, a_lay="blk", o_lay="blk", out_dtype=BF16,
                    after=[dg_mem], carry=[c_o])
    s_out = to_sibling("out", [("w_out", row_blocks(dWout.reshape(D, D)))])
    dmix = mm("mm_dmix", "nt", dh1b, Wout, M=T, N=D, K=D, out_dtype=BF16, after=[dWout], carry=[c_q])
    c_kv = to_chips("kv", s_kv, after=[dmix])
    du1, dcv, dbg, dcaw, dcab, dlng, dlnb, dcbw = _mixer_bwd1(dmix, proj, u1, caw, ln_a_g, ln_a_b, cbw, T, CW)
    c_out = to_chips("out", s_out, after=[du1])
    dav, dag, dcg, dbh = _mixer_bwd2(du1, dcv, proj, caw, cbw, T, CW)
    dproj = jnp.concatenate([dav, dag, dbg, dcg, dbh], axis=1)
    dWin = mm("mm_dw_in", "tn", xn1, dproj, M=D, N=INB, K=T, nb=N_DEV, b_lay="col", o_lay="blk",
                   out_dtype=BF16, tm=512, tn=INB, carry=[c_kv])
    s_in = to_sibling("in", [("w_in", dWin), ("conv_a_w", conv_blocks(dcaw, KA)),
                             ("conv_b_w", conv_blocks(dcbw, KB))])
    finish(c_out, after=[dWin])
    c_in = to_chips("in", s_in, after=list(pending))
    dxn1 = mm("mm_dxn1", "nt", dproj, Win, M=T, N=D, K=INB, nb=N_DEV, a_lay="col", b_lay="blk",
                   red_block=True, group=2, out_dtype=BF16, after=[dWin])
    dx, _, dg_mix = _rms_bwd("rms_bwd_mix", dxn1, x2, g_mix, dh1, after=list(pending))

    def pair(a, b):
        return jnp.concatenate([a, b], axis=1)

    zeros_half = jnp.zeros((1, CW), F32)
    small_g = jnp.concatenate([
        dg_mix, pair(dcab, dlng), pair(dlnb, zeros_half), dg_xattn, dg_mem, dg_ffn, dg_final,
        jnp.broadcast_to(loss_part[:, :1], (1, D))], axis=0)
    small_sum = _all_reduce_rows("ar_small", small_g)
    loss = small_sum[7, 0]
    finish(c_in, after=[small_sum])

    def pack(a_mix, a_cab, a_lng, a_lnb, a_xattn, a_mem, a_ffn, a_final):
        return jnp.concatenate([a_mix, pair(a_cab, a_lng), pair(a_lnb, zeros_half), a_xattn, a_mem.reshape(1, D),
                                a_ffn, a_final.reshape(1, D), jnp.zeros((1, D), F32)], axis=0)

    small = _adamw("adamw_small", small_sum[None],
                   pack(g_mix, conv_a_b, ln_a_g, ln_a_b, g_xattn, g_mem, g_ffn, g_final),
                   pack(m_g_mix, m_conv_a_b, m_ln_a_g, m_ln_a_b, m_g_xattn, m_g_mem, m_g_ffn, m_g_final),
                   pack(v_g_mix, v_conv_a_b, v_ln_a_g, v_ln_a_b, v_g_xattn, v_g_mem, v_g_ffn, v_g_final))

    def unpack(a):
        return {"g_mix": a[0:1], "conv_a_b": a[1:2, :CW], "ln_a_g": a[1:2, CW:], "ln_a_b": a[2:3, :CW],
                "g_xattn": a[3:4], "g_mem": a[4], "g_ffn": a[5:6], "g_final": a[6]}

    small = [unpack(a) for a in small]
    order = ["g_mix", "w_in", "conv_a_w", "conv_a_b", "ln_a_g", "ln_a_b", "conv_b_w", "w_out", "g_xattn", "g_mem",
             "w_q", "w_k", "w_v", "w_o", "g_ffn", "w_gate", "w_up", "conv_f_w", "w_down", "g_final"]
    outs = [loss, dx[None]]
    for kind in range(4):
        for n in order:
            if n in ("w_gate", "w_up"):
                outs.append(jnp.transpose(res[n][kind])[None])
            else:
                outs.append(res[n][kind] if n in res else small[kind][n])
    return tuple(outs)
```

```python
import functools
from typing import NamedTuple

import jax
import jax.numpy as jnp
from jax import lax
from jax.experimental import pallas as pl
from jax.experimental.pallas import tpu as pltpu
from jax.experimental.pallas import tpu_sc as plsc

F32 = jnp.float32
BF16 = jnp.bfloat16

N_DEV = 8
EPS = 1e-6
GROUP_DIM = 128
N_XATTN_HEADS = 4
ADAM_LR = 0.001
ADAM_B1 = 0.9
ADAM_B2 = 0.999
ADAM_EPS = 1e-08
ADAM_WD = 0.01
ADAM_STEP = 10

AG_ID, SIBLING_ID, CHIPS_ID = 1, 2, 3

SIDE_TILES = 32
HALO = 32
VMEM_V7X_BYTES = 64 * 1024 * 1024
VMEM_TEMP_ALLOWANCE = 12 * 1024 * 1024

VMEM_SPEC = pl.BlockSpec(memory_space=pltpu.VMEM)
MESH = pl.DeviceIdType.MESH


def _tile(n, pref, align):
    if n <= pref:
        return n
    t = (pref // align) * align
    while t >= align:
        if n % t == 0:
            return t
        t -= align
    return n


def _nbytes(shape, dtype):
    if shape is None:
        return 0
    n = 1
    for d in shape:
        if d is not None:
            n *= d
    return n * jnp.dtype(dtype).itemsize


def _call(body, name, grid, in_specs, out_specs, out_shape, operands, scratch=(), sem=None, after=()):
    outs = out_shape if isinstance(out_shape, (tuple, list)) else (out_shape,)
    ospecs = out_specs if isinstance(out_specs, (tuple, list)) else (out_specs,)
    est = 0
    for spec, arr in list(zip(in_specs, operands)) + list(zip(ospecs, outs)):
        est += 2 * _nbytes(spec.block_shape, arr.dtype)
    for s in scratch:
        if getattr(s, "memory_space", None) == pltpu.VMEM:
            est += _nbytes(s.shape, s.dtype)
    limit = min(est + VMEM_TEMP_ALLOWANCE, VMEM_V7X_BYTES - 4 * 1024 * 1024)
    if sem is None:
        sem = ("arbitrary",) * len(grid)
    n_in, n_after = len(operands), len(after)
    operands = [pltpu.with_memory_space_constraint(o, pltpu.HBM) for o in operands]
    after = [pltpu.with_memory_space_constraint(o, pltpu.HBM) for o in after]
    in_hbm = [pltpu.HBM(o.shape, o.dtype) for o in outs]
    out_shape = in_hbm if isinstance(out_shape, (tuple, list)) else in_hbm[0]

    def ordered_body(*refs):
        body(*refs[:n_in], *refs[n_in + n_after:])

    return pl.pallas_call(
        ordered_body if n_after else body, name=name, grid=grid,
        in_specs=list(in_specs) + [pl.BlockSpec(memory_space=pl.ANY)] * n_after,
        out_specs=out_specs, out_shape=out_shape, scratch_shapes=list(scratch),
        compiler_params=pltpu.CompilerParams(dimension_semantics=sem, vmem_limit_bytes=int(limit)),
    )(*operands, *after)


_DOT_DIMS = {"nn": (((1,), (0,)), ((), ())), "nt": (((1,), (1,)), ((), ())), "tn": (((0,), (0,)), ((), ()))}


def _operand_spec(layout, tr, tc, cols_per_block, pick, group=None):
    if layout == "plain":
        return pl.BlockSpec((tr, tc), lambda *g: pick(*g)[1:])
    if layout == "blk":
        return pl.BlockSpec((group, tr, tc), lambda *g: pick(*g))
    assert layout == "col"
    if group:
        assert tc == cols_per_block
        return pl.BlockSpec((tr, group * tc), lambda *g: (pick(*g)[1], pick(*g)[0]))
    per = cols_per_block // tc
    return pl.BlockSpec((tr, tc), lambda *g: (pick(*g)[1], pick(*g)[0] * per + pick(*g)[2]))


def _matmul(name, dims, a, b, *, M, N, K, nb=1, a_lay="plain", b_lay="plain", o_lay="plain",
            red_block=False, group=None, out_dtype=F32, res=None, tm=1024, tn=None, after=(), side=()):
    tm = _tile(M, tm, 128 if dims == "tn" else 16)
    tn = _tile(N, tn or (1024 if red_block else 512), 128)
    tk = K
    gi, gj, gk = M // tm, N // tn, K // tk
    if red_block:
        grid = (gi, gj, nb // (group or 1), gk)
        unpack = lambda i, j, bb, k: (bb, i, j, k)
        red_axes, sem = (2, 3), ("parallel", "parallel", "arbitrary", "arbitrary")
    else:
        grid = (nb, gi, gj, gk)
        unpack = lambda bb, i, j, k: (bb, i, j, k)
        red_axes, sem = (3,), ("parallel", "parallel", "parallel", "arbitrary")

    def picker(f):
        return lambda *g: f(*unpack(*g))

    if dims == "tn":
        a_spec = _operand_spec(a_lay, tk, tm, M, picker(lambda bb, i, j, k: (bb, k, i)), group)
    else:
        a_spec = _operand_spec(a_lay, tm, tk, K, picker(lambda bb, i, j, k: (bb, i, k)), group)
    if dims == "nt":
        b_spec = _operand_spec(b_lay, tn, tk, K, picker(lambda bb, i, j, k: (bb, j, k)), group)
    else:
        b_spec = _operand_spec(b_lay, tk, tn, N, picker(lambda bb, i, j, k: (bb, k, j)), group)
    o_spec = _operand_spec(o_lay, tm, tn, N, picker(lambda bb, i, j, k: (bb, i, j)))
    if o_lay == "plain":
        out_shape = jax.ShapeDtypeStruct((M, N), out_dtype)
    elif o_lay == "blk":
        out_shape = jax.ShapeDtypeStruct((nb, M, N), out_dtype)
    else:
        out_shape = jax.ShapeDtypeStruct((M, nb * N), out_dtype)
    n_red = [grid[ax] for ax in red_axes]
    has_res = res is not None
    one_step = all(n == 1 for n in n_red)

    def contract(a_ref, b_ref):
        if group:
            parts = [(a_ref[p] if a_lay == "blk" else a_ref[:, p * K:(p + 1) * K], b_ref[p]) for p in range(group)]
        else:
            parts = [(a_ref[...], b_ref[...])]
        r = None
        for a_part, b_part in parts:
            d = lax.dot_general(a_part, b_part, _DOT_DIMS[dims], preferred_element_type=F32)
            r = d if r is None else r + d
        return r

    def step_index(*g):
        s = g[0]
        for ax in range(1, len(grid)):
            s = s * grid[ax] + g[ax]
        return s

    def side_spec(block):
        shape, index = block
        return pl.BlockSpec(shape, lambda *g: index(jnp.minimum(step_index(*g), SIDE_TILES - 1)))

    asked_to_carry = bool(side)
    if functools.reduce(lambda p, q: p * q, grid) < SIDE_TILES:
        side = ()
    n_main_in = 3 if has_res else 2
    n_side_in = sum(len(job.operands) for job in side)
    n_side_out = sum(len(job.out_shapes) for job in side)

    def side_work(side_in, side_out):
        @pl.when(step_index(*[pl.program_id(ax) for ax in range(len(grid))]) < SIDE_TILES)
        def _():
            i0 = o0 = 0
            for job in side:
                n_i, n_o = len(job.operands), len(job.out_shapes)
                job.body(*side_in[i0:i0 + n_i], *side_out[o0:o0 + n_o])
                i0, o0 = i0 + n_i, o0 + n_o

    def body(*refs):
        a_ref, b_ref = refs[:2]
        side_in = refs[n_main_in:n_main_in + n_side_in]
        o_ref = refs[n_main_in + n_side_in]
        side_out = refs[n_main_in + n_side_in + 1:n_main_in + n_side_in + 1 + n_side_out]
        if one_step:
            r = contract(a_ref, b_ref)
            if has_res:
                r = r + refs[2][...]
            o_ref[...] = r.astype(o_ref.dtype)
        else:
            acc = refs[-1]
            first = functools.reduce(jnp.logical_and, [pl.program_id(ax) == 0 for ax in red_axes])
            last = functools.reduce(jnp.logical_and,
                                    [pl.program_id(ax) == n - 1 for ax, n in zip(red_axes, n_red)])

            @pl.when(first)
            def _():
                acc[...] = jnp.zeros_like(acc)

            acc[...] += contract(a_ref, b_ref)

            @pl.when(last)
            def _():
                r = acc[...]
                if has_res:
                    r = r + refs[2][...]
                o_ref[...] = r.astype(o_ref.dtype)
        if side:
            side_work(side_in, side_out)

    in_specs = [a_spec, b_spec]
    operands = [a, b]
    if has_res:
        in_specs.append(_operand_spec("plain", tm, tn, N, picker(lambda bb, i, j, k: (bb, i, j))))
        operands.append(res)
    out_specs, out_shapes = [o_spec], [out_shape]
    for job in side:
        in_specs += [side_spec(blk) for blk in job.in_blocks]
        operands += job.operands
        out_specs += [side_spec(blk) for blk in job.out_blocks]
        out_shapes += job.out_shapes
    scratch = [] if one_step else [pltpu.VMEM((tm, tn), F32)]
    if not side:
        out = _call(body, name, grid, in_specs, o_spec, out_shape, operands, scratch=scratch, sem=sem, after=after)
        return (out, None) if asked_to_carry else out
    outs = _call(body, name, grid, in_specs, tuple(out_specs), tuple(out_shapes), operands, scratch=scratch,
                 sem=("arbitrary",) * len(grid), after=after)
    results, pos = [], 1
    for job in side:
        results.append(tuple(outs[pos:pos + len(job.out_shapes)]))
        pos += len(job.out_shapes)
    return outs[0], results


def _rows_spec(arr, tr):
    lead = arr.ndim - 2
    return pl.BlockSpec((None,) * lead + (tr, arr.shape[-1]), lambda i: (0,) * lead + (i, 0))


def _cast_bf16(name, w):
    R, C = w.shape[-2:]
    tr = _tile(R, max(8, (3 << 20) // C), 16)

    def body(w_ref, o_ref):
        o_ref[...] = w_ref[...].astype(BF16)

    return _call(body, name, (R // tr,), [_rows_spec(w, tr)],
                 pl.BlockSpec((tr, C), lambda i: (i, 0)), jax.ShapeDtypeStruct((R, C), BF16), [w],
                 sem=("parallel",))


def _rms_fwd(name, x, g, after=()):
    T, D = x.shape
    tm = _tile(T, 128, 16)

    def body(x_ref, g_ref, o_ref):
        xv = x_ref[...]
        r = lax.rsqrt(jnp.mean(xv * xv, axis=-1, keepdims=True) + EPS)
        o_ref[...] = (xv * r * g_ref[...]).astype(BF16)

    return _call(body, name, (T // tm,),
                 [pl.BlockSpec((tm, D), lambda i: (i, 0)), pl.BlockSpec((1, D), lambda i: (0, 0))],
                 pl.BlockSpec((tm, D), lambda i: (i, 0)), jax.ShapeDtypeStruct((T, D), BF16), [x, g],
                 sem=("parallel",), after=after)


def _rms_bwd(name, dxn, x, g, dh=None, after=()):
    T, D = x.shape
    tm = _tile(T, 128, 16)
    with_dx = dh is not None
    n_i = T // tm
    SLOTS = 3

    def math(dy, xv, g_ref, dg_ref):
        r = lax.rsqrt(jnp.mean(xv * xv, axis=-1, keepdims=True) + EPS)
        xh = xv * r

        @pl.when(pl.program_id(0) == 0)
        def _():
            dg_ref[...] = jnp.zeros_like(dg_ref)

        dg_ref[...] += jnp.sum(dy * xh, axis=0, keepdims=True)
        dyg = dy * g_ref[...]
        return r * (dyg - xh * jnp.mean(dyg * xh, axis=-1, keepdims=True))

    def body_dg(dxn_ref, x_ref, g_ref, dg_ref):
        math(dxn_ref[...].astype(F32), x_ref[...], g_ref, dg_ref)

    def body_dx(dxn_hbm, x_hbm, dh_hbm, g_ref, o_ref, ob_ref, dg_ref, dxn_buf, x_buf, dh_buf, sems):
        i = pl.program_id(0)

        def copies(t, slot):
            rows = pl.ds(pl.multiple_of(t * tm, tm), tm)
            return [pltpu.make_async_copy(src.at[rows], buf.at[slot], sems.at[k, slot])
                    for k, (src, buf) in enumerate(((dxn_hbm, dxn_buf), (x_hbm, x_buf), (dh_hbm, dh_buf)))]

        @pl.when(i == 0)
        def _():
            for t in range(min(SLOTS - 1, n_i)):
                for cp in copies(t, t):
                    cp.start()

        @pl.when(i + SLOTS - 1 < n_i)
        def _():
            for cp in copies(i + SLOTS - 1, (i + SLOTS - 1) % SLOTS):
                cp.start()

        slot = i % SLOTS
        for cp in copies(i, slot):
            cp.wait()
        tot = dh_buf[slot] + math(dxn_buf[slot].astype(F32), x_buf[slot], g_ref, dg_ref)
        o_ref[...] = tot
        ob_ref[...] = tot.astype(BF16)

    row = pl.BlockSpec((tm, D), lambda i: (i, 0))
    vec = pl.BlockSpec((1, D), lambda i: (0, 0))
    if with_dx:
        hbm = pl.BlockSpec(memory_space=pltpu.HBM)
        return _call(body_dx, name, (n_i,), [hbm, hbm, hbm, vec], (row, row, vec),
                     (jax.ShapeDtypeStruct((T, D), F32), jax.ShapeDtypeStruct((T, D), BF16),
                      jax.ShapeDtypeStruct((1, D), F32)), [dxn, x, dh, g], after=after,
                     scratch=[pltpu.VMEM((SLOTS, tm, D), dxn.dtype), pltpu.VMEM((SLOTS, tm, D), F32),
                              pltpu.VMEM((SLOTS, tm, D), F32), pltpu.SemaphoreType.DMA((3, SLOTS))])
    return _call(body_dg, name, (n_i,), [row, row, vec], vec, jax.ShapeDtypeStruct((1, D), F32), [dxn, x, g])


def _loss_head(h, target, g):
    T, D = h.shape
    tm = _tile(T, 128, 16)

    def body(h_ref, t_ref, g_ref, o_ref, ob_ref, loss_ref, dg_ref):
        xv = h_ref[...]
        gv = g_ref[...]
        r = lax.rsqrt(jnp.mean(xv * xv, axis=-1, keepdims=True) + EPS)
        xh = xv * r
        e = xh * gv - t_ref[...]

        @pl.when(pl.program_id(0) == 0)
        def _():
            dg_ref[...] = jnp.zeros_like(dg_ref)
            loss_ref[...] = jnp.zeros_like(loss_ref)

        loss_ref[...] += 0.5 * jnp.sum(jnp.mean(e * e, axis=-1, keepdims=True), axis=0, keepdims=True)
        dy = e * (1.0 / D)
        dg_ref[...] += jnp.sum(dy * xh, axis=0, keepdims=True)
        dyg = dy * gv
        dx = r * (dyg - xh * jnp.mean(dyg * xh, axis=-1, keepdims=True))
        o_ref[...] = dx
        ob_ref[...] = dx.astype(BF16)

    row = pl.BlockSpec((tm, D), lambda i: (i, 0))
    vec = pl.BlockSpec((1, D), lambda i: (0, 0))
    return _call(body, "loss_head", (T // tm,), [row, row, vec],
                 (row, row, pl.BlockSpec((1, 128), lambda i: (0, 0)), vec),
                 (jax.ShapeDtypeStruct((T, D), F32), jax.ShapeDtypeStruct((T, D), BF16),
                  jax.ShapeDtypeStruct((1, 128), F32), jax.ShapeDtypeStruct((1, D), F32)), [h, target, g])


ROW_CHUNK = 64
SUBLANES = 8


def _col_chunks(width):
    return [slice(c0, min(c0 + GROUP_DIM, width)) for c0 in range(0, width, GROUP_DIM)]


def _row_chunks(n_rows):
    return [(r0, min(ROW_CHUNK, n_rows - r0)) for r0 in range(0, n_rows, ROW_CHUNK)]


def _pad_rows(K):
    return -(-(K - 1) // SUBLANES) * SUBLANES


def _shifted_back(buf, K, r0, nr, cs):
    pad = _pad_rows(K)
    win = buf[pl.ds(HALO + r0 - pad, nr + pad), cs]
    for b in range(min(SUBLANES, K)):
        rolled = win if b == 0 else pltpu.roll(win, b, axis=0)
        for a in range((K - 1 - b) // SUBLANES + 1):
            yield K - 1 - (SUBLANES * a + b), rolled[pad - SUBLANES * a:pad - SUBLANES * a + nr]


def _conv_fwd(buf, w_ref, K, r0, nr, cs):
    y = None
    for k, xs in _shifted_back(buf, K, r0, nr, cs):
        term = xs * w_ref[pl.ds(k, 1), cs]
        y = term if y is None else y + term
    return y


def _conv_bwd_input(buf, w_ref, K, r0, nr, cs):
    pad = _pad_rows(K)
    win = buf[pl.ds(r0, nr + pad), cs]
    dx = None
    for b in range(min(SUBLANES, K)):
        rolled = win if b == 0 else pltpu.roll(win, nr + pad - b, axis=0)
        for a in range((K - 1 - b) // SUBLANES + 1):
            k = K - 1 - (SUBLANES * a + b)
            term = rolled[SUBLANES * a:SUBLANES * a + nr] * w_ref[pl.ds(k, 1), cs]
            dx = term if dx is None else dx + term
    return dx


def _fold_rows(v):
    nr, lanes = v.shape
    if nr % SUBLANES:
        return jnp.sum(v, axis=0, keepdims=True)
    return jnp.sum(v.reshape(nr // SUBLANES, SUBLANES, lanes), axis=0)


def _conv_bwd_weight(accs, dy, buf, K, r0, nr, cs):
    accs = list(accs)
    for k, xs in _shifted_back(buf, K, r0, nr, cs):
        accs[k] = accs[k] + _fold_rows(dy * xs)
    return accs


def _add_row(ref, row, cs, acc):
    ref[pl.ds(row, 1), cs] += jnp.sum(acc, axis=0, keepdims=True)


def _sigmoid(z):
    return 0.5 * jnp.tanh(0.5 * z) + 0.5


def _silu_grad(z, sig):
    return sig * (1.0 + z * (1.0 - sig))


def _group_norm(xg):
    xc = xg - jnp.mean(xg, axis=-1, keepdims=True)
    rstd = lax.rsqrt(jnp.mean(xc * xc, axis=-1, keepdims=True) + EPS)
    return xc * rstd, rstd


def _mixer_tiles(T, CW):
    tm = _tile(T, 512, HALO)
    tc = _tile(CW, 256, GROUP_DIM)
    return tm, tc, tm // HALO, CW // tc


def _mixer_fwd(proj, caw, cab, lng, lnb, cbw, T, CW):
    KA, KB = caw.shape[0], cbw.shape[0]
    tm, tc, hb, nc = _mixer_tiles(T, CW)

    def sec(s):
        return pl.BlockSpec((tm, tc), lambda i, c: (i, s * nc + c))

    def sec_prev(s):
        return pl.BlockSpec((HALO, tc), lambda i, c: (jnp.maximum(i * hb - 1, 0), s * nc + c))

    def chan(rows):
        return pl.BlockSpec((rows, tc), lambda i, c: (0, c))

    def body(av, ag, bg, cg, bh, avh, agh, cgh, bhh, caw_ref, cab_ref, lng_ref, lnb_ref, cbw_ref,
             mix_ref, u1_ref, bufa, bufb):
        first = pl.program_id(0) == 0
        bufa[pl.ds(0, HALO), :] = jnp.where(first, 0.0, avh[...].astype(F32) * _sigmoid(agh[...].astype(F32)))
        bufb[pl.ds(0, HALO), :] = jnp.where(first, 0.0, cgh[...].astype(F32) * bhh[...].astype(F32))
        for cs in _col_chunks(tc):
            for r0, nr in _row_chunks(tm):
                rows = pl.ds(r0, nr)
                bufa[pl.ds(HALO + r0, nr), cs] = av[rows, cs].astype(F32) * _sigmoid(ag[rows, cs].astype(F32))
                bufb[pl.ds(HALO + r0, nr), cs] = cg[rows, cs].astype(F32) * bh[rows, cs].astype(F32)
        for cs in _col_chunks(tc):
            for r0, nr in _row_chunks(tm):
                rows = pl.ds(r0, nr)
                u1 = _conv_fwd(bufa, caw_ref, KA, r0, nr, cs) + cab_ref[:, cs]
                u1_ref[rows, cs] = u1
                y, _ = _group_norm(u1)
                z = y * lng_ref[:, cs] + lnb_ref[:, cs]
                mix_ref[0, rows, cs] = (z * _sigmoid(z)).astype(BF16)
                mix_ref[1, rows, cs] = (bg[rows, cs].astype(F32) * _conv_fwd(bufb, cbw_ref, KB, r0, nr, cs)).astype(BF16)

    in_specs = [sec(0), sec(1), sec(2), sec(3), sec(4), sec_prev(0), sec_prev(1), sec_prev(3), sec_prev(4),
                chan(KA), chan(1), chan(1), chan(1), chan(KB)]
    operands = [proj] * 9 + [caw, cab, lng, lnb, cbw]
    return _call(body, "mixer_fwd", (T // tm, nc), in_specs,
                 (pl.BlockSpec((2, tm, tc), lambda i, c: (0, i, c)), pl.BlockSpec((tm, tc), lambda i, c: (i, c))),
                 (jax.ShapeDtypeStruct((2, T, CW), BF16), jax.ShapeDtypeStruct((T, CW), F32)), operands,
                 scratch=[pltpu.VMEM((HALO + tm, tc), F32), pltpu.VMEM((HALO + tm, tc), F32)],
                 sem=("parallel", "parallel"))


def _mixer_bwd1(dmix, proj, u1, caw, lng, lnb, cbw, T, CW):
    KA, KB = caw.shape[0], cbw.shape[0]
    tm, tc, hb, nc = _mixer_tiles(T, CW)

    def sec(s):
        return pl.BlockSpec((tm, tc), lambda c, i: (i, s * nc + c))

    def sec_prev(s):
        return pl.BlockSpec((HALO, tc), lambda c, i: (jnp.maximum(i * hb - 1, 0), s * nc + c))

    def chan(rows):
        return pl.BlockSpec((rows, tc), lambda c, i: (0, c))

    tile = pl.BlockSpec((tm, tc), lambda c, i: (i, c))

    def body(du, dv, u1_ref, av, ag, bg, cg, bh, avh, agh, cgh, bhh, lng_ref, lnb_ref, cbw_ref,
             du1_ref, dcv_ref, dbg_ref, dcaw_ref, dcab_ref, dlng_ref, dlnb_ref, dcbw_ref, bufa, bufb):
        first = pl.program_id(1) == 0

        @pl.when(first)
        def _():
            for r in (dcaw_ref, dcab_ref, dlng_ref, dlnb_ref, dcbw_ref):
                r[...] = jnp.zeros_like(r)

        bufa[pl.ds(0, HALO), :] = jnp.where(first, 0.0, avh[...].astype(F32) * _sigmoid(agh[...].astype(F32)))
        bufb[pl.ds(0, HALO), :] = jnp.where(first, 0.0, cgh[...].astype(F32) * bhh[...].astype(F32))
        for cs in _col_chunks(tc):
            for r0, nr in _row_chunks(tm):
                rows = pl.ds(r0, nr)
                bufa[pl.ds(HALO + r0, nr), cs] = av[rows, cs].astype(F32) * _sigmoid(ag[rows, cs].astype(F32))
                bufb[pl.ds(HALO + r0, nr), cs] = cg[rows, cs].astype(F32) * bh[rows, cs].astype(F32)
        for cs in _col_chunks(tc):
            lanes = cs.stop - cs.start
            zero = jnp.zeros((SUBLANES, lanes), F32)
            a_lng, a_lnb, a_cab = zero, zero, zero
            a_caw, a_cbw = [zero] * KA, [zero] * KB
            gamma, beta = lng_ref[:, cs], lnb_ref[:, cs]
            for r0, nr in _row_chunks(tm):
                rows = pl.ds(r0, nr)
                y, rstd = _group_norm(u1_ref[rows, cs])
                z = y * gamma + beta
                dz = du[rows, cs].astype(F32) * _silu_grad(z, _sigmoid(z))
                a_lng = a_lng + _fold_rows(dz * y)
                a_lnb = a_lnb + _fold_rows(dz)
                dy = dz * gamma
                du1 = rstd * (dy - jnp.mean(dy, axis=-1, keepdims=True)
                              - y * jnp.mean(dy * y, axis=-1, keepdims=True))
                du1_ref[rows, cs] = du1
                a_cab = a_cab + _fold_rows(du1)
                a_caw = _conv_bwd_weight(a_caw, du1, bufa, KA, r0, nr, cs)

                dvv = dv[rows, cs].astype(F32)
                dbg_ref[rows, cs] = (dvv * _conv_fwd(bufb, cbw_ref, KB, r0, nr, cs)).astype(BF16)
                dcv = dvv * bg[rows, cs].astype(F32)
                dcv_ref[rows, cs] = dcv
                a_cbw = _conv_bwd_weight(a_cbw, dcv, bufb, KB, r0, nr, cs)
            _add_row(dlng_ref, 0, cs, a_lng)
            _add_row(dlnb_ref, 0, cs, a_lnb)
            _add_row(dcab_ref, 0, cs, a_cab)
            for k in range(KA):
                _add_row(dcaw_ref, k, cs, a_caw[k])
            for k in range(KB):
                _add_row(dcbw_ref, k, cs, a_cbw[k])

    in_specs = [sec(0), sec(1), tile, sec(0), sec(1), sec(2), sec(3), sec(4),
                sec_prev(0), sec_prev(1), sec_prev(3), sec_prev(4), chan(1), chan(1), chan(KB)]
    operands = [dmix, dmix, u1] + [proj] * 9 + [lng, lnb, cbw]
    return _call(body, "mixer_bwd1", (nc, T // tm), in_specs,
                 (tile, tile, tile, chan(KA), chan(1), chan(1), chan(1), chan(KB)),
                 (jax.ShapeDtypeStruct((T, CW), F32), jax.ShapeDtypeStruct((T, CW), F32),
                  jax.ShapeDtypeStruct((T, CW), BF16), jax.ShapeDtypeStruct((KA, CW), F32),
                  jax.ShapeDtypeStruct((1, CW), F32), jax.ShapeDtypeStruct((1, CW), F32),
                  jax.ShapeDtypeStruct((1, CW), F32), jax.ShapeDtypeStruct((KB, CW), F32)), operands,
                 scratch=[pltpu.VMEM((HALO + tm, tc), F32), pltpu.VMEM((HALO + tm, tc), F32)],
                 sem=("parallel", "arbitrary"))


def _mixer_bwd2(du1, dcv, proj, caw, cbw, T, CW):
    KA, KB = caw.shape[0], cbw.shape[0]
    tm, tc, hb, nc = _mixer_tiles(T, CW)
    n_i = T // tm

    def sec(s):
        return pl.BlockSpec((tm, tc), lambda i, c: (i, s * nc + c))

    def chan(rows):
        return pl.BlockSpec((rows, tc), lambda i, c: (0, c))

    tile = pl.BlockSpec((tm, tc), lambda i, c: (i, c))
    nxt = pl.BlockSpec((HALO, tc), lambda i, c: (jnp.minimum((i + 1) * hb, n_i * hb - 1), c))

    def body(du1_ref, du1n, dcv_ref, dcvn, av, ag, cg, bh, caw_ref, cbw_ref, dav, dag, dcg, dbh, bufa, bufb):
        last = pl.program_id(0) == n_i - 1
        bufa[pl.ds(0, tm), :] = du1_ref[...]
        bufa[pl.ds(tm, HALO), :] = jnp.where(last, 0.0, du1n[...])
        bufb[pl.ds(0, tm), :] = dcv_ref[...]
        bufb[pl.ds(tm, HALO), :] = jnp.where(last, 0.0, dcvn[...])
        for cs in _col_chunks(tc):
            for r0, nr in _row_chunks(tm):
                rows = pl.ds(r0, nr)
                du0 = _conv_bwd_input(bufa, caw_ref, KA, r0, nr, cs)
                sig = _sigmoid(ag[rows, cs].astype(F32))
                dav[rows, cs] = (du0 * sig).astype(BF16)
                dag[rows, cs] = (du0 * av[rows, cs].astype(F32) * (sig * (1.0 - sig))).astype(BF16)
                dch = _conv_bwd_input(bufb, cbw_ref, KB, r0, nr, cs)
                dcg[rows, cs] = (dch * bh[rows, cs].astype(F32)).astype(BF16)
                dbh[rows, cs] = (dch * cg[rows, cs].astype(F32)).astype(BF16)

    in_specs = [tile, nxt, tile, nxt, sec(0), sec(1), sec(3), sec(4), chan(KA), chan(KB)]
    operands = [du1, du1, dcv, dcv, proj, proj, proj, proj, caw, cbw]
    out = jax.ShapeDtypeStruct((T, CW), BF16)
    return _call(body, "mixer_bwd2", (n_i, nc), in_specs, (tile, tile, tile, tile), (out, out, out, out),
                 operands, scratch=[pltpu.VMEM((HALO + tm, tc), F32), pltpu.VMEM((HALO + tm, tc), F32)],
                 sem=("parallel", "parallel"))


def _ffn_tiles(T):
    tm = _tile(T, 512, HALO)
    return tm, tm // HALO, T // tm


def _ffn_act_fwd(gpre, up, cfw):
    nb, T, F = gpre.shape
    KF = cfw.shape[1]
    tm, hb, n_i = _ffn_tiles(T)
    tile = pl.BlockSpec((None, tm, F), lambda b, i: (b, i, 0))
    prev = pl.BlockSpec((None, HALO, F), lambda b, i: (b, jnp.maximum(i * hb - 1, 0), 0))
    wspec = pl.BlockSpec((None, KF, F), lambda b, i: (b, 0, 0))

    def body(g_ref, gh_ref, up_ref, w_ref, f_ref, buf):
        buf[pl.ds(HALO, tm), :] = g_ref[...].astype(F32)
        buf[pl.ds(0, HALO), :] = jnp.where(pl.program_id(1) == 0, 0.0, gh_ref[...].astype(F32))
        for cs in _col_chunks(F):
            for r0, nr in _row_chunks(tm):
                rows = pl.ds(r0, nr)
                g = _conv_fwd(buf, w_ref, KF, r0, nr, cs)
                f_ref[rows, cs] = (g * _sigmoid(g) * up_ref[rows, cs].astype(F32)).astype(BF16)

    return _call(body, "ffn_act_fwd", (nb, n_i), [tile, prev, tile, wspec], tile,
                 jax.ShapeDtypeStruct((nb, T, F), BF16), [gpre, gpre, up, cfw],
                 scratch=[pltpu.VMEM((HALO + tm, F), F32)], sem=("parallel", "parallel"))


def _ffn_act_bwd(df, gpre, up, cfw):
    nb, T, F = gpre.shape
    KF = cfw.shape[1]
    tm, hb, n_i = _ffn_tiles(T)
    extra = 2 * SUBLANES
    assert KF - 1 <= SUBLANES and HALO >= extra
    tile = pl.BlockSpec((None, tm, F), lambda b, i: (b, i, 0))
    prev = pl.BlockSpec((None, HALO, F), lambda b, i: (b, jnp.maximum(i * hb - 1, 0), 0))
    nxt = pl.BlockSpec((None, HALO, F), lambda b, i: (b, jnp.minimum((i + 1) * hb, n_i * hb - 1), 0))
    wspec = pl.BlockSpec((None, KF, F), lambda b, i: (b, 0, 0))

    def body(df_ref, g_ref, gh_ref, up_ref, dfn_ref, gn_ref, upn_ref, w_ref, dgpre_ref, dup_ref, dw_ref, gbuf, dgbuf):
        first = pl.program_id(1) == 0
        last = pl.program_id(1) == n_i - 1

        @pl.when(first)
        def _():
            dw_ref[...] = jnp.zeros_like(dw_ref)

        gbuf[pl.ds(0, HALO), :] = jnp.where(first, 0.0, gh_ref[...].astype(F32))
        gbuf[pl.ds(HALO, tm), :] = g_ref[...].astype(F32)
        gbuf[pl.ds(HALO + tm, HALO), :] = gn_ref[...].astype(F32)
        for cs in _col_chunks(F):
            accs = [jnp.zeros((SUBLANES, cs.stop - cs.start), F32)] * KF
            for r0, nr in _row_chunks(tm):
                rows = pl.ds(r0, nr)
                g = _conv_fwd(gbuf, w_ref, KF, r0, nr, cs)
                sig = _sigmoid(g)
                dfv = df_ref[rows, cs].astype(F32)
                dup_ref[rows, cs] = (dfv * (g * sig)).astype(BF16)
                dg = dfv * up_ref[rows, cs].astype(F32) * _silu_grad(g, sig)
                dgbuf[rows, cs] = dg
                accs = _conv_bwd_weight(accs, dg, gbuf, KF, r0, nr, cs)
            for k in range(KF):
                _add_row(dw_ref, k, cs, accs[k])
            g = _conv_fwd(gbuf, w_ref, KF, tm, extra, cs)
            dg_next = (dfn_ref[pl.ds(0, extra), cs].astype(F32) * upn_ref[pl.ds(0, extra), cs].astype(F32)
                       * _silu_grad(g, _sigmoid(g)))
            dgbuf[pl.ds(tm, extra), cs] = jnp.where(last, 0.0, dg_next)
        for cs in _col_chunks(F):
            for r0, nr in _row_chunks(tm):
                dgpre_ref[pl.ds(r0, nr), cs] = _conv_bwd_input(dgbuf, w_ref, KF, r0, nr, cs).astype(BF16)

    out = jax.ShapeDtypeStruct((nb, T, F), BF16)
    return _call(body, "ffn_act_bwd", (nb, n_i), [tile, tile, prev, tile, nxt, nxt, nxt, wspec], (tile, tile, wspec),
                 (out, out, jax.ShapeDtypeStruct((nb, KF, F), F32)), [df, gpre, gpre, up, df, gpre, up, cfw],
                 scratch=[pltpu.VMEM((2 * HALO + tm, F), F32), pltpu.VMEM((tm + extra, F), F32)],
                 sem=("parallel", "arbitrary"))


def _softmax_rows(s):
    e = jnp.exp(s - jnp.max(s, axis=-1, keepdims=True))
    return e / jnp.sum(e, axis=-1, keepdims=True)


def _attn_fwd(q, k, v):
    T, D = q.shape
    Mm = k.shape[0]
    hd = D // N_XATTN_HEADS
    scale = hd ** -0.5
    tm = _tile(T, 256, 16)

    def body(q_ref, k_ref, v_ref, o_ref):
        for h in range(N_XATTN_HEADS):
            sl = slice(h * hd, (h + 1) * hd)
            s = lax.dot_general(q_ref[:, sl], k_ref[:, sl], _DOT_DIMS["nt"], preferred_element_type=F32) * scale
            p = _softmax_rows(s).astype(BF16)
            o_ref[:, sl] = jnp.dot(p, v_ref[:, sl], preferred_element_type=F32).astype(BF16)

    row = pl.BlockSpec((tm, D), lambda i: (i, 0))
    full = pl.BlockSpec((Mm, D), lambda i: (0, 0))
    return _call(body, "attn_fwd", (T // tm,), [row, full, full], row, jax.ShapeDtypeStruct((T, D), BF16),
                 [q, k, v], sem=("parallel",))


def _attn_bwd(q, k, v, do):
    T, D = q.shape
    Mm = k.shape[0]
    hd = D // N_XATTN_HEADS
    scale = hd ** -0.5
    tm = _tile(T, 256, 16)
    n_i = T // tm

    def body(q_ref, do_ref, k_ref, v_ref, dq_ref, dk_ref, dv_ref, dk_acc, dv_acc):
        @pl.when(pl.program_id(0) == 0)
        def _():
            dk_acc[...] = jnp.zeros_like(dk_acc)
            dv_acc[...] = jnp.zeros_like(dv_acc)

        for h in range(N_XATTN_HEADS):
            sl = slice(h * hd, (h + 1) * hd)
            qh, kh, doh = q_ref[:, sl], k_ref[:, sl], do_ref[:, sl]
            s = lax.dot_general(qh, kh, _DOT_DIMS["nt"], preferred_element_type=F32) * scale
            p = _softmax_rows(s)
            dv_acc[:, sl] += lax.dot_general(p.astype(BF16), doh, _DOT_DIMS["tn"], preferred_element_type=F32)
            dp = lax.dot_general(doh, v_ref[:, sl], _DOT_DIMS["nt"], preferred_element_type=F32)
            ds = (p * (dp - jnp.sum(dp * p, axis=-1, keepdims=True)) * scale).astype(BF16)
            dq_ref[:, sl] = jnp.dot(ds, kh, preferred_element_type=F32).astype(BF16)
            dk_acc[:, sl] += lax.dot_general(ds, qh, _DOT_DIMS["tn"], preferred_element_type=F32)

        @pl.when(pl.program_id(0) == n_i - 1)
        def _():
            dk_ref[...] = dk_acc[...].astype(BF16)
            dv_ref[...] = dv_acc[...].astype(BF16)

    row = pl.BlockSpec((tm, D), lambda i: (i, 0))
    full = pl.BlockSpec((Mm, D), lambda i: (0, 0))
    return _call(body, "attn_bwd", (n_i,), [row, row, full, full], (row, full, full),
                 (jax.ShapeDtypeStruct((T, D), BF16), jax.ShapeDtypeStruct((Mm, D), BF16),
                  jax.ShapeDtypeStruct((Mm, D), BF16)), [q, do, k, v],
                 scratch=[pltpu.VMEM((Mm, D), F32), pltpu.VMEM((Mm, D), F32)])


def _position():
    x, y, c = lax.axis_index("x"), lax.axis_index("y"), lax.axis_index("c")
    return x, y, c


def _peer(pos, k):
    x, y, c = pos
    return (1 - x if k & 4 else x, 1 - y if k & 2 else y, 1 - c if k & 1 else c)


def _index(pos):
    x, y, c = pos
    return 4 * x + 2 * y + c


def _sequencer_kernel(body, name, collective_id, out_type, operands):
    return pl.kernel(
        body, name=name, out_type=out_type,
        mesh=plsc.ScalarSubcoreMesh(axis_name="sequencer", num_cores=1),
        scratch_types=[pltpu.SemaphoreType.DMA, pltpu.SemaphoreType.DMA((7,)), pltpu.SemaphoreType.DMA],
        compiler_params=pltpu.CompilerParams(collective_id=collective_id),
    )(*operands)


def _handshake(peers):
    barrier = pltpu.get_barrier_semaphore()
    for peer in peers:
        pl.semaphore_signal(barrier, inc=1, device_id=peer, device_id_type=MESH)
    pl.semaphore_wait(barrier, len(peers))


def _sequencer_all_gather(name, collective_id, shards):
    n = len(shards)

    def body(*refs):
        x_refs, out_refs = refs[:n], refs[n:2 * n]
        send_sem, recv_sems, local_sem = refs[2 * n:]
        me = _position()
        x, y, c = me
        sibling = _peer(me, 1)
        first = (x + (1 - c) - 2 * x * (1 - c), y + c - 2 * y * c, c)
        second = (x + c - 2 * x * c, y + (1 - c) - 2 * y * (1 - c), c)
        diagonal = _peer(me, 6)
        _handshake([sibling, first, second])

        def copy(a, k, block, to, own=False):
            dst = out_refs[a].at[_index(block)]
            return pltpu.make_async_remote_copy(
                src_ref=x_refs[a] if own else dst, dst_ref=dst, send_sem=send_sem, recv_sem=recv_sems.at[k],
                device_id=to, device_id_type=MESH)

        local = [pltpu.make_async_copy(x_refs[a], out_refs[a].at[_index(me)], local_sem) for a in range(n)]
        started = [copy(a, 1 + j, me, peer, own=True) for a in range(n) for j, peer in enumerate((first, second))]
        started += [copy(a, 0, me, sibling, own=True) for a in range(n)]
        for cp in started + local:
            cp.start()
        for k, origin in ((1, first), (2, second), (3, diagonal)):
            for a in range(n):
                copy(a, k, origin, me).wait_recv()
            passed = [copy(a, 3 + k, origin, sibling) for a in range(n)]
            if k == 1:
                passed = [copy(a, 3, origin, second) for a in range(n)] + passed
            for cp in passed:
                cp.start()
            started += passed
        for k in (0, 4, 5, 6):
            for a in range(n):
                copy(a, k, sibling, me).wait_recv()
        for cp in started:
            cp.wait_send()
        for cp in local:
            cp.wait()

    return _sequencer_kernel(body, name, collective_id,
                             [jax.ShapeDtypeStruct((N_DEV,) + s.shape, s.dtype) for s in shards], shards)


def _chip_index(pos):
    return 2 * pos[0] + pos[1]


def _sequencer_to_sibling(name, collective_id, parts):
    n = len(parts)

    def body(*refs):
        p_refs, out_refs = refs[:n], refs[n:2 * n]
        send_sem, recv_sems, _ = refs[2 * n:]
        me = _position()
        sibling = _peer(me, 1)
        _handshake([sibling])
        copies = [pltpu.make_async_remote_copy(
            src_ref=p_refs[a].at[2 * q + sibling[2]], dst_ref=out_refs[a].at[q], send_sem=send_sem,
            recv_sem=recv_sems.at[0], device_id=sibling, device_id_type=MESH)
            for a in range(n) for q in range(N_DEV // 2)]
        for cp in copies:
            cp.start()
        for cp in copies:
            cp.wait_recv()
        for cp in copies:
            cp.wait_send()

    return _sequencer_kernel(body, name, collective_id,
                             [jax.ShapeDtypeStruct((N_DEV // 2,) + p.shape[1:], p.dtype) for p in parts], parts)


def _sequencer_to_chips(name, collective_id, sums):
    n = len(sums)

    def body(*refs):
        s_refs, out_refs = refs[:n], refs[n:2 * n]
        send_sem, recv_sems, local_sem = refs[2 * n:]
        me = _position()
        my_chip = _chip_index(me)
        peers = [_peer(me, 4), _peer(me, 2), _peer(me, 6)]
        _handshake(peers)
        local = [pltpu.make_async_copy(s_refs[a].at[my_chip], out_refs[a].at[my_chip], local_sem) for a in range(n)]
        sends = [pltpu.make_async_remote_copy(
            src_ref=s_refs[a].at[_chip_index(peer)], dst_ref=out_refs[a].at[my_chip], send_sem=send_sem,
            recv_sem=recv_sems.at[1 + j], device_id=peer, device_id_type=MESH)
            for a in range(n) for j, peer in enumerate(peers)]
        for cp in sends + local:
            cp.start()
        for j, peer in enumerate(peers):
            for a in range(n):
                pltpu.make_async_remote_copy(
                    src_ref=s_refs[a].at[my_chip], dst_ref=out_refs[a].at[_chip_index(peer)], send_sem=send_sem,
                    recv_sem=recv_sems.at[1 + j], device_id=peer, device_id_type=MESH).wait_recv()
        for cp in sends:
            cp.wait_send()
        for cp in local:
            cp.wait()

    return _sequencer_kernel(body, name, collective_id,
                             [jax.ShapeDtypeStruct(s.shape, s.dtype) for s in sums], sums)


def _chip_sum(name, parts, got, after=()):
    _, R, C = parts.shape
    tr = _tile(R, max(8, (3 << 20) // C), 16)
    n_after = len(after)
    limit = 6 * _nbytes((tr, C), parts.dtype) + VMEM_TEMP_ALLOWANCE

    def body(c_ref, p_ref, g_ref, *rest):
        o_ref = rest[n_after]
        o_ref[...] = (p_ref[...].astype(F32) + g_ref[...].astype(F32)).astype(o_ref.dtype)

    blk = pl.BlockSpec((None, tr, C), lambda q, i, c_ref: (q, i, 0))
    mine = pl.BlockSpec((None, tr, C), lambda q, i, c_ref: (2 * q + c_ref[0], i, 0))
    core = lax.axis_index("c").astype(jnp.int32).reshape(1)
    parts, got = [pltpu.with_memory_space_constraint(o, pltpu.HBM) for o in (parts, got)]
    after = [pltpu.with_memory_space_constraint(o, pltpu.HBM) for o in after]
    return pl.pallas_call(
        body, name=name, out_shape=pltpu.HBM((N_DEV // 2, R, C), parts.dtype),
        grid_spec=pltpu.PrefetchScalarGridSpec(
            num_scalar_prefetch=1, grid=(N_DEV // 2, R // tr),
            in_specs=[mine, blk] + [pl.BlockSpec(memory_space=pl.ANY)] * n_after, out_specs=blk),
        compiler_params=pltpu.CompilerParams(dimension_semantics=("parallel", "parallel"),
                                             vmem_limit_bytes=int(limit)),
    )(core, parts, got, *after)


def _all_reduce_rows(name, v):
    R, C = v.shape

    def body(v_ref, out_ref, gath, send_sems, recv_sems):
        me = _position()
        gath[_index(me)] = v_ref[...]
        sends = []
        for k in range(1, N_DEV):
            peer = _peer(me, k)
            sends.append(pltpu.make_async_remote_copy(
                src_ref=v_ref, dst_ref=gath.at[_index(me)], send_sem=send_sems.at[k - 1],
                recv_sem=recv_sems.at[k - 1], device_id=peer, device_id_type=MESH))
        for cp in sends:
            cp.start()
        for k in range(1, N_DEV):
            peer = _peer(me, k)
            pltpu.make_async_remote_copy(
                src_ref=v_ref, dst_ref=gath.at[_index(peer)], send_sem=send_sems.at[k - 1],
                recv_sem=recv_sems.at[k - 1], device_id=peer, device_id_type=MESH).wait_recv()
        for cp in sends:
            cp.wait_send()
        tot = gath[0]
        for s in range(1, N_DEV):
            tot = tot + gath[s]
        out_ref[...] = tot

    return pl.pallas_call(
        body, name=name, out_shape=jax.ShapeDtypeStruct((R, C), F32),
        in_specs=[VMEM_SPEC], out_specs=VMEM_SPEC,
        scratch_shapes=[pltpu.VMEM((N_DEV, R, C), F32), pltpu.SemaphoreType.DMA((7,)),
                        pltpu.SemaphoreType.DMA((7,))],
    )(v)


def _adamw_math(g, w, m, v):
    m = ADAM_B1 * m + (1.0 - ADAM_B1) * g
    v = ADAM_B2 * v + (1.0 - ADAM_B2) * (g * g)
    m_hat = m / (1.0 - ADAM_B1 ** ADAM_STEP)
    v_hat = v / (1.0 - ADAM_B2 ** ADAM_STEP)
    delta = -ADAM_LR * (m_hat / (jnp.sqrt(v_hat) + ADAM_EPS) + ADAM_WD * w)
    return delta, m, v


def _adamw_tile(n):
    def body(p_ref, w_ref, m_ref, v_ref, g_ref, d_ref, nm_ref, nv_ref):
        g = p_ref[0].astype(F32)
        for s in range(1, n):
            g = g + p_ref[s].astype(F32)
        g_ref[...] = g
        d_ref[...], nm_ref[...], nv_ref[...] = _adamw_math(g, w_ref[...], m_ref[...], v_ref[...])
    return body


class _SideJob(NamedTuple):
    operands: list
    in_blocks: list
    out_blocks: list
    out_shapes: list
    body: object


def _adamw_job(parts, w, m, v):
    n, R, C = parts.shape
    lead = w.ndim - 2
    if R % (SIDE_TILES * 16) == 0:
        tile, index = (R // SIDE_TILES, C), lambda t: (t, 0)
    elif C % (SIDE_TILES * 128) == 0:
        tile, index = (R, C // SIDE_TILES), lambda t: (0, t)
    else:
        return None
    p_blk = ((n,) + tile, lambda t: (0,) + index(t))
    w_blk = ((None,) * lead + tile, lambda t: (0,) * lead + index(t))
    return _SideJob([parts, w, m, v], [p_blk, w_blk, w_blk, w_blk], [w_blk] * 4,
                    [jax.ShapeDtypeStruct(w.shape, F32)] * 4, _adamw_tile(n))


def _adamw(name, parts, w, m, v, after=()):
    n, R, C = parts.shape
    tr = _tile(R, max(8, (1 << 18) // C), 16)
    blk = _rows_spec(w, tr)
    out = jax.ShapeDtypeStruct(w.shape, F32)
    return _call(_adamw_tile(n), name, (R // tr,), [pl.BlockSpec((n, tr, C), lambda i: (0, i, 0)), blk, blk, blk],
                 (blk, blk, blk, blk), (out, out, out, out), [parts, w, m, v], sem=("parallel",), after=after)


def kernel(x, mem, g_mix, w_in, conv_a_w, conv_a_b, ln_a_g, ln_a_b, conv_b_w, w_out, g_xattn, g_mem, w_q, w_k, w_v, w_o, g_ffn, w_gate, w_up, conv_f_w, w_down, g_final, loss_target, m_g_mix, m_w_in, m_conv_a_w, m_conv_a_b, m_ln_a_g, m_ln_a_b, m_conv_b_w, m_w_out, m_g_xattn, m_g_mem, m_w_q, m_w_k, m_w_v, m_w_o, m_g_ffn, m_w_gate, m_w_up, m_conv_f_w, m_w_down, m_g_final, v_g_mix, v_w_in, v_conv_a_w, v_conv_a_b, v_ln_a_g, v_ln_a_b, v_conv_b_w, v_w_out, v_g_xattn, v_g_mem, v_w_q, v_w_k, v_w_v, v_w_o, v_g_ffn, v_w_gate, v_w_up, v_conv_f_w, v_w_down, v_g_final):
    T, D = x.shape[1], x.shape[2]
    Mm = mem.shape[1]
    CW = conv_a_b.shape[1]
    INB = w_in.shape[2]
    FB = w_gate.shape[2]
    KA, KB, KF = conv_a_w.shape[1], conv_b_w.shape[1], conv_f_w.shape[1]
    DB = D // N_DEV
    assert 5 * CW == N_DEV * INB and 2 * CW == D

    x2, mem2, tgt = x[0], mem[0], loss_target[0]
    g_mem2, g_final2 = g_mem.reshape(1, D), g_final.reshape(1, D)

    def bf16(name, w):
        return _cast_bf16("cast_" + name, w)

    Win, caw, cbw = _sequencer_all_gather(
        "ag_in", AG_ID, [bf16("w_in", w_in), conv_a_w[0], conv_b_w[0]])
    Wout, = _sequencer_all_gather("ag_out", AG_ID, [bf16("w_out", w_out)])
    Wq, Wk, Wv, Wo = _sequencer_all_gather(
        "ag_attn", AG_ID, [bf16("w_q", w_q), bf16("w_k", w_k), bf16("w_v", w_v), bf16("w_o", w_o)])
    def transposed(w):
        return jnp.transpose(w[0])

    gate_t = [transposed(a) for a in (w_gate, m_w_gate, v_w_gate)]
    up_t = [transposed(a) for a in (w_up, m_w_up, v_w_up)]
    WgateT, cfw = _sequencer_all_gather("ag_gate", AG_ID, [bf16("w_gate", gate_t[0]), conv_f_w[0]])
    WupT, = _sequencer_all_gather("ag_up", AG_ID, [bf16("w_up", up_t[0])])
    Wdown, = _sequencer_all_gather("ag_down", AG_ID, [bf16("w_down", w_down)])
    Wout, Wq, Wk, Wv, Wo = [w.reshape(D, D) for w in (Wout, Wq, Wk, Wv, Wo)]
    caw = jnp.transpose(caw, (1, 0, 2)).reshape(KA, CW)
    cbw = jnp.transpose(cbw, (1, 0, 2)).reshape(KB, CW)

    xn1 = _rms_fwd("rms_mix", x2, g_mix)
    proj = _matmul("mm_proj", "nn", xn1, Win, M=T, N=INB, K=D, nb=N_DEV, b_lay="blk", o_lay="col", tn=INB,
                   out_dtype=BF16)
    mix, u1 = _mixer_fwd(proj, caw, conv_a_b, ln_a_g, ln_a_b, cbw, T, CW)
    h1 = _matmul("mm_h1", "nn", mix, Wout.reshape(2, CW, D), M=T, N=D, K=CW, nb=2, a_lay="blk", b_lay="blk",
                 red_block=True, group=2, tn=512, res=x2)
    xn2 = _rms_fwd("rms_xattn", h1, g_xattn)
    q = _matmul("mm_q", "nn", xn2, Wq, M=T, N=D, K=D, out_dtype=BF16, tn=1024)
    memn = _rms_fwd("rms_mem", mem2, g_mem2, after=[q])
    kk = _matmul("mm_k", "nn", memn, Wk, M=Mm, N=D, K=D, out_dtype=BF16)
    vv = _matmul("mm_v", "nn", memn, Wv, M=Mm, N=D, K=D, out_dtype=BF16)
    o = _attn_fwd(q, kk, vv)
    h2 = _matmul("mm_h2", "nn", o, Wo, M=T, N=D, K=D, res=h1, tn=1024)
    xn3 = _rms_fwd("rms_ffn", h2, g_ffn)
    gpre = _matmul("mm_gate", "nt", xn3, WgateT, M=T, N=FB, K=D, nb=N_DEV, b_lay="blk", o_lay="blk", tn=FB,
                   out_dtype=BF16)
    up = _matmul("mm_up", "nt", xn3, WupT, M=T, N=FB, K=D, nb=N_DEV, b_lay="blk", o_lay="blk", tn=FB,
                 out_dtype=BF16)
    f = _ffn_act_fwd(gpre, up, cfw)
    h3 = _matmul("mm_h3", "nn", f, Wdown, M=T, N=D, K=FB, nb=N_DEV, a_lay="blk", b_lay="blk", red_block=True,
                 group=2, res=h2)
    dh3, dh3b, loss_part, dg_final = _loss_head(h3, tgt, g_final2)

    wmv = {"w_in": (w_in, m_w_in, v_w_in), "conv_a_w": (conv_a_w, m_conv_a_w, v_conv_a_w),
           "conv_b_w": (conv_b_w, m_conv_b_w, v_conv_b_w), "w_out": (w_out, m_w_out, v_w_out),
           "w_q": (w_q, m_w_q, v_w_q), "w_k": (w_k, m_w_k, v_w_k), "w_v": (w_v, m_w_v, v_w_v),
           "w_o": (w_o, m_w_o, v_w_o), "w_gate": gate_t, "w_up": up_t,
           "conv_f_w": (conv_f_w, m_conv_f_w, v_conv_f_w), "w_down": (w_down, m_w_down, v_w_down)}
    res = {}
    pending = []

    def mm(*args, after=(), carry=(), **kwargs):
        behind = list(after) + pending
        pending.clear()
        jobs, alone = [], []
        for names, got in carry:
            for n, g in zip(names, got):
                job = _adamw_job(g, *wmv[n])
                (alone if job is None else jobs).append((n, g, job))
        results = None
        if jobs:
            out, results = _matmul(*args, after=behind, side=[job for _, _, job in jobs], **kwargs)
        else:
            out = _matmul(*args, after=behind, **kwargs)
        if results is None:
            alone, jobs = alone + jobs, []
        for (n, _, _), r in zip(jobs, results or []):
            res[n] = r
            pending.append(r[0])
        for n, g, _ in alone:
            res[n] = _adamw("adamw_" + n, g, *wmv[n], after=[out])
            pending.append(res[n][0])
        return out

    def to_sibling(tag, named_parts):
        got = _sequencer_to_sibling("rs1_" + tag, SIBLING_ID, [p for _, p in named_parts])
        return named_parts, got

    def to_chips(tag, stage1, after):
        named_parts, got = stage1
        sums = [_chip_sum("sum_" + n, p, g, after=after) for (n, p), g in zip(named_parts, got)]
        pending.extend(sums)
        return [n for n, _ in named_parts], _sequencer_to_chips("rs2_" + tag, CHIPS_ID, sums)

    def finish(stage2, after):
        names, got = stage2
        for n, g in zip(names, got):
            w, m, v = wmv[n]
            res[n] = _adamw("adamw_" + n, g, w, m, v, after=after)
            pending.append(res[n][0])

    def row_blocks(dw):
        return dw.reshape(N_DEV, DB, D)

    def conv_blocks(dw, K):
        return jnp.transpose(dw.reshape(K, N_DEV, CW // N_DEV), (1, 0, 2))

    dWdown = mm("mm_dw_down", "tn", f, dh3b, M=FB, N=D, K=T, nb=N_DEV, a_lay="blk", o_lay="blk",
                     out_dtype=BF16, tm=FB)
    s_down = to_sibling("down", [("w_down", dWdown)])
    df = mm("mm_df", "nt", dh3b, Wdown, M=T, N=FB, K=D, nb=N_DEV, b_lay="blk", o_lay="blk", tn=FB, out_dtype=BF16,
                 after=[dWdown])
    dgpre, dup, dcfw = _ffn_act_bwd(df, gpre, up, cfw)
    c_down = to_chips("down", s_down, after=[dgpre])
    dWgate = mm("mm_dw_gate", "tn", dgpre, xn3, M=FB, N=D, K=T, nb=N_DEV, a_lay="blk", o_lay="blk",
                     out_dtype=BF16, tm=FB)
    s_gate = to_sibling("gate", [("w_gate", dWgate), ("conv_f_w", dcfw)])
    dWup = mm("mm_dw_up", "tn", dup, xn3, M=FB, N=D, K=T, nb=N_DEV, a_lay="blk", o_lay="blk",
                   out_dtype=BF16, tm=FB, after=[dWgate])
    s_up = to_sibling("up", [("w_up", dWup)])
    dxn3 = mm("mm_dxn3_gate", "nn", dgpre, WgateT, M=T, N=D, K=FB, nb=N_DEV, a_lay="blk", b_lay="blk",
                   red_block=True, group=2, out_dtype=BF16, after=[dWup], carry=[c_down])
    c_gate = to_chips("gate", s_gate, after=[dxn3])
    dxn3 = mm("mm_dxn3_up", "nn", dup, WupT, M=T, N=D, K=FB, nb=N_DEV, a_lay="blk", b_lay="blk",
                   red_block=True, group=2, out_dtype=BF16, res=dxn3)
    c_up = to_chips("up", s_up, after=[dxn3])
    dh2, dh2b, dg_ffn = _rms_bwd("rms_bwd_ffn", dxn3, h2, g_ffn, dh3)

    dWo = mm("mm_dw_o", "tn", o, dh2b, M=D, N=D, K=T, out_dtype=BF16, tn=1024)
    s_o = to_sibling("o", [("w_o", row_blocks(dWo))])
    do = mm("mm_do", "nt", dh2b, Wo, M=T, N=D, K=D, out_dtype=BF16, tn=1024, after=[dWo])
    dq, dk, dv = _attn_bwd(q, kk, vv, do)
    dWq = mm("mm_dw_q", "tn", xn2, dq, M=D, N=D, K=T, out_dtype=BF16, carry=[c_gate])
    s_q = to_sibling("q", [("w_q", row_blocks(dWq))])
    dxn2 = mm("mm_dxn2", "nt", dq, Wq, M=T, N=D, K=D, out_dtype=BF16, after=[dWq], carry=[c_up])
    c_o = to_chips("o", s_o, after=[dxn2])
    c_q = to_chips("q", s_q, after=[dxn2])
    dh1, dh1b, dg_xattn = _rms_bwd("rms_bwd_xattn", dxn2, h1, g_xattn, dh2)
    dWk = mm("mm_dw_k", "tn", memn, dk, M=D, N=D, K=Mm, out_dtype=BF16, after=[dh1b])
    dWv = mm("mm_dw_v", "tn", memn, dv, M=D, N=D, K=Mm, out_dtype=BF16, after=[dh1b])
    s_kv = to_sibling("kv", [("w_k", row_blocks(dWk)), ("w_v", row_blocks(dWv))])
    dmemn = mm("mm_dmem_k", "nt", dk, Wk, M=Mm, N=D, K=D, after=[dWk, dWv])
    dmemn = mm("mm_dmem_v", "nt", dv, Wv, M=Mm, N=D, K=D, res=dmemn)
    dg_mem = _rms_bwd("rms_bwd_mem", dmemn, mem2, g_mem2)

    dWout = mm("mm_dw_out", "tn", mix, dh1b, M=CW, N=D, K=T, nb=2, a_lay="blk", o_lay="blk", out_dtype=BF16,
                    after=[dg_mem], carry=[c_o])
    s_out = to_sibling("out", [("w_out", row_blocks(dWout.reshape(D, D)))])
    dmix = mm("mm_dmix", "nt", dh1b, Wout, M=T, N=D, K=D, out_dtype=BF16, after=[dWout], carry=[c_q])
    c_kv = to_chips("kv", s_kv, after=[dmix])
    du1, dcv, dbg, dcaw, dcab, dlng, dlnb, dcbw = _mixer_bwd1(dmix, proj, u1, caw, ln_a_g, ln_a_b, cbw, T, CW)
    c_out = to_chips("out", s_out, after=[du1])
    dav, dag, dcg, dbh = _mixer_bwd2(du1, dcv, proj, caw, cbw, T, CW)
    dproj = jnp.concatenate([dav, dag, dbg, dcg, dbh], axis=1)
    dWin = mm("mm_dw_in", "tn", xn1, dproj, M=D, N=INB, K=T, nb=N_DEV, b_lay="col", o_lay="blk",
                   out_dtype=BF16, tm=512, tn=INB, carry=[c_kv])
    s_in = to_sibling("in", [("w_in", dWin), ("conv_a_w", conv_blocks(dcaw, KA)),
                             ("conv_b_w", conv_blocks(dcbw, KB))])
    finish(c_out, after=[dWin])
    c_in = to_chips("in", s_in, after=list(pending))
    dxn1 = mm("mm_dxn1", "nt", dproj, Win, M=T, N=D, K=INB, nb=N_DEV, a_lay="col", b_lay="blk",
                   red_block=True, group=2, out_dtype=BF16, after=[dWin])
    dx, _, dg_mix = _rms_bwd("rms_bwd_mix", dxn1, x2, g_mix, dh1, after=list(pending))

    def pair(a, b):
        return jnp.concatenate([a, b], axis=1)

    zeros_half = jnp.zeros((1, CW), F32)
    small_g = jnp.concatenate([
        dg_mix, pair(dcab, dlng), pair(dlnb, zeros_half), dg_xattn, dg_mem, dg_ffn, dg_final,
        jnp.broadcast_to(loss_part[:, :1], (1, D))], axis=0)
    small_sum = _all_reduce_rows("ar_small", small_g)
    loss = small_sum[7, 0]
    finish(c_in, after=[small_sum])

    def pack(a_mix, a_cab, a_lng, a_lnb, a_xattn, a_mem, a_ffn, a_final):
        return jnp.concatenate([a_mix, pair(a_cab, a_lng), pair(a_lnb, zeros_half), a_xattn, a_mem.reshape(1, D),
                                a_ffn, a_final.reshape(1, D), jnp.zeros((1, D), F32)], axis=0)

    small = _adamw("adamw_small", small_sum[None],
                   pack(g_mix, conv_a_b, ln_a_g, ln_a_b, g_xattn, g_mem, g_ffn, g_final),
                   pack(m_g_mix, m_conv_a_b, m_ln_a_g, m_ln_a_b, m_g_xattn, m_g_mem, m_g_ffn, m_g_final),
                   pack(v_g_mix, v_conv_a_b, v_ln_a_g, v_ln_a_b, v_g_xattn, v_g_mem, v_g_ffn, v_g_final))

    def unpack(a):
        return {"g_mix": a[0:1], "conv_a_b": a[1:2, :CW], "ln_a_g": a[1:2, CW:], "ln_a_b": a[2:3, :CW],
                "g_xattn": a[3:4], "g_mem": a[4], "g_ffn": a[5:6], "g_final": a[6]}

    small = [unpack(a) for a in small]
    order = ["g_mix", "w_in", "conv_a_w", "conv_a_b", "ln_a_g", "ln_a_b", "conv_b_w", "w_out", "g_xattn", "g_mem",
             "w_q", "w_k", "w_v", "w_o", "g_ffn", "w_gate", "w_up", "conv_f_w", "w_down", "g_final"]
    outs = [loss, dx[None]]
    for kind in range(4):
        for n in order:
            if n in ("w_gate", "w_up"):
                outs.append(jnp.transpose(res[n][kind])[None])
            else:
                outs.append(res[n][kind] if n in res else small[kind][n])
    return tuple(outs)
```
